```python
import jax, jax.numpy as jnp
from jax import lax
import numpy as np

D_MODEL = 1024
BATCH = 2
SEQ = 8192
DEPTH = 4

N_MIXERS = 3
HEAD_DIM = 64
RMS_EPS = 1e-6
SWA_Q_HEADS = D_MODEL // HEAD_DIM
SWA_KV_HEADS = SWA_Q_HEADS // 4
SWA_WINDOW = 128
ROPE_THETA = 150000.0
GLA_HEADS = 4
GLA_DK = D_MODEL // 2
GLA_DV = D_MODEL
GLA_RANK = 16
GLA_TAU = 16.0
GLA_CHUNK = 64
FOX_HEADS = D_MODEL // HEAD_DIM
FOX_BLOCK = 128
D_FF = -(-8 * D_MODEL // (3 * 256)) * 256

kernel_name = "hybrid_swa_gla_fox_adaln_trunk"


def rmsnorm(x, g):
    xf = x.astype(jnp.float32)
    y = xf * lax.rsqrt(jnp.mean(xf * xf, axis=-1, keepdims=True) + RMS_EPS)
    return (y * g.astype(jnp.float32)).astype(x.dtype)


def modulate(h, shift, scale):
    return h * (1 + scale[:, None, :]) + shift[:, None, :]


def rope(x, pos):
    hd = x.shape[-1]
    inv = 1.0 / (ROPE_THETA ** (jnp.arange(0, hd, 2, dtype=jnp.float32) / hd))
    ang = pos[:, None] * inv[None, :]
    cos = jnp.cos(ang)[None, :, None, :]
    sin = jnp.sin(ang)[None, :, None, :]
    xf = x.astype(jnp.float32)
    x1, x2 = jnp.split(xf, 2, axis=-1)
    return jnp.concatenate([x1 * cos - x2 * sin, x2 * cos + x1 * sin], axis=-1).astype(x.dtype)


def swa_mixer(h, w_in, sinks, w_o):
    B, S, _ = h.shape
    W, hd, Hq, Hkv = SWA_WINDOW, HEAD_DIM, SWA_Q_HEADS, SWA_KV_HEADS
    G = Hq // Hkv
    nb = S // W
    proj = h @ w_in
    q, k, v = jnp.split(proj, [Hq * hd, (Hq + Hkv) * hd], axis=-1)
    pos = jnp.arange(S, dtype=jnp.float32)
    q = rope(q.reshape(B, S, Hq, hd), pos)
    k = rope(k.reshape(B, S, Hkv, hd), pos)
    v = v.reshape(B, S, Hkv, hd)
    qb = q.reshape(B, nb, W, Hkv, G, hd)
    kb = k.reshape(B, nb, W, Hkv, hd)
    vb = v.reshape(B, nb, W, Hkv, hd)
    pad = ((0, 0), (1, 0), (0, 0), (0, 0), (0, 0))
    kk = jnp.concatenate([jnp.pad(kb, pad)[:, :-1], kb], axis=2)
    vv = jnp.concatenate([jnp.pad(vb, pad)[:, :-1], vb], axis=2)
    scores = jnp.einsum('bnqkgd,bnskd->bnkgqs', qb, kk).astype(jnp.float32) * (hd ** -0.5)
    i = jnp.arange(W)[:, None]
    j = jnp.arange(2 * W)[None, :]
    diff = i - j + W
    band = (diff >= 0) & (diff < W)
    has_prev = (jnp.arange(nb) > 0)[:, None, None] | (j >= W)[None]
    mask = band[None] & has_prev
    scores = jnp.where(mask[None, :, None, None], scores, -jnp.inf)
    sink_col = jnp.broadcast_to(
        sinks.astype(jnp.float32).reshape(Hkv, G)[None, None, :, :, None, None],
        scores.shape[:-1] + (1,))
    probs = jax.nn.softmax(jnp.concatenate([scores, sink_col], axis=-1), axis=-1)[..., :-1]
    out = jnp.einsum('bnkgqs,bnskd->bnqkgd', probs.astype(vv.dtype), vv)
    return out.reshape(B, S, Hq * hd) @ w_o


def gla_chunk_step(state, inp):
    q, k, v, b = inp
    C = q.shape[2]
    causal = jnp.tril(jnp.ones((C, C), dtype=bool))
    diff = b[:, :, :, None, :] - b[:, :, None, :, :]
    decay = jnp.exp(jnp.where(causal[None, None, :, :, None], diff, -jnp.inf))
    attn = jnp.einsum('bhtd,bhsd,bhtsd->bhts', q, k, decay)
    o = attn @ v + jnp.einsum('bhtd,bhde->bhte', q * jnp.exp(b), state)
    b_last = b[:, :, -1:, :]
    state = jnp.exp(b_last[:, :, 0, :])[..., None] * state + \
        jnp.einsum('bhsd,bhse->bhde', k * jnp.exp(b_last - b), v)
    return state, o


def gla_mixer(h, w_in, w_gate_up, b_gate, head_norm, w_o):
    B, S, _ = h.shape
    H, C = GLA_HEADS, GLA_CHUNK
    dk, dv = GLA_DK // H, GLA_DV // H
    nc = S // C
    proj = h @ w_in
    q, k, v, r, a_low = jnp.split(
        proj, [GLA_DK, 2 * GLA_DK, 2 * GLA_DK + GLA_DV, 2 * GLA_DK + 2 * GLA_DV], axis=-1)
    log_alpha = jax.nn.log_sigmoid((a_low @ w_gate_up + b_gate).astype(jnp.float32)) / GLA_TAU

    def to_chunks(t, d):
        return t.astype(jnp.float32).reshape(B, nc, C, H, d).transpose(1, 0, 3, 2, 4)

    qc = to_chunks(q, dk) * (dk ** -0.5)
    kc = to_chunks(k, dk)
    vc = to_chunks(v, dv)
    bc = jnp.cumsum(to_chunks(log_alpha, dk), axis=3)
    state0 = jnp.zeros((B, H, dk, dv), jnp.float32)
    _, o = lax.scan(gla_chunk_step, state0, (qc, kc, vc, bc))
    o = o.transpose(1, 0, 3, 2, 4).reshape(B, S, H, dv).astype(h.dtype)
    o = rmsnorm(o, head_norm)
    o = o * jax.nn.silu(r).reshape(B, S, H, dv)
    return o.reshape(B, S, GLA_DV) @ w_o


def fox_mixer(h, w_in, b_f, w_o):
    B, S, D = h.shape
    H, hd, blk = FOX_HEADS, HEAD_DIM, FOX_BLOCK
    proj = h @ w_in
    q, k, v, f_logit = jnp.split(proj, [D, 2 * D, 3 * D], axis=-1)
    log_f = jax.nn.log_sigmoid(f_logit.astype(jnp.float32) + b_f.astype(jnp.float32))
    lc = jnp.cumsum(log_f, axis=1).transpose(0, 2, 1)
    q = q.reshape(B, S, H, hd).transpose(0, 2, 1, 3)
    k = k.reshape(B, S, H, hd).transpose(0, 2, 1, 3)
    v = v.reshape(B, S, H, hd).transpose(0, 2, 1, 3)
    outs = []
    for i in range(S // blk):
        t0, t1 = i * blk, (i + 1) * blk
        s = jnp.einsum('bhqd,bhkd->bhqk', q[:, :, t0:t1], k[:, :, :t1]).astype(jnp.float32) * (hd ** -0.5)
        s = s + lc[:, :, t0:t1, None] - lc[:, :, None, :t1]
        mask = jnp.arange(t1)[None, :] <= (t0 + jnp.arange(blk))[:, None]
        p = jax.nn.softmax(jnp.where(mask[None, None], s, -jnp.inf), axis=-1)
        outs.append(jnp.einsum('bhqk,bhkd->bhqd', p.astype(v.dtype), v[:, :, :t1]))
    o = jnp.concatenate(outs, axis=2).transpose(0, 2, 1, 3).reshape(B, S, D)
    return o @ w_o


def swiglu(h, w_gu, w_down):
    g, u = jnp.split(h @ w_gu, 2, axis=-1)
    return (jax.nn.silu(g) * u) @ w_down


def setup_inputs(seed: int = 0) -> dict:
    key = jax.random.key(seed)
    ks = iter(jax.random.split(key, 32))
    D = D_MODEL
    n_swa, n_gla, n_fox = (DEPTH + 2) // 3, (DEPTH + 1) // 3, DEPTH // 3
    nrm = lambda k, shape, s: s * jax.random.normal(k, shape, jnp.float32)
    swa_cols = (SWA_Q_HEADS + 2 * SWA_KV_HEADS) * HEAD_DIM
    gla_cols = 2 * GLA_DK + 2 * GLA_DV + GLA_RANK
    fox_cols = 3 * D + FOX_HEADS
    return {
        "x": nrm(next(ks), (BATCH, SEQ, D), 1.0),
        "c": nrm(next(ks), (BATCH, D), 1.0),
        "ada_w": nrm(next(ks), (DEPTH, D, 6 * D), 0.5 * D ** -0.5),
        "ada_b": nrm(next(ks), (DEPTH, 6 * D), 0.02),
        "norm_gain": 1.0 + nrm(next(ks), (DEPTH, 2, D), 0.05),
        "ffn_w_gu": nrm(next(ks), (DEPTH, D, 2 * D_FF), D ** -0.5),
        "ffn_w_down": nrm(next(ks), (DEPTH, D_FF, D), D_FF ** -0.5),
        "swa_w_in": nrm(next(ks), (n_swa, D, swa_cols), D ** -0.5),
        "swa_sinks": nrm(next(ks), (n_swa, SWA_Q_HEADS), 0.5),
        "swa_w_o": nrm(next(ks), (n_swa, SWA_Q_HEADS * HEAD_DIM, D), (SWA_Q_HEADS * HEAD_DIM) ** -0.5),
        "gla_w_in": nrm(next(ks), (n_gla, D, gla_cols), D ** -0.5),
        "gla_w_gate_up": nrm(next(ks), (n_gla, GLA_RANK, GLA_DK), GLA_RANK ** -0.5),
        "gla_b_gate": nrm(next(ks), (n_gla, GLA_DK), 0.1),
        "gla_head_norm": 1.0 + nrm(next(ks), (n_gla, GLA_DV // GLA_HEADS), 0.05),
        "gla_w_o": nrm(next(ks), (n_gla, GLA_DV, D), GLA_DV ** -0.5),
        "fox_w_in": nrm(next(ks), (n_fox, D, fox_cols), D ** -0.5),
        "fox_b_f": jax.random.uniform(next(ks), (n_fox, FOX_HEADS), jnp.float32, 0.0, 3.0),
        "fox_w_o": nrm(next(ks), (n_fox, D, D), D ** -0.5),
        "final_norm": 1.0 + nrm(next(ks), (D,), 0.05),
    }


def reference(x, c, ada_w, ada_b, norm_gain, ffn_w_gu, ffn_w_down,
              swa_w_in, swa_sinks, swa_w_o,
              gla_w_in, gla_w_gate_up, gla_b_gate, gla_head_norm, gla_w_o,
              fox_w_in, fox_b_f, fox_w_o, final_norm):
    c_act = jax.nn.silu(c)
    for i in range(DEPTH):
        kind, j = i % N_MIXERS, i // N_MIXERS
        mod = c_act @ ada_w[i] + ada_b[i]
        sh1, sc1, g1, sh2, sc2, g2 = jnp.split(mod, 6, axis=-1)
        h = modulate(rmsnorm(x, norm_gain[i, 0]), sh1, sc1)
        if kind == 0:
            y = swa_mixer(h, swa_w_in[j], swa_sinks[j], swa_w_o[j])
        elif kind == 1:
            y = gla_mixer(h, gla_w_in[j], gla_w_gate_up[j], gla_b_gate[j], gla_head_norm[j], gla_w_o[j])
        else:
            y = fox_mixer(h, fox_w_in[j], fox_b_f[j], fox_w_o[j])
        x = x + g1[:, None, :] * y
        h = modulate(rmsnorm(x, norm_gain[i, 1]), sh2, sc2)
        x = x + g2[:, None, :] * swiglu(h, ffn_w_gu[i], ffn_w_down[i])
    return rmsnorm(x, final_norm)
```

```python
import functools

import numpy as np
import jax
import jax.numpy as jnp
from jax import lax
from jax.experimental import pallas as pl
from jax.experimental.pallas import tpu as pltpu

D_MODEL = 1024
HEAD_DIM = 64
RMS_EPS = 1e-6
SWA_Q_HEADS = 16
SWA_KV_HEADS = 4
SWA_WINDOW = 128
ROPE_THETA = 150000.0
GLA_HEADS = 4
GLA_DK = 128
GLA_DV = 256
GLA_RANK = 16
GLA_TAU = 16.0
GLA_CHUNK = 64
GLA_SUB = 16
FOX_HEADS = 16
D_FF = 2816
N_MIXERS = 3

LANES = 128
NEG_BIG = -1e30
VMEM_LIMIT = 56 * 1024 * 1024

BF16 = jnp.bfloat16
F32 = jnp.float32


def _dot(a, b):
    return jnp.dot(a, b, preferred_element_type=F32)


def _dot_nt(a, b):
    return lax.dot_general(a, b, (((1,), (1,)), ((), ())), preferred_element_type=F32)


def _dot_tn(a, b):
    return lax.dot_general(a, b, (((0,), (0,)), ((), ())), preferred_element_type=F32)


def _split3(x):
    hi = x.astype(BF16)
    r1 = x - hi.astype(F32)
    mid = r1.astype(BF16)
    lo = (r1 - mid.astype(F32)).astype(BF16)
    return hi, mid, lo


def _cumsum_rows(x):
    n = x.shape[0]
    row = lax.broadcasted_iota(jnp.int32, (n, n), 0)
    col = lax.broadcasted_iota(jnp.int32, (n, n), 1)
    tril = jnp.where(row >= col, 1.0, 0.0).astype(BF16)
    hi, mid, lo = _split3(x)
    return _dot(tril, hi) + _dot(tril, mid) + _dot(tril, lo)


def _log_sigmoid(x):
    return jnp.minimum(x, 0.0) - jnp.log(1.0 + jnp.exp(-jnp.abs(x)))


def _silu(x):
    return x * (1.0 / (1.0 + jnp.exp(-x)))


def _rms(x, gain):
    ms = jnp.mean(x * x, axis=-1, keepdims=True)
    return x * lax.rsqrt(ms + RMS_EPS) * gain


def _norm_mod(x, gain, shift, scale):
    return _rms(x, gain) * (1.0 + scale) + shift


def _params(*sem):
    return pltpu.CompilerParams(dimension_semantics=sem, vmem_limit_bytes=VMEM_LIMIT)


def _const_spec(shape):
    nd = len(shape)
    return pl.BlockSpec(shape, lambda *_: (0,) * nd)


def _ada_kernel(ct_ref, w_ref, b_ref, out_ref):
    ca = _silu(ct_ref[...])
    w = w_ref[...]
    for b in range(ct_ref.shape[1]):
        col = ca[:, b:b + 1]
        out_ref[b:b + 1, :] = jnp.sum(col * w, axis=0, keepdims=True) + b_ref[...]


def _ada_call(c, ada_w, ada_b):
    depth, d, n = ada_w.shape
    bsz = c.shape[0]
    tn = 768
    return pl.pallas_call(
        _ada_kernel,
        grid=(depth, n // tn),
        in_specs=[
            pl.BlockSpec((d, bsz), lambda l, j: (0, 0)),
            pl.BlockSpec((None, d, tn), lambda l, j: (l, 0, j)),
            pl.BlockSpec((None, 1, tn), lambda l, j: (l, 0, j)),
        ],
        out_specs=pl.BlockSpec((None, bsz, tn), lambda l, j: (l, 0, j)),
        out_shape=jax.ShapeDtypeStruct((depth, bsz, n), F32),
        compiler_params=_params("arbitrary", "arbitrary"),
        name="ada_mod",
    )(c.T, ada_w, ada_b.reshape(depth, 1, n))


def _rope(x, cos, sin_signed):
    width = x.shape[1]
    reps = width // cos.shape[1]
    c = jnp.tile(cos, (1, reps))
    s = jnp.tile(sin_signed, (1, reps))
    lane = lax.broadcasted_iota(jnp.int32, x.shape, 1)
    first_half = (lane % HEAD_DIM) < (HEAD_DIM // 2)
    rot = jnp.where(first_half,
                    pltpu.roll(x, width - HEAD_DIM // 2, 1),
                    pltpu.roll(x, HEAD_DIM // 2, 1))
    return x * c + rot * s


def _swa_proj_kernel(x_ref, mod_ref, gain_ref, w_ref, cos_ref, sin_ref, q_ref, k_ref, v_ref):
    h = _norm_mod(x_ref[...], gain_ref[...], mod_ref[0:1, :], mod_ref[1:2, :]).astype(BF16)
    cos, sin = cos_ref[...], sin_ref[...]
    nq = q_ref.shape[1]
    nk = k_ref.shape[1]
    q = _dot(h, w_ref[:, :nq])
    q_ref[...] = (_rope(q, cos, sin) * (HEAD_DIM ** -0.5)).astype(BF16)
    k = _dot(h, w_ref[:, nq:nq + nk])
    k_ref[...] = _rope(k, cos, sin).astype(BF16)
    v_ref[...] = _dot(h, w_ref[:, nq + nk:]).astype(BF16)


def _swa_proj_call(x, mod, gain, w, cos, sin, tm=512):
    bsz, s, d = x.shape
    nq, nkv = D_MODEL, 2 * SWA_KV_HEADS * HEAD_DIM
    row = lambda b, i: (b, i, 0)
    return pl.pallas_call(
        _swa_proj_kernel,
        grid=(bsz, s // tm),
        in_specs=[
            pl.BlockSpec((None, tm, d), row),
            pl.BlockSpec((None, 6, d), lambda b, i: (b, 0, 0)),
            _const_spec((1, d)),
            _const_spec(w.shape),
            pl.BlockSpec((tm, LANES), lambda b, i: (i, 0)),
            pl.BlockSpec((tm, LANES), lambda b, i: (i, 0)),
        ],
        out_specs=[
            pl.BlockSpec((None, tm, nq), row),
            pl.BlockSpec((None, tm, nkv), row),
            pl.BlockSpec((None, tm, nkv), row),
        ],
        out_shape=[
            jax.ShapeDtypeStruct((bsz, s, nq), BF16),
            jax.ShapeDtypeStruct((bsz, s, nkv), BF16),
            jax.ShapeDtypeStruct((bsz, s, nkv), BF16),
        ],
        compiler_params=_params("arbitrary", "arbitrary"),
        name="swa_proj",
    )(x, mod, gain, w, cos, sin)


def _gla_proj_kernel(x_ref, mod_ref, gain_ref, w_ref, wa_ref, wg_ref, bg_ref,
                     q_ref, k_ref, v_ref, r_ref, la_ref):
    h = _norm_mod(x_ref[...], gain_ref[...], mod_ref[0:1, :], mod_ref[1:2, :]).astype(BF16)
    nk = q_ref.shape[1]
    nv = v_ref.shape[1]
    q_ref[...] = _dot(h, w_ref[:, :nk]).astype(BF16)
    k_ref[...] = _dot(h, w_ref[:, nk:2 * nk]).astype(BF16)
    v_ref[...] = _dot(h, w_ref[:, 2 * nk:2 * nk + nv]).astype(BF16)
    r_ref[...] = _dot(h, w_ref[:, 2 * nk + nv:]).astype(BF16)
    a_low = _dot(h, wa_ref[...]).astype(BF16)
    z = _dot(a_low, wg_ref[...]) + bg_ref[...]
    la_ref[...] = _log_sigmoid(z) * (1.0 / GLA_TAU)


def _gla_proj_call(x, mod, gain, w, wa, wg, bg, tm=512):
    bsz, s, d = x.shape
    nk, nv = GLA_HEADS * GLA_DK, GLA_HEADS * GLA_DV
    row = lambda b, i: (b, i, 0)
    return pl.pallas_call(
        _gla_proj_kernel,
        grid=(bsz, s // tm),
        in_specs=[
            pl.BlockSpec((None, tm, d), row),
            pl.BlockSpec((None, 6, d), lambda b, i: (b, 0, 0)),
            _const_spec((1, d)),
            _const_spec(w.shape),
            _const_spec(wa.shape),
            _const_spec(wg.shape),
            _const_spec(bg.shape),
        ],
        out_specs=[
            pl.BlockSpec((None, tm, nk), row),
            pl.BlockSpec((None, tm, nk), row),
            pl.BlockSpec((None, tm, nv), row),
            pl.BlockSpec((None, tm, nv), row),
            pl.BlockSpec((None, tm, nk), row),
        ],
        out_shape=[
            jax.ShapeDtypeStruct((bsz, s, nk), BF16),
            jax.ShapeDtypeStruct((bsz, s, nk), BF16),
            jax.ShapeDtypeStruct((bsz, s, nv), BF16),
            jax.ShapeDtypeStruct((bsz, s, nv), BF16),
            jax.ShapeDtypeStruct((bsz, s, nk), F32),
        ],
        compiler_params=_params("arbitrary", "arbitrary"),
        name="gla_proj",
    )(x, mod, gain, w, wa, wg, bg)


def _fox_proj_kernel(x_ref, mod_ref, gain_ref, wq_ref, wk_ref, wv_ref, wf_ref, bf_ref,
                     pq_ref, pk_ref, q_ref, k_ref, v_ref, carry_ref):
    @pl.when(pl.program_id(1) == 0)
    def _():
        carry_ref[...] = jnp.zeros_like(carry_ref)

    h = _norm_mod(x_ref[...], gain_ref[...], mod_ref[0:1, :], mod_ref[1:2, :]).astype(BF16)
    log_f = _log_sigmoid(_dot(h, wf_ref[...]) + bf_ref[...])
    lc = _cumsum_rows(log_f) + carry_ref[...]
    carry_ref[...] = lc[lc.shape[0] - 1:, :]
    hi, mid, lo = _split3(lc)
    aug = jnp.concatenate([hi, mid, lo, jnp.ones_like(hi)], axis=1)
    q_ref[...] = (_dot(h, wq_ref[...]) * (HEAD_DIM ** -0.5) + _dot(aug, pq_ref[...])).astype(BF16)
    k_ref[...] = (_dot(h, wk_ref[...]) + _dot(aug, pk_ref[...])).astype(BF16)
    v_ref[...] = _dot(h, wv_ref[...]).astype(BF16)


def _fox_proj_call(x, mod, gain, wq, wk, wv, wf, bf, pq, pk, tm=256):
    bsz, s, d = x.shape
    nqk = FOX_HEADS * LANES
    row = lambda b, i: (b, i, 0)
    return pl.pallas_call(
        _fox_proj_kernel,
        grid=(bsz, s // tm),
        in_specs=[
            pl.BlockSpec((None, tm, d), row),
            pl.BlockSpec((None, 6, d), lambda b, i: (b, 0, 0)),
            _const_spec((1, d)),
            _const_spec(wq.shape),
            _const_spec(wk.shape),
            _const_spec(wv.shape),
            _const_spec(wf.shape),
            _const_spec(bf.shape),
            _const_spec(pq.shape),
            _const_spec(pk.shape),
        ],
        out_specs=[
            pl.BlockSpec((None, tm, nqk), row),
            pl.BlockSpec((None, tm, nqk), row),
            pl.BlockSpec((None, tm, d), row),
        ],
        out_shape=[
            jax.ShapeDtypeStruct((bsz, s, nqk), BF16),
            jax.ShapeDtypeStruct((bsz, s, nqk), BF16),
            jax.ShapeDtypeStruct((bsz, s, d), BF16),
        ],
        scratch_shapes=[pltpu.VMEM((1, FOX_HEADS), F32)],
        compiler_params=_params("arbitrary", "arbitrary"),
        name="fox_proj",
    )(x, mod, gain, wq, wk, wv, wf, bf, pq, pk)


def _swa_attn_kernel(sink_ref, q_ref, kc_ref, kp_ref, vc_ref, vp_ref, o_ref):
    w = SWA_WINDOW
    has_prev = pl.program_id(1) > 0
    row = lax.broadcasted_iota(jnp.int32, (w, w), 0)
    col = lax.broadcasted_iota(jnp.int32, (w, w), 1)
    mask_cur = col <= row
    mask_prev = jnp.logical_and(col > row, has_prev)
    lane = lax.broadcasted_iota(jnp.int32, (w, LANES), 1)
    low = lane < HEAD_DIM
    group = SWA_Q_HEADS // SWA_KV_HEADS
    for pair in range(SWA_Q_HEADS // 2):
        qs = q_ref[:, pair * LANES:(pair + 1) * LANES]
        outs = []
        for e in range(2):
            head = 2 * pair + e
            g = head // group
            qh = jnp.where(low if e == 0 else jnp.logical_not(low), qs, jnp.zeros_like(qs))
            kc = kc_ref[:, g * LANES:(g + 1) * LANES]
            kp = kp_ref[:, g * LANES:(g + 1) * LANES]
            s_c = jnp.where(mask_cur, _dot_nt(qh, kc), NEG_BIG)
            s_p = jnp.where(mask_prev, _dot_nt(qh, kp), NEG_BIG)
            sink = sink_ref[head]
            m = jnp.maximum(jnp.maximum(jnp.max(s_c, axis=1, keepdims=True),
                                        jnp.max(s_p, axis=1, keepdims=True)), sink)
            p_c = jnp.exp(s_c - m)
            p_p = jnp.exp(s_p - m)
            den = (jnp.sum(p_c, axis=1, keepdims=True) + jnp.sum(p_p, axis=1, keepdims=True)
                   + jnp.exp(sink - m))
            o = (_dot(p_c.astype(BF16), vc_ref[:, g * LANES:(g + 1) * LANES])
                 + _dot(p_p.astype(BF16), vp_ref[:, g * LANES:(g + 1) * LANES]))
            outs.append(o / den)
        o_ref[:, pair * LANES:(pair + 1) * LANES] = jnp.where(low, outs[0], outs[1]).astype(BF16)


def _swa_attn_call(q, k, v, sinks):
    bsz, s, d = q.shape
    w = SWA_WINDOW
    nkv = k.shape[2]
    cur = lambda b, i: (b, i, 0)
    prev = lambda b, i: (b, jnp.maximum(i - 1, 0), 0)
    return pl.pallas_call(
        _swa_attn_kernel,
        grid=(bsz, s // w),
        in_specs=[
            pl.BlockSpec(memory_space=pltpu.SMEM),
            pl.BlockSpec((None, w, d), cur),
            pl.BlockSpec((None, w, nkv), cur),
            pl.BlockSpec((None, w, nkv), prev),
            pl.BlockSpec((None, w, nkv), cur),
            pl.BlockSpec((None, w, nkv), prev),
        ],
        out_specs=pl.BlockSpec((None, w, d), cur),
        out_shape=jax.ShapeDtypeStruct((bsz, s, d), BF16),
        compiler_params=_params("arbitrary", "arbitrary"),
        name="swa_attn",
    )(sinks, q, k, k, v, v)


def _gla_chunk(q, k, v, la, state_t):
    c, sub = GLA_CHUNK, GLA_SUB
    b = _cumsum_rows(la)
    b_last = b[c - 1:c, :]
    q_in = (q * jnp.exp(b)).astype(BF16)
    k_out = (k * jnp.exp(b_last - b)).astype(BF16)
    inter = _dot_nt(q_in, state_t.astype(BF16))

    col = lax.broadcasted_iota(jnp.int32, (sub, c), 1)
    row = lax.broadcasted_iota(jnp.int32, (sub, c), 0)
    blocks = []
    for i in range(c // sub):
        lo = i * sub
        q_i = q[lo:lo + sub, :]
        b_i = b[lo:lo + sub, :]
        if i == 0:
            a = jnp.zeros((sub, c), F32)
        else:
            ref = b[lo - 1:lo, :]
            q_t = (q_i * jnp.exp(b_i - ref)).astype(BF16)
            k_t = (k * jnp.exp(jnp.minimum(ref - b, 0.0))).astype(BF16)
            a = jnp.where(col < lo, _dot_nt(q_t, k_t), 0.0)
        for s in range(lo, lo + sub):
            w = jnp.exp(jnp.minimum(b_i - b[s:s + 1, :], 0.0))
            val = jnp.sum(q_i * k[s:s + 1, :] * w, axis=1, keepdims=True)
            a = jnp.where(col == s, jnp.where(row + lo >= s, val, 0.0), a)
        blocks.append(a)
    attn = jnp.concatenate(blocks, axis=0)
    o = inter + _dot(attn.astype(BF16), v)
    new_state_t = state_t * jnp.exp(b_last) + _dot_tn(v, k_out)
    return o, new_state_t


def _gla_kernel(q_ref, k_ref, v_ref, r_ref, la_ref, hn_ref, o_ref, state_ref):
    @pl.when(pl.program_id(2) == 0)
    def _():
        state_ref[...] = jnp.zeros_like(state_ref)

    c = GLA_CHUNK
    state_t = state_ref[...]
    for ci in range(q_ref.shape[0] // c):
        rows = slice(ci * c, (ci + 1) * c)
        q = q_ref[rows, :].astype(F32) * (GLA_DK ** -0.5)
        k = k_ref[rows, :].astype(F32)
        o, state_t = _gla_chunk(q, k, v_ref[rows, :], la_ref[rows, :], state_t)
        r = r_ref[rows, :].astype(F32)
        o_ref[rows, :] = (_rms(o, hn_ref[...]) * _silu(r)).astype(BF16)
    state_ref[...] = state_t


def _gla_call(q, k, v, r, la, head_norm, tm=256):
    bsz, s, _ = q.shape
    dk, dv = GLA_DK, GLA_DV
    blk = lambda b, h, i: (b, i, h)
    return pl.pallas_call(
        _gla_kernel,
        grid=(bsz, GLA_HEADS, s // tm),
        in_specs=[
            pl.BlockSpec((None, tm, dk), blk),
            pl.BlockSpec((None, tm, dk), blk),
            pl.BlockSpec((None, tm, dv), blk),
            pl.BlockSpec((None, tm, dv), blk),
            pl.BlockSpec((None, tm, dk), blk),
            _const_spec((1, dv)),
        ],
        out_specs=pl.BlockSpec((None, tm, dv), blk),
        out_shape=jax.ShapeDtypeStruct((bsz, s, GLA_HEADS * dv), BF16),
        scratch_shapes=[pltpu.VMEM((dv, dk), F32)],
        compiler_params=_params("arbitrary", "arbitrary", "arbitrary"),
        name="gla_mix",
    )(q, k, v, r, la, head_norm)


def _fox_attn_kernel(q_ref, k_ref, v_ref, o_ref, m_ref, l_ref, acc_ref, *, tk):
    tq = q_ref.shape[0]
    i = pl.program_id(2)
    m_ref[...] = jnp.full_like(m_ref, NEG_BIG)
    l_ref[...] = jnp.zeros_like(l_ref)
    acc_ref[...] = jnp.zeros_like(acc_ref)

    def step(off, masked):
        kj = k_ref[pl.ds(off, tk), :]
        vj = v_ref[pl.ds(off, tk), :]
        for e in range(2):
            s = _dot_nt(q_ref[:, e * LANES:(e + 1) * LANES], kj[:, e * LANES:(e + 1) * LANES])
            if masked:
                row = lax.broadcasted_iota(jnp.int32, s.shape, 0)
                col = lax.broadcasted_iota(jnp.int32, s.shape, 1)
                s = jnp.where(col <= row, s, NEG_BIG)
            m_old = m_ref[e]
            m_new = jnp.maximum(m_old, jnp.max(s, axis=1, keepdims=True))
            alpha = jnp.exp(m_old - m_new)
            p = jnp.exp(s - m_new)
            l_ref[e] = alpha * l_ref[e] + jnp.sum(p, axis=1, keepdims=True)
            acc_ref[e] = alpha * acc_ref[e] + _dot(p.astype(BF16), vj)
            m_ref[e] = m_new

    def body(j, carry):
        step(pl.multiple_of(j * tk, tk), False)
        return carry

    lax.fori_loop(0, i, body, 0)
    step(pl.multiple_of(i * tq, tq), True)

    lane = lax.broadcasted_iota(jnp.int32, (tq, LANES), 1)
    o0 = acc_ref[0] / l_ref[0]
    o1 = acc_ref[1] / l_ref[1]
    o_ref[...] = jnp.where(lane < HEAD_DIM, o0, o1).astype(BF16)


def _fox_attn_call(q, k, v, tq=256):
    bsz, s, _ = q.shape
    pairs = FOX_HEADS // 2
    return pl.pallas_call(
        functools.partial(_fox_attn_kernel, tk=tq),
        grid=(bsz, pairs, s // tq),
        in_specs=[
            pl.BlockSpec((None, tq, 2 * LANES), lambda b, p, i: (b, i, p)),
            pl.BlockSpec((None, s, 2 * LANES), lambda b, p, i: (b, 0, p)),
            pl.BlockSpec((None, s, LANES), lambda b, p, i: (b, 0, p)),
        ],
        out_specs=pl.BlockSpec((None, tq, LANES), lambda b, p, i: (b, i, p)),
        out_shape=jax.ShapeDtypeStruct((bsz, s, D_MODEL), BF16),
        scratch_shapes=[
            pltpu.VMEM((2, tq, 1), F32),
            pltpu.VMEM((2, tq, 1), F32),
            pltpu.VMEM((2, tq, LANES), F32),
        ],
        compiler_params=_params("arbitrary", "arbitrary", "arbitrary"),
        name="fox_attn",
    )(q, k, v)


def _post_kernel(x_ref, o_ref, mod_ref, gain_ref, wo_ref, wgu_ref, wd_ref, fn_ref, out_ref,
                 *, ff_chunk, final):
    x1 = x_ref[...] + mod_ref[2:3, :] * _dot(o_ref[...], wo_ref[...])
    h = _norm_mod(x1, gain_ref[...], mod_ref[3:4, :], mod_ref[4:5, :]).astype(BF16)
    acc = jnp.zeros(x1.shape, F32)
    for c0 in range(0, D_FF, ff_chunk):
        g = _dot(h, wgu_ref[:, c0:c0 + ff_chunk])
        u = _dot(h, wgu_ref[:, D_FF + c0:D_FF + c0 + ff_chunk])
        acc = acc + _dot((_silu(g) * u).astype(BF16), wd_ref[c0:c0 + ff_chunk, :])
    x2 = x1 + mod_ref[5:6, :] * acc
    if final:
        x2 = _rms(x2, fn_ref[...])
    out_ref[...] = x2


def _post_call(x, o, mod, gain, wo, wgu, wd, final_norm, final, tm=512, ff_chunk=256):
    bsz, s, d = x.shape
    row = lambda b, i: (b, i, 0)
    return pl.pallas_call(
        functools.partial(_post_kernel, ff_chunk=ff_chunk, final=final),
        grid=(bsz, s // tm),
        in_specs=[
            pl.BlockSpec((None, tm, d), row),
            pl.BlockSpec((None, tm, d), row),
            pl.BlockSpec((None, 6, d), lambda b, i: (b, 0, 0)),
            _const_spec((1, d)),
            _const_spec(wo.shape),
            _const_spec(wgu.shape),
            _const_spec(wd.shape),
            _const_spec((1, d)),
        ],
        out_specs=pl.BlockSpec((None, tm, d), row),
        out_shape=jax.ShapeDtypeStruct((bsz, s, d), F32),
        compiler_params=_params("arbitrary", "arbitrary"),
        name="post_ffn",
    )(x, o, mod, gain, wo, wgu, wd, final_norm)


def _rope_tables(s):
    half = HEAD_DIM // 2
    inv = 1.0 / (ROPE_THETA ** (jnp.arange(0, HEAD_DIM, 2, dtype=F32) / HEAD_DIM))
    ang = jnp.arange(s, dtype=F32)[:, None] * inv[None, :]
    cos, sin = jnp.cos(ang), jnp.sin(ang)
    reps = LANES // HEAD_DIM
    cos_t = jnp.tile(jnp.concatenate([cos, cos], axis=1), (1, reps))
    sin_t = jnp.tile(jnp.concatenate([-sin, sin], axis=1), (1, reps))
    assert half * 2 == HEAD_DIM
    return cos_t, sin_t


def _dup_heads(w, heads):
    w3 = w.reshape(w.shape[0], heads, HEAD_DIM)
    return jnp.concatenate([w3, w3], axis=2).reshape(w.shape[0], heads * LANES)


def _pad_heads(w, heads):
    w3 = w.reshape(w.shape[0], heads, HEAD_DIM)
    return jnp.concatenate([w3, jnp.zeros_like(w3)], axis=2).reshape(w.shape[0], heads * LANES)


def _fox_placement():
    h = FOX_HEADS
    pq = np.zeros((4 * h, h * LANES), np.float32)
    pk = np.zeros((4 * h, h * LANES), np.float32)
    for head in range(h):
        base = head * LANES + HEAD_DIM
        for part in range(3):
            pq[part * h + head, base + part] = 1.0
            pk[3 * h + head, base + part] = 1.0
            pq[3 * h + head, base + 3 + part] = 1.0
            pk[part * h + head, base + 3 + part] = -1.0
    return jnp.asarray(pq, BF16), jnp.asarray(pk, BF16)


def kernel(x, c, ada_w, ada_b, norm_gain, ffn_w_gu, ffn_w_down, swa_w_in, swa_sinks, swa_w_o,
           gla_w_in, gla_w_gate_up, gla_b_gate, gla_head_norm, gla_w_o, fox_w_in, fox_b_f, fox_w_o,
           final_norm):
    bsz, s, d = x.shape
    depth = ada_w.shape[0]
    mod_all = _ada_call(c, ada_w, ada_b).reshape(depth, bsz, 6, d)
    cos_t, sin_t = _rope_tables(s)
    pq, pk = _fox_placement()
    fn = final_norm.reshape(1, d)

    for i in range(depth):
        kind, j = i % N_MIXERS, i // N_MIXERS
        mod = mod_all[i]
        gain1 = norm_gain[i, 0].reshape(1, d)
        gain2 = norm_gain[i, 1].reshape(1, d)
        if kind == 0:
            w = swa_w_in[j]
            nq, nkv = SWA_Q_HEADS * HEAD_DIM, SWA_KV_HEADS * HEAD_DIM
            w_all = jnp.concatenate([w[:, :nq],
                                     _dup_heads(w[:, nq:nq + nkv], SWA_KV_HEADS),
                                     _dup_heads(w[:, nq + nkv:], SWA_KV_HEADS)], axis=1).astype(BF16)
            q, k, v = _swa_proj_call(x, mod, gain1, w_all, cos_t, sin_t)
            o = _swa_attn_call(q, k, v, swa_sinks[j])
            wo = swa_w_o[j]
        elif kind == 1:
            w = gla_w_in[j]
            n_main = 2 * GLA_HEADS * GLA_DK + 2 * GLA_HEADS * GLA_DV
            q, k, v, r, la = _gla_proj_call(
                x, mod, gain1, w[:, :n_main].astype(BF16), w[:, n_main:].astype(BF16),
                gla_w_gate_up[j].astype(BF16), gla_b_gate[j].reshape(1, -1))
            o = _gla_call(q, k, v, r, la, gla_head_norm[j].reshape(1, -1))
            wo = gla_w_o[j]
        else:
            w = fox_w_in[j]
            q, k, v = _fox_proj_call(
                x, mod, gain1,
                _pad_heads(w[:, :d], FOX_HEADS).astype(BF16),
                _pad_heads(w[:, d:2 * d], FOX_HEADS).astype(BF16),
                w[:, 2 * d:3 * d].astype(BF16), w[:, 3 * d:].astype(BF16),
                fox_b_f[j].reshape(1, -1), pq, pk)
            o = _fox_attn_call(q, k, v)
            wo = fox_w_o[j]
        x = _post_call(x, o, mod, gain2, wo.astype(BF16), ffn_w_gu[i].astype(BF16),
                       ffn_w_down[i].astype(BF16), fn, final=(i == depth - 1))
    return x
```

```python
import functools

import numpy as np
import jax
import jax.numpy as jnp
from jax import lax
from jax.experimental import pallas as pl
from jax.experimental.pallas import tpu as pltpu

D_MODEL = 1024
HEAD_DIM = 64
RMS_EPS = 1e-6
SWA_Q_HEADS = 16
SWA_KV_HEADS = 4
SWA_WINDOW = 128
ROPE_THETA = 150000.0
GLA_HEADS = 4
GLA_DK = 128
GLA_DV = 256
GLA_RANK = 16
GLA_TAU = 16.0
GLA_CHUNK = 64
GLA_SUB = 16
FOX_HEADS = 16
D_FF = 2816
N_MIXERS = 3

LANES = 128
NEG_BIG = -1e30
VMEM_LIMIT = 56 * 1024 * 1024

BF16 = jnp.bfloat16
F32 = jnp.float32


def _dot(a, b):
    return jnp.dot(a, b, preferred_element_type=F32)


def _dot_nt(a, b):
    return lax.dot_general(a, b, (((1,), (1,)), ((), ())), preferred_element_type=F32)


def _dot_tn(a, b):
    return lax.dot_general(a, b, (((0,), (0,)), ((), ())), preferred_element_type=F32)


def _split3(x):
    hi = x.astype(BF16)
    r1 = x - hi.astype(F32)
    mid = r1.astype(BF16)
    lo = (r1 - mid.astype(F32)).astype(BF16)
    return hi, mid, lo


def _cumsum_rows(x):
    n = x.shape[0]
    row = lax.broadcasted_iota(jnp.int32, (n, n), 0)
    col = lax.broadcasted_iota(jnp.int32, (n, n), 1)
    tril = jnp.where(row >= col, 1.0, 0.0).astype(BF16)
    hi, mid, lo = _split3(x)
    return _dot(tril, hi) + _dot(tril, mid) + _dot(tril, lo)


def _log_sigmoid(x):
    return jnp.minimum(x, 0.0) - jnp.log(1.0 + jnp.exp(-jnp.abs(x)))


def _silu(x):
    return x * (1.0 / (1.0 + jnp.exp(-x)))


def _rms(x, gain):
    ms = jnp.mean(x * x, axis=-1, keepdims=True)
    return x * lax.rsqrt(ms + RMS_EPS) * gain


def _norm_mod(x, gain, shift, scale):
    return _rms(x, gain) * (1.0 + scale) + shift


def _params(*sem):
    return pltpu.CompilerParams(dimension_semantics=sem, vmem_limit_bytes=VMEM_LIMIT)


def _const_spec(shape):
    nd = len(shape)
    return pl.BlockSpec(shape, lambda *_: (0,) * nd, pipeline_mode=pl.Buffered(1))


def _ada_kernel(ct_ref, w_ref, b_ref, out_ref):
    ca = _silu(ct_ref[...])
    w = w_ref[...]
    for b in range(ct_ref.shape[1]):
        col = ca[:, b:b + 1]
        out_ref[b:b + 1, :] = jnp.sum(col * w, axis=0, keepdims=True) + b_ref[...]


def _ada_call(c, ada_w, ada_b):
    depth, d, n = ada_w.shape
    bsz = c.shape[0]
    tn = 768
    return pl.pallas_call(
        _ada_kernel,
        grid=(depth, n // tn),
        in_specs=[
            pl.BlockSpec((d, bsz), lambda l, j: (0, 0)),
            pl.BlockSpec((None, d, tn), lambda l, j: (l, 0, j)),
            pl.BlockSpec((None, 1, tn), lambda l, j: (l, 0, j)),
        ],
        out_specs=pl.BlockSpec((None, bsz, tn), lambda l, j: (l, 0, j)),
        out_shape=jax.ShapeDtypeStruct((depth, bsz, n), F32),
        compiler_params=_params("arbitrary", "arbitrary"),
        name="ada_mod",
    )(c.T, ada_w, ada_b.reshape(depth, 1, n))


def _rope(x, cos, sin_signed):
    width = x.shape[1]
    reps = width // cos.shape[1]
    c = jnp.tile(cos, (1, reps))
    s = jnp.tile(sin_signed, (1, reps))
    lane = lax.broadcasted_iota(jnp.int32, x.shape, 1)
    first_half = (lane % HEAD_DIM) < (HEAD_DIM // 2)
    rot = jnp.where(first_half,
                    pltpu.roll(x, width - HEAD_DIM // 2, 1),
                    pltpu.roll(x, HEAD_DIM // 2, 1))
    return x * c + rot * s


def _swa_proj_kernel(x_ref, mod_ref, gain_ref, w_ref, cos_ref, sin_ref, q_ref, k_ref, v_ref):
    h = _norm_mod(x_ref[...], gain_ref[...], mod_ref[0:1, :], mod_ref[1:2, :]).astype(BF16)
    cos, sin = cos_ref[...], sin_ref[...]
    nq = q_ref.shape[1]
    nk = k_ref.shape[1]
    q = _dot(h, w_ref[:, :nq])
    q_ref[...] = (_rope(q, cos, sin) * (HEAD_DIM ** -0.5)).astype(BF16)
    k = _dot(h, w_ref[:, nq:nq + nk])
    k_ref[...] = _rope(k, cos, sin).astype(BF16)
    v_ref[...] = _dot(h, w_ref[:, nq + nk:]).astype(BF16)


def _swa_proj_call(x, mod, gain, w, cos, sin, tm=512):
    bsz, s, d = x.shape
    nq, nkv = D_MODEL, 2 * SWA_KV_HEADS * HEAD_DIM
    row = lambda b, i: (b, i, 0)
    return pl.pallas_call(
        _swa_proj_kernel,
        grid=(bsz, s // tm),
        in_specs=[
            pl.BlockSpec((None, tm, d), row),
            pl.BlockSpec((None, 6, d), lambda b, i: (b, 0, 0)),
            _const_spec((1, d)),
            _const_spec(w.shape),
            pl.BlockSpec((tm, LANES), lambda b, i: (i, 0)),
            pl.BlockSpec((tm, LANES), lambda b, i: (i, 0)),
        ],
        out_specs=[
            pl.BlockSpec((None, tm, nq), row),
            pl.BlockSpec((None, tm, nkv), row),
            pl.BlockSpec((None, tm, nkv), row),
        ],
        out_shape=[
            jax.ShapeDtypeStruct((bsz, s, nq), BF16),
            jax.ShapeDtypeStruct((bsz, s, nkv), BF16),
            jax.ShapeDtypeStruct((bsz, s, nkv), BF16),
        ],
        compiler_params=_params("arbitrary", "arbitrary"),
        name="swa_proj",
    )(x, mod, gain, w, cos, sin)


def _gla_proj_kernel(x_ref, mod_ref, gain_ref, w_ref, wa_ref, wg_ref, bg_ref,
                     q_ref, k_ref, v_ref, r_ref, la_ref):
    h = _norm_mod(x_ref[...], gain_ref[...], mod_ref[0:1, :], mod_ref[1:2, :]).astype(BF16)
    nk = q_ref.shape[1]
    nv = v_ref.shape[1]
    q_ref[...] = _dot(h, w_ref[:, :nk]).astype(BF16)
    k_ref[...] = _dot(h, w_ref[:, nk:2 * nk]).astype(BF16)
    v_ref[...] = _dot(h, w_ref[:, 2 * nk:2 * nk + nv]).astype(BF16)
    r_ref[...] = _dot(h, w_ref[:, 2 * nk + nv:]).astype(BF16)
    a_low = _dot(h, wa_ref[...]).astype(BF16)
    z = _dot(a_low, wg_ref[...]) + bg_ref[...]
    la_ref[...] = _log_sigmoid(z) * (1.0 / GLA_TAU)


def _gla_proj_call(x, mod, gain, w, wa, wg, bg, tm=512):
    bsz, s, d = x.shape
    nk, nv = GLA_HEADS * GLA_DK, GLA_HEADS * GLA_DV
    row = lambda b, i: (b, i, 0)
    return pl.pallas_call(
        _gla_proj_kernel,
        grid=(bsz, s // tm),
        in_specs=[
            pl.BlockSpec((None, tm, d), row),
            pl.BlockSpec((None, 6, d), lambda b, i: (b, 0, 0)),
            _const_spec((1, d)),
            _const_spec(w.shape),
            _const_spec(wa.shape),
            _const_spec(wg.shape),
            _const_spec(bg.shape),
        ],
        out_specs=[
            pl.BlockSpec((None, tm, nk), row),
            pl.BlockSpec((None, tm, nk), row),
            pl.BlockSpec((None, tm, nv), row),
            pl.BlockSpec((None, tm, nv), row),
            pl.BlockSpec((None, tm, nk), row),
        ],
        out_shape=[
            jax.ShapeDtypeStruct((bsz, s, nk), BF16),
            jax.ShapeDtypeStruct((bsz, s, nk), BF16),
            jax.ShapeDtypeStruct((bsz, s, nv), BF16),
            jax.ShapeDtypeStruct((bsz, s, nv), BF16),
            jax.ShapeDtypeStruct((bsz, s, nk), F32),
        ],
        compiler_params=_params("arbitrary", "arbitrary"),
        name="gla_proj",
    )(x, mod, gain, w, wa, wg, bg)


def _fox_proj_kernel(x_ref, mod_ref, gain_ref, wq_ref, wk_ref, wvt_ref, wf_ref, bf_ref,
                     pq_ref, pk_ref, q_ref, k_ref, vt_ref, carry_ref):
    @pl.when(pl.program_id(1) == 0)
    def _():
        carry_ref[...] = jnp.zeros_like(carry_ref)

    h = _norm_mod(x_ref[...], gain_ref[...], mod_ref[0:1, :], mod_ref[1:2, :]).astype(BF16)
    log_f = _log_sigmoid(_dot(h, wf_ref[...]) + bf_ref[...])
    lc = _cumsum_rows(log_f) + carry_ref[...]
    carry_ref[...] = lc[lc.shape[0] - 1:, :]
    hi, mid, lo = _split3(lc)
    aug = jnp.concatenate([hi, mid, lo, jnp.ones_like(hi)], axis=1)
    q_ref[...] = (_dot(h, wq_ref[...]) * (HEAD_DIM ** -0.5) + _dot(aug, pq_ref[...])).astype(BF16)
    k_ref[...] = (_dot(h, wk_ref[...]) + _dot(aug, pk_ref[...])).astype(BF16)
    vt = _dot_nt(wvt_ref[...], h)
    ones_row = lax.broadcasted_iota(jnp.int32, vt.shape, 0) % LANES == HEAD_DIM
    vt_ref[...] = jnp.where(ones_row, 1.0, vt).astype(BF16)


def _fox_proj_call(x, mod, gain, wq, wk, wvt, wf, bf, pq, pk, tm=512):
    bsz, s, d = x.shape
    nqk = FOX_HEADS * LANES
    row = lambda b, i: (b, i, 0)
    return pl.pallas_call(
        _fox_proj_kernel,
        grid=(bsz, s // tm),
        in_specs=[
            pl.BlockSpec((None, tm, d), row),
            pl.BlockSpec((None, 6, d), lambda b, i: (b, 0, 0)),
            _const_spec((1, d)),
            _const_spec(wq.shape),
            _const_spec(wk.shape),
            _const_spec(wvt.shape),
            _const_spec(wf.shape),
            _const_spec(bf.shape),
            _const_spec(pq.shape),
            _const_spec(pk.shape),
        ],
        out_specs=[
            pl.BlockSpec((None, tm, nqk), row),
            pl.BlockSpec((None, tm, nqk), row),
            pl.BlockSpec((None, None, nqk, tm), lambda b, i: (b, i, 0, 0)),
        ],
        out_shape=[
            jax.ShapeDtypeStruct((bsz, s, nqk), BF16),
            jax.ShapeDtypeStruct((bsz, s, nqk), BF16),
            jax.ShapeDtypeStruct((bsz, s // tm, nqk, tm), BF16),
        ],
        scratch_shapes=[pltpu.VMEM((1, FOX_HEADS), F32)],
        compiler_params=_params("arbitrary", "arbitrary"),
        name="fox_proj",
    )(x, mod, gain, wq, wk, wvt, wf, bf, pq, pk)


def _swa_attn_kernel(sink_ref, q_ref, kc_ref, kp_ref, vc_ref, vp_ref, o_ref):
    w = SWA_WINDOW
    has_prev = pl.program_id(1) > 0
    row = lax.broadcasted_iota(jnp.int32, (w, w), 0)
    col = lax.broadcasted_iota(jnp.int32, (w, w), 1)
    mask_cur = col <= row
    mask_prev = jnp.logical_and(col > row, has_prev)
    lane = lax.broadcasted_iota(jnp.int32, (w, LANES), 1)
    low = lane < HEAD_DIM
    group = SWA_Q_HEADS // SWA_KV_HEADS
    for pair in range(SWA_Q_HEADS // 2):
        qs = q_ref[:, pair * LANES:(pair + 1) * LANES]
        outs = []
        for e in range(2):
            head = 2 * pair + e
            g = head // group
            qh = jnp.where(low if e == 0 else jnp.logical_not(low), qs, jnp.zeros_like(qs))
            kc = kc_ref[:, g * LANES:(g + 1) * LANES]
            kp = kp_ref[:, g * LANES:(g + 1) * LANES]
            s_c = jnp.where(mask_cur, _dot_nt(qh, kc), NEG_BIG)
            s_p = jnp.where(mask_prev, _dot_nt(qh, kp), NEG_BIG)
            sink = sink_ref[head]
            m = jnp.maximum(jnp.maximum(jnp.max(s_c, axis=1, keepdims=True),
                                        jnp.max(s_p, axis=1, keepdims=True)), sink)
            p_c = jnp.exp(s_c - m)
            p_p = jnp.exp(s_p - m)
            den = (jnp.sum(p_c, axis=1, keepdims=True) + jnp.sum(p_p, axis=1, keepdims=True)
                   + jnp.exp(sink - m))
            o = (_dot(p_c.astype(BF16), vc_ref[:, g * LANES:(g + 1) * LANES])
                 + _dot(p_p.astype(BF16), vp_ref[:, g * LANES:(g + 1) * LANES]))
            outs.append(o / den)
        o_ref[:, pair * LANES:(pair + 1) * LANES] = jnp.where(low, outs[0], outs[1]).astype(BF16)


def _swa_attn_call(q, k, v, sinks):
    bsz, s, d = q.shape
    w = SWA_WINDOW
    nkv = k.shape[2]
    cur = lambda b, i: (b, i, 0)
    prev = lambda b, i: (b, jnp.maximum(i - 1, 0), 0)
    return pl.pallas_call(
        _swa_attn_kernel,
        grid=(bsz, s // w),
        in_specs=[
            pl.BlockSpec(memory_space=pltpu.SMEM),
            pl.BlockSpec((None, w, d), cur),
            pl.BlockSpec((None, w, nkv), cur),
            pl.BlockSpec((None, w, nkv), prev),
            pl.BlockSpec((None, w, nkv), cur),
            pl.BlockSpec((None, w, nkv), prev),
        ],
        out_specs=pl.BlockSpec((None, w, d), cur),
        out_shape=jax.ShapeDtypeStruct((bsz, s, d), BF16),
        compiler_params=_params("arbitrary", "arbitrary"),
        name="swa_attn",
    )(sinks, q, k, k, v, v)


def _gla_chunk(q, k, v, la, state_t):
    c, sub = GLA_CHUNK, GLA_SUB
    b = _cumsum_rows(la)
    b_last = b[c - 1:c, :]
    q_in = (q * jnp.exp(b)).astype(BF16)
    k_out = (k * jnp.exp(b_last - b)).astype(BF16)
    inter = _dot_nt(q_in, state_t.astype(BF16))

    col = lax.broadcasted_iota(jnp.int32, (sub, c), 1)
    row = lax.broadcasted_iota(jnp.int32, (sub, c), 0)
    blocks = []
    for i in range(c // sub):
        lo = i * sub
        q_i = q[lo:lo + sub, :]
        b_i = b[lo:lo + sub, :]
        if i == 0:
            a = jnp.zeros((sub, c), F32)
        else:
            ref = b[lo - 1:lo, :]
            q_t = (q_i * jnp.exp(b_i - ref)).astype(BF16)
            k_t = (k * jnp.exp(jnp.minimum(ref - b, 0.0))).astype(BF16)
            a = jnp.where(col < lo, _dot_nt(q_t, k_t), 0.0)
        for s in range(lo, lo + sub):
            w = jnp.exp(jnp.minimum(b_i - b[s:s + 1, :], 0.0))
            val = jnp.sum(q_i * k[s:s + 1, :] * w, axis=1, keepdims=True)
            a = jnp.where(col == s, jnp.where(row + lo >= s, val, 0.0), a)
        blocks.append(a)
    attn = jnp.concatenate(blocks, axis=0)
    o = inter + _dot(attn.astype(BF16), v)
    new_state_t = state_t * jnp.exp(b_last) + _dot_tn(v, k_out)
    return o, new_state_t


def _gla_kernel(q_ref, k_ref, v_ref, r_ref, la_ref, hn_ref, o_ref, state_ref):
    @pl.when(pl.program_id(2) == 0)
    def _():
        state_ref[...] = jnp.zeros_like(state_ref)

    c = GLA_CHUNK
    state_t = state_ref[...]
    for ci in range(q_ref.shape[0] // c):
        rows = slice(ci * c, (ci + 1) * c)
        q = q_ref[rows, :].astype(F32) * (GLA_DK ** -0.5)
        k = k_ref[rows, :].astype(F32)
        o, state_t = _gla_chunk(q, k, v_ref[rows, :], la_ref[rows, :], state_t)
        r = r_ref[rows, :].astype(F32)
        o_ref[rows, :] = (_rms(o, hn_ref[...]) * _silu(r)).astype(BF16)
    state_ref[...] = state_t


def _gla_call(q, k, v, r, la, head_norm, tm=256):
    bsz, s, _ = q.shape
    dk, dv = GLA_DK, GLA_DV
    blk = lambda b, h, i: (b, i, h)
    return pl.pallas_call(
        _gla_kernel,
        grid=(bsz, GLA_HEADS, s // tm),
        in_specs=[
            pl.BlockSpec((None, tm, dk), blk),
            pl.BlockSpec((None, tm, dk), blk),
            pl.BlockSpec((None, tm, dv), blk),
            pl.BlockSpec((None, tm, dv), blk),
            pl.BlockSpec((None, tm, dk), blk),
            _const_spec((1, dv)),
        ],
        out_specs=pl.BlockSpec((None, tm, dv), blk),
        out_shape=jax.ShapeDtypeStruct((bsz, s, GLA_HEADS * dv), BF16),
        scratch_shapes=[pltpu.VMEM((dv, dk), F32)],
        compiler_params=_params("arbitrary", "arbitrary", "arbitrary"),
        name="gla_mix",
    )(q, k, v, r, la, head_norm)


def _fox_attn_kernel(q_ref, k_ref, vt_ref, o_ref, m_ref, acc_ref):
    tk = vt_ref.shape[2]
    i = pl.program_id(2)
    m_ref[...] = jnp.full_like(m_ref, NEG_BIG)
    acc_ref[...] = jnp.zeros_like(acc_ref)

    def step(j, masked):
        kj = k_ref[pl.ds(pl.multiple_of(j * tk, tk), tk), :]
        for e in range(2):
            lanes = slice(e * LANES, (e + 1) * LANES)
            st = _dot_nt(kj[:, lanes], q_ref[:, lanes])
            if masked:
                key = lax.broadcasted_iota(jnp.int32, st.shape, 0)
                qry = lax.broadcasted_iota(jnp.int32, st.shape, 1)
                st = jnp.where(key <= qry, st, NEG_BIG)
            m_old = m_ref[e]
            m_new = jnp.maximum(m_old, jnp.max(st, axis=0, keepdims=True))
            alpha = jnp.exp(m_old - m_new)
            pt = jnp.exp(st - m_new).astype(BF16)
            acc_ref[e] = alpha * acc_ref[e] + _dot(vt_ref[j, lanes, :], pt)
            m_ref[e] = m_new

    def body(j, carry):
        step(j, False)
        return carry

    lax.fori_loop(0, i, body, 0)
    step(i, True)

    outs = []
    for e in range(2):
        acc = acc_ref[e]
        outs.append(acc[:HEAD_DIM, :] / acc[HEAD_DIM:HEAD_DIM + 1, :])
    o_ref[...] = jnp.concatenate(outs, axis=0).T.astype(BF16)


def _fox_attn_call(q, k, vt):
    bsz, s, _ = q.shape
    tk = vt.shape[3]
    tq = tk
    pairs = FOX_HEADS // 2
    return pl.pallas_call(
        _fox_attn_kernel,
        grid=(bsz, pairs, s // tq),
        in_specs=[
            pl.BlockSpec((None, tq, 2 * LANES), lambda b, p, i: (b, i, p)),
            pl.BlockSpec((None, s, 2 * LANES), lambda b, p, i: (b, 0, p)),
            pl.BlockSpec((None, s // tk, 2 * LANES, tk), lambda b, p, i: (b, 0, p, 0)),
        ],
        out_specs=pl.BlockSpec((None, tq, LANES), lambda b, p, i: (b, i, p)),
        out_shape=jax.ShapeDtypeStruct((bsz, s, D_MODEL), BF16),
        scratch_shapes=[
            pltpu.VMEM((2, 1, tq), F32),
            pltpu.VMEM((2, LANES, tq), F32),
        ],
        compiler_params=_params("arbitrary", "arbitrary", "arbitrary"),
        name="fox_attn",
    )(q, k, vt)


def _post_kernel(x_ref, o_ref, mod_ref, gain_ref, wo_ref, wgu_ref, wd_ref, fn_ref, out_ref,
                 *, ff_chunk, final):
    x1 = x_ref[...] + mod_ref[2:3, :] * _dot(o_ref[...], wo_ref[...])
    h = _norm_mod(x1, gain_ref[...], mod_ref[3:4, :], mod_ref[4:5, :]).astype(BF16)
    acc = jnp.zeros(x1.shape, F32)
    for c0 in range(0, D_FF, ff_chunk):
        g = _dot(h, wgu_ref[:, c0:c0 + ff_chunk])
        u = _dot(h, wgu_ref[:, D_FF + c0:D_FF + c0 + ff_chunk])
        acc = acc + _dot((_silu(g) * u).astype(BF16), wd_ref[c0:c0 + ff_chunk, :])
    x2 = x1 + mod_ref[5:6, :] * acc
    if final:
        x2 = _rms(x2, fn_ref[...])
    out_ref[...] = x2


def _post_call(x, o, mod, gain, wo, wgu, wd, final_norm, final, tm=512, ff_chunk=256):
    bsz, s, d = x.shape
    row = lambda b, i: (b, i, 0)
    return pl.pallas_call(
        functools.partial(_post_kernel, ff_chunk=ff_chunk, final=final),
        grid=(bsz, s // tm),
        in_specs=[
            pl.BlockSpec((None, tm, d), row),
            pl.BlockSpec((None, tm, d), row),
            pl.BlockSpec((None, 6, d), lambda b, i: (b, 0, 0)),
            _const_spec((1, d)),
            _const_spec(wo.shape),
            _const_spec(wgu.shape),
            _const_spec(wd.shape),
            _const_spec((1, d)),
        ],
        out_specs=pl.BlockSpec((None, tm, d), row),
        out_shape=jax.ShapeDtypeStruct((bsz, s, d), F32),
        compiler_params=_params("arbitrary", "arbitrary"),
        name="post_ffn",
    )(x, o, mod, gain, wo, wgu, wd, final_norm)


def _rope_tables(s):
    half = HEAD_DIM // 2
    inv = 1.0 / (ROPE_THETA ** (jnp.arange(0, HEAD_DIM, 2, dtype=F32) / HEAD_DIM))
    ang = jnp.arange(s, dtype=F32)[:, None] * inv[None, :]
    cos, sin = jnp.cos(ang), jnp.sin(ang)
    reps = LANES // HEAD_DIM
    cos_t = jnp.tile(jnp.concatenate([cos, cos], axis=1), (1, reps))
    sin_t = jnp.tile(jnp.concatenate([-sin, sin], axis=1), (1, reps))
    assert half * 2 == HEAD_DIM
    return cos_t, sin_t


def _dup_heads(w, heads):
    w3 = w.reshape(w.shape[0], heads, HEAD_DIM)
    return jnp.concatenate([w3, w3], axis=2).reshape(w.shape[0], heads * LANES)


def _pad_heads(w, heads):
    w3 = w.reshape(w.shape[0], heads, HEAD_DIM)
    return jnp.concatenate([w3, jnp.zeros_like(w3)], axis=2).reshape(w.shape[0], heads * LANES)


def _fox_placement():
    h = FOX_HEADS
    pq = np.zeros((4 * h, h * LANES), np.float32)
    pk = np.zeros((4 * h, h * LANES), np.float32)
    for head in range(h):
        base = head * LANES + HEAD_DIM
        for part in range(3):
            pq[part * h + head, base + part] = 1.0
            pk[3 * h + head, base + part] = 1.0
            pq[3 * h + head, base + 3 + part] = 1.0
            pk[part * h + head, base + 3 + part] = -1.0
    return jnp.asarray(pq, BF16), jnp.asarray(pk, BF16)


def kernel(x, c, ada_w, ada_b, norm_gain, ffn_w_gu, ffn_w_down, swa_w_in, swa_sinks, swa_w_o,
           gla_w_in, gla_w_gate_up, gla_b_gate, gla_head_norm, gla_w_o, fox_w_in, fox_b_f, fox_w_o,
           final_norm):
    bsz, s, d = x.shape
    depth = ada_w.shape[0]
    mod_all = _ada_call(c, ada_w, ada_b).reshape(depth, bsz, 6, d)
    cos_t, sin_t = _rope_tables(s)
    pq, pk = _fox_placement()
    fn = final_norm.reshape(1, d)

    for i in range(depth):
        kind, j = i % N_MIXERS, i // N_MIXERS
        mod = mod_all[i]
        gain1 = norm_gain[i, 0].reshape(1, d)
        gain2 = norm_gain[i, 1].reshape(1, d)
        if kind == 0:
            w = swa_w_in[j]
            nq, nkv = SWA_Q_HEADS * HEAD_DIM, SWA_KV_HEADS * HEAD_DIM
            w_all = jnp.concatenate([w[:, :nq],
                                     _dup_heads(w[:, nq:nq + nkv], SWA_KV_HEADS),
                                     _dup_heads(w[:, nq + nkv:], SWA_KV_HEADS)], axis=1).astype(BF16)
            q, k, v = _swa_proj_call(x, mod, gain1, w_all, cos_t, sin_t)
            o = _swa_attn_call(q, k, v, swa_sinks[j])
            wo = swa_w_o[j]
        elif kind == 1:
            w = gla_w_in[j]
            n_main = 2 * GLA_HEADS * GLA_DK + 2 * GLA_HEADS * GLA_DV
            q, k, v, r, la = _gla_proj_call(
                x, mod, gain1, w[:, :n_main].astype(BF16), w[:, n_main:].astype(BF16),
                gla_w_gate_up[j].astype(BF16), gla_b_gate[j].reshape(1, -1))
            o = _gla_call(q, k, v, r, la, gla_head_norm[j].reshape(1, -1))
            wo = gla_w_o[j]
        else:
            w = fox_w_in[j]
            q, k, v = _fox_proj_call(
                x, mod, gain1,
                _pad_heads(w[:, :d], FOX_HEADS).astype(BF16),
                _pad_heads(w[:, d:2 * d], FOX_HEADS).astype(BF16),
                _pad_heads(w[:, 2 * d:3 * d], FOX_HEADS).T.astype(BF16), w[:, 3 * d:].astype(BF16),
                fox_b_f[j].reshape(1, -1), pq, pk)
            o = _fox_attn_call(q, k, v)
            wo = fox_w_o[j]
        x = _post_call(x, o, mod, gain2, wo.astype(BF16), ffn_w_gu[i].astype(BF16),
                       ffn_w_down[i].astype(BF16), fn, final=(i == depth - 1))
    return x
```

```python
import functools

import numpy as np
import jax
import jax.numpy as jnp
from jax import lax
from jax.experimental import pallas as pl
from jax.experimental.pallas import tpu as pltpu

D_MODEL = 1024
HEAD_DIM = 64
RMS_EPS = 1e-6
SWA_Q_HEADS = 16
SWA_KV_HEADS = 4
SWA_WINDOW = 128
ROPE_THETA = 150000.0
GLA_HEADS = 4
GLA_DK = 128
GLA_DV = 256
GLA_RANK = 16
GLA_TAU = 16.0
GLA_CHUNK = 64
GLA_SUB = 16
FOX_HEADS = 16
D_FF = 2816
N_MIXERS = 3

LANES = 128
NEG_BIG = -1e30
LOG2E = 1.4426950408889634
VMEM_LIMIT = 56 * 1024 * 1024

BF16 = jnp.bfloat16
F32 = jnp.float32


def _dot(a, b):
    return jnp.dot(a, b, preferred_element_type=F32)


def _dot_nt(a, b):
    return lax.dot_general(a, b, (((1,), (1,)), ((), ())), preferred_element_type=F32)


def _dot_tn(a, b):
    return lax.dot_general(a, b, (((0,), (0,)), ((), ())), preferred_element_type=F32)


def _split3(x):
    hi = x.astype(BF16)
    r1 = x - hi.astype(F32)
    mid = r1.astype(BF16)
    lo = (r1 - mid.astype(F32)).astype(BF16)
    return hi, mid, lo


def _cumsum_rows(x):
    n = x.shape[0]
    row = lax.broadcasted_iota(jnp.int32, (n, n), 0)
    col = lax.broadcasted_iota(jnp.int32, (n, n), 1)
    tril = jnp.where(row >= col, 1.0, 0.0).astype(BF16)
    hi, mid, lo = _split3(x)
    return _dot(tril, hi) + _dot(tril, mid) + _dot(tril, lo)


def _log_sigmoid(x):
    return jnp.minimum(x, 0.0) - jnp.log(1.0 + jnp.exp(-jnp.abs(x)))


def _silu(x):
    return x * (1.0 / (1.0 + jnp.exp(-x)))


def _rms(x, gain):
    ms = jnp.mean(x * x, axis=-1, keepdims=True)
    return x * lax.rsqrt(ms + RMS_EPS) * gain


def _norm_mod(x, gain, shift, scale):
    return _rms(x, gain) * (1.0 + scale) + shift


def _params(*sem):
    return pltpu.CompilerParams(dimension_semantics=sem, vmem_limit_bytes=VMEM_LIMIT)


def _const_spec(shape):
    nd = len(shape)
    return pl.BlockSpec(shape, lambda *_: (0,) * nd, pipeline_mode=pl.Buffered(1))


def _ada_kernel(ct_ref, w_ref, b_ref, out_ref):
    ca = _silu(ct_ref[...])
    w = w_ref[...]
    for b in range(ct_ref.shape[1]):
        col = ca[:, b:b + 1]
        out_ref[b:b + 1, :] = jnp.sum(col * w, axis=0, keepdims=True) + b_ref[...]


def _ada_call(c, ada_w, ada_b):
    depth, d, n = ada_w.shape
    bsz = c.shape[0]
    tn = 768
    return pl.pallas_call(
        _ada_kernel,
        grid=(depth, n // tn),
        in_specs=[
            pl.BlockSpec((d, bsz), lambda l, j: (0, 0)),
            pl.BlockSpec((None, d, tn), lambda l, j: (l, 0, j)),
            pl.BlockSpec((None, 1, tn), lambda l, j: (l, 0, j)),
        ],
        out_specs=pl.BlockSpec((None, bsz, tn), lambda l, j: (l, 0, j)),
        out_shape=jax.ShapeDtypeStruct((depth, bsz, n), F32),
        compiler_params=_params("arbitrary", "arbitrary"),
        name="ada_mod",
    )(c.T, ada_w, ada_b.reshape(depth, 1, n))


def _rope(x, cos, sin_signed):
    width = x.shape[1]
    reps = width // cos.shape[1]
    c = jnp.tile(cos, (1, reps))
    s = jnp.tile(sin_signed, (1, reps))
    lane = lax.broadcasted_iota(jnp.int32, x.shape, 1)
    first_half = (lane % HEAD_DIM) < (HEAD_DIM // 2)
    rot = jnp.where(first_half,
                    pltpu.roll(x, width - HEAD_DIM // 2, 1),
                    pltpu.roll(x, HEAD_DIM // 2, 1))
    return x * c + rot * s


def _ones_row_64(vt):
    ones_row = lax.broadcasted_iota(jnp.int32, vt.shape, 0) % LANES == HEAD_DIM
    return jnp.where(ones_row, 1.0, vt)


def _swa_proj_kernel(x_ref, mod_ref, gain_ref, w_ref, wvt_ref, cos_ref, sin_ref, q_ref, k_ref, vt_ref):
    h = _norm_mod(x_ref[...], gain_ref[...], mod_ref[0:1, :], mod_ref[1:2, :]).astype(BF16)
    cos, sin = cos_ref[...], sin_ref[...]
    nq = q_ref.shape[1]
    q = _dot(h, w_ref[:, :nq])
    q_ref[...] = (_rope(q, cos, sin) * (HEAD_DIM ** -0.5)).astype(BF16)
    k = _dot(h, w_ref[:, nq:])
    k_ref[...] = _rope(k, cos, sin).astype(BF16)
    vt_ref[...] = _ones_row_64(_dot_nt(wvt_ref[...], h)).astype(BF16)


def _swa_proj_call(x, mod, gain, w, wvt, cos, sin, tm=512):
    bsz, s, d = x.shape
    nq, nkv = D_MODEL, SWA_KV_HEADS * LANES
    row = lambda b, i: (b, i, 0)
    return pl.pallas_call(
        _swa_proj_kernel,
        grid=(bsz, s // tm),
        in_specs=[
            pl.BlockSpec((None, tm, d), row),
            pl.BlockSpec((None, 6, d), lambda b, i: (b, 0, 0)),
            _const_spec((1, d)),
            _const_spec(w.shape),
            _const_spec(wvt.shape),
            pl.BlockSpec((tm, LANES), lambda b, i: (i, 0)),
            pl.BlockSpec((tm, LANES), lambda b, i: (i, 0)),
        ],
        out_specs=[
            pl.BlockSpec((None, tm, nq), row),
            pl.BlockSpec((None, tm, nkv), row),
            pl.BlockSpec((None, nkv, tm), lambda b, i: (b, 0, i)),
        ],
        out_shape=[
            jax.ShapeDtypeStruct((bsz, s, nq), BF16),
            jax.ShapeDtypeStruct((bsz, s, nkv), BF16),
            jax.ShapeDtypeStruct((bsz, nkv, s), BF16),
        ],
        compiler_params=_params("arbitrary", "arbitrary"),
        name="swa_proj",
    )(x, mod, gain, w, wvt, cos, sin)


def _gla_proj_kernel(x_ref, mod_ref, gain_ref, w_ref, wa_ref, wg_ref, bg_ref,
                     q_ref, k_ref, v_ref, r_ref, la_ref):
    h = _norm_mod(x_ref[...], gain_ref[...], mod_ref[0:1, :], mod_ref[1:2, :]).astype(BF16)
    nk = q_ref.shape[1]
    nv = v_ref.shape[1]
    q_ref[...] = _dot(h, w_ref[:, :nk]).astype(BF16)
    k_ref[...] = _dot(h, w_ref[:, nk:2 * nk]).astype(BF16)
    v_ref[...] = _dot(h, w_ref[:, 2 * nk:2 * nk + nv]).astype(BF16)
    r_ref[...] = _dot(h, w_ref[:, 2 * nk + nv:]).astype(BF16)
    a_low = _dot(h, wa_ref[...]).astype(BF16)
    z = _dot(a_low, wg_ref[...]) + bg_ref[...]
    la_ref[...] = _log_sigmoid(z) * (1.0 / GLA_TAU)


def _gla_proj_call(x, mod, gain, w, wa, wg, bg, tm=512):
    bsz, s, d = x.shape
    nk, nv = GLA_HEADS * GLA_DK, GLA_HEADS * GLA_DV
    row = lambda b, i: (b, i, 0)
    return pl.pallas_call(
        _gla_proj_kernel,
        grid=(bsz, s // tm),
        in_specs=[
            pl.BlockSpec((None, tm, d), row),
            pl.BlockSpec((None, 6, d), lambda b, i: (b, 0, 0)),
            _const_spec((1, d)),
            _const_spec(w.shape),
            _const_spec(wa.shape),
            _const_spec(wg.shape),
            _const_spec(bg.shape),
        ],
        out_specs=[
            pl.BlockSpec((None, tm, nk), row),
            pl.BlockSpec((None, tm, nk), row),
            pl.BlockSpec((None, tm, nv), row),
            pl.BlockSpec((None, tm, nv), row),
            pl.BlockSpec((None, tm, nk), row),
        ],
        out_shape=[
            jax.ShapeDtypeStruct((bsz, s, nk), BF16),
            jax.ShapeDtypeStruct((bsz, s, nk), BF16),
            jax.ShapeDtypeStruct((bsz, s, nv), BF16),
            jax.ShapeDtypeStruct((bsz, s, nv), BF16),
            jax.ShapeDtypeStruct((bsz, s, nk), F32),
        ],
        compiler_params=_params("arbitrary", "arbitrary"),
        name="gla_proj",
    )(x, mod, gain, w, wa, wg, bg)


def _fox_proj_kernel(x_ref, mod_ref, gain_ref, wq_ref, wk_ref, wvt_ref, wf_ref, bf_ref,
                     pq_ref, pk_ref, q_ref, k_ref, vt_ref, carry_ref):
    @pl.when(pl.program_id(1) == 0)
    def _():
        carry_ref[...] = jnp.zeros_like(carry_ref)

    h = _norm_mod(x_ref[...], gain_ref[...], mod_ref[0:1, :], mod_ref[1:2, :]).astype(BF16)
    log_f = _log_sigmoid(_dot(h, wf_ref[...]) + bf_ref[...])
    lc = _cumsum_rows(log_f) + carry_ref[...]
    carry_ref[...] = lc[lc.shape[0] - 1:, :]
    hi, mid, lo = _split3(lc * LOG2E)
    aug = jnp.concatenate([hi, mid, lo, jnp.ones_like(hi)], axis=1)
    q_ref[...] = (_dot(h, wq_ref[...]) * (HEAD_DIM ** -0.5 * LOG2E)
                  + _dot(aug, pq_ref[...])).astype(BF16)
    k_ref[...] = (_dot(h, wk_ref[...]) + _dot(aug, pk_ref[...])).astype(BF16)
    vt_ref[...] = _ones_row_64(_dot_nt(wvt_ref[...], h)).astype(BF16)


def _fox_proj_call(x, mod, gain, wq, wk, wvt, wf, bf, pq, pk, tm=512):
    bsz, s, d = x.shape
    nqk = FOX_HEADS * LANES
    row = lambda b, i: (b, i, 0)
    return pl.pallas_call(
        _fox_proj_kernel,
        grid=(bsz, s // tm),
        in_specs=[
            pl.BlockSpec((None, tm, d), row),
            pl.BlockSpec((None, 6, d), lambda b, i: (b, 0, 0)),
            _const_spec((1, d)),
            _const_spec(wq.shape),
            _const_spec(wk.shape),
            _const_spec(wvt.shape),
            _const_spec(wf.shape),
            _const_spec(bf.shape),
            _const_spec(pq.shape),
            _const_spec(pk.shape),
        ],
        out_specs=[
            pl.BlockSpec((None, tm, nqk), row),
            pl.BlockSpec((None, tm, nqk), row),
            pl.BlockSpec((None, None, nqk, tm), lambda b, i: (b, i, 0, 0)),
        ],
        out_shape=[
            jax.ShapeDtypeStruct((bsz, s, nqk), BF16),
            jax.ShapeDtypeStruct((bsz, s, nqk), BF16),
            jax.ShapeDtypeStruct((bsz, s // tm, nqk, tm), BF16),
        ],
        scratch_shapes=[pltpu.VMEM((1, FOX_HEADS), F32)],
        compiler_params=_params("arbitrary", "arbitrary"),
        name="fox_proj",
    )(x, mod, gain, wq, wk, wvt, wf, bf, pq, pk)


def _swa_band_bias():
    w, group = SWA_WINDOW, SWA_Q_HEADS // SWA_KV_HEADS
    key = np.arange(2 * w)[:, None]
    qry = np.arange(group * w)[None, :] % w
    dist = (w + qry) - key
    band = (dist >= 0) & (dist < w)
    allowed = np.stack([band & (key >= w), band])
    return jnp.asarray(np.where(allowed, 0.0, NEG_BIG), F32)


def _swa_attn_kernel(sink_ref, bias_ref, q_ref, kc_ref, kp_ref, vtc_ref, vtp_ref, o_ref):
    w = SWA_WINDOW
    group = SWA_Q_HEADS // SWA_KV_HEADS
    nq = group * w
    bias = bias_ref[jnp.minimum(pl.program_id(1), 1)]
    lane = lax.broadcasted_iota(jnp.int32, (w, LANES), 1)
    low = lane < HEAD_DIM
    for g in range(SWA_KV_HEADS):
        slabs = []
        for hh in range(group):
            head = g * group + hh
            qs = q_ref[:, (head // 2) * LANES:(head // 2 + 1) * LANES]
            keep = low if head % 2 == 0 else jnp.logical_not(low)
            slabs.append(jnp.where(keep, qs, jnp.zeros_like(qs)))
        q_stack = jnp.concatenate(slabs, axis=0)
        k_both = jnp.concatenate([kp_ref[:, g * LANES:(g + 1) * LANES],
                                  kc_ref[:, g * LANES:(g + 1) * LANES]], axis=0)
        st = _dot_nt(k_both, q_stack) + bias
        sink = sink_ref[:, g * nq:(g + 1) * nq]
        m = jnp.maximum(jnp.max(st, axis=0, keepdims=True), sink)
        pt = jnp.exp(st - m).astype(BF16)
        vt_both = jnp.concatenate([vtp_ref[g * LANES:(g + 1) * LANES, :],
                                   vtc_ref[g * LANES:(g + 1) * LANES, :]], axis=1)
        acc = _dot(vt_both, pt)
        den = acc[HEAD_DIM:HEAD_DIM + 1, :] + jnp.exp(sink - m)
        ot = acc[:HEAD_DIM, :] / den
        o_t = jnp.concatenate([ot[:, hh * w:(hh + 1) * w] for hh in range(group)], axis=0)
        o_ref[:, g * group * HEAD_DIM:(g + 1) * group * HEAD_DIM] = o_t.T.astype(BF16)


def _swa_attn_call(q, k, vt, sinks):
    bsz, s, d = q.shape
    w = SWA_WINDOW
    nkv = k.shape[2]
    sink_row = jnp.repeat(sinks, w).reshape(1, -1)
    bias = _swa_band_bias()
    cur = lambda b, i: (b, i, 0)
    prev = lambda b, i: (b, jnp.maximum(i - 1, 0), 0)
    return pl.pallas_call(
        _swa_attn_kernel,
        grid=(bsz, s // w),
        in_specs=[
            _const_spec(sink_row.shape),
            _const_spec(bias.shape),
            pl.BlockSpec((None, w, d), cur),
            pl.BlockSpec((None, w, nkv), cur),
            pl.BlockSpec((None, w, nkv), prev),
            pl.BlockSpec((None, nkv, w), lambda b, i: (b, 0, i)),
            pl.BlockSpec((None, nkv, w), lambda b, i: (b, 0, jnp.maximum(i - 1, 0))),
        ],
        out_specs=pl.BlockSpec((None, w, d), cur),
        out_shape=jax.ShapeDtypeStruct((bsz, s, d), BF16),
        compiler_params=_params("arbitrary", "arbitrary"),
        name="swa_attn",
    )(sink_row, bias, q, k, k, vt, vt)


def _gla_chunk(q, k, v, la, state_t):
    c, sub = GLA_CHUNK, GLA_SUB
    b = _cumsum_rows(la)
    b_last = b[c - 1:c, :]
    q_in = (q * jnp.exp(b)).astype(BF16)
    k_out = (k * jnp.exp(b_last - b)).astype(BF16)
    inter = _dot_nt(q_in, state_t.astype(BF16))

    col = lax.broadcasted_iota(jnp.int32, (sub, c), 1)
    row = lax.broadcasted_iota(jnp.int32, (sub, c), 0)
    blocks = []
    for i in range(c // sub):
        lo = i * sub
        q_i = q[lo:lo + sub, :]
        b_i = b[lo:lo + sub, :]
        if i == 0:
            a = jnp.zeros((sub, c), F32)
        else:
            ref = b[lo - 1:lo, :]
            q_t = (q_i * jnp.exp(b_i - ref)).astype(BF16)
            k_t = (k * jnp.exp(jnp.minimum(ref - b, 0.0))).astype(BF16)
            a = jnp.where(col < lo, _dot_nt(q_t, k_t), 0.0)
        for s in range(lo, lo + sub):
            w = jnp.exp(jnp.minimum(b_i - b[s:s + 1, :], 0.0))
            val = jnp.sum(q_i * k[s:s + 1, :] * w, axis=1, keepdims=True)
            a = jnp.where(col == s, jnp.where(row + lo >= s, val, 0.0), a)
        blocks.append(a)
    attn = jnp.concatenate(blocks, axis=0)
    o = inter + _dot(attn.astype(BF16), v)
    new_state_t = state_t * jnp.exp(b_last) + _dot_tn(v, k_out)
    return o, new_state_t


def _gla_kernel(q_ref, k_ref, v_ref, r_ref, la_ref, hn_ref, o_ref, state_ref):
    @pl.when(pl.program_id(2) == 0)
    def _():
        state_ref[...] = jnp.zeros_like(state_ref)

    c = GLA_CHUNK
    state_t = state_ref[...]
    for ci in range(q_ref.shape[0] // c):
        rows = slice(ci * c, (ci + 1) * c)
        q = q_ref[rows, :].astype(F32) * (GLA_DK ** -0.5)
        k = k_ref[rows, :].astype(F32)
        o, state_t = _gla_chunk(q, k, v_ref[rows, :], la_ref[rows, :], state_t)
        r = r_ref[rows, :].astype(F32)
        o_ref[rows, :] = (_rms(o, hn_ref[...]) * _silu(r)).astype(BF16)
    state_ref[...] = state_t


def _gla_call(q, k, v, r, la, head_norm, tm=256):
    bsz, s, _ = q.shape
    dk, dv = GLA_DK, GLA_DV
    blk = lambda b, h, i: (b, i, h)
    return pl.pallas_call(
        _gla_kernel,
        grid=(bsz, GLA_HEADS, s // tm),
        in_specs=[
            pl.BlockSpec((None, tm, dk), blk),
            pl.BlockSpec((None, tm, dk), blk),
            pl.BlockSpec((None, tm, dv), blk),
            pl.BlockSpec((None, tm, dv), blk),
            pl.BlockSpec((None, tm, dk), blk),
            _const_spec((1, dv)),
        ],
        out_specs=pl.BlockSpec((None, tm, dv), blk),
        out_shape=jax.ShapeDtypeStruct((bsz, s, GLA_HEADS * dv), BF16),
        scratch_shapes=[pltpu.VMEM((dv, dk), F32)],
        compiler_params=_params("arbitrary", "arbitrary", "arbitrary"),
        name="gla_mix",
    )(q, k, v, r, la, head_norm)


def _fox_attn_kernel(q_ref, k_ref, vt_ref, o_ref, m_ref, acc_ref, sa_ref, sb_ref):
    tk = vt_ref.shape[2]
    i = pl.program_id(2)
    m_ref[...] = jnp.full_like(m_ref, NEG_BIG)
    acc_ref[...] = jnp.zeros_like(acc_ref)

    def scores(j, buf):
        kj = k_ref[pl.ds(pl.multiple_of(j * tk, tk), tk), :]
        for e in range(2):
            lanes = slice(e * LANES, (e + 1) * LANES)
            buf[e] = _dot_nt(kj[:, lanes], q_ref[:, lanes])

    def accumulate(j, buf, masked):
        for e in range(2):
            lanes = slice(e * LANES, (e + 1) * LANES)
            st = buf[e]
            if masked:
                key = lax.broadcasted_iota(jnp.int32, st.shape, 0)
                qry = lax.broadcasted_iota(jnp.int32, st.shape, 1)
                st = jnp.where(key <= qry, st, NEG_BIG)
            m_old = m_ref[e]
            m_new = jnp.maximum(m_old, jnp.max(st, axis=0, keepdims=True))
            alpha = jnp.exp2(m_old - m_new)
            pt = jnp.exp2(st - m_new).astype(BF16)
            acc_ref[e] = alpha * acc_ref[e] + _dot(vt_ref[j, lanes, :], pt)
            m_ref[e] = m_new

    scores(0, sa_ref)

    def body(t, carry):
        scores(2 * t + 1, sb_ref)
        accumulate(2 * t, sa_ref, False)
        scores(2 * t + 2, sa_ref)
        accumulate(2 * t + 1, sb_ref, False)
        return carry

    lax.fori_loop(0, i // 2, body, 0)

    @pl.when(i % 2 == 0)
    def _():
        accumulate(i, sa_ref, True)

    @pl.when(i % 2 == 1)
    def _():
        scores(i, sb_ref)
        accumulate(i - 1, sa_ref, False)
        accumulate(i, sb_ref, True)

    outs = []
    for e in range(2):
        acc = acc_ref[e]
        outs.append(acc[:HEAD_DIM, :] / acc[HEAD_DIM:HEAD_DIM + 1, :])
    o_ref[...] = jnp.concatenate(outs, axis=0).T.astype(BF16)


def _fox_attn_call(q, k, vt):
    bsz, s, _ = q.shape
    tk = vt.shape[3]
    tq = tk
    pairs = FOX_HEADS // 2
    return pl.pallas_call(
        _fox_attn_kernel,
        grid=(bsz, pairs, s // tq),
        in_specs=[
            pl.BlockSpec((None, tq, 2 * LANES), lambda b, p, i: (b, i, p)),
            pl.BlockSpec((None, s, 2 * LANES), lambda b, p, i: (b, 0, p)),
            pl.BlockSpec((None, s // tk, 2 * LANES, tk), lambda b, p, i: (b, 0, p, 0)),
        ],
        out_specs=pl.BlockSpec((None, tq, LANES), lambda b, p, i: (b, i, p)),
        out_shape=jax.ShapeDtypeStruct((bsz, s, D_MODEL), BF16),
        scratch_shapes=[
            pltpu.VMEM((2, 1, tq), F32),
            pltpu.VMEM((2, LANES, tq), F32),
            pltpu.VMEM((2, tk, tq), F32),
            pltpu.VMEM((2, tk, tq), F32),
        ],
        compiler_params=_params("arbitrary", "arbitrary", "arbitrary"),
        name="fox_attn",
    )(q, k, vt)


def _post_kernel(x_ref, o_ref, mod_ref, gain_ref, wo_ref, wgu_ref, wd_ref, fn_ref, out_ref,
                 *, ff_chunk, final):
    x1 = x_ref[...] + mod_ref[2:3, :] * _dot(o_ref[...], wo_ref[...])
    h = _norm_mod(x1, gain_ref[...], mod_ref[3:4, :], mod_ref[4:5, :]).astype(BF16)
    acc = jnp.zeros(x1.shape, F32)
    for c0 in range(0, D_FF, ff_chunk):
        g = _dot(h, wgu_ref[:, c0:c0 + ff_chunk])
        u = _dot(h, wgu_ref[:, D_FF + c0:D_FF + c0 + ff_chunk])
        acc = acc + _dot((_silu(g) * u).astype(BF16), wd_ref[c0:c0 + ff_chunk, :])
    x2 = x1 + mod_ref[5:6, :] * acc
    if final:
        x2 = _rms(x2, fn_ref[...])
    out_ref[...] = x2


def _post_call(x, o, mod, gain, wo, wgu, wd, final_norm, final, tm=512, ff_chunk=256):
    bsz, s, d = x.shape
    row = lambda b, i: (b, i, 0)
    return pl.pallas_call(
        functools.partial(_post_kernel, ff_chunk=ff_chunk, final=final),
        grid=(bsz, s // tm),
        in_specs=[
            pl.BlockSpec((None, tm, d), row),
            pl.BlockSpec((None, tm, d), row),
            pl.BlockSpec((None, 6, d), lambda b, i: (b, 0, 0)),
            _const_spec((1, d)),
            _const_spec(wo.shape),
            _const_spec(wgu.shape),
            _const_spec(wd.shape),
            _const_spec((1, d)),
        ],
        out_specs=pl.BlockSpec((None, tm, d), row),
        out_shape=jax.ShapeDtypeStruct((bsz, s, d), F32),
        compiler_params=_params("arbitrary", "arbitrary"),
        name="post_ffn",
    )(x, o, mod, gain, wo, wgu, wd, final_norm)


def _rope_tables(s):
    half = HEAD_DIM // 2
    inv = 1.0 / (ROPE_THETA ** (jnp.arange(0, HEAD_DIM, 2, dtype=F32) / HEAD_DIM))
    ang = jnp.arange(s, dtype=F32)[:, None] * inv[None, :]
    cos, sin = jnp.cos(ang), jnp.sin(ang)
    reps = LANES // HEAD_DIM
    cos_t = jnp.tile(jnp.concatenate([cos, cos], axis=1), (1, reps))
    sin_t = jnp.tile(jnp.concatenate([-sin, sin], axis=1), (1, reps))
    assert half * 2 == HEAD_DIM
    return cos_t, sin_t


def _dup_heads(w, heads):
    w3 = w.reshape(w.shape[0], heads, HEAD_DIM)
    return jnp.concatenate([w3, w3], axis=2).reshape(w.shape[0], heads * LANES)


def _pad_heads(w, heads):
    w3 = w.reshape(w.shape[0], heads, HEAD_DIM)
    return jnp.concatenate([w3, jnp.zeros_like(w3)], axis=2).reshape(w.shape[0], heads * LANES)


def _fox_placement():
    h = FOX_HEADS
    pq = np.zeros((4 * h, h * LANES), np.float32)
    pk = np.zeros((4 * h, h * LANES), np.float32)
    for head in range(h):
        base = head * LANES + HEAD_DIM
        for part in range(3):
            pq[part * h + head, base + part] = 1.0
            pk[3 * h + head, base + part] = 1.0
            pq[3 * h + head, base + 3 + part] = 1.0
            pk[part * h + head, base + 3 + part] = -1.0
    return jnp.asarray(pq, BF16), jnp.asarray(pk, BF16)


def kernel(x, c, ada_w, ada_b, norm_gain, ffn_w_gu, ffn_w_down, swa_w_in, swa_sinks, swa_w_o,
           gla_w_in, gla_w_gate_up, gla_b_gate, gla_head_norm, gla_w_o, fox_w_in, fox_b_f, fox_w_o,
           final_norm):
    bsz, s, d = x.shape
    depth = ada_w.shape[0]
    mod_all = _ada_call(c, ada_w, ada_b).reshape(depth, bsz, 6, d)
    cos_t, sin_t = _rope_tables(s)
    pq, pk = _fox_placement()
    fn = final_norm.reshape(1, d)

    for i in range(depth):
        kind, j = i % N_MIXERS, i // N_MIXERS
        mod = mod_all[i]
        gain1 = norm_gain[i, 0].reshape(1, d)
        gain2 = norm_gain[i, 1].reshape(1, d)
        if kind == 0:
            w = swa_w_in[j]
            nq, nkv = SWA_Q_HEADS * HEAD_DIM, SWA_KV_HEADS * HEAD_DIM
            w_all = jnp.concatenate([w[:, :nq], _dup_heads(w[:, nq:nq + nkv], SWA_KV_HEADS)],
                                    axis=1).astype(BF16)
            wvt = _pad_heads(w[:, nq + nkv:], SWA_KV_HEADS).T.astype(BF16)
            q, k, v = _swa_proj_call(x, mod, gain1, w_all, wvt, cos_t, sin_t)
            o = _swa_attn_call(q, k, v, swa_sinks[j])
            wo = swa_w_o[j]
        elif kind == 1:
            w = gla_w_in[j]
            n_main = 2 * GLA_HEADS * GLA_DK + 2 * GLA_HEADS * GLA_DV
            q, k, v, r, la = _gla_proj_call(
                x, mod, gain1, w[:, :n_main].astype(BF16), w[:, n_main:].astype(BF16),
                gla_w_gate_up[j].astype(BF16), gla_b_gate[j].reshape(1, -1))
            o = _gla_call(q, k, v, r, la, gla_head_norm[j].reshape(1, -1))
            wo = gla_w_o[j]
        else:
            w = fox_w_in[j]
            q, k, v = _fox_proj_call(
                x, mod, gain1,
                _pad_heads(w[:, :d], FOX_HEADS).astype(BF16),
                _pad_heads(w[:, d:2 * d], FOX_HEADS).astype(BF16),
                _pad_heads(w[:, 2 * d:3 * d], FOX_HEADS).T.astype(BF16), w[:, 3 * d:].astype(BF16),
                fox_b_f[j].reshape(1, -1), pq, pk)
            o = _fox_attn_call(q, k, v)
            wo = fox_w_o[j]
        x = _post_call(x, o, mod, gain2, wo.astype(BF16), ffn_w_gu[i].astype(BF16),
                       ffn_w_down[i].astype(BF16), fn, final=(i == depth - 1))
    return x
```

```python
import functools

import numpy as np
import jax
import jax.numpy as jnp
from jax import lax
from jax.experimental import pallas as pl
from jax.experimental.pallas import tpu as pltpu

D_MODEL = 1024
HEAD_DIM = 64
RMS_EPS = 1e-6
SWA_Q_HEADS = 16
SWA_KV_HEADS = 4
SWA_WINDOW = 128
ROPE_THETA = 150000.0
GLA_HEADS = 4
GLA_DK = 128
GLA_DV = 256
GLA_RANK = 16
GLA_TAU = 16.0
GLA_CHUNK = 64
GLA_SUB = 8
FOX_HEADS = 16
D_FF = 2816
N_MIXERS = 3

LANES = 128
BF16_ROWS = 16
NEG_BIG = -1e30
LOG2E = 1.4426950408889634
VMEM_LIMIT = 56 * 1024 * 1024

BF16 = jnp.bfloat16
F32 = jnp.float32


def _dot(a, b):
    return jnp.dot(a, b, preferred_element_type=F32)


def _dot_nt(a, b):
    return lax.dot_general(a, b, (((1,), (1,)), ((), ())), preferred_element_type=F32)


def _dot_tn(a, b):
    return lax.dot_general(a, b, (((0,), (0,)), ((), ())), preferred_element_type=F32)


def _split3(x):
    hi = x.astype(BF16)
    r1 = x - hi.astype(F32)
    mid = r1.astype(BF16)
    lo = (r1 - mid.astype(F32)).astype(BF16)
    return hi, mid, lo


def _cumsum_rows(x):
    n = x.shape[0]
    row = lax.broadcasted_iota(jnp.int32, (n, n), 0)
    col = lax.broadcasted_iota(jnp.int32, (n, n), 1)
    tril = jnp.where(row >= col, 1.0, 0.0).astype(BF16)
    hi, mid, lo = _split3(x)
    return _dot(tril, hi) + _dot(tril, mid) + _dot(tril, lo)


def _log_sigmoid(x):
    return jnp.minimum(x, 0.0) - jnp.log(1.0 + jnp.exp(-jnp.abs(x)))


def _silu(x):
    return x * (1.0 / (1.0 + jnp.exp(-x)))


def _rms(x, gain):
    ms = jnp.mean(x * x, axis=-1, keepdims=True)
    return x * lax.rsqrt(ms + RMS_EPS) * gain


def _norm_mod(x, gain, shift, scale):
    return _rms(x, gain) * (1.0 + scale) + shift


def _params(*sem):
    return pltpu.CompilerParams(dimension_semantics=sem, vmem_limit_bytes=VMEM_LIMIT)


def _const_spec(shape):
    nd = len(shape)
    return pl.BlockSpec(shape, lambda *_: (0,) * nd, pipeline_mode=pl.Buffered(1))


def _ada_kernel(ct_ref, w_ref, b_ref, out_ref):
    ca = _silu(ct_ref[...])
    w = w_ref[...]
    for b in range(ct_ref.shape[1]):
        col = ca[:, b:b + 1]
        out_ref[b:b + 1, :] = jnp.sum(col * w, axis=0, keepdims=True) + b_ref[...]


def _ada_call(c, ada_w, ada_b):
    depth, d, n = ada_w.shape
    bsz = c.shape[0]
    tn = 768
    return pl.pallas_call(
        _ada_kernel,
        grid=(depth, n // tn),
        in_specs=[
            pl.BlockSpec((d, bsz), lambda l, j: (0, 0)),
            pl.BlockSpec((None, d, tn), lambda l, j: (l, 0, j)),
            pl.BlockSpec((None, 1, tn), lambda l, j: (l, 0, j)),
        ],
        out_specs=pl.BlockSpec((None, bsz, tn), lambda l, j: (l, 0, j)),
        out_shape=jax.ShapeDtypeStruct((depth, bsz, n), F32),
        compiler_params=_params("arbitrary", "arbitrary"),
        name="ada_mod",
    )(c.T, ada_w, ada_b.reshape(depth, 1, n))


def _rope(x, cos, sin_signed):
    width = x.shape[1]
    reps = width // cos.shape[1]
    c = jnp.tile(cos, (1, reps))
    s = jnp.tile(sin_signed, (1, reps))
    lane = lax.broadcasted_iota(jnp.int32, x.shape, 1)
    first_half = (lane % HEAD_DIM) < (HEAD_DIM // 2)
    rot = jnp.where(first_half,
                    pltpu.roll(x, width - HEAD_DIM // 2, 1),
                    pltpu.roll(x, HEAD_DIM // 2, 1))
    return x * c + rot * s


def _ones_row_64(vt):
    ones_row = lax.broadcasted_iota(jnp.int32, vt.shape, 0) % LANES == HEAD_DIM
    return jnp.where(ones_row, 1.0, vt)


def _swa_proj_kernel(x_ref, mod_ref, gain_ref, w_ref, wvt_ref, cos_ref, sin_ref, q_ref, k_ref, vt_ref):
    h = _norm_mod(x_ref[...], gain_ref[...], mod_ref[0:1, :], mod_ref[1:2, :]).astype(BF16)
    cos, sin = cos_ref[...], sin_ref[...]
    nq = q_ref.shape[1]
    q = _dot(h, w_ref[:, :nq])
    q_ref[...] = (_rope(q, cos, sin) * (HEAD_DIM ** -0.5 * LOG2E)).astype(BF16)
    k = _dot(h, w_ref[:, nq:])
    k_ref[...] = _rope(k, cos, sin).astype(BF16)
    vt_ref[...] = _ones_row_64(_dot_nt(wvt_ref[...], h)).astype(BF16)


def _swa_proj_call(x, mod, gain, w, wvt, cos, sin, tm=512):
    bsz, s, d = x.shape
    nq, nkv = D_MODEL, SWA_KV_HEADS * LANES
    row = lambda b, i: (b, i, 0)
    return pl.pallas_call(
        _swa_proj_kernel,
        grid=(bsz, s // tm),
        in_specs=[
            pl.BlockSpec((None, tm, d), row),
            pl.BlockSpec((None, 6, d), lambda b, i: (b, 0, 0)),
            _const_spec((1, d)),
            _const_spec(w.shape),
            _const_spec(wvt.shape),
            pl.BlockSpec((tm, LANES), lambda b, i: (i, 0)),
            pl.BlockSpec((tm, LANES), lambda b, i: (i, 0)),
        ],
        out_specs=[
            pl.BlockSpec((None, tm, nq), row),
            pl.BlockSpec((None, tm, nkv), row),
            pl.BlockSpec((None, nkv, tm), lambda b, i: (b, 0, i)),
        ],
        out_shape=[
            jax.ShapeDtypeStruct((bsz, s, nq), BF16),
            jax.ShapeDtypeStruct((bsz, s, nkv), BF16),
            jax.ShapeDtypeStruct((bsz, nkv, s), BF16),
        ],
        compiler_params=_params("arbitrary", "arbitrary"),
        name="swa_proj",
    )(x, mod, gain, w, wvt, cos, sin)


def _gla_proj_kernel(x_ref, mod_ref, gain_ref, w_ref, wa_ref, wg_ref, bg_ref,
                     q_ref, k_ref, v_ref, r_ref, la_ref):
    h = _norm_mod(x_ref[...], gain_ref[...], mod_ref[0:1, :], mod_ref[1:2, :]).astype(BF16)
    nk = q_ref.shape[1]
    nv = v_ref.shape[1]
    q_ref[...] = _dot(h, w_ref[:, :nk]).astype(BF16)
    k_ref[...] = _dot(h, w_ref[:, nk:2 * nk]).astype(BF16)
    v_ref[...] = _dot(h, w_ref[:, 2 * nk:2 * nk + nv]).astype(BF16)
    r_ref[...] = _dot(h, w_ref[:, 2 * nk + nv:]).astype(BF16)
    a_low = _dot(h, wa_ref[...]).astype(BF16)
    z = _dot(a_low, wg_ref[...]) + bg_ref[...]
    la_ref[...] = _log_sigmoid(z) * (1.0 / GLA_TAU)


def _gla_proj_call(x, mod, gain, w, wa, wg, bg, tm=512):
    bsz, s, d = x.shape
    nk, nv = GLA_HEADS * GLA_DK, GLA_HEADS * GLA_DV
    row = lambda b, i: (b, i, 0)
    return pl.pallas_call(
        _gla_proj_kernel,
        grid=(bsz, s // tm),
        in_specs=[
            pl.BlockSpec((None, tm, d), row),
            pl.BlockSpec((None, 6, d), lambda b, i: (b, 0, 0)),
            _const_spec((1, d)),
            _const_spec(w.shape),
            _const_spec(wa.shape),
            _const_spec(wg.shape),
            _const_spec(bg.shape),
        ],
        out_specs=[
            pl.BlockSpec((None, tm, nk), row),
            pl.BlockSpec((None, tm, nk), row),
            pl.BlockSpec((None, tm, nv), row),
            pl.BlockSpec((None, tm, nv), row),
            pl.BlockSpec((None, tm, nk), row),
        ],
        out_shape=[
            jax.ShapeDtypeStruct((bsz, s, nk), BF16),
            jax.ShapeDtypeStruct((bsz, s, nk), BF16),
            jax.ShapeDtypeStruct((bsz, s, nv), BF16),
            jax.ShapeDtypeStruct((bsz, s, nv), BF16),
            jax.ShapeDtypeStruct((bsz, s, nk), F32),
        ],
        compiler_params=_params("arbitrary", "arbitrary"),
        name="gla_proj",
    )(x, mod, gain, w, wa, wg, bg)


def _spread_heads(x, extra, out_ref):
    lane = lax.broadcasted_iota(jnp.int32, (x.shape[0], LANES), 1)
    low = lane < HEAD_DIM
    for p in range(x.shape[1] // LANES):
        xs = x[:, p * LANES:(p + 1) * LANES]
        ex = extra[:, p * LANES:(p + 1) * LANES]
        out_ref[:, (2 * p) * LANES:(2 * p + 1) * LANES] = jnp.where(low, xs, ex).astype(out_ref.dtype)
        odd = pltpu.roll(jnp.where(low, ex, xs), HEAD_DIM, 1)
        out_ref[:, (2 * p + 1) * LANES:(2 * p + 2) * LANES] = odd.astype(out_ref.dtype)


def _fox_proj_kernel(x_ref, mod_ref, gain_ref, wq_ref, wk_ref, wvt_ref, wf_ref, bf_ref,
                     pq_ref, pk_ref, q_ref, k_ref, vt_ref, carry_ref):
    @pl.when(pl.program_id(1) == 0)
    def _():
        carry_ref[...] = jnp.zeros_like(carry_ref)

    h = _norm_mod(x_ref[...], gain_ref[...], mod_ref[0:1, :], mod_ref[1:2, :]).astype(BF16)
    log_f = _log_sigmoid(_dot(h, wf_ref[...]) + bf_ref[...])
    lc = _cumsum_rows(log_f) + carry_ref[...]
    carry_ref[...] = lc[lc.shape[0] - 1:, :]
    hi, mid, lo = _split3(lc * LOG2E)
    aug = jnp.concatenate([hi, mid, lo, jnp.ones_like(hi)], axis=1)
    _spread_heads(_dot(h, wq_ref[...]) * (HEAD_DIM ** -0.5 * LOG2E), _dot(aug, pq_ref[...]), q_ref)
    _spread_heads(_dot(h, wk_ref[...]), _dot(aug, pk_ref[...]), k_ref)
    vt = _dot_nt(wvt_ref[...], h).astype(BF16)
    pad = jnp.where(lax.broadcasted_iota(jnp.int32, (HEAD_DIM, vt.shape[1]), 0) == 0,
                    1.0, 0.0).astype(BF16)
    for hd in range(FOX_HEADS):
        vt_ref[hd * LANES:hd * LANES + HEAD_DIM, :] = vt[hd * HEAD_DIM:(hd + 1) * HEAD_DIM, :]
        vt_ref[hd * LANES + HEAD_DIM:(hd + 1) * LANES, :] = pad


def _fox_proj_call(x, mod, gain, wq, wk, wvt, wf, bf, pq, pk, tm=512):
    bsz, s, d = x.shape
    nqk = FOX_HEADS * LANES
    row = lambda b, i: (b, i, 0)
    return pl.pallas_call(
        _fox_proj_kernel,
        grid=(bsz, s // tm),
        in_specs=[
            pl.BlockSpec((None, tm, d), row),
            pl.BlockSpec((None, 6, d), lambda b, i: (b, 0, 0)),
            _const_spec((1, d)),
            _const_spec(wq.shape),
            _const_spec(wk.shape),
            _const_spec(wvt.shape),
            _const_spec(wf.shape),
            _const_spec(bf.shape),
            _const_spec(pq.shape),
            _const_spec(pk.shape),
        ],
        out_specs=[
            pl.BlockSpec((None, tm, nqk), row),
            pl.BlockSpec((None, tm, nqk), row),
            pl.BlockSpec((None, None, nqk, tm), lambda b, i: (b, i, 0, 0)),
        ],
        out_shape=[
            jax.ShapeDtypeStruct((bsz, s, nqk), BF16),
            jax.ShapeDtypeStruct((bsz, s, nqk), BF16),
            jax.ShapeDtypeStruct((bsz, s // tm, nqk, tm), BF16),
        ],
        scratch_shapes=[pltpu.VMEM((1, FOX_HEADS), F32)],
        compiler_params=_params("arbitrary", "arbitrary"),
        name="fox_proj",
    )(x, mod, gain, wq, wk, wvt, wf, bf, pq, pk)


def _swa_band_bias():
    w, group = SWA_WINDOW, SWA_Q_HEADS // SWA_KV_HEADS
    key = np.arange(2 * w)[:, None]
    qry = np.arange(group * w)[None, :] % w
    dist = (w + qry) - key
    band = (dist >= 0) & (dist < w)
    allowed = np.stack([band & (key >= w), band])
    return jnp.asarray(np.where(allowed, 0.0, NEG_BIG), F32)


def _swa_attn_kernel(sink_ref, bias_ref, q_ref, kc_ref, kp_ref, vtc_ref, vtp_ref, o_ref):
    w = SWA_WINDOW
    group = SWA_Q_HEADS // SWA_KV_HEADS
    nq = group * w
    bias = bias_ref[jnp.minimum(pl.program_id(1), 1)]
    lane = lax.broadcasted_iota(jnp.int32, (w, LANES), 1)
    low = lane < HEAD_DIM
    for g in range(SWA_KV_HEADS):
        slabs = []
        for hh in range(group):
            head = g * group + hh
            qs = q_ref[:, (head // 2) * LANES:(head // 2 + 1) * LANES]
            keep = low if head % 2 == 0 else jnp.logical_not(low)
            slabs.append(jnp.where(keep, qs, jnp.zeros_like(qs)))
        q_stack = jnp.concatenate(slabs, axis=0)
        k_both = jnp.concatenate([kp_ref[:, g * LANES:(g + 1) * LANES],
                                  kc_ref[:, g * LANES:(g + 1) * LANES]], axis=0)
        st = _dot_nt(k_both, q_stack) + bias
        sink = sink_ref[:, g * nq:(g + 1) * nq] * LOG2E
        m = jnp.maximum(jnp.max(st, axis=0, keepdims=True), sink)
        pt = jnp.exp2(st - m).astype(BF16)
        vt_both = jnp.concatenate([vtp_ref[g * LANES:(g + 1) * LANES, :],
                                   vtc_ref[g * LANES:(g + 1) * LANES, :]], axis=1)
        acc = _dot(vt_both, pt)
        den = acc[HEAD_DIM:HEAD_DIM + 1, :] + jnp.exp2(sink - m)
        ot = acc[:HEAD_DIM, :] / den
        o_t = jnp.concatenate([ot[:, hh * w:(hh + 1) * w] for hh in range(group)], axis=0)
        o_ref[:, g * group * HEAD_DIM:(g + 1) * group * HEAD_DIM] = o_t.T.astype(BF16)


def _swa_attn_call(q, k, vt, sinks):
    bsz, s, d = q.shape
    w = SWA_WINDOW
    nkv = k.shape[2]
    sink_row = jnp.repeat(sinks, w).reshape(1, -1)
    bias = _swa_band_bias()
    cur = lambda b, i: (b, i, 0)
    prev = lambda b, i: (b, jnp.maximum(i - 1, 0), 0)
    return pl.pallas_call(
        _swa_attn_kernel,
        grid=(bsz, s // w),
        in_specs=[
            _const_spec(sink_row.shape),
            _const_spec(bias.shape),
            pl.BlockSpec((None, w, d), cur),
            pl.BlockSpec((None, w, nkv), cur),
            pl.BlockSpec((None, w, nkv), prev),
            pl.BlockSpec((None, nkv, w), lambda b, i: (b, 0, i)),
            pl.BlockSpec((None, nkv, w), lambda b, i: (b, 0, jnp.maximum(i - 1, 0))),
        ],
        out_specs=pl.BlockSpec((None, w, d), cur),
        out_shape=jax.ShapeDtypeStruct((bsz, s, d), BF16),
        compiler_params=_params("arbitrary", "arbitrary"),
        name="swa_attn",
    )(sink_row, bias, q, k, k, vt, vt)


def _chunk_cumsum(x):
    n = x.shape[0]
    row = lax.broadcasted_iota(jnp.int32, (n, n), 0)
    col = lax.broadcasted_iota(jnp.int32, (n, n), 1)
    same_chunk = (row // GLA_CHUNK) == (col // GLA_CHUNK)
    tril = jnp.where(jnp.logical_and(row >= col, same_chunk), 1.0, 0.0).astype(BF16)
    hi, mid, lo = _split3(x)
    return _dot(tril, hi) + _dot(tril, mid) + _dot(tril, lo)


def _gla_intra(q, k, b2):
    c, sub = GLA_CHUNK, GLA_SUB
    col = lax.broadcasted_iota(jnp.int32, (sub, c), 1)
    row = lax.broadcasted_iota(jnp.int32, (sub, c), 0)
    blocks = []
    for i in range(c // sub):
        lo = i * sub
        q_i = q[lo:lo + sub, :]
        b_i = b2[lo:lo + sub, :]
        if i == 0:
            a = jnp.zeros((sub, c), F32)
        else:
            ref = b2[lo - 1:lo, :]
            n = -(-lo // BF16_ROWS) * BF16_ROWS
            q_t = (q_i * jnp.exp2(b_i - ref)).astype(BF16)
            k_t = (k[:n, :] * jnp.exp2(jnp.minimum(ref - b2[:n, :], 0.0))).astype(BF16)
            if n < c:
                k_t = jnp.concatenate([k_t, jnp.zeros((c - n, k_t.shape[1]), BF16)], axis=0)
            a = _dot_nt(q_t, k_t)
        for s in range(lo, lo + sub):
            w = jnp.exp2(jnp.minimum(b_i - b2[s:s + 1, :], 0.0))
            val = jnp.sum(q_i * k[s:s + 1, :] * w, axis=1, keepdims=True)
            a = jnp.where(col == s, val, a)
        blocks.append(jnp.where(row + lo >= col, a, 0.0))
    return jnp.concatenate(blocks, axis=0)


def _gla_kernel(q_ref, k_ref, v_ref, r_ref, la_ref, hn_ref, o_ref, state_ref):
    @pl.when(pl.program_id(2) == 0)
    def _():
        state_ref[...] = jnp.zeros_like(state_ref)

    c = GLA_CHUNK
    nc = q_ref.shape[0] // c
    rows = [slice(ci * c, (ci + 1) * c) for ci in range(nc)]
    b2_all = _chunk_cumsum(la_ref[...]) * LOG2E
    q_all = q_ref[...].astype(F32) * (GLA_DK ** -0.5)
    k_all = k_ref[...].astype(F32)
    b2 = [b2_all[r] for r in rows]
    q = [q_all[r] for r in rows]
    k = [k_all[r] for r in rows]
    last = [b[c - 1:c, :] for b in b2]
    q_in = [(q[i] * jnp.exp2(b2[i])).astype(BF16) for i in range(nc)]
    k_out = [(k[i] * jnp.exp2(last[i] - b2[i])).astype(BF16) for i in range(nc)]
    kv = [_dot_tn(v_ref[rows[i], :], k_out[i]) for i in range(nc)]
    attn = [_gla_intra(q[i], k[i], b2[i]).astype(BF16) for i in range(nc)]
    intra = [_dot(attn[i], v_ref[rows[i], :]) for i in range(nc)]
    state_t = state_ref[...]
    for i in range(nc):
        o = intra[i] + _dot_nt(q_in[i], state_t.astype(BF16))
        state_t = state_t * jnp.exp2(last[i]) + kv[i]
        r = r_ref[rows[i], :].astype(F32)
        o_ref[rows[i], :] = (_rms(o, hn_ref[...]) * _silu(r)).astype(BF16)
    state_ref[...] = state_t


def _gla_call(q, k, v, r, la, head_norm, tm=256):
    bsz, s, _ = q.shape
    dk, dv = GLA_DK, GLA_DV
    blk = lambda b, h, i: (b, i, h)
    return pl.pallas_call(
        _gla_kernel,
        grid=(bsz, GLA_HEADS, s // tm),
        in_specs=[
            pl.BlockSpec((None, tm, dk), blk),
            pl.BlockSpec((None, tm, dk), blk),
            pl.BlockSpec((None, tm, dv), blk),
            pl.BlockSpec((None, tm, dv), blk),
            pl.BlockSpec((None, tm, dk), blk),
            _const_spec((1, dv)),
        ],
        out_specs=pl.BlockSpec((None, tm, dv), blk),
        out_shape=jax.ShapeDtypeStruct((bsz, s, GLA_HEADS * dv), BF16),
        scratch_shapes=[pltpu.VMEM((dv, dk), F32)],
        compiler_params=_params("arbitrary", "arbitrary", "arbitrary"),
        name="gla_mix",
    )(q, k, v, r, la, head_norm)


def _fox_attn_kernel(q_ref, k_ref, vt_ref, o_ref, m_ref, acc_ref, sa_ref, sb_ref):
    tk = vt_ref.shape[2]
    i = pl.program_id(2)
    m_ref[...] = jnp.full_like(m_ref, NEG_BIG)
    acc_ref[...] = jnp.zeros_like(acc_ref)

    nh = m_ref.shape[0]

    def scores(j, buf, e):
        lanes = slice(e * LANES, (e + 1) * LANES)
        kj = k_ref[pl.ds(pl.multiple_of(j * tk, tk), tk), lanes]
        buf[e] = _dot_nt(kj, q_ref[:, lanes])

    def accumulate(j, buf, e, masked):
        st = buf[e]
        if masked:
            key = lax.broadcasted_iota(jnp.int32, st.shape, 0)
            qry = lax.broadcasted_iota(jnp.int32, st.shape, 1)
            st = jnp.where(key <= qry, st, NEG_BIG)
        m_old = m_ref[e]
        m_new = jnp.maximum(m_old, jnp.max(st, axis=0, keepdims=True))
        alpha = jnp.exp2(m_old - m_new)
        pt = jnp.exp2(st - m_new).astype(BF16)
        acc_ref[e] = alpha * acc_ref[e] + _dot(vt_ref[j, e * LANES:(e + 1) * LANES, :], pt)
        m_ref[e] = m_new

    def scores_and_accumulate(j_next, buf_next, j_cur, buf_cur):
        for e in range(nh):
            scores(j_next, buf_next, e)
            accumulate(j_cur, buf_cur, e, False)

    for e in range(nh):
        scores(0, sa_ref, e)

    def body(t, carry):
        scores_and_accumulate(2 * t + 1, sb_ref, 2 * t, sa_ref)
        scores_and_accumulate(2 * t + 2, sa_ref, 2 * t + 1, sb_ref)
        return carry

    lax.fori_loop(0, i // 2, body, 0)

    @pl.when(i % 2 == 0)
    def _():
        for e in range(nh):
            accumulate(i, sa_ref, e, True)

    @pl.when(i % 2 == 1)
    def _():
        scores_and_accumulate(i, sb_ref, i - 1, sa_ref)
        for e in range(nh):
            accumulate(i, sb_ref, e, True)

    outs = []
    for e in range(nh):
        acc = acc_ref[e]
        outs.append(acc[:HEAD_DIM, :] / acc[HEAD_DIM:HEAD_DIM + 1, :])
    o_ref[...] = jnp.concatenate(outs, axis=0).T.astype(BF16)


def _fox_attn_call(q, k, vt, nh=4):
    bsz, s, _ = q.shape
    tk = vt.shape[3]
    tq = tk
    return pl.pallas_call(
        _fox_attn_kernel,
        grid=(bsz, FOX_HEADS // nh, s // tq),
        in_specs=[
            pl.BlockSpec((None, tq, nh * LANES), lambda b, p, i: (b, i, p)),
            pl.BlockSpec((None, s, nh * LANES), lambda b, p, i: (b, 0, p)),
            pl.BlockSpec((None, s // tk, nh * LANES, tk), lambda b, p, i: (b, 0, p, 0)),
        ],
        out_specs=pl.BlockSpec((None, tq, nh * HEAD_DIM), lambda b, p, i: (b, i, p)),
        out_shape=jax.ShapeDtypeStruct((bsz, s, D_MODEL), BF16),
        scratch_shapes=[
            pltpu.VMEM((nh, 1, tq), F32),
            pltpu.VMEM((nh, LANES, tq), F32),
            pltpu.VMEM((nh, tk, tq), F32),
            pltpu.VMEM((nh, tk, tq), F32),
        ],
        compiler_params=_params("arbitrary", "arbitrary", "arbitrary"),
        name="fox_attn",
    )(q, k, vt)


def _post_kernel(x_ref, o_ref, mod_ref, gain_ref, wo_ref, wgu_ref, wd_ref, fn_ref, out_ref,
                 *, ff_chunk, final):
    x1 = x_ref[...] + mod_ref[2:3, :] * _dot(o_ref[...], wo_ref[...])
    h = _norm_mod(x1, gain_ref[...], mod_ref[3:4, :], mod_ref[4:5, :]).astype(BF16)
    acc = jnp.zeros(x1.shape, F32)
    for c0 in range(0, D_FF, ff_chunk):
        g = _dot(h, wgu_ref[:, c0:c0 + ff_chunk])
        u = _dot(h, wgu_ref[:, D_FF + c0:D_FF + c0 + ff_chunk])
        acc = acc + _dot((_silu(g) * u).astype(BF16), wd_ref[c0:c0 + ff_chunk, :])
    x2 = x1 + mod_ref[5:6, :] * acc
    if final:
        x2 = _rms(x2, fn_ref[...])
    out_ref[...] = x2


def _post_call(x, o, mod, gain, wo, wgu, wd, final_norm, final, tm=512, ff_chunk=256):
    bsz, s, d = x.shape
    row = lambda b, i: (b, i, 0)
    return pl.pallas_call(
        functools.partial(_post_kernel, ff_chunk=ff_chunk, final=final),
        grid=(bsz, s // tm),
        in_specs=[
            pl.BlockSpec((None, tm, d), row),
            pl.BlockSpec((None, tm, d), row),
            pl.BlockSpec((None, 6, d), lambda b, i: (b, 0, 0)),
            _const_spec((1, d)),
            _const_spec(wo.shape),
            _const_spec(wgu.shape),
            _const_spec(wd.shape),
            _const_spec((1, d)),
        ],
        out_specs=pl.BlockSpec((None, tm, d), row),
        out_shape=jax.ShapeDtypeStruct((bsz, s, d), F32),
        compiler_params=_params("arbitrary", "arbitrary"),
        name="post_ffn",
    )(x, o, mod, gain, wo, wgu, wd, final_norm)


def _rope_tables(s):
    half = HEAD_DIM // 2
    inv = 1.0 / (ROPE_THETA ** (jnp.arange(0, HEAD_DIM, 2, dtype=F32) / HEAD_DIM))
    ang = jnp.arange(s, dtype=F32)[:, None] * inv[None, :]
    cos, sin = jnp.cos(ang), jnp.sin(ang)
    reps = LANES // HEAD_DIM
    cos_t = jnp.tile(jnp.concatenate([cos, cos], axis=1), (1, reps))
    sin_t = jnp.tile(jnp.concatenate([-sin, sin], axis=1), (1, reps))
    assert half * 2 == HEAD_DIM
    return cos_t, sin_t


def _dup_heads(w, heads):
    w3 = w.reshape(w.shape[0], heads, HEAD_DIM)
    return jnp.concatenate([w3, w3], axis=2).reshape(w.shape[0], heads * LANES)


def _pad_heads(w, heads):
    w3 = w.reshape(w.shape[0], heads, HEAD_DIM)
    return jnp.concatenate([w3, jnp.zeros_like(w3)], axis=2).reshape(w.shape[0], heads * LANES)


def _fox_placement():
    h = FOX_HEADS
    pq = np.zeros((4 * h, h * HEAD_DIM), np.float32)
    pk = np.zeros((4 * h, h * HEAD_DIM), np.float32)
    for head in range(h):
        base = (head // 2) * LANES + (HEAD_DIM if head % 2 == 0 else 0)
        for part in range(3):
            pq[part * h + head, base + part] = 1.0
            pk[3 * h + head, base + part] = 1.0
            pq[3 * h + head, base + 3 + part] = 1.0
            pk[part * h + head, base + 3 + part] = -1.0
    return jnp.asarray(pq, BF16), jnp.asarray(pk, BF16)


def kernel(x, c, ada_w, ada_b, norm_gain, ffn_w_gu, ffn_w_down, swa_w_in, swa_sinks, swa_w_o,
           gla_w_in, gla_w_gate_up, gla_b_gate, gla_head_norm, gla_w_o, fox_w_in, fox_b_f, fox_w_o,
           final_norm):
    bsz, s, d = x.shape
    depth = ada_w.shape[0]
    mod_all = _ada_call(c, ada_w, ada_b).reshape(depth, bsz, 6, d)
    cos_t, sin_t = _rope_tables(s)
    pq, pk = _fox_placement()
    fn = final_norm.reshape(1, d)

    for i in range(depth):
        kind, j = i % N_MIXERS, i // N_MIXERS
        mod = mod_all[i]
        gain1 = norm_gain[i, 0].reshape(1, d)
        gain2 = norm_gain[i, 1].reshape(1, d)
        if kind == 0:
            w = swa_w_in[j]
            nq, nkv = SWA_Q_HEADS * HEAD_DIM, SWA_KV_HEADS * HEAD_DIM
            w_all = jnp.concatenate([w[:, :nq], _dup_heads(w[:, nq:nq + nkv], SWA_KV_HEADS)],
                                    axis=1).astype(BF16)
            wvt = _pad_heads(w[:, nq + nkv:], SWA_KV_HEADS).T.astype(BF16)
            q, k, v = _swa_proj_call(x, mod, gain1, w_all, wvt, cos_t, sin_t)
            o = _swa_attn_call(q, k, v, swa_sinks[j])
            wo = swa_w_o[j]
        elif kind == 1:
            w = gla_w_in[j]
            n_main = 2 * GLA_HEADS * GLA_DK + 2 * GLA_HEADS * GLA_DV
            q, k, v, r, la = _gla_proj_call(
                x, mod, gain1, w[:, :n_main].astype(BF16), w[:, n_main:].astype(BF16),
                gla_w_gate_up[j].astype(BF16), gla_b_gate[j].reshape(1, -1))
            o = _gla_call(q, k, v, r, la, gla_head_norm[j].reshape(1, -1))
            wo = gla_w_o[j]
        else:
            w = fox_w_in[j]
            q, k, v = _fox_proj_call(
                x, mod, gain1,
                w[:, :d].astype(BF16), w[:, d:2 * d].astype(BF16),
                w[:, 2 * d:3 * d].T.astype(BF16), w[:, 3 * d:].astype(BF16),
                fox_b_f[j].reshape(1, -1), pq, pk)
            o = _fox_attn_call(q, k, v)
            wo = fox_w_o[j]
        x = _post_call(x, o, mod, gain2, wo.astype(BF16), ffn_w_gu[i].astype(BF16),
                       ffn_w_down[i].astype(BF16), fn, final=(i == depth - 1))
    return x
```

```python
import functools

import numpy as np
import jax
import jax.numpy as jnp
from jax import lax
from jax.experimental import pallas as pl
from jax.experimental.pallas import tpu as pltpu

D_MODEL = 1024
HEAD_DIM = 64
RMS_EPS = 1e-6
SWA_Q_HEADS = 16
SWA_KV_HEADS = 4
SWA_WINDOW = 128
ROPE_THETA = 150000.0
GLA_HEADS = 4
GLA_DK = 128
GLA_DV = 256
GLA_RANK = 16
GLA_TAU = 16.0
GLA_CHUNK = 64
GLA_SUB = 8
FOX_HEADS = 16
D_FF = 2816
N_MIXERS = 3

LANES = 128
BF16_ROWS = 16
NEG_BIG = -1e30
LOG2E = 1.4426950408889634
VMEM_LIMIT = 56 * 1024 * 1024

BF16 = jnp.bfloat16
F32 = jnp.float32


def _dot(a, b):
    return jnp.dot(a, b, preferred_element_type=F32)


def _dot_nt(a, b):
    return lax.dot_general(a, b, (((1,), (1,)), ((), ())), preferred_element_type=F32)


def _dot_tn(a, b):
    return lax.dot_general(a, b, (((0,), (0,)), ((), ())), preferred_element_type=F32)


def _split3(x):
    hi = x.astype(BF16)
    r1 = x - hi.astype(F32)
    mid = r1.astype(BF16)
    lo = (r1 - mid.astype(F32)).astype(BF16)
    return hi, mid, lo


def _cumsum_rows(x):
    n = x.shape[0]
    row = lax.broadcasted_iota(jnp.int32, (n, n), 0)
    col = lax.broadcasted_iota(jnp.int32, (n, n), 1)
    tril = jnp.where(row >= col, 1.0, 0.0).astype(BF16)
    hi, mid, lo = _split3(x)
    return _dot(tril, hi) + _dot(tril, mid) + _dot(tril, lo)


def _log_sigmoid(x):
    return jnp.minimum(x, 0.0) - jnp.log(1.0 + jnp.exp(-jnp.abs(x)))


def _silu(x):
    return x * (1.0 / (1.0 + jnp.exp(-x)))


def _rms(x, gain):
    ms = jnp.mean(x * x, axis=-1, keepdims=True)
    return x * lax.rsqrt(ms + RMS_EPS) * gain


def _norm_mod(x, gain, shift, scale):
    return _rms(x, gain) * (1.0 + scale) + shift


def _params(*sem):
    return pltpu.CompilerParams(dimension_semantics=sem, vmem_limit_bytes=VMEM_LIMIT)


def _const_spec(shape):
    nd = len(shape)
    return pl.BlockSpec(shape, lambda *_: (0,) * nd, pipeline_mode=pl.Buffered(1))


def _ada_kernel(ct_ref, w_ref, b_ref, out_ref):
    ca = _silu(ct_ref[...])
    w = w_ref[...]
    for b in range(ct_ref.shape[1]):
        col = ca[:, b:b + 1]
        out_ref[b:b + 1, :] = jnp.sum(col * w, axis=0, keepdims=True) + b_ref[...]


def _ada_call(c, ada_w, ada_b):
    depth, d, n = ada_w.shape
    bsz = c.shape[0]
    tn = 768
    return pl.pallas_call(
        _ada_kernel,
        grid=(depth, n // tn),
        in_specs=[
            pl.BlockSpec((d, bsz), lambda l, j: (0, 0)),
            pl.BlockSpec((None, d, tn), lambda l, j: (l, 0, j)),
            pl.BlockSpec((None, 1, tn), lambda l, j: (l, 0, j)),
        ],
        out_specs=pl.BlockSpec((None, bsz, tn), lambda l, j: (l, 0, j)),
        out_shape=jax.ShapeDtypeStruct((depth, bsz, n), F32),
        compiler_params=_params("arbitrary", "arbitrary"),
        name="ada_mod",
    )(c.T, ada_w, ada_b.reshape(depth, 1, n))


def _rope(x, cos, sin_signed):
    width = x.shape[1]
    reps = width // cos.shape[1]
    c = jnp.tile(cos, (1, reps))
    s = jnp.tile(sin_signed, (1, reps))
    lane = lax.broadcasted_iota(jnp.int32, x.shape, 1)
    first_half = (lane % HEAD_DIM) < (HEAD_DIM // 2)
    rot = jnp.where(first_half,
                    pltpu.roll(x, width - HEAD_DIM // 2, 1),
                    pltpu.roll(x, HEAD_DIM // 2, 1))
    return x * c + rot * s


def _ones_row_64(vt):
    ones_row = lax.broadcasted_iota(jnp.int32, vt.shape, 0) % LANES == HEAD_DIM
    return jnp.where(ones_row, 1.0, vt)


def _swa_proj_kernel(x_ref, mod_ref, gain_ref, w_ref, wvt_ref, cos_ref, sin_ref, q_ref, k_ref, vt_ref):
    h = _norm_mod(x_ref[...], gain_ref[...], mod_ref[0:1, :], mod_ref[1:2, :]).astype(BF16)
    cos, sin = cos_ref[...], sin_ref[...]
    nq = q_ref.shape[1]
    q = _dot(h, w_ref[:, :nq])
    q_ref[...] = (_rope(q, cos, sin) * (HEAD_DIM ** -0.5 * LOG2E)).astype(BF16)
    k = _dot(h, w_ref[:, nq:])
    k_ref[...] = _rope(k, cos, sin).astype(BF16)
    vt_ref[...] = _ones_row_64(_dot_nt(wvt_ref[...], h)).astype(BF16)


def _swa_proj_call(x, mod, gain, w, wvt, cos, sin, tm=512):
    bsz, s, d = x.shape
    nq, nkv = D_MODEL, SWA_KV_HEADS * LANES
    row = lambda b, i: (b, i, 0)
    return pl.pallas_call(
        _swa_proj_kernel,
        grid=(bsz, s // tm),
        in_specs=[
            pl.BlockSpec((None, tm, d), row),
            pl.BlockSpec((None, 6, d), lambda b, i: (b, 0, 0)),
            _const_spec((1, d)),
            _const_spec(w.shape),
            _const_spec(wvt.shape),
            pl.BlockSpec((tm, LANES), lambda b, i: (i, 0)),
            pl.BlockSpec((tm, LANES), lambda b, i: (i, 0)),
        ],
        out_specs=[
            pl.BlockSpec((None, tm, nq), row),
            pl.BlockSpec((None, tm, nkv), row),
            pl.BlockSpec((None, nkv, tm), lambda b, i: (b, 0, i)),
        ],
        out_shape=[
            jax.ShapeDtypeStruct((bsz, s, nq), BF16),
            jax.ShapeDtypeStruct((bsz, s, nkv), BF16),
            jax.ShapeDtypeStruct((bsz, nkv, s), BF16),
        ],
        compiler_params=_params("arbitrary", "arbitrary"),
        name="swa_proj",
    )(x, mod, gain, w, wvt, cos, sin)


def _gla_proj_kernel(x_ref, mod_ref, gain_ref, w_ref, wa_ref, wg_ref, bg_ref,
                     q_ref, k_ref, v_ref, r_ref, la_ref):
    h = _norm_mod(x_ref[...], gain_ref[...], mod_ref[0:1, :], mod_ref[1:2, :]).astype(BF16)
    nk = q_ref.shape[1]
    nv = v_ref.shape[1]
    q_ref[...] = _dot(h, w_ref[:, :nk]).astype(BF16)
    k_ref[...] = _dot(h, w_ref[:, nk:2 * nk]).astype(BF16)
    v_ref[...] = _dot(h, w_ref[:, 2 * nk:2 * nk + nv]).astype(BF16)
    r_ref[...] = _dot(h, w_ref[:, 2 * nk + nv:]).astype(BF16)
    a_low = _dot(h, wa_ref[...]).astype(BF16)
    z = _dot(a_low, wg_ref[...]) + bg_ref[...]
    la_ref[...] = _log_sigmoid(z) * (1.0 / GLA_TAU)


def _gla_proj_call(x, mod, gain, w, wa, wg, bg, tm=512):
    bsz, s, d = x.shape
    nk, nv = GLA_HEADS * GLA_DK, GLA_HEADS * GLA_DV
    row = lambda b, i: (b, i, 0)
    return pl.pallas_call(
        _gla_proj_kernel,
        grid=(bsz, s // tm),
        in_specs=[
            pl.BlockSpec((None, tm, d), row),
            pl.BlockSpec((None, 6, d), lambda b, i: (b, 0, 0)),
            _const_spec((1, d)),
            _const_spec(w.shape),
            _const_spec(wa.shape),
            _const_spec(wg.shape),
            _const_spec(bg.shape),
        ],
        out_specs=[
            pl.BlockSpec((None, tm, nk), row),
            pl.BlockSpec((None, tm, nk), row),
            pl.BlockSpec((None, tm, nv), row),
            pl.BlockSpec((None, tm, nv), row),
            pl.BlockSpec((None, tm, nk), row),
        ],
        out_shape=[
            jax.ShapeDtypeStruct((bsz, s, nk), BF16),
            jax.ShapeDtypeStruct((bsz, s, nk), BF16),
            jax.ShapeDtypeStruct((bsz, s, nv), BF16),
            jax.ShapeDtypeStruct((bsz, s, nv), BF16),
            jax.ShapeDtypeStruct((bsz, s, nk), F32),
        ],
        compiler_params=_params("arbitrary", "arbitrary"),
        name="gla_proj",
    )(x, mod, gain, w, wa, wg, bg)


def _spread_heads(x, extra, out_ref):
    lane = lax.broadcasted_iota(jnp.int32, (x.shape[0], LANES), 1)
    low = lane < HEAD_DIM
    for p in range(x.shape[1] // LANES):
        xs = x[:, p * LANES:(p + 1) * LANES]
        ex = extra[:, p * LANES:(p + 1) * LANES]
        out_ref[:, (2 * p) * LANES:(2 * p + 1) * LANES] = jnp.where(low, xs, ex).astype(out_ref.dtype)
        odd = pltpu.roll(jnp.where(low, ex, xs), HEAD_DIM, 1)
        out_ref[:, (2 * p + 1) * LANES:(2 * p + 2) * LANES] = odd.astype(out_ref.dtype)


def _fox_proj_kernel(x_ref, mod_ref, gain_ref, wq_ref, wk_ref, wvt_ref, wf_ref, bf_ref,
                     pq_ref, pk_ref, q_ref, k_ref, vt_ref, carry_ref):
    @pl.when(pl.program_id(1) == 0)
    def _():
        carry_ref[...] = jnp.zeros_like(carry_ref)

    h = _norm_mod(x_ref[...], gain_ref[...], mod_ref[0:1, :], mod_ref[1:2, :]).astype(BF16)
    log_f = _log_sigmoid(_dot(h, wf_ref[...]) + bf_ref[...])
    lc = _cumsum_rows(log_f) + carry_ref[...]
    carry_ref[...] = lc[lc.shape[0] - 1:, :]
    hi, mid, lo = _split3(lc * LOG2E)
    aug = jnp.concatenate([hi, mid, lo, jnp.ones_like(hi)], axis=1)
    _spread_heads(_dot(h, wq_ref[...]) * (HEAD_DIM ** -0.5 * LOG2E), _dot(aug, pq_ref[...]), q_ref)
    _spread_heads(_dot(h, wk_ref[...]), _dot(aug, pk_ref[...]), k_ref)
    vt = _dot_nt(wvt_ref[...], h).astype(BF16)
    pad = jnp.where(lax.broadcasted_iota(jnp.int32, (HEAD_DIM, vt.shape[1]), 0) == 0,
                    1.0, 0.0).astype(BF16)
    for hd in range(FOX_HEADS):
        vt_ref[hd * LANES:hd * LANES + HEAD_DIM, :] = vt[hd * HEAD_DIM:(hd + 1) * HEAD_DIM, :]
        vt_ref[hd * LANES + HEAD_DIM:(hd + 1) * LANES, :] = pad


def _fox_proj_call(x, mod, gain, wq, wk, wvt, wf, bf, pq, pk, tm=512):
    bsz, s, d = x.shape
    nqk = FOX_HEADS * LANES
    row = lambda b, i: (b, i, 0)
    return pl.pallas_call(
        _fox_proj_kernel,
        grid=(bsz, s // tm),
        in_specs=[
            pl.BlockSpec((None, tm, d), row),
            pl.BlockSpec((None, 6, d), lambda b, i: (b, 0, 0)),
            _const_spec((1, d)),
            _const_spec(wq.shape),
            _const_spec(wk.shape),
            _const_spec(wvt.shape),
            _const_spec(wf.shape),
            _const_spec(bf.shape),
            _const_spec(pq.shape),
            _const_spec(pk.shape),
        ],
        out_specs=[
            pl.BlockSpec((None, tm, nqk), row),
            pl.BlockSpec((None, tm, nqk), row),
            pl.BlockSpec((None, None, nqk, tm), lambda b, i: (b, i, 0, 0)),
        ],
        out_shape=[
            jax.ShapeDtypeStruct((bsz, s, nqk), BF16),
            jax.ShapeDtypeStruct((bsz, s, nqk), BF16),
            jax.ShapeDtypeStruct((bsz, s // tm, nqk, tm), BF16),
        ],
        scratch_shapes=[pltpu.VMEM((1, FOX_HEADS), F32)],
        compiler_params=_params("arbitrary", "arbitrary"),
        name="fox_proj",
    )(x, mod, gain, wq, wk, wvt, wf, bf, pq, pk)


def _swa_band_bias():
    w, group = SWA_WINDOW, SWA_Q_HEADS // SWA_KV_HEADS
    key = np.arange(2 * w)[:, None]
    qry = np.arange(group * w)[None, :] % w
    dist = (w + qry) - key
    band = (dist >= 0) & (dist < w)
    allowed = np.stack([band & (key >= w), band])
    return jnp.asarray(np.where(allowed, 0.0, NEG_BIG), F32)


def _swa_attn_kernel(sink_ref, bias_ref, q_ref, kc_ref, kp_ref, vtc_ref, vtp_ref, o_ref, st_ref):
    w = SWA_WINDOW
    group = SWA_Q_HEADS // SWA_KV_HEADS
    nq = group * w
    nb = q_ref.shape[0] // w
    lane = lax.broadcasted_iota(jnp.int32, (w, LANES), 1)
    low = lane < HEAD_DIM
    for u in range(nb):
        rows = slice(u * w, (u + 1) * w)
        for g in range(SWA_KV_HEADS):
            cols = slice(g * LANES, (g + 1) * LANES)
            slabs = []
            for hh in range(group):
                head = g * group + hh
                qs = q_ref[rows, (head // 2) * LANES:(head // 2 + 1) * LANES]
                keep = low if head % 2 == 0 else jnp.logical_not(low)
                slabs.append(jnp.where(keep, qs, jnp.zeros_like(qs)))
            q_stack = jnp.concatenate(slabs, axis=0)
            k_prev = kp_ref[:, cols] if u == 0 else kc_ref[(u - 1) * w:u * w, cols]
            k_both = jnp.concatenate([k_prev, kc_ref[rows, cols]], axis=0)
            st_ref[u * SWA_KV_HEADS + g] = _dot_nt(k_both, q_stack)
    for u in range(nb):
        rows = slice(u * w, (u + 1) * w)
        bias = bias_ref[jnp.minimum(pl.program_id(1), 1)] if u == 0 else bias_ref[1]
        for g in range(SWA_KV_HEADS):
            cols = slice(g * LANES, (g + 1) * LANES)
            st = st_ref[u * SWA_KV_HEADS + g] + bias
            sink = sink_ref[:, g * nq:(g + 1) * nq] * LOG2E
            m = jnp.maximum(jnp.max(st, axis=0, keepdims=True), sink)
            pt = jnp.exp2(st - m).astype(BF16)
            vt_prev = vtp_ref[cols, :] if u == 0 else vtc_ref[cols, (u - 1) * w:u * w]
            vt_both = jnp.concatenate([vt_prev, vtc_ref[cols, rows]], axis=1)
            acc = _dot(vt_both, pt)
            den = acc[HEAD_DIM:HEAD_DIM + 1, :] + jnp.exp2(sink - m)
            ot = acc[:HEAD_DIM, :] / den
            o_t = jnp.concatenate([ot[:, hh * w:(hh + 1) * w] for hh in range(group)], axis=0)
            o_ref[rows, g * group * HEAD_DIM:(g + 1) * group * HEAD_DIM] = o_t.T.astype(BF16)


def _swa_attn_call(q, k, vt, sinks, nb=4):
    bsz, s, d = q.shape
    w = SWA_WINDOW
    nkv = k.shape[2]
    sink_row = jnp.repeat(sinks, w).reshape(1, -1)
    bias = _swa_band_bias()
    cur = lambda b, i: (b, i, 0)
    return pl.pallas_call(
        _swa_attn_kernel,
        grid=(bsz, s // (nb * w)),
        in_specs=[
            _const_spec(sink_row.shape),
            _const_spec(bias.shape),
            pl.BlockSpec((None, nb * w, d), cur),
            pl.BlockSpec((None, nb * w, nkv), cur),
            pl.BlockSpec((None, w, nkv), lambda b, i: (b, jnp.maximum(i * nb - 1, 0), 0)),
            pl.BlockSpec((None, nkv, nb * w), lambda b, i: (b, 0, i)),
            pl.BlockSpec((None, nkv, w), lambda b, i: (b, 0, jnp.maximum(i * nb - 1, 0))),
        ],
        out_specs=pl.BlockSpec((None, nb * w, d), cur),
        out_shape=jax.ShapeDtypeStruct((bsz, s, d), BF16),
        scratch_shapes=[pltpu.VMEM((nb * SWA_KV_HEADS, 2 * w, (SWA_Q_HEADS // SWA_KV_HEADS) * w), F32)],
        compiler_params=_params("arbitrary", "arbitrary"),
        name="swa_attn",
    )(sink_row, bias, q, k, k, vt, vt)


def _chunk_cumsum(x):
    n = x.shape[0]
    row = lax.broadcasted_iota(jnp.int32, (n, n), 0)
    col = lax.broadcasted_iota(jnp.int32, (n, n), 1)
    same_chunk = (row // GLA_CHUNK) == (col // GLA_CHUNK)
    tril = jnp.where(jnp.logical_and(row >= col, same_chunk), 1.0, 0.0).astype(BF16)
    hi, mid, lo = _split3(x)
    return _dot(tril, hi) + _dot(tril, mid) + _dot(tril, lo)


def _gla_intra(q, k, b2):
    c, sub = GLA_CHUNK, GLA_SUB
    col = lax.broadcasted_iota(jnp.int32, (sub, c), 1)
    row = lax.broadcasted_iota(jnp.int32, (sub, c), 0)
    blocks = []
    for i in range(c // sub):
        lo = i * sub
        q_i = q[lo:lo + sub, :]
        b_i = b2[lo:lo + sub, :]
        if i == 0:
            a = jnp.zeros((sub, c), F32)
        else:
            ref = b2[lo - 1:lo, :]
            n = -(-lo // BF16_ROWS) * BF16_ROWS
            q_t = (q_i * jnp.exp2(b_i - ref)).astype(BF16)
            k_t = (k[:n, :] * jnp.exp2(jnp.minimum(ref - b2[:n, :], 0.0))).astype(BF16)
            if n < c:
                k_t = jnp.concatenate([k_t, jnp.zeros((c - n, k_t.shape[1]), BF16)], axis=0)
            a = _dot_nt(q_t, k_t)
        for s in range(lo, lo + sub):
            w = jnp.exp2(jnp.minimum(b_i - b2[s:s + 1, :], 0.0))
            val = jnp.sum(q_i * k[s:s + 1, :] * w, axis=1, keepdims=True)
            a = jnp.where(col == s, val, a)
        blocks.append(jnp.where(row + lo >= col, a, 0.0))
    return jnp.concatenate(blocks, axis=0)


def _gla_kernel(q_ref, k_ref, v_ref, r_ref, la_ref, hn_ref, o_ref, state_ref):
    @pl.when(pl.program_id(2) == 0)
    def _():
        state_ref[...] = jnp.zeros_like(state_ref)

    for hh in range(state_ref.shape[0]):
        kq = slice(hh * GLA_DK, (hh + 1) * GLA_DK)
        vv = slice(hh * GLA_DV, (hh + 1) * GLA_DV)
        _gla_head(q_ref.at[:, kq], k_ref.at[:, kq], v_ref.at[:, vv], r_ref.at[:, vv], la_ref.at[:, kq],
                  hn_ref, o_ref.at[:, vv], state_ref.at[hh])


def _gla_head(q_ref, k_ref, v_ref, r_ref, la_ref, hn_ref, o_ref, state_ref):
    c = GLA_CHUNK
    nc = q_ref.shape[0] // c
    rows = [slice(ci * c, (ci + 1) * c) for ci in range(nc)]
    b2_all = _chunk_cumsum(la_ref[...]) * LOG2E
    q_all = q_ref[...].astype(F32) * (GLA_DK ** -0.5)
    k_all = k_ref[...].astype(F32)
    b2 = [b2_all[r] for r in rows]
    q = [q_all[r] for r in rows]
    k = [k_all[r] for r in rows]
    last = [b[c - 1:c, :] for b in b2]
    q_in = [(q[i] * jnp.exp2(b2[i])).astype(BF16) for i in range(nc)]
    k_out = [(k[i] * jnp.exp2(last[i] - b2[i])).astype(BF16) for i in range(nc)]
    kv = [_dot_tn(v_ref[rows[i], :], k_out[i]) for i in range(nc)]
    attn = [_gla_intra(q[i], k[i], b2[i]).astype(BF16) for i in range(nc)]
    intra = [_dot(attn[i], v_ref[rows[i], :]) for i in range(nc)]
    state_t = state_ref[...]
    for i in range(nc):
        o = intra[i] + _dot_nt(q_in[i], state_t.astype(BF16))
        state_t = state_t * jnp.exp2(last[i]) + kv[i]
        r = r_ref[rows[i], :].astype(F32)
        o_ref[rows[i], :] = (_rms(o, hn_ref[...]) * _silu(r)).astype(BF16)
    state_ref[...] = state_t


def _gla_call(q, k, v, r, la, head_norm, tm=256, nh=2):
    bsz, s, _ = q.shape
    dk, dv = GLA_DK, GLA_DV
    blk = lambda b, h, i: (b, i, h)
    return pl.pallas_call(
        _gla_kernel,
        grid=(bsz, GLA_HEADS // nh, s // tm),
        in_specs=[
            pl.BlockSpec((None, tm, nh * dk), blk),
            pl.BlockSpec((None, tm, nh * dk), blk),
            pl.BlockSpec((None, tm, nh * dv), blk),
            pl.BlockSpec((None, tm, nh * dv), blk),
            pl.BlockSpec((None, tm, nh * dk), blk),
            _const_spec((1, dv)),
        ],
        out_specs=pl.BlockSpec((None, tm, nh * dv), blk),
        out_shape=jax.ShapeDtypeStruct((bsz, s, GLA_HEADS * dv), BF16),
        scratch_shapes=[pltpu.VMEM((nh, dv, dk), F32)],
        compiler_params=_params("arbitrary", "arbitrary", "arbitrary"),
        name="gla_mix",
    )(q, k, v, r, la, head_norm)


def _fox_attn_kernel(q_ref, k_ref, vt_ref, o_ref, m_ref, acc_ref, sa_ref, sb_ref):
    tk = vt_ref.shape[2]
    i = pl.program_id(2)
    m_ref[...] = jnp.full_like(m_ref, NEG_BIG)
    acc_ref[...] = jnp.zeros_like(acc_ref)

    nh = m_ref.shape[0]

    def scores(j, buf, e):
        lanes = slice(e * LANES, (e + 1) * LANES)
        kj = k_ref[pl.ds(pl.multiple_of(j * tk, tk), tk), lanes]
        buf[e] = _dot_nt(kj, q_ref[:, lanes])

    def accumulate(j, buf, e, masked):
        st = buf[e]
        if masked:
            key = lax.broadcasted_iota(jnp.int32, st.shape, 0)
            qry = lax.broadcasted_iota(jnp.int32, st.shape, 1)
            st = jnp.where(key <= qry, st, NEG_BIG)
        m_old = m_ref[e]
        m_new = jnp.maximum(m_old, jnp.max(st, axis=0, keepdims=True))
        alpha = jnp.exp2(m_old - m_new)
        pt = jnp.exp2(st - m_new).astype(BF16)
        acc_ref[e] = alpha * acc_ref[e] + _dot(vt_ref[j, e * LANES:(e + 1) * LANES, :], pt)
        m_ref[e] = m_new

    def scores_and_accumulate(j_next, buf_next, j_cur, buf_cur):
        for e in range(nh):
            scores(j_next, buf_next, e)
            accumulate(j_cur, buf_cur, e, False)

    for e in range(nh):
        scores(0, sa_ref, e)

    def body(t, carry):
        scores_and_accumulate(2 * t + 1, sb_ref, 2 * t, sa_ref)
        scores_and_accumulate(2 * t + 2, sa_ref, 2 * t + 1, sb_ref)
        return carry

    lax.fori_loop(0, i // 2, body, 0)

    @pl.when(i % 2 == 0)
    def _():
        for e in range(nh):
            accumulate(i, sa_ref, e, True)

    @pl.when(i % 2 == 1)
    def _():
        scores_and_accumulate(i, sb_ref, i - 1, sa_ref)
        for e in range(nh):
            accumulate(i, sb_ref, e, True)

    outs = []
    for e in range(nh):
        acc = acc_ref[e]
        outs.append(acc[:HEAD_DIM, :] / acc[HEAD_DIM:HEAD_DIM + 1, :])
    o_ref[...] = jnp.concatenate(outs, axis=0).T.astype(BF16)


def _fox_attn_call(q, k, vt, nh=4):
    bsz, s, _ = q.shape
    tk = vt.shape[3]
    tq = tk
    return pl.pallas_call(
        _fox_attn_kernel,
        grid=(bsz, FOX_HEADS // nh, s // tq),
        in_specs=[
            pl.BlockSpec((None, tq, nh * LANES), lambda b, p, i: (b, i, p)),
            pl.BlockSpec((None, s, nh * LANES), lambda b, p, i: (b, 0, p)),
            pl.BlockSpec((None, s // tk, nh * LANES, tk), lambda b, p, i: (b, 0, p, 0)),
        ],
        out_specs=pl.BlockSpec((None, tq, nh * HEAD_DIM), lambda b, p, i: (b, i, p)),
        out_shape=jax.ShapeDtypeStruct((bsz, s, D_MODEL), BF16),
        scratch_shapes=[
            pltpu.VMEM((nh, 1, tq), F32),
            pltpu.VMEM((nh, LANES, tq), F32),
            pltpu.VMEM((nh, tk, tq), F32),
            pltpu.VMEM((nh, tk, tq), F32),
        ],
        compiler_params=_params("arbitrary", "arbitrary", "arbitrary"),
        name="fox_attn",
    )(q, k, vt)


def _post_kernel(x_ref, o_ref, mod_ref, gain_ref, wo_ref, wgu_ref, wd_ref, fn_ref, out_ref,
                 *, ff_chunk, final):
    x1 = x_ref[...] + mod_ref[2:3, :] * _dot(o_ref[...], wo_ref[...])
    h = _norm_mod(x1, gain_ref[...], mod_ref[3:4, :], mod_ref[4:5, :]).astype(BF16)
    acc = jnp.zeros(x1.shape, F32)
    for c0 in range(0, D_FF, ff_chunk):
        g = _dot(h, wgu_ref[:, c0:c0 + ff_chunk])
        u = _dot(h, wgu_ref[:, D_FF + c0:D_FF + c0 + ff_chunk])
        acc = acc + _dot((_silu(g) * u).astype(BF16), wd_ref[c0:c0 + ff_chunk, :])
    x2 = x1 + mod_ref[5:6, :] * acc
    if final:
        x2 = _rms(x2, fn_ref[...])
    out_ref[...] = x2


def _post_call(x, o, mod, gain, wo, wgu, wd, final_norm, final, tm=512, ff_chunk=256):
    bsz, s, d = x.shape
    row = lambda b, i: (b, i, 0)
    return pl.pallas_call(
        functools.partial(_post_kernel, ff_chunk=ff_chunk, final=final),
        grid=(bsz, s // tm),
        in_specs=[
            pl.BlockSpec((None, tm, d), row),
            pl.BlockSpec((None, tm, d), row),
            pl.BlockSpec((None, 6, d), lambda b, i: (b, 0, 0)),
            _const_spec((1, d)),
            _const_spec(wo.shape),
            _const_spec(wgu.shape),
            _const_spec(wd.shape),
            _const_spec((1, d)),
        ],
        out_specs=pl.BlockSpec((None, tm, d), row),
        out_shape=jax.ShapeDtypeStruct((bsz, s, d), F32),
        compiler_params=_params("arbitrary", "arbitrary"),
        name="post_ffn",
    )(x, o, mod, gain, wo, wgu, wd, final_norm)


def _rope_tables(s):
    half = HEAD_DIM // 2
    inv = 1.0 / (ROPE_THETA ** (jnp.arange(0, HEAD_DIM, 2, dtype=F32) / HEAD_DIM))
    ang = jnp.arange(s, dtype=F32)[:, None] * inv[None, :]
    cos, sin = jnp.cos(ang), jnp.sin(ang)
    reps = LANES // HEAD_DIM
    cos_t = jnp.tile(jnp.concatenate([cos, cos], axis=1), (1, reps))
    sin_t = jnp.tile(jnp.concatenate([-sin, sin], axis=1), (1, reps))
    assert half * 2 == HEAD_DIM
    return cos_t, sin_t


def _dup_heads(w, heads):
    w3 = w.reshape(w.shape[0], heads, HEAD_DIM)
    return jnp.concatenate([w3, w3], axis=2).reshape(w.shape[0], heads * LANES)


def _pad_heads(w, heads):
    w3 = w.reshape(w.shape[0], heads, HEAD_DIM)
    return jnp.concatenate([w3, jnp.zeros_like(w3)], axis=2).reshape(w.shape[0], heads * LANES)


def _fox_placement():
    h = FOX_HEADS
    pq = np.zeros((4 * h, h * HEAD_DIM), np.float32)
    pk = np.zeros((4 * h, h * HEAD_DIM), np.float32)
    for head in range(h):
        base = (head // 2) * LANES + (HEAD_DIM if head % 2 == 0 else 0)
        for part in range(3):
            pq[part * h + head, base + part] = 1.0
            pk[3 * h + head, base + part] = 1.0
            pq[3 * h + head, base + 3 + part] = 1.0
            pk[part * h + head, base + 3 + part] = -1.0
    return jnp.asarray(pq, BF16), jnp.asarray(pk, BF16)


def kernel(x, c, ada_w, ada_b, norm_gain, ffn_w_gu, ffn_w_down, swa_w_in, swa_sinks, swa_w_o,
           gla_w_in, gla_w_gate_up, gla_b_gate, gla_head_norm, gla_w_o, fox_w_in, fox_b_f, fox_w_o,
           final_norm):
    bsz, s, d = x.shape
    depth = ada_w.shape[0]
    mod_all = _ada_call(c, ada_w, ada_b).reshape(depth, bsz, 6, d)
    cos_t, sin_t = _rope_tables(s)
    pq, pk = _fox_placement()
    fn = final_norm.reshape(1, d)

    for i in range(depth):
        kind, j = i % N_MIXERS, i // N_MIXERS
        mod = mod_all[i]
        gain1 = norm_gain[i, 0].reshape(1, d)
        gain2 = norm_gain[i, 1].reshape(1, d)
        if kind == 0:
            w = swa_w_in[j]
            nq, nkv = SWA_Q_HEADS * HEAD_DIM, SWA_KV_HEADS * HEAD_DIM
            w_all = jnp.concatenate([w[:, :nq], _dup_heads(w[:, nq:nq + nkv], SWA_KV_HEADS)],
                                    axis=1).astype(BF16)
            wvt = _pad_heads(w[:, nq + nkv:], SWA_KV_HEADS).T.astype(BF16)
            q, k, v = _swa_proj_call(x, mod, gain1, w_all, wvt, cos_t, sin_t)
            o = _swa_attn_call(q, k, v, swa_sinks[j])
            wo = swa_w_o[j]
        elif kind == 1:
            w = gla_w_in[j]
            n_main = 2 * GLA_HEADS * GLA_DK + 2 * GLA_HEADS * GLA_DV
            q, k, v, r, la = _gla_proj_call(
                x, mod, gain1, w[:, :n_main].astype(BF16), w[:, n_main:].astype(BF16),
                gla_w_gate_up[j].astype(BF16), gla_b_gate[j].reshape(1, -1))
            o = _gla_call(q, k, v, r, la, gla_head_norm[j].reshape(1, -1))
            wo = gla_w_o[j]
        else:
            w = fox_w_in[j]
            q, k, v = _fox_proj_call(
                x, mod, gain1,
                w[:, :d].astype(BF16), w[:, d:2 * d].astype(BF16),
                w[:, 2 * d:3 * d].T.astype(BF16), w[:, 3 * d:].astype(BF16),
                fox_b_f[j].reshape(1, -1), pq, pk)
            o = _fox_attn_call(q, k, v)
            wo = fox_w_o[j]
        x = _post_call(x, o, mod, gain2, wo.astype(BF16), ffn_w_gu[i].astype(BF16),
                       ffn_w_down[i].astype(BF16), fn, final=(i == depth - 1))
    return x
```

```python
import functools

import numpy as np
import jax
import jax.numpy as jnp
from jax import lax
from jax.experimental import pallas as pl
from jax.experimental.pallas import tpu as pltpu

D_MODEL = 1024
HEAD_DIM = 64
RMS_EPS = 1e-6
SWA_Q_HEADS = 16
SWA_KV_HEADS = 4
SWA_WINDOW = 128
ROPE_THETA = 150000.0
GLA_HEADS = 4
GLA_DK = 128
GLA_DV = 256
GLA_RANK = 16
GLA_TAU = 16.0
GLA_CHUNK = 64
GLA_SUB = 8
FOX_HEADS = 16
FOX_STATS = 8
Q_MAX, K_MAX, QK_MAX, LC_FIRST, LC_LAST = range(5)
FOX_DEAD_LOG2 = -160.0
FOX_BOUND_SLACK = 1.02
D_FF = 2816
N_MIXERS = 3

LANES = 128
BF16_ROWS = 16
NEG_BIG = -1e30
LOG2E = 1.4426950408889634
VMEM_LIMIT = 56 * 1024 * 1024

BF16 = jnp.bfloat16
F32 = jnp.float32


def _dot(a, b):
    return jnp.dot(a, b, preferred_element_type=F32)


def _dot_nt(a, b):
    return lax.dot_general(a, b, (((1,), (1,)), ((), ())), preferred_element_type=F32)


def _dot_tn(a, b):
    return lax.dot_general(a, b, (((0,), (0,)), ((), ())), preferred_element_type=F32)


def _split3(x):
    hi = x.astype(BF16)
    r1 = x - hi.astype(F32)
    mid = r1.astype(BF16)
    lo = (r1 - mid.astype(F32)).astype(BF16)
    return hi, mid, lo


def _cumsum_rows(x):
    n = x.shape[0]
    row = lax.broadcasted_iota(jnp.int32, (n, n), 0)
    col = lax.broadcasted_iota(jnp.int32, (n, n), 1)
    tril = jnp.where(row >= col, 1.0, 0.0).astype(BF16)
    hi, mid, lo = _split3(x)
    return _dot(tril, hi) + _dot(tril, mid) + _dot(tril, lo)


def _log_sigmoid(x):
    return jnp.minimum(x, 0.0) - jnp.log(1.0 + jnp.exp(-jnp.abs(x)))


def _silu(x):
    return x * (1.0 / (1.0 + jnp.exp(-x)))


def _rms(x, gain):
    ms = jnp.mean(x * x, axis=-1, keepdims=True)
    return x * lax.rsqrt(ms + RMS_EPS) * gain


def _norm_mod(x, gain, shift, scale):
    return _rms(x, gain) * (1.0 + scale) + shift


def _params(*sem):
    return pltpu.CompilerParams(dimension_semantics=sem, vmem_limit_bytes=VMEM_LIMIT)


def _const_spec(shape):
    nd = len(shape)
    return pl.BlockSpec(shape, lambda *_: (0,) * nd, pipeline_mode=pl.Buffered(1))


def _ada_kernel(ct_ref, w_ref, b_ref, out_ref):
    ca = _silu(ct_ref[...])
    w = w_ref[...]
    for b in range(ct_ref.shape[1]):
        col = ca[:, b:b + 1]
        out_ref[b:b + 1, :] = jnp.sum(col * w, axis=0, keepdims=True) + b_ref[...]


def _ada_call(c, ada_w, ada_b):
    depth, d, n = ada_w.shape
    bsz = c.shape[0]
    tn = 768
    return pl.pallas_call(
        _ada_kernel,
        grid=(depth, n // tn),
        in_specs=[
            pl.BlockSpec((d, bsz), lambda l, j: (0, 0)),
            pl.BlockSpec((None, d, tn), lambda l, j: (l, 0, j)),
            pl.BlockSpec((None, 1, tn), lambda l, j: (l, 0, j)),
        ],
        out_specs=pl.BlockSpec((None, bsz, tn), lambda l, j: (l, 0, j)),
        out_shape=jax.ShapeDtypeStruct((depth, bsz, n), F32),
        compiler_params=_params("arbitrary", "arbitrary"),
        name="ada_mod",
    )(c.T, ada_w, ada_b.reshape(depth, 1, n))


def _rope(x, cos, sin_signed):
    width = x.shape[1]
    reps = width // cos.shape[1]
    c = jnp.tile(cos, (1, reps))
    s = jnp.tile(sin_signed, (1, reps))
    lane = lax.broadcasted_iota(jnp.int32, x.shape, 1)
    first_half = (lane % HEAD_DIM) < (HEAD_DIM // 2)
    rot = jnp.where(first_half,
                    pltpu.roll(x, width - HEAD_DIM // 2, 1),
                    pltpu.roll(x, HEAD_DIM // 2, 1))
    return x * c + rot * s


def _ones_row_64(vt):
    ones_row = lax.broadcasted_iota(jnp.int32, vt.shape, 0) % LANES == HEAD_DIM
    return jnp.where(ones_row, 1.0, vt)


def _swa_proj_kernel(x_ref, mod_ref, gain_ref, w_ref, wvt_ref, cos_ref, sin_ref, q_ref, k_ref, vt_ref):
    h = _norm_mod(x_ref[...], gain_ref[...], mod_ref[0:1, :], mod_ref[1:2, :]).astype(BF16)
    cos, sin = cos_ref[...], sin_ref[...]
    nq = q_ref.shape[1]
    q = _dot(h, w_ref[:, :nq])
    q_ref[...] = (_rope(q, cos, sin) * (HEAD_DIM ** -0.5 * LOG2E)).astype(BF16)
    k = _dot(h, w_ref[:, nq:])
    k_ref[...] = _rope(k, cos, sin).astype(BF16)
    vt_ref[...] = _ones_row_64(_dot_nt(wvt_ref[...], h)).astype(BF16)


def _swa_proj_call(x, mod, gain, w, wvt, cos, sin, tm=512):
    bsz, s, d = x.shape
    nq, nkv = D_MODEL, SWA_KV_HEADS * LANES
    row = lambda b, i: (b, i, 0)
    return pl.pallas_call(
        _swa_proj_kernel,
        grid=(bsz, s // tm),
        in_specs=[
            pl.BlockSpec((None, tm, d), row),
            pl.BlockSpec((None, 6, d), lambda b, i: (b, 0, 0)),
            _const_spec((1, d)),
            _const_spec(w.shape),
            _const_spec(wvt.shape),
            pl.BlockSpec((tm, LANES), lambda b, i: (i, 0)),
            pl.BlockSpec((tm, LANES), lambda b, i: (i, 0)),
        ],
        out_specs=[
            pl.BlockSpec((None, tm, nq), row),
            pl.BlockSpec((None, tm, nkv), row),
            pl.BlockSpec((None, nkv, tm), lambda b, i: (b, 0, i)),
        ],
        out_shape=[
            jax.ShapeDtypeStruct((bsz, s, nq), BF16),
            jax.ShapeDtypeStruct((bsz, s, nkv), BF16),
            jax.ShapeDtypeStruct((bsz, nkv, s), BF16),
        ],
        compiler_params=_params("arbitrary", "arbitrary"),
        name="swa_proj",
    )(x, mod, gain, w, wvt, cos, sin)


def _gla_proj_kernel(x_ref, mod_ref, gain_ref, w_ref, wa_ref, wg_ref, bg_ref,
                     q_ref, k_ref, v_ref, r_ref, la_ref):
    h = _norm_mod(x_ref[...], gain_ref[...], mod_ref[0:1, :], mod_ref[1:2, :]).astype(BF16)
    nk = q_ref.shape[1]
    nv = v_ref.shape[1]
    q_ref[...] = _dot(h, w_ref[:, :nk]).astype(BF16)
    k_ref[...] = _dot(h, w_ref[:, nk:2 * nk]).astype(BF16)
    v_ref[...] = _dot(h, w_ref[:, 2 * nk:2 * nk + nv]).astype(BF16)
    r_ref[...] = _dot(h, w_ref[:, 2 * nk + nv:]).astype(BF16)
    a_low = _dot(h, wa_ref[...]).astype(BF16)
    z = _dot(a_low, wg_ref[...]) + bg_ref[...]
    la_ref[...] = _log_sigmoid(z) * (1.0 / GLA_TAU)


def _gla_proj_call(x, mod, gain, w, wa, wg, bg, tm=512):
    bsz, s, d = x.shape
    nk, nv = GLA_HEADS * GLA_DK, GLA_HEADS * GLA_DV
    row = lambda b, i: (b, i, 0)
    return pl.pallas_call(
        _gla_proj_kernel,
        grid=(bsz, s // tm),
        in_specs=[
            pl.BlockSpec((None, tm, d), row),
            pl.BlockSpec((None, 6, d), lambda b, i: (b, 0, 0)),
            _const_spec((1, d)),
            _const_spec(w.shape),
            _const_spec(wa.shape),
            _const_spec(wg.shape),
            _const_spec(bg.shape),
        ],
        out_specs=[
            pl.BlockSpec((None, tm, nk), row),
            pl.BlockSpec((None, tm, nk), row),
            pl.BlockSpec((None, tm, nv), row),
            pl.BlockSpec((None, tm, nv), row),
            pl.BlockSpec((None, tm, nk), row),
        ],
        out_shape=[
            jax.ShapeDtypeStruct((bsz, s, nk), BF16),
            jax.ShapeDtypeStruct((bsz, s, nk), BF16),
            jax.ShapeDtypeStruct((bsz, s, nv), BF16),
            jax.ShapeDtypeStruct((bsz, s, nv), BF16),
            jax.ShapeDtypeStruct((bsz, s, nk), F32),
        ],
        compiler_params=_params("arbitrary", "arbitrary"),
        name="gla_proj",
    )(x, mod, gain, w, wa, wg, bg)


def _spread_heads(x, extra, out_ref):
    lane = lax.broadcasted_iota(jnp.int32, (x.shape[0], LANES), 1)
    low = lane < HEAD_DIM
    for p in range(x.shape[1] // LANES):
        xs = x[:, p * LANES:(p + 1) * LANES]
        ex = extra[:, p * LANES:(p + 1) * LANES]
        out_ref[:, (2 * p) * LANES:(2 * p + 1) * LANES] = jnp.where(low, xs, ex).astype(out_ref.dtype)
        odd = pltpu.roll(jnp.where(low, ex, xs), HEAD_DIM, 1)
        out_ref[:, (2 * p + 1) * LANES:(2 * p + 2) * LANES] = odd.astype(out_ref.dtype)


def _fox_proj_kernel(x_ref, mod_ref, gain_ref, wq_ref, wk_ref, wvt_ref, wf_ref, bf_ref,
                     pq_ref, pk_ref, hsel_ref, q_ref, k_ref, vt_ref, stats_ref, carry_ref):
    @pl.when(pl.program_id(1) == 0)
    def _():
        carry_ref[...] = jnp.zeros_like(carry_ref)

    h = _norm_mod(x_ref[...], gain_ref[...], mod_ref[0:1, :], mod_ref[1:2, :]).astype(BF16)
    log_f = _log_sigmoid(_dot(h, wf_ref[...]) + bf_ref[...])
    lc = _cumsum_rows(log_f) + carry_ref[...]
    carry_ref[...] = lc[lc.shape[0] - 1:, :]
    lc2 = lc * LOG2E
    hi, mid, lo = _split3(lc2)
    aug = jnp.concatenate([hi, mid, lo, jnp.ones_like(hi)], axis=1)
    qs = _dot(h, wq_ref[...]) * (HEAD_DIM ** -0.5 * LOG2E)
    ks = _dot(h, wk_ref[...])
    _spread_heads(qs, _dot(aug, pq_ref[...]), q_ref)
    _spread_heads(ks, _dot(aug, pk_ref[...]), k_ref)
    qn2 = _dot((qs * qs).astype(BF16), hsel_ref[...])
    kn2 = _dot((ks * ks).astype(BF16), hsel_ref[...])
    tm = lc2.shape[0]
    stats_ref[...] = jnp.concatenate([
        jnp.sqrt(jnp.max(qn2, axis=0, keepdims=True)),
        jnp.sqrt(jnp.max(kn2, axis=0, keepdims=True)),
        jnp.sqrt(jnp.max(qn2 * kn2, axis=0, keepdims=True)),
        lc2[0:1, :], lc2[tm - 1:tm, :],
        jnp.zeros((FOX_STATS - 5, lc2.shape[1]), F32)], axis=0)
    vt = _dot_nt(wvt_ref[...], h).astype(BF16)
    pad = jnp.where(lax.broadcasted_iota(jnp.int32, (HEAD_DIM, vt.shape[1]), 0) == 0,
                    1.0, 0.0).astype(BF16)
    for hd in range(FOX_HEADS):
        vt_ref[hd * LANES:hd * LANES + HEAD_DIM, :] = vt[hd * HEAD_DIM:(hd + 1) * HEAD_DIM, :]
        vt_ref[hd * LANES + HEAD_DIM:(hd + 1) * LANES, :] = pad


def _fox_proj_call(x, mod, gain, wq, wk, wvt, wf, bf, pq, pk, tm=512):
    bsz, s, d = x.shape
    nqk = FOX_HEADS * LANES
    row = lambda b, i: (b, i, 0)
    hsel = jnp.asarray(np.repeat(np.eye(FOX_HEADS, dtype=np.float32), HEAD_DIM, axis=0), BF16)
    return pl.pallas_call(
        _fox_proj_kernel,
        grid=(bsz, s // tm),
        in_specs=[
            pl.BlockSpec((None, tm, d), row),
            pl.BlockSpec((None, 6, d), lambda b, i: (b, 0, 0)),
            _const_spec((1, d)),
            _const_spec(wq.shape),
            _const_spec(wk.shape),
            _const_spec(wvt.shape),
            _const_spec(wf.shape),
            _const_spec(bf.shape),
            _const_spec(pq.shape),
            _const_spec(pk.shape),
            _const_spec(hsel.shape),
        ],
        out_specs=[
            pl.BlockSpec((None, tm, nqk), row),
            pl.BlockSpec((None, tm, nqk), row),
            pl.BlockSpec((None, None, nqk, tm), lambda b, i: (b, i, 0, 0)),
            pl.BlockSpec((None, None, FOX_STATS, FOX_HEADS), lambda b, i: (b, i, 0, 0)),
        ],
        out_shape=[
            jax.ShapeDtypeStruct((bsz, s, nqk), BF16),
            jax.ShapeDtypeStruct((bsz, s, nqk), BF16),
            jax.ShapeDtypeStruct((bsz, s // tm, nqk, tm), BF16),
            jax.ShapeDtypeStruct((bsz, s // tm, FOX_STATS, FOX_HEADS), F32),
        ],
        scratch_shapes=[pltpu.VMEM((1, FOX_HEADS), F32)],
        compiler_params=_params("arbitrary", "arbitrary"),
        name="fox_proj",
    )(x, mod, gain, wq, wk, wvt, wf, bf, pq, pk, hsel)


def _swa_band_bias():
    w, group = SWA_WINDOW, SWA_Q_HEADS // SWA_KV_HEADS
    key = np.arange(2 * w)[:, None]
    qry = np.arange(group * w)[None, :] % w
    dist = (w + qry) - key
    band = (dist >= 0) & (dist < w)
    allowed = np.stack([band & (key >= w), band])
    return jnp.asarray(np.where(allowed, 0.0, NEG_BIG), F32)


def _swa_attn_kernel(sink_ref, bias_ref, q_ref, kc_ref, kp_ref, vtc_ref, vtp_ref, o_ref, st_ref):
    w = SWA_WINDOW
    group = SWA_Q_HEADS // SWA_KV_HEADS
    nq = group * w
    nb = q_ref.shape[0] // w
    lane = lax.broadcasted_iota(jnp.int32, (w, LANES), 1)
    low = lane < HEAD_DIM
    for u in range(nb):
        rows = slice(u * w, (u + 1) * w)
        for g in range(SWA_KV_HEADS):
            cols = slice(g * LANES, (g + 1) * LANES)
            slabs = []
            for hh in range(group):
                head = g * group + hh
                qs = q_ref[rows, (head // 2) * LANES:(head // 2 + 1) * LANES]
                keep = low if head % 2 == 0 else jnp.logical_not(low)
                slabs.append(jnp.where(keep, qs, jnp.zeros_like(qs)))
            q_stack = jnp.concatenate(slabs, axis=0)
            k_prev = kp_ref[:, cols] if u == 0 else kc_ref[(u - 1) * w:u * w, cols]
            k_both = jnp.concatenate([k_prev, kc_ref[rows, cols]], axis=0)
            st_ref[u * SWA_KV_HEADS + g] = _dot_nt(k_both, q_stack)
    for u in range(nb):
        rows = slice(u * w, (u + 1) * w)
        bias = bias_ref[jnp.minimum(pl.program_id(1), 1)] if u == 0 else bias_ref[1]
        for g in range(SWA_KV_HEADS):
            cols = slice(g * LANES, (g + 1) * LANES)
            st = st_ref[u * SWA_KV_HEADS + g] + bias
            sink = sink_ref[:, g * nq:(g + 1) * nq] * LOG2E
            m = jnp.maximum(jnp.max(st, axis=0, keepdims=True), sink)
            pt = jnp.exp2(st - m).astype(BF16)
            vt_prev = vtp_ref[cols, :] if u == 0 else vtc_ref[cols, (u - 1) * w:u * w]
            vt_both = jnp.concatenate([vt_prev, vtc_ref[cols, rows]], axis=1)
            acc = _dot(vt_both, pt)
            den = acc[HEAD_DIM:HEAD_DIM + 1, :] + jnp.exp2(sink - m)
            ot = acc[:HEAD_DIM, :] / den
            o_t = jnp.concatenate([ot[:, hh * w:(hh + 1) * w] for hh in range(group)], axis=0)
            o_ref[rows, g * group * HEAD_DIM:(g + 1) * group * HEAD_DIM] = o_t.T.astype(BF16)


def _swa_attn_call(q, k, vt, sinks, nb=4):
    bsz, s, d = q.shape
    w = SWA_WINDOW
    nkv = k.shape[2]
    sink_row = jnp.repeat(sinks, w).reshape(1, -1)
    bias = _swa_band_bias()
    cur = lambda b, i: (b, i, 0)
    return pl.pallas_call(
        _swa_attn_kernel,
        grid=(bsz, s // (nb * w)),
        in_specs=[
            _const_spec(sink_row.shape),
            _const_spec(bias.shape),
            pl.BlockSpec((None, nb * w, d), cur),
            pl.BlockSpec((None, nb * w, nkv), cur),
            pl.BlockSpec((None, w, nkv), lambda b, i: (b, jnp.maximum(i * nb - 1, 0), 0)),
            pl.BlockSpec((None, nkv, nb * w), lambda b, i: (b, 0, i)),
            pl.BlockSpec((None, nkv, w), lambda b, i: (b, 0, jnp.maximum(i * nb - 1, 0))),
        ],
        out_specs=pl.BlockSpec((None, nb * w, d), cur),
        out_shape=jax.ShapeDtypeStruct((bsz, s, d), BF16),
        scratch_shapes=[pltpu.VMEM((nb * SWA_KV_HEADS, 2 * w, (SWA_Q_HEADS // SWA_KV_HEADS) * w), F32)],
        compiler_params=_params("arbitrary", "arbitrary"),
        name="swa_attn",
    )(sink_row, bias, q, k, k, vt, vt)


def _chunk_cumsum(x):
    n = x.shape[0]
    row = lax.broadcasted_iota(jnp.int32, (n, n), 0)
    col = lax.broadcasted_iota(jnp.int32, (n, n), 1)
    same_chunk = (row // GLA_CHUNK) == (col // GLA_CHUNK)
    tril = jnp.where(jnp.logical_and(row >= col, same_chunk), 1.0, 0.0).astype(BF16)
    hi, mid, lo = _split3(x)
    return _dot(tril, hi) + _dot(tril, mid) + _dot(tril, lo)


def _gla_intra(q, k, b2):
    c, sub = GLA_CHUNK, GLA_SUB
    col = lax.broadcasted_iota(jnp.int32, (sub, c), 1)
    row = lax.broadcasted_iota(jnp.int32, (sub, c), 0)
    blocks = []
    for i in range(c // sub):
        lo = i * sub
        q_i = q[lo:lo + sub, :]
        b_i = b2[lo:lo + sub, :]
        if i == 0:
            a = jnp.zeros((sub, c), F32)
        else:
            ref = b2[lo - 1:lo, :]
            n = -(-lo // BF16_ROWS) * BF16_ROWS
            q_t = (q_i * jnp.exp2(b_i - ref)).astype(BF16)
            k_t = (k[:n, :] * jnp.exp2(jnp.minimum(ref - b2[:n, :], 0.0))).astype(BF16)
            if n < c:
                k_t = jnp.concatenate([k_t, jnp.zeros((c - n, k_t.shape[1]), BF16)], axis=0)
            a = _dot_nt(q_t, k_t)
        for s in range(lo, lo + sub):
            w = jnp.exp2(jnp.minimum(b_i - b2[s:s + 1, :], 0.0))
            val = jnp.sum(q_i * k[s:s + 1, :] * w, axis=1, keepdims=True)
            a = jnp.where(col == s, val, a)
        blocks.append(jnp.where(row + lo >= col, a, 0.0))
    return jnp.concatenate(blocks, axis=0)


def _gla_kernel(q_ref, k_ref, v_ref, r_ref, la_ref, hn_ref, o_ref, state_ref):
    @pl.when(pl.program_id(2) == 0)
    def _():
        state_ref[...] = jnp.zeros_like(state_ref)

    for hh in range(state_ref.shape[0]):
        kq = slice(hh * GLA_DK, (hh + 1) * GLA_DK)
        vv = slice(hh * GLA_DV, (hh + 1) * GLA_DV)
        _gla_head(q_ref.at[:, kq], k_ref.at[:, kq], v_ref.at[:, vv], r_ref.at[:, vv], la_ref.at[:, kq],
                  hn_ref, o_ref.at[:, vv], state_ref.at[hh])


def _gla_head(q_ref, k_ref, v_ref, r_ref, la_ref, hn_ref, o_ref, state_ref):
    c = GLA_CHUNK
    nc = q_ref.shape[0] // c
    rows = [slice(ci * c, (ci + 1) * c) for ci in range(nc)]
    b2_all = _chunk_cumsum(la_ref[...]) * LOG2E
    q_all = q_ref[...].astype(F32) * (GLA_DK ** -0.5)
    k_all = k_ref[...].astype(F32)
    b2 = [b2_all[r] for r in rows]
    q = [q_all[r] for r in rows]
    k = [k_all[r] for r in rows]
    last = [b[c - 1:c, :] for b in b2]
    q_in = [(q[i] * jnp.exp2(b2[i])).astype(BF16) for i in range(nc)]
    k_out = [(k[i] * jnp.exp2(last[i] - b2[i])).astype(BF16) for i in range(nc)]
    kv = [_dot_tn(v_ref[rows[i], :], k_out[i]) for i in range(nc)]
    attn = [_gla_intra(q[i], k[i], b2[i]).astype(BF16) for i in range(nc)]
    intra = [_dot(attn[i], v_ref[rows[i], :]) for i in range(nc)]
    state_t = state_ref[...]
    for i in range(nc):
        o = intra[i] + _dot_nt(q_in[i], state_t.astype(BF16))
        state_t = state_t * jnp.exp2(last[i]) + kv[i]
        r = r_ref[rows[i], :].astype(F32)
        o_ref[rows[i], :] = (_rms(o, hn_ref[...]) * _silu(r)).astype(BF16)
    state_ref[...] = state_t


def _gla_call(q, k, v, r, la, head_norm, tm=256, nh=2):
    bsz, s, _ = q.shape
    dk, dv = GLA_DK, GLA_DV
    blk = lambda b, h, i: (b, i, h)
    return pl.pallas_call(
        _gla_kernel,
        grid=(bsz, GLA_HEADS // nh, s // tm),
        in_specs=[
            pl.BlockSpec((None, tm, nh * dk), blk),
            pl.BlockSpec((None, tm, nh * dk), blk),
            pl.BlockSpec((None, tm, nh * dv), blk),
            pl.BlockSpec((None, tm, nh * dv), blk),
            pl.BlockSpec((None, tm, nh * dk), blk),
            _const_spec((1, dv)),
        ],
        out_specs=pl.BlockSpec((None, tm, nh * dv), blk),
        out_shape=jax.ShapeDtypeStruct((bsz, s, GLA_HEADS * dv), BF16),
        scratch_shapes=[pltpu.VMEM((nh, dv, dk), F32)],
        compiler_params=_params("arbitrary", "arbitrary", "arbitrary"),
        name="gla_mix",
    )(q, k, v, r, la, head_norm)


def _fox_first_live_block(stats_ref, b, first_head, nh, i):
    def live(j):
        alive = False
        for e in range(nh):
            hd = first_head + e
            upper = (FOX_BOUND_SLACK * stats_ref[b, i, Q_MAX, hd] * stats_ref[b, j, K_MAX, hd]
                     + stats_ref[b, i, LC_FIRST, hd] - stats_ref[b, j, LC_LAST, hd])
            floor = -FOX_BOUND_SLACK * stats_ref[b, i, QK_MAX, hd]
            alive = jnp.logical_or(alive, upper - floor > FOX_DEAD_LOG2)
        return alive

    def body(r, j0):
        j = i - 1 - r
        return jnp.where(live(j), j, j0)

    return lax.fori_loop(0, i, body, i)


def _fox_attn_kernel(stats_ref, q_ref, k_ref, vt_ref, o_ref, m_ref, acc_ref, sa_ref, sb_ref):
    tk = vt_ref.shape[2]
    nh = m_ref.shape[0]
    i = pl.program_id(2)
    j0 = _fox_first_live_block(stats_ref, pl.program_id(0), pl.program_id(1) * nh, nh, i)
    m_ref[...] = jnp.full_like(m_ref, NEG_BIG)
    acc_ref[...] = jnp.zeros_like(acc_ref)

    def scores(j, buf, e):
        lanes = slice(e * LANES, (e + 1) * LANES)
        kj = k_ref[pl.ds(pl.multiple_of(j * tk, tk), tk), lanes]
        buf[e] = _dot_nt(kj, q_ref[:, lanes])

    def accumulate(j, buf, e, masked):
        st = buf[e]
        if masked:
            key = lax.broadcasted_iota(jnp.int32, st.shape, 0)
            qry = lax.broadcasted_iota(jnp.int32, st.shape, 1)
            st = jnp.where(key <= qry, st, NEG_BIG)
        m_old = m_ref[e]
        m_new = jnp.maximum(m_old, jnp.max(st, axis=0, keepdims=True))
        alpha = jnp.exp2(m_old - m_new)
        pt = jnp.exp2(st - m_new).astype(BF16)
        acc_ref[e] = alpha * acc_ref[e] + _dot(vt_ref[j, e * LANES:(e + 1) * LANES, :], pt)
        m_ref[e] = m_new

    def scores_and_accumulate(j_next, buf_next, j_cur, buf_cur):
        for e in range(nh):
            scores(j_next, buf_next, e)
            accumulate(j_cur, buf_cur, e, False)

    for e in range(nh):
        scores(j0, sa_ref, e)
    n_full = i - j0

    def body(t, carry):
        j = j0 + 2 * t
        scores_and_accumulate(j + 1, sb_ref, j, sa_ref)
        scores_and_accumulate(j + 2, sa_ref, j + 1, sb_ref)
        return carry

    lax.fori_loop(0, n_full // 2, body, 0)

    @pl.when(n_full % 2 == 0)
    def _():
        for e in range(nh):
            accumulate(i, sa_ref, e, True)

    @pl.when(n_full % 2 == 1)
    def _():
        scores_and_accumulate(i, sb_ref, i - 1, sa_ref)
        for e in range(nh):
            accumulate(i, sb_ref, e, True)

    outs = []
    for e in range(nh):
        acc = acc_ref[e]
        outs.append(acc[:HEAD_DIM, :] / acc[HEAD_DIM:HEAD_DIM + 1, :])
    o_ref[...] = jnp.concatenate(outs, axis=0).T.astype(BF16)


def _fox_attn_call(q, k, vt, stats, nh=4):
    bsz, s, _ = q.shape
    tk = vt.shape[3]
    tq = tk
    return pl.pallas_call(
        _fox_attn_kernel,
        grid=(bsz, FOX_HEADS // nh, s // tq),
        in_specs=[
            pl.BlockSpec(memory_space=pltpu.SMEM),
            pl.BlockSpec((None, tq, nh * LANES), lambda b, p, i: (b, i, p)),
            pl.BlockSpec((None, s, nh * LANES), lambda b, p, i: (b, 0, p)),
            pl.BlockSpec((None, s // tk, nh * LANES, tk), lambda b, p, i: (b, 0, p, 0)),
        ],
        out_specs=pl.BlockSpec((None, tq, nh * HEAD_DIM), lambda b, p, i: (b, i, p)),
        out_shape=jax.ShapeDtypeStruct((bsz, s, D_MODEL), BF16),
        scratch_shapes=[
            pltpu.VMEM((nh, 1, tq), F32),
            pltpu.VMEM((nh, LANES, tq), F32),
            pltpu.VMEM((nh, tk, tq), F32),
            pltpu.VMEM((nh, tk, tq), F32),
        ],
        compiler_params=_params("arbitrary", "arbitrary", "arbitrary"),
        name="fox_attn",
    )(stats, q, k, vt)


def _post_kernel(x_ref, o_ref, mod_ref, gain_ref, wo_ref, wgu_ref, wd_ref, fn_ref, out_ref,
                 *, ff_chunk, final):
    x1 = x_ref[...] + mod_ref[2:3, :] * _dot(o_ref[...], wo_ref[...])
    h = _norm_mod(x1, gain_ref[...], mod_ref[3:4, :], mod_ref[4:5, :]).astype(BF16)
    acc = jnp.zeros(x1.shape, F32)
    for c0 in range(0, D_FF, ff_chunk):
        g = _dot(h, wgu_ref[:, c0:c0 + ff_chunk])
        u = _dot(h, wgu_ref[:, D_FF + c0:D_FF + c0 + ff_chunk])
        acc = acc + _dot((_silu(g) * u).astype(BF16), wd_ref[c0:c0 + ff_chunk, :])
    x2 = x1 + mod_ref[5:6, :] * acc
    if final:
        x2 = _rms(x2, fn_ref[...])
    out_ref[...] = x2


def _post_call(x, o, mod, gain, wo, wgu, wd, final_norm, final, tm=512, ff_chunk=256):
    bsz, s, d = x.shape
    row = lambda b, i: (b, i, 0)
    return pl.pallas_call(
        functools.partial(_post_kernel, ff_chunk=ff_chunk, final=final),
        grid=(bsz, s // tm),
        in_specs=[
            pl.BlockSpec((None, tm, d), row),
            pl.BlockSpec((None, tm, d), row),
            pl.BlockSpec((None, 6, d), lambda b, i: (b, 0, 0)),
            _const_spec((1, d)),
            _const_spec(wo.shape),
            _const_spec(wgu.shape),
            _const_spec(wd.shape),
            _const_spec((1, d)),
        ],
        out_specs=pl.BlockSpec((None, tm, d), row),
        out_shape=jax.ShapeDtypeStruct((bsz, s, d), F32),
        compiler_params=_params("arbitrary", "arbitrary"),
        name="post_ffn",
    )(x, o, mod, gain, wo, wgu, wd, final_norm)


def _rope_tables(s):
    half = HEAD_DIM // 2
    inv = 1.0 / (ROPE_THETA ** (jnp.arange(0, HEAD_DIM, 2, dtype=F32) / HEAD_DIM))
    ang = jnp.arange(s, dtype=F32)[:, None] * inv[None, :]
    cos, sin = jnp.cos(ang), jnp.sin(ang)
    reps = LANES // HEAD_DIM
    cos_t = jnp.tile(jnp.concatenate([cos, cos], axis=1), (1, reps))
    sin_t = jnp.tile(jnp.concatenate([-sin, sin], axis=1), (1, reps))
    assert half * 2 == HEAD_DIM
    return cos_t, sin_t


def _dup_heads(w, heads):
    w3 = w.reshape(w.shape[0], heads, HEAD_DIM)
    return jnp.concatenate([w3, w3], axis=2).reshape(w.shape[0], heads * LANES)


def _pad_heads(w, heads):
    w3 = w.reshape(w.shape[0], heads, HEAD_DIM)
    return jnp.concatenate([w3, jnp.zeros_like(w3)], axis=2).reshape(w.shape[0], heads * LANES)


def _fox_placement():
    h = FOX_HEADS
    pq = np.zeros((4 * h, h * HEAD_DIM), np.float32)
    pk = np.zeros((4 * h, h * HEAD_DIM), np.float32)
    for head in range(h):
        base = (head // 2) * LANES + (HEAD_DIM if head % 2 == 0 else 0)
        for part in range(3):
            pq[part * h + head, base + part] = 1.0
            pk[3 * h + head, base + part] = 1.0
            pq[3 * h + head, base + 3 + part] = 1.0
            pk[part * h + head, base + 3 + part] = -1.0
    return jnp.asarray(pq, BF16), jnp.asarray(pk, BF16)


def kernel(x, c, ada_w, ada_b, norm_gain, ffn_w_gu, ffn_w_down, swa_w_in, swa_sinks, swa_w_o,
           gla_w_in, gla_w_gate_up, gla_b_gate, gla_head_norm, gla_w_o, fox_w_in, fox_b_f, fox_w_o,
           final_norm):
    bsz, s, d = x.shape
    depth = ada_w.shape[0]
    mod_all = _ada_call(c, ada_w, ada_b).reshape(depth, bsz, 6, d)
    cos_t, sin_t = _rope_tables(s)
    pq, pk = _fox_placement()
    fn = final_norm.reshape(1, d)

    for i in range(depth):
        kind, j = i % N_MIXERS, i // N_MIXERS
        mod = mod_all[i]
        gain1 = norm_gain[i, 0].reshape(1, d)
        gain2 = norm_gain[i, 1].reshape(1, d)
        if kind == 0:
            w = swa_w_in[j]
            nq, nkv = SWA_Q_HEADS * HEAD_DIM, SWA_KV_HEADS * HEAD_DIM
            w_all = jnp.concatenate([w[:, :nq], _dup_heads(w[:, nq:nq + nkv], SWA_KV_HEADS)],
                                    axis=1).astype(BF16)
            wvt = _pad_heads(w[:, nq + nkv:], SWA_KV_HEADS).T.astype(BF16)
            q, k, v = _swa_proj_call(x, mod, gain1, w_all, wvt, cos_t, sin_t)
            o = _swa_attn_call(q, k, v, swa_sinks[j])
            wo = swa_w_o[j]
        elif kind == 1:
            w = gla_w_in[j]
            n_main = 2 * GLA_HEADS * GLA_DK + 2 * GLA_HEADS * GLA_DV
            q, k, v, r, la = _gla_proj_call(
                x, mod, gain1, w[:, :n_main].astype(BF16), w[:, n_main:].astype(BF16),
                gla_w_gate_up[j].astype(BF16), gla_b_gate[j].reshape(1, -1))
            o = _gla_call(q, k, v, r, la, gla_head_norm[j].reshape(1, -1))
            wo = gla_w_o[j]
        else:
            w = fox_w_in[j]
            q, k, v, stats = _fox_proj_call(
                x, mod, gain1,
                w[:, :d].astype(BF16), w[:, d:2 * d].astype(BF16),
                w[:, 2 * d:3 * d].T.astype(BF16), w[:, 3 * d:].astype(BF16),
                fox_b_f[j].reshape(1, -1), pq, pk)
            o = _fox_attn_call(q, k, v, stats)
            wo = fox_w_o[j]
        x = _post_call(x, o, mod, gain2, wo.astype(BF16), ffn_w_gu[i].astype(BF16),
                       ffn_w_down[i].astype(BF16), fn, final=(i == depth - 1))
    return x
```

```python
import functools

import numpy as np
import jax
import jax.numpy as jnp
from jax import lax
from jax.experimental import pallas as pl
from jax.experimental.pallas import tpu as pltpu

D_MODEL = 1024
HEAD_DIM = 64
RMS_EPS = 1e-6
SWA_Q_HEADS = 16
SWA_KV_HEADS = 4
SWA_WINDOW = 128
ROPE_THETA = 150000.0
GLA_HEADS = 4
GLA_DK = 128
GLA_DV = 256
GLA_RANK = 16
GLA_TAU = 16.0
GLA_CHUNK = 64
GLA_SUB = 8
FOX_HEADS = 16
FOX_STATS = 8
Q_MAX, K_MAX, QK_MAX, LC_FIRST, LC_LAST = range(5)
FOX_DEAD_LOG2 = -160.0
FOX_BOUND_SLACK = 1.02
D_FF = 2816
N_MIXERS = 3

LANES = 128
BF16_ROWS = 16
NEG_BIG = -1e30
LOG2E = 1.4426950408889634
VMEM_LIMIT = 56 * 1024 * 1024

BF16 = jnp.bfloat16
F32 = jnp.float32


def _dot(a, b):
    return jnp.dot(a, b, preferred_element_type=F32)


def _dot_nt(a, b):
    return lax.dot_general(a, b, (((1,), (1,)), ((), ())), preferred_element_type=F32)


def _dot_tn(a, b):
    return lax.dot_general(a, b, (((0,), (0,)), ((), ())), preferred_element_type=F32)


def _split3(x):
    hi = x.astype(BF16)
    r1 = x - hi.astype(F32)
    mid = r1.astype(BF16)
    lo = (r1 - mid.astype(F32)).astype(BF16)
    return hi, mid, lo


def _cumsum_rows(x):
    n = x.shape[0]
    row = lax.broadcasted_iota(jnp.int32, (n, n), 0)
    col = lax.broadcasted_iota(jnp.int32, (n, n), 1)
    tril = jnp.where(row >= col, 1.0, 0.0).astype(BF16)
    hi, mid, lo = _split3(x)
    return _dot(tril, hi) + _dot(tril, mid) + _dot(tril, lo)


def _log_sigmoid(x):
    return jnp.minimum(x, 0.0) - jnp.log(1.0 + jnp.exp(-jnp.abs(x)))


def _silu(x):
    return x * (1.0 / (1.0 + jnp.exp(-x)))


def _rms(x, gain):
    ms = jnp.mean(x * x, axis=-1, keepdims=True)
    return x * lax.rsqrt(ms + RMS_EPS) * gain


def _norm_mod(x, gain, shift, scale):
    return _rms(x, gain) * (1.0 + scale) + shift


def _params(*sem):
    return pltpu.CompilerParams(dimension_semantics=sem, vmem_limit_bytes=VMEM_LIMIT)


def _const_spec(shape):
    nd = len(shape)
    return pl.BlockSpec(shape, lambda *_: (0,) * nd, pipeline_mode=pl.Buffered(1))


def _ada_kernel(ct_ref, w_ref, b_ref, out_ref):
    ca = _silu(ct_ref[...])
    w = w_ref[...]
    for b in range(ct_ref.shape[1]):
        col = ca[:, b:b + 1]
        out_ref[b:b + 1, :] = jnp.sum(col * w, axis=0, keepdims=True) + b_ref[...]


def _ada_call(c, ada_w, ada_b):
    depth, d, n = ada_w.shape
    bsz = c.shape[0]
    tn = 768
    return pl.pallas_call(
        _ada_kernel,
        grid=(depth, n // tn),
        in_specs=[
            pl.BlockSpec((d, bsz), lambda l, j: (0, 0)),
            pl.BlockSpec((None, d, tn), lambda l, j: (l, 0, j)),
            pl.BlockSpec((None, 1, tn), lambda l, j: (l, 0, j)),
        ],
        out_specs=pl.BlockSpec((None, bsz, tn), lambda l, j: (l, 0, j)),
        out_shape=jax.ShapeDtypeStruct((depth, bsz, n), F32),
        compiler_params=_params("arbitrary", "arbitrary"),
        name="ada_mod",
    )(c.T, ada_w, ada_b.reshape(depth, 1, n))


def _rope(x, cos, sin_signed):
    width = x.shape[1]
    reps = width // cos.shape[1]
    c = jnp.tile(cos, (1, reps))
    s = jnp.tile(sin_signed, (1, reps))
    lane = lax.broadcasted_iota(jnp.int32, x.shape, 1)
    first_half = (lane % HEAD_DIM) < (HEAD_DIM // 2)
    rot = jnp.where(first_half,
                    pltpu.roll(x, width - HEAD_DIM // 2, 1),
                    pltpu.roll(x, HEAD_DIM // 2, 1))
    return x * c + rot * s


def _ones_row_64(vt):
    ones_row = lax.broadcasted_iota(jnp.int32, vt.shape, 0) % LANES == HEAD_DIM
    return jnp.where(ones_row, 1.0, vt)


def _swa_proj_kernel(x_ref, mod_ref, gain_ref, w_ref, wvt_ref, cos_ref, sin_ref, q_ref, k_ref, vt_ref):
    h = _norm_mod(x_ref[...], gain_ref[...], mod_ref[0:1, :], mod_ref[1:2, :]).astype(BF16)
    cos, sin = cos_ref[...], sin_ref[...]
    nq = q_ref.shape[1]
    q = _dot(h, w_ref[:, :nq])
    q_ref[...] = (_rope(q, cos, sin) * (HEAD_DIM ** -0.5 * LOG2E)).astype(BF16)
    k = _dot(h, w_ref[:, nq:])
    k_ref[...] = _rope(k, cos, sin).astype(BF16)
    vt_ref[...] = _ones_row_64(_dot_nt(wvt_ref[...], h)).astype(BF16)


def _swa_proj_call(x, mod, gain, w, wvt, cos, sin, tm=512):
    bsz, s, d = x.shape
    nq, nkv = D_MODEL, SWA_KV_HEADS * LANES
    row = lambda b, i: (b, i, 0)
    return pl.pallas_call(
        _swa_proj_kernel,
        grid=(bsz, s // tm),
        in_specs=[
            pl.BlockSpec((None, tm, d), row),
            pl.BlockSpec((None, 6, d), lambda b, i: (b, 0, 0)),
            _const_spec((1, d)),
            _const_spec(w.shape),
            _const_spec(wvt.shape),
            pl.BlockSpec((tm, LANES), lambda b, i: (i, 0)),
            pl.BlockSpec((tm, LANES), lambda b, i: (i, 0)),
        ],
        out_specs=[
            pl.BlockSpec((None, tm, nq), row),
            pl.BlockSpec((None, tm, nkv), row),
            pl.BlockSpec((None, nkv, tm), lambda b, i: (b, 0, i)),
        ],
        out_shape=[
            jax.ShapeDtypeStruct((bsz, s, nq), BF16),
            jax.ShapeDtypeStruct((bsz, s, nkv), BF16),
            jax.ShapeDtypeStruct((bsz, nkv, s), BF16),
        ],
        compiler_params=_params("arbitrary", "arbitrary"),
        name="swa_proj",
    )(x, mod, gain, w, wvt, cos, sin)


def _gla_proj_kernel(x_ref, mod_ref, gain_ref, w_ref, wa_ref, wg_ref, bg_ref,
                     q_ref, k_ref, v_ref, r_ref, la_ref):
    h = _norm_mod(x_ref[...], gain_ref[...], mod_ref[0:1, :], mod_ref[1:2, :]).astype(BF16)
    nk = q_ref.shape[1]
    nv = v_ref.shape[1]
    q_ref[...] = _dot(h, w_ref[:, :nk]).astype(BF16)
    k_ref[...] = _dot(h, w_ref[:, nk:2 * nk]).astype(BF16)
    v_ref[...] = _dot(h, w_ref[:, 2 * nk:2 * nk + nv]).astype(BF16)
    r_ref[...] = _dot(h, w_ref[:, 2 * nk + nv:]).astype(BF16)
    a_low = _dot(h, wa_ref[...]).astype(BF16)
    z = _dot(a_low, wg_ref[...]) + bg_ref[...]
    la_ref[...] = _log_sigmoid(z) * (1.0 / GLA_TAU)


def _gla_proj_call(x, mod, gain, w, wa, wg, bg, tm=512):
    bsz, s, d = x.shape
    nk, nv = GLA_HEADS * GLA_DK, GLA_HEADS * GLA_DV
    row = lambda b, i: (b, i, 0)
    return pl.pallas_call(
        _gla_proj_kernel,
        grid=(bsz, s // tm),
        in_specs=[
            pl.BlockSpec((None, tm, d), row),
            pl.BlockSpec((None, 6, d), lambda b, i: (b, 0, 0)),
            _const_spec((1, d)),
            _const_spec(w.shape),
            _const_spec(wa.shape),
            _const_spec(wg.shape),
            _const_spec(bg.shape),
        ],
        out_specs=[
            pl.BlockSpec((None, tm, nk), row),
            pl.BlockSpec((None, tm, nk), row),
            pl.BlockSpec((None, tm, nv), row),
            pl.BlockSpec((None, tm, nv), row),
            pl.BlockSpec((None, tm, nk), row),
        ],
        out_shape=[
            jax.ShapeDtypeStruct((bsz, s, nk), BF16),
            jax.ShapeDtypeStruct((bsz, s, nk), BF16),
            jax.ShapeDtypeStruct((bsz, s, nv), BF16),
            jax.ShapeDtypeStruct((bsz, s, nv), BF16),
            jax.ShapeDtypeStruct((bsz, s, nk), F32),
        ],
        compiler_params=_params("arbitrary", "arbitrary"),
        name="gla_proj",
    )(x, mod, gain, w, wa, wg, bg)


def _spread_heads(x, extra, out_ref):
    lane = lax.broadcasted_iota(jnp.int32, (x.shape[0], LANES), 1)
    low = lane < HEAD_DIM
    for p in range(x.shape[1] // LANES):
        xs = x[:, p * LANES:(p + 1) * LANES]
        ex = extra[:, p * LANES:(p + 1) * LANES]
        out_ref[:, (2 * p) * LANES:(2 * p + 1) * LANES] = jnp.where(low, xs, ex).astype(out_ref.dtype)
        odd = pltpu.roll(jnp.where(low, ex, xs), HEAD_DIM, 1)
        out_ref[:, (2 * p + 1) * LANES:(2 * p + 2) * LANES] = odd.astype(out_ref.dtype)


def _fox_proj_kernel(x_ref, mod_ref, gain_ref, wq_ref, wk_ref, wvt_ref, wf_ref, bf_ref,
                     pq_ref, pk_ref, hsel_ref, q_ref, k_ref, vt_ref, stats_ref, carry_ref):
    @pl.when(pl.program_id(1) == 0)
    def _():
        carry_ref[...] = jnp.zeros_like(carry_ref)

    h = _norm_mod(x_ref[...], gain_ref[...], mod_ref[0:1, :], mod_ref[1:2, :]).astype(BF16)
    log_f = _log_sigmoid(_dot(h, wf_ref[...]) + bf_ref[...])
    lc = _cumsum_rows(log_f) + carry_ref[...]
    carry_ref[...] = lc[lc.shape[0] - 1:, :]
    lc2 = lc * LOG2E
    hi, mid, lo = _split3(lc2)
    aug = jnp.concatenate([hi, mid, lo, jnp.ones_like(hi)], axis=1)
    qs = _dot(h, wq_ref[...]) * (HEAD_DIM ** -0.5 * LOG2E)
    ks = _dot(h, wk_ref[...])
    _spread_heads(qs, _dot(aug, pq_ref[...]), q_ref)
    _spread_heads(ks, _dot(aug, pk_ref[...]), k_ref)
    qn2 = _dot((qs * qs).astype(BF16), hsel_ref[...])
    kn2 = _dot((ks * ks).astype(BF16), hsel_ref[...])
    tm = lc2.shape[0]
    stats_ref[...] = jnp.concatenate([
        jnp.sqrt(jnp.max(qn2, axis=0, keepdims=True)),
        jnp.sqrt(jnp.max(kn2, axis=0, keepdims=True)),
        jnp.sqrt(jnp.max(qn2 * kn2, axis=0, keepdims=True)),
        lc2[0:1, :], lc2[tm - 1:tm, :],
        jnp.zeros((FOX_STATS - 5, lc2.shape[1]), F32)], axis=0)
    vt = _dot_nt(wvt_ref[...], h).astype(BF16)
    pad = jnp.where(lax.broadcasted_iota(jnp.int32, (HEAD_DIM, vt.shape[1]), 0) == 0,
                    1.0, 0.0).astype(BF16)
    for hd in range(FOX_HEADS):
        vt_ref[hd * LANES:hd * LANES + HEAD_DIM, :] = vt[hd * HEAD_DIM:(hd + 1) * HEAD_DIM, :]
        vt_ref[hd * LANES + HEAD_DIM:(hd + 1) * LANES, :] = pad


def _fox_proj_call(x, mod, gain, wq, wk, wvt, wf, bf, pq, pk, tm=512):
    bsz, s, d = x.shape
    nqk = FOX_HEADS * LANES
    row = lambda b, i: (b, i, 0)
    hsel = jnp.asarray(np.repeat(np.eye(FOX_HEADS, dtype=np.float32), HEAD_DIM, axis=0), BF16)
    return pl.pallas_call(
        _fox_proj_kernel,
        grid=(bsz, s // tm),
        in_specs=[
            pl.BlockSpec((None, tm, d), row),
            pl.BlockSpec((None, 6, d), lambda b, i: (b, 0, 0)),
            _const_spec((1, d)),
            _const_spec(wq.shape),
            _const_spec(wk.shape),
            _const_spec(wvt.shape),
            _const_spec(wf.shape),
            _const_spec(bf.shape),
            _const_spec(pq.shape),
            _const_spec(pk.shape),
            _const_spec(hsel.shape),
        ],
        out_specs=[
            pl.BlockSpec((None, tm, nqk), row),
            pl.BlockSpec((None, tm, nqk), row),
            pl.BlockSpec((None, None, nqk, tm), lambda b, i: (b, i, 0, 0)),
            pl.BlockSpec((None, None, FOX_STATS, FOX_HEADS), lambda b, i: (b, i, 0, 0)),
        ],
        out_shape=[
            jax.ShapeDtypeStruct((bsz, s, nqk), BF16),
            jax.ShapeDtypeStruct((bsz, s, nqk), BF16),
            jax.ShapeDtypeStruct((bsz, s // tm, nqk, tm), BF16),
            jax.ShapeDtypeStruct((bsz, s // tm, FOX_STATS, FOX_HEADS), F32),
        ],
        scratch_shapes=[pltpu.VMEM((1, FOX_HEADS), F32)],
        compiler_params=_params("arbitrary", "arbitrary"),
        name="fox_proj",
    )(x, mod, gain, wq, wk, wvt, wf, bf, pq, pk, hsel)


def _swa_band_bias():
    w, group = SWA_WINDOW, SWA_Q_HEADS // SWA_KV_HEADS
    key = np.arange(2 * w)[:, None]
    qry = np.arange(group * w)[None, :] % w
    dist = (w + qry) - key
    band = (dist >= 0) & (dist < w)
    allowed = np.stack([band & (key >= w), band])
    return jnp.asarray(np.where(allowed, 0.0, NEG_BIG), F32)


def _swa_attn_kernel(sink_ref, bias_ref, q_ref, kc_ref, vtc_ref, o_ref, st_ref, kp_ref, vtp_ref):
    w = SWA_WINDOW
    group = SWA_Q_HEADS // SWA_KV_HEADS
    nq = group * w
    nb = q_ref.shape[0] // w
    lane = lax.broadcasted_iota(jnp.int32, (w, LANES), 1)
    low = lane < HEAD_DIM

    @pl.when(pl.program_id(1) == 0)
    def _():
        kp_ref[...] = jnp.zeros_like(kp_ref)
        vtp_ref[...] = jnp.zeros_like(vtp_ref)

    for u in range(nb):
        rows = slice(u * w, (u + 1) * w)
        for g in range(SWA_KV_HEADS):
            cols = slice(g * LANES, (g + 1) * LANES)
            slabs = []
            for hh in range(group):
                head = g * group + hh
                qs = q_ref[rows, (head // 2) * LANES:(head // 2 + 1) * LANES]
                keep = low if head % 2 == 0 else jnp.logical_not(low)
                slabs.append(jnp.where(keep, qs, jnp.zeros_like(qs)))
            q_stack = jnp.concatenate(slabs, axis=0)
            k_prev = kp_ref[:, cols] if u == 0 else kc_ref[(u - 1) * w:u * w, cols]
            k_both = jnp.concatenate([k_prev, kc_ref[rows, cols]], axis=0)
            st_ref[u * SWA_KV_HEADS + g] = _dot_nt(k_both, q_stack)
    for u in range(nb):
        rows = slice(u * w, (u + 1) * w)
        bias = bias_ref[jnp.minimum(pl.program_id(1), 1)] if u == 0 else bias_ref[1]
        for g in range(SWA_KV_HEADS):
            cols = slice(g * LANES, (g + 1) * LANES)
            st = st_ref[u * SWA_KV_HEADS + g] + bias
            sink = sink_ref[:, g * nq:(g + 1) * nq] * LOG2E
            m = jnp.maximum(jnp.max(st, axis=0, keepdims=True), sink)
            pt = jnp.exp2(st - m).astype(BF16)
            vt_prev = vtp_ref[cols, :] if u == 0 else vtc_ref[cols, (u - 1) * w:u * w]
            vt_both = jnp.concatenate([vt_prev, vtc_ref[cols, rows]], axis=1)
            acc = _dot(vt_both, pt)
            den = acc[HEAD_DIM:HEAD_DIM + 1, :] + jnp.exp2(sink - m)
            ot = acc[:HEAD_DIM, :] / den
            o_t = jnp.concatenate([ot[:, hh * w:(hh + 1) * w] for hh in range(group)], axis=0)
            o_ref[rows, g * group * HEAD_DIM:(g + 1) * group * HEAD_DIM] = o_t.T.astype(BF16)
    kp_ref[...] = kc_ref[(nb - 1) * w:, :]
    vtp_ref[...] = vtc_ref[:, (nb - 1) * w:]


def _swa_attn_call(q, k, vt, sinks, nb=4):
    bsz, s, d = q.shape
    w = SWA_WINDOW
    nkv = k.shape[2]
    sink_row = jnp.repeat(sinks, w).reshape(1, -1)
    bias = _swa_band_bias()
    cur = lambda b, i: (b, i, 0)
    return pl.pallas_call(
        _swa_attn_kernel,
        grid=(bsz, s // (nb * w)),
        in_specs=[
            _const_spec(sink_row.shape),
            _const_spec(bias.shape),
            pl.BlockSpec((None, nb * w, d), cur),
            pl.BlockSpec((None, nb * w, nkv), cur),
            pl.BlockSpec((None, nkv, nb * w), lambda b, i: (b, 0, i)),
        ],
        out_specs=pl.BlockSpec((None, nb * w, d), cur),
        out_shape=jax.ShapeDtypeStruct((bsz, s, d), BF16),
        scratch_shapes=[
            pltpu.VMEM((nb * SWA_KV_HEADS, 2 * w, (SWA_Q_HEADS // SWA_KV_HEADS) * w), F32),
            pltpu.VMEM((w, nkv), BF16),
            pltpu.VMEM((nkv, w), BF16),
        ],
        compiler_params=_params("arbitrary", "arbitrary"),
        name="swa_attn",
    )(sink_row, bias, q, k, vt)


def _chunk_cumsum(x):
    n = x.shape[0]
    row = lax.broadcasted_iota(jnp.int32, (n, n), 0)
    col = lax.broadcasted_iota(jnp.int32, (n, n), 1)
    same_chunk = (row // GLA_CHUNK) == (col // GLA_CHUNK)
    tril = jnp.where(jnp.logical_and(row >= col, same_chunk), 1.0, 0.0).astype(BF16)
    hi, mid, lo = _split3(x)
    return _dot(tril, hi) + _dot(tril, mid) + _dot(tril, lo)


def _gla_intra(q, k, b2):
    c, sub = GLA_CHUNK, GLA_SUB
    col = lax.broadcasted_iota(jnp.int32, (sub, c), 1)
    row = lax.broadcasted_iota(jnp.int32, (sub, c), 0)
    blocks = []
    for i in range(c // sub):
        lo = i * sub
        q_i = q[lo:lo + sub, :]
        b_i = b2[lo:lo + sub, :]
        if i == 0:
            a = jnp.zeros((sub, c), F32)
        else:
            ref = b2[lo - 1:lo, :]
            n = -(-lo // BF16_ROWS) * BF16_ROWS
            q_t = (q_i * jnp.exp2(b_i - ref)).astype(BF16)
            k_t = (k[:n, :] * jnp.exp2(jnp.minimum(ref - b2[:n, :], 0.0))).astype(BF16)
            if n < c:
                k_t = jnp.concatenate([k_t, jnp.zeros((c - n, k_t.shape[1]), BF16)], axis=0)
            a = _dot_nt(q_t, k_t)
        for s in range(lo, lo + sub):
            w = jnp.exp2(jnp.minimum(b_i - b2[s:s + 1, :], 0.0))
            val = jnp.sum(q_i * k[s:s + 1, :] * w, axis=1, keepdims=True)
            a = jnp.where(col == s, val, a)
        blocks.append(jnp.where(row + lo >= col, a, 0.0))
    return jnp.concatenate(blocks, axis=0)


def _gla_kernel(q_ref, k_ref, v_ref, r_ref, la_ref, hn_ref, o_ref, state_ref):
    @pl.when(pl.program_id(2) == 0)
    def _():
        state_ref[...] = jnp.zeros_like(state_ref)

    for hh in range(state_ref.shape[0]):
        kq = slice(hh * GLA_DK, (hh + 1) * GLA_DK)
        vv = slice(hh * GLA_DV, (hh + 1) * GLA_DV)
        _gla_head(q_ref.at[:, kq], k_ref.at[:, kq], v_ref.at[:, vv], r_ref.at[:, vv], la_ref.at[:, kq],
                  hn_ref, o_ref.at[:, vv], state_ref.at[hh])


def _gla_head(q_ref, k_ref, v_ref, r_ref, la_ref, hn_ref, o_ref, state_ref):
    c = GLA_CHUNK
    nc = q_ref.shape[0] // c
    rows = [slice(ci * c, (ci + 1) * c) for ci in range(nc)]
    b2_all = _chunk_cumsum(la_ref[...]) * LOG2E
    q_all = q_ref[...].astype(F32) * (GLA_DK ** -0.5)
    k_all = k_ref[...].astype(F32)
    b2 = [b2_all[r] for r in rows]
    q = [q_all[r] for r in rows]
    k = [k_all[r] for r in rows]
    last = [b[c - 1:c, :] for b in b2]
    q_in = [(q[i] * jnp.exp2(b2[i])).astype(BF16) for i in range(nc)]
    k_out = [(k[i] * jnp.exp2(last[i] - b2[i])).astype(BF16) for i in range(nc)]
    kv = [_dot_tn(v_ref[rows[i], :], k_out[i]) for i in range(nc)]
    attn = [_gla_intra(q[i], k[i], b2[i]).astype(BF16) for i in range(nc)]
    intra = [_dot(attn[i], v_ref[rows[i], :]) for i in range(nc)]
    state_t = state_ref[...]
    for i in range(nc):
        o = intra[i] + _dot_nt(q_in[i], state_t.astype(BF16))
        state_t = state_t * jnp.exp2(last[i]) + kv[i]
        r = r_ref[rows[i], :].astype(F32)
        o_ref[rows[i], :] = (_rms(o, hn_ref[...]) * _silu(r)).astype(BF16)
    state_ref[...] = state_t


def _gla_call(q, k, v, r, la, head_norm, tm=256, nh=2):
    bsz, s, _ = q.shape
    dk, dv = GLA_DK, GLA_DV
    blk = lambda b, h, i: (b, i, h)
    return pl.pallas_call(
        _gla_kernel,
        grid=(bsz, GLA_HEADS // nh, s // tm),
        in_specs=[
            pl.BlockSpec((None, tm, nh * dk), blk),
            pl.BlockSpec((None, tm, nh * dk), blk),
            pl.BlockSpec((None, tm, nh * dv), blk),
            pl.BlockSpec((None, tm, nh * dv), blk),
            pl.BlockSpec((None, tm, nh * dk), blk),
            _const_spec((1, dv)),
        ],
        out_specs=pl.BlockSpec((None, tm, nh * dv), blk),
        out_shape=jax.ShapeDtypeStruct((bsz, s, GLA_HEADS * dv), BF16),
        scratch_shapes=[pltpu.VMEM((nh, dv, dk), F32)],
        compiler_params=_params("arbitrary", "arbitrary", "arbitrary"),
        name="gla_mix",
    )(q, k, v, r, la, head_norm)


def _fox_first_live_block(stats_ref, b, first_head, nh, i):
    def live(j):
        alive = False
        for e in range(nh):
            hd = first_head + e
            upper = (FOX_BOUND_SLACK * stats_ref[b, i, Q_MAX, hd] * stats_ref[b, j, K_MAX, hd]
                     + stats_ref[b, i, LC_FIRST, hd] - stats_ref[b, j, LC_LAST, hd])
            floor = -FOX_BOUND_SLACK * stats_ref[b, i, QK_MAX, hd]
            alive = jnp.logical_or(alive, upper - floor > FOX_DEAD_LOG2)
        return alive

    def body(r, j0):
        j = i - 1 - r
        return jnp.where(live(j), j, j0)

    return lax.fori_loop(0, i, body, i)


def _fox_attn_kernel(stats_ref, causal_ref, q_ref, k_ref, vt_ref, o_ref, m_ref, acc_ref, sa_ref, sb_ref):
    tk = vt_ref.shape[2]
    nh = m_ref.shape[0]
    i = pl.program_id(2)
    j0 = _fox_first_live_block(stats_ref, pl.program_id(0), pl.program_id(1) * nh, nh, i)
    m_ref[...] = jnp.full_like(m_ref, NEG_BIG)
    acc_ref[...] = jnp.zeros_like(acc_ref)

    def scores(j, buf, e):
        lanes = slice(e * LANES, (e + 1) * LANES)
        kj = k_ref[pl.ds(pl.multiple_of(j * tk, tk), tk), lanes]
        buf[e] = _dot_nt(kj, q_ref[:, lanes])

    def accumulate(j, buf, e, masked):
        st = buf[e]
        if masked:
            st = st + causal_ref[...]
        m_old = m_ref[e]
        m_new = jnp.maximum(m_old, jnp.max(st, axis=0, keepdims=True))
        alpha = jnp.exp2(m_old - m_new)
        pt = jnp.exp2(st - m_new).astype(BF16)
        acc_ref[e] = alpha * acc_ref[e] + _dot(vt_ref[j, e * LANES:(e + 1) * LANES, :], pt)
        m_ref[e] = m_new

    def scores_and_accumulate(j_next, buf_next, j_cur, buf_cur):
        for e in range(nh):
            scores(j_next, buf_next, e)
            accumulate(j_cur, buf_cur, e, False)

    for e in range(nh):
        scores(j0, sa_ref, e)
    n_full = i - j0

    def body(t, carry):
        j = j0 + 2 * t
        scores_and_accumulate(j + 1, sb_ref, j, sa_ref)
        scores_and_accumulate(j + 2, sa_ref, j + 1, sb_ref)
        return carry

    lax.fori_loop(0, n_full // 2, body, 0)

    @pl.when(n_full % 2 == 0)
    def _():
        for e in range(nh):
            accumulate(i, sa_ref, e, True)

    @pl.when(n_full % 2 == 1)
    def _():
        scores_and_accumulate(i, sb_ref, i - 1, sa_ref)
        for e in range(nh):
            accumulate(i, sb_ref, e, True)

    outs = []
    for e in range(nh):
        acc = acc_ref[e]
        outs.append(acc[:HEAD_DIM, :] / acc[HEAD_DIM:HEAD_DIM + 1, :])
    o_ref[...] = jnp.concatenate(outs, axis=0).T.astype(BF16)


def _fox_attn_call(q, k, vt, stats, nh=4):
    bsz, s, _ = q.shape
    tk = vt.shape[3]
    tq = tk
    causal = jnp.asarray(np.where(np.arange(tk)[:, None] <= np.arange(tq)[None, :], 0.0, NEG_BIG), F32)
    return pl.pallas_call(
        _fox_attn_kernel,
        grid=(bsz, FOX_HEADS // nh, s // tq),
        in_specs=[
            pl.BlockSpec(memory_space=pltpu.SMEM),
            _const_spec(causal.shape),
            pl.BlockSpec((None, tq, nh * LANES), lambda b, p, i: (b, i, p)),
            pl.BlockSpec((None, s, nh * LANES), lambda b, p, i: (b, 0, p)),
            pl.BlockSpec((None, s // tk, nh * LANES, tk), lambda b, p, i: (b, 0, p, 0)),
        ],
        out_specs=pl.BlockSpec((None, tq, nh * HEAD_DIM), lambda b, p, i: (b, i, p)),
        out_shape=jax.ShapeDtypeStruct((bsz, s, D_MODEL), BF16),
        scratch_shapes=[
            pltpu.VMEM((nh, 1, tq), F32),
            pltpu.VMEM((nh, LANES, tq), F32),
            pltpu.VMEM((nh, tk, tq), F32),
            pltpu.VMEM((nh, tk, tq), F32),
        ],
        compiler_params=_params("arbitrary", "arbitrary", "arbitrary"),
        name="fox_attn",
    )(stats, causal, q, k, vt)


def _post_kernel(x_ref, o_ref, mod_ref, gain_ref, wo_ref, wgu_ref, wd_ref, fn_ref, out_ref,
                 *, ff_chunk, final):
    x1 = x_ref[...] + mod_ref[2:3, :] * _dot(o_ref[...], wo_ref[...])
    h = _norm_mod(x1, gain_ref[...], mod_ref[3:4, :], mod_ref[4:5, :]).astype(BF16)
    acc = jnp.zeros(x1.shape, F32)
    for c0 in range(0, D_FF, ff_chunk):
        g = _dot(h, wgu_ref[:, c0:c0 + ff_chunk])
        u = _dot(h, wgu_ref[:, D_FF + c0:D_FF + c0 + ff_chunk])
        acc = acc + _dot((_silu(g) * u).astype(BF16), wd_ref[c0:c0 + ff_chunk, :])
    x2 = x1 + mod_ref[5:6, :] * acc
    if final:
        x2 = _rms(x2, fn_ref[...])
    out_ref[...] = x2


def _layer_spec(stack, layer):
    return pl.BlockSpec((None,) + stack.shape[1:], lambda *_: (layer, 0, 0),
                        pipeline_mode=pl.Buffered(1))


def _post_call(x, o, mod, gain, wo, wgu_stack, wd_stack, layer, final_norm, final, tm=512, ff_chunk=256):
    bsz, s, d = x.shape
    row = lambda b, i: (b, i, 0)
    return pl.pallas_call(
        functools.partial(_post_kernel, ff_chunk=ff_chunk, final=final),
        grid=(bsz, s // tm),
        in_specs=[
            pl.BlockSpec((None, tm, d), row),
            pl.BlockSpec((None, tm, d), row),
            pl.BlockSpec((None, 6, d), lambda b, i: (b, 0, 0)),
            _const_spec((1, d)),
            _const_spec(wo.shape),
            _layer_spec(wgu_stack, layer),
            _layer_spec(wd_stack, layer),
            _const_spec((1, d)),
        ],
        out_specs=pl.BlockSpec((None, tm, d), row),
        out_shape=jax.ShapeDtypeStruct((bsz, s, d), F32),
        compiler_params=_params("arbitrary", "arbitrary"),
        name="post_ffn",
    )(x, o, mod, gain, wo, wgu_stack, wd_stack, final_norm)


def _rope_tables(s):
    half = HEAD_DIM // 2
    inv = 1.0 / (ROPE_THETA ** (jnp.arange(0, HEAD_DIM, 2, dtype=F32) / HEAD_DIM))
    ang = jnp.arange(s, dtype=F32)[:, None] * inv[None, :]
    cos, sin = jnp.cos(ang), jnp.sin(ang)
    reps = LANES // HEAD_DIM
    cos_t = jnp.tile(jnp.concatenate([cos, cos], axis=1), (1, reps))
    sin_t = jnp.tile(jnp.concatenate([-sin, sin], axis=1), (1, reps))
    assert half * 2 == HEAD_DIM
    return cos_t, sin_t


def _dup_heads(w, heads):
    w3 = w.reshape(w.shape[0], heads, HEAD_DIM)
    return jnp.concatenate([w3, w3], axis=2).reshape(w.shape[0], heads * LANES)


def _pad_heads(w, heads):
    w3 = w.reshape(w.shape[0], heads, HEAD_DIM)
    return jnp.concatenate([w3, jnp.zeros_like(w3)], axis=2).reshape(w.shape[0], heads * LANES)


def _fox_placement():
    h = FOX_HEADS
    pq = np.zeros((4 * h, h * HEAD_DIM), np.float32)
    pk = np.zeros((4 * h, h * HEAD_DIM), np.float32)
    for head in range(h):
        base = (head // 2) * LANES + (HEAD_DIM if head % 2 == 0 else 0)
        for part in range(3):
            pq[part * h + head, base + part] = 1.0
            pk[3 * h + head, base + part] = 1.0
            pq[3 * h + head, base + 3 + part] = 1.0
            pk[part * h + head, base + 3 + part] = -1.0
    return jnp.asarray(pq, BF16), jnp.asarray(pk, BF16)


def kernel(x, c, ada_w, ada_b, norm_gain, ffn_w_gu, ffn_w_down, swa_w_in, swa_sinks, swa_w_o,
           gla_w_in, gla_w_gate_up, gla_b_gate, gla_head_norm, gla_w_o, fox_w_in, fox_b_f, fox_w_o,
           final_norm):
    bsz, s, d = x.shape
    depth = ada_w.shape[0]
    mod_all = _ada_call(c, ada_w, ada_b).reshape(depth, bsz, 6, d)
    cos_t, sin_t = _rope_tables(s)
    pq, pk = _fox_placement()
    fn = final_norm.reshape(1, d)
    wgu_stack = ffn_w_gu.astype(BF16)
    wd_stack = ffn_w_down.astype(BF16)

    for i in range(depth):
        kind, j = i % N_MIXERS, i // N_MIXERS
        mod = mod_all[i]
        gain1 = norm_gain[i, 0].reshape(1, d)
        gain2 = norm_gain[i, 1].reshape(1, d)
        if kind == 0:
            w = swa_w_in[j]
            nq, nkv = SWA_Q_HEADS * HEAD_DIM, SWA_KV_HEADS * HEAD_DIM
            w_all = jnp.concatenate([w[:, :nq], _dup_heads(w[:, nq:nq + nkv], SWA_KV_HEADS)],
                                    axis=1).astype(BF16)
            wvt = _pad_heads(w[:, nq + nkv:], SWA_KV_HEADS).T.astype(BF16)
            q, k, v = _swa_proj_call(x, mod, gain1, w_all, wvt, cos_t, sin_t)
            o = _swa_attn_call(q, k, v, swa_sinks[j])
            wo = swa_w_o[j]
        elif kind == 1:
            w = gla_w_in[j]
            n_main = 2 * GLA_HEADS * GLA_DK + 2 * GLA_HEADS * GLA_DV
            q, k, v, r, la = _gla_proj_call(
                x, mod, gain1, w[:, :n_main].astype(BF16), w[:, n_main:].astype(BF16),
                gla_w_gate_up[j].astype(BF16), gla_b_gate[j].reshape(1, -1))
            o = _gla_call(q, k, v, r, la, gla_head_norm[j].reshape(1, -1))
            wo = gla_w_o[j]
        else:
            w = fox_w_in[j]
            q, k, v, stats = _fox_proj_call(
                x, mod, gain1,
                w[:, :d].astype(BF16), w[:, d:2 * d].astype(BF16),
                w[:, 2 * d:3 * d].T.astype(BF16), w[:, 3 * d:].astype(BF16),
                fox_b_f[j].reshape(1, -1), pq, pk)
            o = _fox_attn_call(q, k, v, stats)
            wo = fox_w_o[j]
        x = _post_call(x, o, mod, gain2, wo.astype(BF16), wgu_stack, wd_stack, i, fn,
                       final=(i == depth - 1))
    return x
```

```python
import functools

import numpy as np
import jax
import jax.numpy as jnp
from jax import lax
from jax.experimental import pallas as pl
from jax.experimental.pallas import tpu as pltpu

D_MODEL = 1024
HEAD_DIM = 64
RMS_EPS = 1e-6
SWA_Q_HEADS = 16
SWA_KV_HEADS = 4
SWA_WINDOW = 128
ROPE_THETA = 150000.0
GLA_HEADS = 4
GLA_DK = 128
GLA_DV = 256
GLA_RANK = 16
GLA_TAU = 16.0
GLA_CHUNK = 64
GLA_SUB = 8
FOX_HEADS = 16
FOX_STATS = 8
Q_MAX, K_MAX, QK_MAX, LC_FIRST, LC_LAST = range(5)
FOX_DEAD_LOG2 = -160.0
FOX_BOUND_SLACK = 1.02
D_FF = 2816
N_MIXERS = 3

LANES = 128
BF16_ROWS = 16
NEG_BIG = -1e30
LOG2E = 1.4426950408889634
VMEM_LIMIT = 56 * 1024 * 1024

BF16 = jnp.bfloat16
F32 = jnp.float32


def _dot(a, b):
    return jnp.dot(a, b, preferred_element_type=F32)


def _dot_nt(a, b):
    return lax.dot_general(a, b, (((1,), (1,)), ((), ())), preferred_element_type=F32)


def _dot_tn(a, b):
    return lax.dot_general(a, b, (((0,), (0,)), ((), ())), preferred_element_type=F32)


def _split3(x):
    hi = x.astype(BF16)
    r1 = x - hi.astype(F32)
    mid = r1.astype(BF16)
    lo = (r1 - mid.astype(F32)).astype(BF16)
    return hi, mid, lo


def _cumsum_rows(x):
    n = x.shape[0]
    row = lax.broadcasted_iota(jnp.int32, (n, n), 0)
    col = lax.broadcasted_iota(jnp.int32, (n, n), 1)
    tril = jnp.where(row >= col, 1.0, 0.0).astype(BF16)
    hi, mid, lo = _split3(x)
    return _dot(tril, hi) + _dot(tril, mid) + _dot(tril, lo)


def _log_sigmoid(x):
    return jnp.minimum(x, 0.0) - jnp.log(1.0 + jnp.exp(-jnp.abs(x)))


def _silu(x):
    return x * (1.0 / (1.0 + jnp.exp(-x)))


def _rms(x, gain):
    ms = jnp.mean(x * x, axis=-1, keepdims=True)
    return x * lax.rsqrt(ms + RMS_EPS) * gain


def _norm_mod(x, gain, shift, scale):
    return _rms(x, gain) * (1.0 + scale) + shift


def _params(*sem):
    return pltpu.CompilerParams(dimension_semantics=sem, vmem_limit_bytes=VMEM_LIMIT)


def _const_spec(shape):
    nd = len(shape)
    return pl.BlockSpec(shape, lambda *_: (0,) * nd, pipeline_mode=pl.Buffered(1))


def _ada_kernel(ct_ref, w_ref, b_ref, out_ref):
    ca = _silu(ct_ref[...])
    w = w_ref[...]
    for b in range(ct_ref.shape[1]):
        col = ca[:, b:b + 1]
        out_ref[b:b + 1, :] = jnp.sum(col * w, axis=0, keepdims=True) + b_ref[...]


def _ada_call(c, ada_w, ada_b):
    depth, d, n = ada_w.shape
    bsz = c.shape[0]
    tn = 768
    return pl.pallas_call(
        _ada_kernel,
        grid=(depth, n // tn),
        in_specs=[
            pl.BlockSpec((d, bsz), lambda l, j: (0, 0)),
            pl.BlockSpec((None, d, tn), lambda l, j: (l, 0, j)),
            pl.BlockSpec((None, 1, tn), lambda l, j: (l, 0, j)),
        ],
        out_specs=pl.BlockSpec((None, bsz, tn), lambda l, j: (l, 0, j)),
        out_shape=jax.ShapeDtypeStruct((depth, bsz, n), F32),
        compiler_params=_params("arbitrary", "arbitrary"),
        name="ada_mod",
    )(c.T, ada_w, ada_b.reshape(depth, 1, n))


def _rope(x, cos, sin_signed):
    width = x.shape[1]
    reps = width // cos.shape[1]
    c = jnp.tile(cos, (1, reps))
    s = jnp.tile(sin_signed, (1, reps))
    lane = lax.broadcasted_iota(jnp.int32, x.shape, 1)
    first_half = (lane % HEAD_DIM) < (HEAD_DIM // 2)
    rot = jnp.where(first_half,
                    pltpu.roll(x, width - HEAD_DIM // 2, 1),
                    pltpu.roll(x, HEAD_DIM // 2, 1))
    return x * c + rot * s


def _ones_row_64(vt):
    ones_row = lax.broadcasted_iota(jnp.int32, vt.shape, 0) % LANES == HEAD_DIM
    return jnp.where(ones_row, 1.0, vt)


def _swa_proj_kernel(x_ref, mod_ref, gain_ref, w_ref, wvt_ref, cos_ref, sin_ref, q_ref, k_ref, vt_ref):
    h = _norm_mod(x_ref[...], gain_ref[...], mod_ref[0:1, :], mod_ref[1:2, :]).astype(BF16)
    cos, sin = cos_ref[...], sin_ref[...]
    nq = q_ref.shape[1]
    q = _dot(h, w_ref[:, :nq])
    q_ref[...] = (_rope(q, cos, sin) * (HEAD_DIM ** -0.5 * LOG2E)).astype(BF16)
    k = _dot(h, w_ref[:, nq:])
    k_ref[...] = _rope(k, cos, sin).astype(BF16)
    vt_ref[...] = _ones_row_64(_dot_nt(wvt_ref[...], h)).astype(BF16)


def _swa_proj_call(x, mod, gain, w, wvt, cos, sin, tm=512):
    bsz, s, d = x.shape
    nq, nkv = D_MODEL, SWA_KV_HEADS * LANES
    row = lambda b, i: (b, i, 0)
    return pl.pallas_call(
        _swa_proj_kernel,
        grid=(bsz, s // tm),
        in_specs=[
            pl.BlockSpec((None, tm, d), row),
            pl.BlockSpec((None, 6, d), lambda b, i: (b, 0, 0)),
            _const_spec((1, d)),
            _const_spec(w.shape),
            _const_spec(wvt.shape),
            pl.BlockSpec((tm, LANES), lambda b, i: (i, 0)),
            pl.BlockSpec((tm, LANES), lambda b, i: (i, 0)),
        ],
        out_specs=[
            pl.BlockSpec((None, tm, nq), row),
            pl.BlockSpec((None, tm, nkv), row),
            pl.BlockSpec((None, nkv, tm), lambda b, i: (b, 0, i)),
        ],
        out_shape=[
            jax.ShapeDtypeStruct((bsz, s, nq), BF16),
            jax.ShapeDtypeStruct((bsz, s, nkv), BF16),
            jax.ShapeDtypeStruct((bsz, nkv, s), BF16),
        ],
        compiler_params=_params("arbitrary", "arbitrary"),
        name="swa_proj",
    )(x, mod, gain, w, wvt, cos, sin)


def _gla_proj_kernel(x_ref, mod_ref, gain_ref, w_ref, wa_ref, wg_ref, bg_ref,
                     q_ref, k_ref, v_ref, r_ref, la_ref):
    h = _norm_mod(x_ref[...], gain_ref[...], mod_ref[0:1, :], mod_ref[1:2, :]).astype(BF16)
    nk = q_ref.shape[1]
    nv = v_ref.shape[1]
    q_ref[...] = _dot(h, w_ref[:, :nk]).astype(BF16)
    k_ref[...] = _dot(h, w_ref[:, nk:2 * nk]).astype(BF16)
    v_ref[...] = _dot(h, w_ref[:, 2 * nk:2 * nk + nv]).astype(BF16)
    r_ref[...] = _dot(h, w_ref[:, 2 * nk + nv:]).astype(BF16)
    a_low = _dot(h, wa_ref[...]).astype(BF16)
    z = _dot(a_low, wg_ref[...]) + bg_ref[...]
    la_ref[...] = _log_sigmoid(z) * (1.0 / GLA_TAU)


def _gla_proj_call(x, mod, gain, w, wa, wg, bg, tm=512):
    bsz, s, d = x.shape
    nk, nv = GLA_HEADS * GLA_DK, GLA_HEADS * GLA_DV
    row = lambda b, i: (b, i, 0)
    return pl.pallas_call(
        _gla_proj_kernel,
        grid=(bsz, s // tm),
        in_specs=[
            pl.BlockSpec((None, tm, d), row),
            pl.BlockSpec((None, 6, d), lambda b, i: (b, 0, 0)),
            _const_spec((1, d)),
            _const_spec(w.shape),
            _const_spec(wa.shape),
            _const_spec(wg.shape),
            _const_spec(bg.shape),
        ],
        out_specs=[
            pl.BlockSpec((None, tm, nk), row),
            pl.BlockSpec((None, tm, nk), row),
            pl.BlockSpec((None, tm, nv), row),
            pl.BlockSpec((None, tm, nv), row),
            pl.BlockSpec((None, tm, nk), row),
        ],
        out_shape=[
            jax.ShapeDtypeStruct((bsz, s, nk), BF16),
            jax.ShapeDtypeStruct((bsz, s, nk), BF16),
            jax.ShapeDtypeStruct((bsz, s, nv), BF16),
            jax.ShapeDtypeStruct((bsz, s, nv), BF16),
            jax.ShapeDtypeStruct((bsz, s, nk), F32),
        ],
        compiler_params=_params("arbitrary", "arbitrary"),
        name="gla_proj",
    )(x, mod, gain, w, wa, wg, bg)


def _spread_heads(x, extra, out_ref):
    lane = lax.broadcasted_iota(jnp.int32, (x.shape[0], LANES), 1)
    low = lane < HEAD_DIM
    for p in range(x.shape[1] // LANES):
        xs = x[:, p * LANES:(p + 1) * LANES]
        ex = extra[:, p * LANES:(p + 1) * LANES]
        out_ref[:, (2 * p) * LANES:(2 * p + 1) * LANES] = jnp.where(low, xs, ex).astype(out_ref.dtype)
        odd = pltpu.roll(jnp.where(low, ex, xs), HEAD_DIM, 1)
        out_ref[:, (2 * p + 1) * LANES:(2 * p + 2) * LANES] = odd.astype(out_ref.dtype)


def _fox_proj_kernel(x_ref, mod_ref, gain_ref, wq_ref, wk_ref, wvt_ref, wf_ref, bf_ref,
                     pq_ref, pk_ref, hsel_ref, q_ref, k_ref, vt_ref, stats_ref, carry_ref):
    @pl.when(pl.program_id(1) == 0)
    def _():
        carry_ref[...] = jnp.zeros_like(carry_ref)

    h = _norm_mod(x_ref[...], gain_ref[...], mod_ref[0:1, :], mod_ref[1:2, :]).astype(BF16)
    log_f = _log_sigmoid(_dot(h, wf_ref[...]) + bf_ref[...])
    lc = _cumsum_rows(log_f) + carry_ref[...]
    carry_ref[...] = lc[lc.shape[0] - 1:, :]
    lc2 = lc * LOG2E
    hi, mid, lo = _split3(lc2)
    aug = jnp.concatenate([hi, mid, lo, jnp.ones_like(hi)], axis=1)
    qs = _dot(h, wq_ref[...]) * (HEAD_DIM ** -0.5 * LOG2E)
    ks = _dot(h, wk_ref[...])
    _spread_heads(qs, _dot(aug, pq_ref[...]), q_ref)
    _spread_heads(ks, _dot(aug, pk_ref[...]), k_ref)
    qn2 = _dot((qs * qs).astype(BF16), hsel_ref[...])
    kn2 = _dot((ks * ks).astype(BF16), hsel_ref[...])
    tm = lc2.shape[0]
    stats_ref[...] = jnp.concatenate([
        jnp.sqrt(jnp.max(qn2, axis=0, keepdims=True)),
        jnp.sqrt(jnp.max(kn2, axis=0, keepdims=True)),
        jnp.sqrt(jnp.max(qn2 * kn2, axis=0, keepdims=True)),
        lc2[0:1, :], lc2[tm - 1:tm, :],
        jnp.zeros((FOX_STATS - 5, lc2.shape[1]), F32)], axis=0)
    vt = _dot_nt(wvt_ref[...], h).astype(BF16)
    pad = jnp.where(lax.broadcasted_iota(jnp.int32, (HEAD_DIM, vt.shape[1]), 0) == 0,
                    1.0, 0.0).astype(BF16)
    for hd in range(FOX_HEADS):
        vt_ref[hd * LANES:hd * LANES + HEAD_DIM, :] = vt[hd * HEAD_DIM:(hd + 1) * HEAD_DIM, :]
        vt_ref[hd * LANES + HEAD_DIM:(hd + 1) * LANES, :] = pad


def _fox_proj_call(x, mod, gain, wq, wk, wvt, wf, bf, pq, pk, tm=512):
    bsz, s, d = x.shape
    nqk = FOX_HEADS * LANES
    row = lambda b, i: (b, i, 0)
    hsel = jnp.asarray(np.repeat(np.eye(FOX_HEADS, dtype=np.float32), HEAD_DIM, axis=0), BF16)
    return pl.pallas_call(
        _fox_proj_kernel,
        grid=(bsz, s // tm),
        in_specs=[
            pl.BlockSpec((None, tm, d), row),
            pl.BlockSpec((None, 6, d), lambda b, i: (b, 0, 0)),
            _const_spec((1, d)),
            _const_spec(wq.shape),
            _const_spec(wk.shape),
            _const_spec(wvt.shape),
            _const_spec(wf.shape),
            _const_spec(bf.shape),
            _const_spec(pq.shape),
            _const_spec(pk.shape),
            _const_spec(hsel.shape),
        ],
        out_specs=[
            pl.BlockSpec((None, tm, nqk), row),
            pl.BlockSpec((None, tm, nqk), row),
            pl.BlockSpec((None, None, nqk, tm), lambda b, i: (b, i, 0, 0)),
            pl.BlockSpec((None, None, FOX_STATS, FOX_HEADS), lambda b, i: (b, i, 0, 0)),
        ],
        out_shape=[
            jax.ShapeDtypeStruct((bsz, s, nqk), BF16),
            jax.ShapeDtypeStruct((bsz, s, nqk), BF16),
            jax.ShapeDtypeStruct((bsz, s // tm, nqk, tm), BF16),
            jax.ShapeDtypeStruct((bsz, s // tm, FOX_STATS, FOX_HEADS), F32),
        ],
        scratch_shapes=[pltpu.VMEM((1, FOX_HEADS), F32)],
        compiler_params=_params("arbitrary", "arbitrary"),
        name="fox_proj",
    )(x, mod, gain, wq, wk, wvt, wf, bf, pq, pk, hsel)


def _swa_band_bias():
    w, group = SWA_WINDOW, SWA_Q_HEADS // SWA_KV_HEADS
    key = np.arange(2 * w)[:, None]
    qry = np.arange(group * w)[None, :] % w
    dist = (w + qry) - key
    band = (dist >= 0) & (dist < w)
    allowed = np.stack([band & (key >= w), band])
    return jnp.asarray(np.where(allowed, 0.0, NEG_BIG), F32)


def _swa_attn_kernel(sink_ref, bias_ref, q_ref, kc_ref, vtc_ref, o_ref, st_ref, kp_ref, vtp_ref):
    w = SWA_WINDOW
    group = SWA_Q_HEADS // SWA_KV_HEADS
    nq = group * w
    nb = q_ref.shape[0] // w
    lane = lax.broadcasted_iota(jnp.int32, (w, LANES), 1)
    low = lane < HEAD_DIM

    @pl.when(pl.program_id(1) == 0)
    def _():
        kp_ref[...] = jnp.zeros_like(kp_ref)
        vtp_ref[...] = jnp.zeros_like(vtp_ref)

    for u in range(nb):
        rows = slice(u * w, (u + 1) * w)
        for g in range(SWA_KV_HEADS):
            cols = slice(g * LANES, (g + 1) * LANES)
            slabs = []
            for hh in range(group):
                head = g * group + hh
                qs = q_ref[rows, (head // 2) * LANES:(head // 2 + 1) * LANES]
                keep = low if head % 2 == 0 else jnp.logical_not(low)
                slabs.append(jnp.where(keep, qs, jnp.zeros_like(qs)))
            q_stack = jnp.concatenate(slabs, axis=0)
            k_prev = kp_ref[:, cols] if u == 0 else kc_ref[(u - 1) * w:u * w, cols]
            k_both = jnp.concatenate([k_prev, kc_ref[rows, cols]], axis=0)
            st_ref[u * SWA_KV_HEADS + g] = _dot_nt(k_both, q_stack)
    for u in range(nb):
        rows = slice(u * w, (u + 1) * w)
        bias = bias_ref[jnp.minimum(pl.program_id(1), 1)] if u == 0 else bias_ref[1]
        for g in range(SWA_KV_HEADS):
            cols = slice(g * LANES, (g + 1) * LANES)
            st = st_ref[u * SWA_KV_HEADS + g] + bias
            sink = sink_ref[:, g * nq:(g + 1) * nq] * LOG2E
            m = jnp.maximum(jnp.max(st, axis=0, keepdims=True), sink)
            pt = jnp.exp2(st - m).astype(BF16)
            vt_prev = vtp_ref[cols, :] if u == 0 else vtc_ref[cols, (u - 1) * w:u * w]
            vt_both = jnp.concatenate([vt_prev, vtc_ref[cols, rows]], axis=1)
            acc = _dot(vt_both, pt)
            den = acc[HEAD_DIM:HEAD_DIM + 1, :] + jnp.exp2(sink - m)
            ot = acc[:HEAD_DIM, :] / den
            o_t = jnp.concatenate([ot[:, hh * w:(hh + 1) * w] for hh in range(group)], axis=0)
            o_ref[rows, g * group * HEAD_DIM:(g + 1) * group * HEAD_DIM] = o_t.T.astype(BF16)
    kp_ref[...] = kc_ref[(nb - 1) * w:, :]
    vtp_ref[...] = vtc_ref[:, (nb - 1) * w:]


def _swa_attn_call(q, k, vt, sinks, nb=4):
    bsz, s, d = q.shape
    w = SWA_WINDOW
    nkv = k.shape[2]
    sink_row = jnp.repeat(sinks, w).reshape(1, -1)
    bias = _swa_band_bias()
    cur = lambda b, i: (b, i, 0)
    return pl.pallas_call(
        _swa_attn_kernel,
        grid=(bsz, s // (nb * w)),
        in_specs=[
            _const_spec(sink_row.shape),
            _const_spec(bias.shape),
            pl.BlockSpec((None, nb * w, d), cur),
            pl.BlockSpec((None, nb * w, nkv), cur),
            pl.BlockSpec((None, nkv, nb * w), lambda b, i: (b, 0, i)),
        ],
        out_specs=pl.BlockSpec((None, nb * w, d), cur),
        out_shape=jax.ShapeDtypeStruct((bsz, s, d), BF16),
        scratch_shapes=[
            pltpu.VMEM((nb * SWA_KV_HEADS, 2 * w, (SWA_Q_HEADS // SWA_KV_HEADS) * w), F32),
            pltpu.VMEM((w, nkv), BF16),
            pltpu.VMEM((nkv, w), BF16),
        ],
        compiler_params=_params("arbitrary", "arbitrary"),
        name="swa_attn",
    )(sink_row, bias, q, k, vt)


def _chunk_cumsum(x):
    n = x.shape[0]
    row = lax.broadcasted_iota(jnp.int32, (n, n), 0)
    col = lax.broadcasted_iota(jnp.int32, (n, n), 1)
    same_chunk = (row // GLA_CHUNK) == (col // GLA_CHUNK)
    tril = jnp.where(jnp.logical_and(row >= col, same_chunk), 1.0, 0.0).astype(BF16)
    hi, mid, lo = _split3(x)
    return _dot(tril, hi) + _dot(tril, mid) + _dot(tril, lo)


def _gla_intra(q, k, b2):
    c, sub = GLA_CHUNK, GLA_SUB
    col = lax.broadcasted_iota(jnp.int32, (sub, c), 1)
    row = lax.broadcasted_iota(jnp.int32, (sub, c), 0)
    blocks = []
    for i in range(c // sub):
        lo = i * sub
        q_i = q[lo:lo + sub, :]
        b_i = b2[lo:lo + sub, :]
        if i == 0:
            a = jnp.zeros((sub, c), F32)
        else:
            ref = b2[lo - 1:lo, :]
            n = -(-lo // BF16_ROWS) * BF16_ROWS
            q_t = (q_i * jnp.exp2(b_i - ref)).astype(BF16)
            k_t = (k[:n, :] * jnp.exp2(jnp.minimum(ref - b2[:n, :], 0.0))).astype(BF16)
            if n < c:
                k_t = jnp.concatenate([k_t, jnp.zeros((c - n, k_t.shape[1]), BF16)], axis=0)
            a = _dot_nt(q_t, k_t)
        for s in range(lo, lo + sub):
            w = jnp.exp2(jnp.minimum(b_i - b2[s:s + 1, :], 0.0))
            val = jnp.sum(q_i * k[s:s + 1, :] * w, axis=1, keepdims=True)
            a = jnp.where(col == s, val, a)
        blocks.append(jnp.where(row + lo >= col, a, 0.0))
    return jnp.concatenate(blocks, axis=0)


def _gla_kernel(q_ref, k_ref, v_ref, r_ref, la_ref, hn_ref, o_ref, state_ref):
    @pl.when(pl.program_id(2) == 0)
    def _():
        state_ref[...] = jnp.zeros_like(state_ref)

    for hh in range(state_ref.shape[0]):
        kq = slice(hh * GLA_DK, (hh + 1) * GLA_DK)
        vv = slice(hh * GLA_DV, (hh + 1) * GLA_DV)
        _gla_head(q_ref.at[:, kq], k_ref.at[:, kq], v_ref.at[:, vv], r_ref.at[:, vv], la_ref.at[:, kq],
                  hn_ref, o_ref.at[:, vv], state_ref.at[hh])


def _gla_head(q_ref, k_ref, v_ref, r_ref, la_ref, hn_ref, o_ref, state_ref):
    c = GLA_CHUNK
    nc = q_ref.shape[0] // c
    rows = [slice(ci * c, (ci + 1) * c) for ci in range(nc)]
    b2_all = _chunk_cumsum(la_ref[...]) * LOG2E
    q_all = q_ref[...].astype(F32) * (GLA_DK ** -0.5)
    k_all = k_ref[...].astype(F32)
    b2 = [b2_all[r] for r in rows]
    q = [q_all[r] for r in rows]
    k = [k_all[r] for r in rows]
    last = [b[c - 1:c, :] for b in b2]
    q_in = [(q[i] * jnp.exp2(b2[i])).astype(BF16) for i in range(nc)]
    k_out = [(k[i] * jnp.exp2(last[i] - b2[i])).astype(BF16) for i in range(nc)]
    kv = [_dot_tn(v_ref[rows[i], :], k_out[i]) for i in range(nc)]
    attn = [_gla_intra(q[i], k[i], b2[i]).astype(BF16) for i in range(nc)]
    intra = [_dot(attn[i], v_ref[rows[i], :]) for i in range(nc)]
    state_t = state_ref[...]
    for i in range(nc):
        o = intra[i] + _dot_nt(q_in[i], state_t.astype(BF16))
        state_t = state_t * jnp.exp2(last[i]) + kv[i]
        r = r_ref[rows[i], :].astype(F32)
        o_ref[rows[i], :] = (_rms(o, hn_ref[...]) * _silu(r)).astype(BF16)
    state_ref[...] = state_t


def _gla_call(q, k, v, r, la, head_norm, tm=256, nh=2):
    bsz, s, _ = q.shape
    dk, dv = GLA_DK, GLA_DV
    blk = lambda b, h, i: (b, i, h)
    return pl.pallas_call(
        _gla_kernel,
        grid=(bsz, GLA_HEADS // nh, s // tm),
        in_specs=[
            pl.BlockSpec((None, tm, nh * dk), blk),
            pl.BlockSpec((None, tm, nh * dk), blk),
            pl.BlockSpec((None, tm, nh * dv), blk),
            pl.BlockSpec((None, tm, nh * dv), blk),
            pl.BlockSpec((None, tm, nh * dk), blk),
            _const_spec((1, dv)),
        ],
        out_specs=pl.BlockSpec((None, tm, nh * dv), blk),
        out_shape=jax.ShapeDtypeStruct((bsz, s, GLA_HEADS * dv), BF16),
        scratch_shapes=[pltpu.VMEM((nh, dv, dk), F32)],
        compiler_params=_params("arbitrary", "arbitrary", "arbitrary"),
        name="gla_mix",
    )(q, k, v, r, la, head_norm)


def _fox_first_live_block(stats_ref, b, first_head, nh, i):
    def live(j):
        alive = False
        for e in range(nh):
            hd = first_head + e
            upper = (FOX_BOUND_SLACK * stats_ref[b, i, Q_MAX, hd] * stats_ref[b, j, K_MAX, hd]
                     + stats_ref[b, i, LC_FIRST, hd] - stats_ref[b, j, LC_LAST, hd])
            floor = -FOX_BOUND_SLACK * stats_ref[b, i, QK_MAX, hd]
            alive = jnp.logical_or(alive, upper - floor > FOX_DEAD_LOG2)
        return alive

    def body(r, j0):
        j = i - 1 - r
        return jnp.where(live(j), j, j0)

    return lax.fori_loop(0, i, body, i)


def _fox_attn_kernel(stats_ref, q_ref, k_ref, vt_ref, o_ref, m_ref, acc_ref, sa_ref, sb_ref):
    tk = vt_ref.shape[2]
    nh = m_ref.shape[0]
    i = pl.program_id(2)
    j0 = _fox_first_live_block(stats_ref, pl.program_id(0), pl.program_id(1) * nh, nh, i)
    m_ref[...] = jnp.full_like(m_ref, NEG_BIG)
    acc_ref[...] = jnp.zeros_like(acc_ref)

    def scores(j, buf, e):
        lanes = slice(e * LANES, (e + 1) * LANES)
        kj = k_ref[pl.ds(pl.multiple_of(j * tk, tk), tk), lanes]
        buf[e] = _dot_nt(kj, q_ref[:, lanes])

    def accumulate(j, buf, e, masked):
        st = buf[e]
        if masked:
            key = lax.broadcasted_iota(jnp.int32, st.shape, 0)
            qry = lax.broadcasted_iota(jnp.int32, st.shape, 1)
            st = jnp.where(key <= qry, st, NEG_BIG)
        m_old = m_ref[e]
        m_new = jnp.maximum(m_old, jnp.max(st, axis=0, keepdims=True))
        alpha = jnp.exp2(m_old - m_new)
        pt = jnp.exp2(st - m_new).astype(BF16)
        acc_ref[e] = alpha * acc_ref[e] + _dot(vt_ref[j, e * LANES:(e + 1) * LANES, :], pt)
        m_ref[e] = m_new

    def scores_and_accumulate(j_next, buf_next, j_cur, buf_cur):
        for e in range(nh):
            scores(j_next, buf_next, e)
            accumulate(j_cur, buf_cur, e, False)

    for e in range(nh):
        scores(i, sa_ref, e)
    n_full = i - j0

    @pl.when(n_full == 0)
    def _():
        for e in range(nh):
            accumulate(i, sa_ref, e, True)

    @pl.when(n_full > 0)
    def _():
        for e in range(nh):
            scores(j0, sb_ref, e)
            accumulate(i, sa_ref, e, True)

    def body(t, carry):
        j = j0 + 2 * t
        scores_and_accumulate(j + 1, sa_ref, j, sb_ref)
        scores_and_accumulate(j + 2, sb_ref, j + 1, sa_ref)
        return carry

    n_pairs = jnp.maximum(n_full - 1, 0) // 2
    lax.fori_loop(0, n_pairs, body, 0)

    left = n_full - 2 * n_pairs

    @pl.when(left == 1)
    def _():
        for e in range(nh):
            accumulate(i - 1, sb_ref, e, False)

    @pl.when(left == 2)
    def _():
        scores_and_accumulate(i - 1, sa_ref, i - 2, sb_ref)
        for e in range(nh):
            accumulate(i - 1, sa_ref, e, False)

    outs = []
    for e in range(nh):
        acc = acc_ref[e]
        outs.append(acc[:HEAD_DIM, :] / acc[HEAD_DIM:HEAD_DIM + 1, :])
    o_ref[...] = jnp.concatenate(outs, axis=0).T.astype(BF16)


def _fox_attn_call(q, k, vt, stats, nh=4):
    bsz, s, _ = q.shape
    tk = vt.shape[3]
    tq = tk
    return pl.pallas_call(
        _fox_attn_kernel,
        grid=(bsz, FOX_HEADS // nh, s // tq),
        in_specs=[
            pl.BlockSpec(memory_space=pltpu.SMEM),
            pl.BlockSpec((None, tq, nh * LANES), lambda b, p, i: (b, i, p)),
            pl.BlockSpec((None, s, nh * LANES), lambda b, p, i: (b, 0, p)),
            pl.BlockSpec((None, s // tk, nh * LANES, tk), lambda b, p, i: (b, 0, p, 0)),
        ],
        out_specs=pl.BlockSpec((None, tq, nh * HEAD_DIM), lambda b, p, i: (b, i, p)),
        out_shape=jax.ShapeDtypeStruct((bsz, s, D_MODEL), BF16),
        scratch_shapes=[
            pltpu.VMEM((nh, 1, tq), F32),
            pltpu.VMEM((nh, LANES, tq), F32),
            pltpu.VMEM((nh, tk, tq), F32),
            pltpu.VMEM((nh, tk, tq), F32),
        ],
        compiler_params=_params("arbitrary", "arbitrary", "arbitrary"),
        name="fox_attn",
    )(stats, q, k, vt)


def _post_kernel(x_ref, o_ref, mod_ref, gain_ref, wo_ref, wgu_ref, wd_ref, fn_ref, out_ref,
                 *, ff_chunk, final):
    x1 = x_ref[...] + mod_ref[2:3, :] * _dot(o_ref[...], wo_ref[...])
    h = _norm_mod(x1, gain_ref[...], mod_ref[3:4, :], mod_ref[4:5, :]).astype(BF16)
    acc = jnp.zeros(x1.shape, F32)
    for c0 in range(0, D_FF, ff_chunk):
        g = _dot(h, wgu_ref[:, c0:c0 + ff_chunk])
        u = _dot(h, wgu_ref[:, D_FF + c0:D_FF + c0 + ff_chunk])
        acc = acc + _dot((_silu(g) * u).astype(BF16), wd_ref[c0:c0 + ff_chunk, :])
    x2 = x1 + mod_ref[5:6, :] * acc
    if final:
        x2 = _rms(x2, fn_ref[...])
    out_ref[...] = x2


def _layer_spec(stack, layer):
    return pl.BlockSpec((None,) + stack.shape[1:], lambda *_: (layer, 0, 0),
                        pipeline_mode=pl.Buffered(1))


def _post_call(x, o, mod, gain, wo, wgu_stack, wd_stack, layer, final_norm, final, tm=1024, ff_chunk=256):
    bsz, s, d = x.shape
    row = lambda b, i: (b, i, 0)
    return pl.pallas_call(
        functools.partial(_post_kernel, ff_chunk=ff_chunk, final=final),
        grid=(bsz, s // tm),
        in_specs=[
            pl.BlockSpec((None, tm, d), row),
            pl.BlockSpec((None, tm, d), row),
            pl.BlockSpec((None, 6, d), lambda b, i: (b, 0, 0)),
            _const_spec((1, d)),
            _const_spec(wo.shape),
            _layer_spec(wgu_stack, layer),
            _layer_spec(wd_stack, layer),
            _const_spec((1, d)),
        ],
        out_specs=pl.BlockSpec((None, tm, d), row),
        out_shape=jax.ShapeDtypeStruct((bsz, s, d), F32),
        compiler_params=_params("arbitrary", "arbitrary"),
        name="post_ffn",
    )(x, o, mod, gain, wo, wgu_stack, wd_stack, final_norm)


def _rope_tables(s):
    half = HEAD_DIM // 2
    inv = 1.0 / (ROPE_THETA ** (jnp.arange(0, HEAD_DIM, 2, dtype=F32) / HEAD_DIM))
    ang = jnp.arange(s, dtype=F32)[:, None] * inv[None, :]
    cos, sin = jnp.cos(ang), jnp.sin(ang)
    reps = LANES // HEAD_DIM
    cos_t = jnp.tile(jnp.concatenate([cos, cos], axis=1), (1, reps))
    sin_t = jnp.tile(jnp.concatenate([-sin, sin], axis=1), (1, reps))
    assert half * 2 == HEAD_DIM
    return cos_t, sin_t


def _dup_heads(w, heads):
    w3 = w.reshape(w.shape[0], heads, HEAD_DIM)
    return jnp.concatenate([w3, w3], axis=2).reshape(w.shape[0], heads * LANES)


def _pad_heads(w, heads):
    w3 = w.reshape(w.shape[0], heads, HEAD_DIM)
    return jnp.concatenate([w3, jnp.zeros_like(w3)], axis=2).reshape(w.shape[0], heads * LANES)


def _fox_placement():
    h = FOX_HEADS
    pq = np.zeros((4 * h, h * HEAD_DIM), np.float32)
    pk = np.zeros((4 * h, h * HEAD_DIM), np.float32)
    for head in range(h):
        base = (head // 2) * LANES + (HEAD_DIM if head % 2 == 0 else 0)
        for part in range(3):
            pq[part * h + head, base + part] = 1.0
            pk[3 * h + head, base + part] = 1.0
            pq[3 * h + head, base + 3 + part] = 1.0
            pk[part * h + head, base + 3 + part] = -1.0
    return jnp.asarray(pq, BF16), jnp.asarray(pk, BF16)


def kernel(x, c, ada_w, ada_b, norm_gain, ffn_w_gu, ffn_w_down, swa_w_in, swa_sinks, swa_w_o,
           gla_w_in, gla_w_gate_up, gla_b_gate, gla_head_norm, gla_w_o, fox_w_in, fox_b_f, fox_w_o,
           final_norm):
    bsz, s, d = x.shape
    depth = ada_w.shape[0]
    mod_all = _ada_call(c, ada_w, ada_b).reshape(depth, bsz, 6, d)
    cos_t, sin_t = _rope_tables(s)
    pq, pk = _fox_placement()
    fn = final_norm.reshape(1, d)
    wgu_stack = ffn_w_gu.astype(BF16)
    wd_stack = ffn_w_down.astype(BF16)

    for i in range(depth):
        kind, j = i % N_MIXERS, i // N_MIXERS
        mod = mod_all[i]
        gain1 = norm_gain[i, 0].reshape(1, d)
        gain2 = norm_gain[i, 1].reshape(1, d)
        if kind == 0:
            w = swa_w_in[j]
            nq, nkv = SWA_Q_HEADS * HEAD_DIM, SWA_KV_HEADS * HEAD_DIM
            w_all = jnp.concatenate([w[:, :nq], _dup_heads(w[:, nq:nq + nkv], SWA_KV_HEADS)],
                                    axis=1).astype(BF16)
            wvt = _pad_heads(w[:, nq + nkv:], SWA_KV_HEADS).T.astype(BF16)
            q, k, v = _swa_proj_call(x, mod, gain1, w_all, wvt, cos_t, sin_t)
            o = _swa_attn_call(q, k, v, swa_sinks[j])
            wo = swa_w_o[j]
        elif kind == 1:
            w = gla_w_in[j]
            n_main = 2 * GLA_HEADS * GLA_DK + 2 * GLA_HEADS * GLA_DV
            q, k, v, r, la = _gla_proj_call(
                x, mod, gain1, w[:, :n_main].astype(BF16), w[:, n_main:].astype(BF16),
                gla_w_gate_up[j].astype(BF16), gla_b_gate[j].reshape(1, -1))
            o = _gla_call(q, k, v, r, la, gla_head_norm[j].reshape(1, -1))
            wo = gla_w_o[j]
        else:
            w = fox_w_in[j]
            q, k, v, stats = _fox_proj_call(
                x, mod, gain1,
                w[:, :d].astype(BF16), w[:, d:2 * d].astype(BF16),
                w[:, 2 * d:3 * d].T.astype(BF16), w[:, 3 * d:].astype(BF16),
                fox_b_f[j].reshape(1, -1), pq, pk)
            o = _fox_attn_call(q, k, v, stats)
            wo = fox_w_o[j]
        x = _post_call(x, o, mod, gain2, wo.astype(BF16), wgu_stack, wd_stack, i, fn,
                       final=(i == depth - 1))
    return x
```

```python
import functools

import numpy as np
import jax
import jax.numpy as jnp
from jax import lax
from jax.experimental import pallas as pl
from jax.experimental.pallas import tpu as pltpu

D_MODEL = 1024
HEAD_DIM = 64
RMS_EPS = 1e-6
SWA_Q_HEADS = 16
SWA_KV_HEADS = 4
SWA_WINDOW = 128
ROPE_THETA = 150000.0
GLA_HEADS = 4
GLA_DK = 128
GLA_DV = 256
GLA_RANK = 16
GLA_TAU = 16.0
GLA_CHUNK = 64
GLA_SUB = 8
FOX_HEADS = 16
FOX_STATS = 8
Q_MAX, K_MAX, QK_MAX, LC_FIRST, LC_LAST = range(5)
FOX_DEAD_LOG2 = -160.0
FOX_BOUND_SLACK = 1.02
D_FF = 2816
N_MIXERS = 3

LANES = 128
BF16_ROWS = 16
NEG_BIG = -1e30
LOG2E = 1.4426950408889634
VMEM_LIMIT = 56 * 1024 * 1024

BF16 = jnp.bfloat16
F32 = jnp.float32


def _dot(a, b):
    return jnp.dot(a, b, preferred_element_type=F32)


def _dot_nt(a, b):
    return lax.dot_general(a, b, (((1,), (1,)), ((), ())), preferred_element_type=F32)


def _dot_tn(a, b):
    return lax.dot_general(a, b, (((0,), (0,)), ((), ())), preferred_element_type=F32)


def _split3(x):
    hi = x.astype(BF16)
    r1 = x - hi.astype(F32)
    mid = r1.astype(BF16)
    lo = (r1 - mid.astype(F32)).astype(BF16)
    return hi, mid, lo


def _cumsum_rows(x):
    n = x.shape[0]
    row = lax.broadcasted_iota(jnp.int32, (n, n), 0)
    col = lax.broadcasted_iota(jnp.int32, (n, n), 1)
    tril = jnp.where(row >= col, 1.0, 0.0).astype(BF16)
    hi, mid, lo = _split3(x)
    return _dot(tril, hi) + _dot(tril, mid) + _dot(tril, lo)


def _log_sigmoid(x):
    return jnp.minimum(x, 0.0) - jnp.log(1.0 + jnp.exp(-jnp.abs(x)))


def _silu(x):
    return x * (1.0 / (1.0 + jnp.exp(-x)))


def _rms(x, gain):
    ms = jnp.mean(x * x, axis=-1, keepdims=True)
    return x * lax.rsqrt(ms + RMS_EPS) * gain


def _norm_mod(x, gain, shift, scale):
    return _rms(x, gain) * (1.0 + scale) + shift


def _params(*sem):
    return pltpu.CompilerParams(dimension_semantics=sem, vmem_limit_bytes=VMEM_LIMIT)


def _const_spec(shape):
    nd = len(shape)
    return pl.BlockSpec(shape, lambda *_: (0,) * nd, pipeline_mode=pl.Buffered(1))


def _ada_kernel(ct_ref, w_ref, b_ref, out_ref):
    ca = _silu(ct_ref[...])
    w = w_ref[...]
    for b in range(ct_ref.shape[1]):
        col = ca[:, b:b + 1]
        out_ref[b:b + 1, :] = jnp.sum(col * w, axis=0, keepdims=True) + b_ref[...]


def _ada_call(c, ada_w, ada_b):
    depth, d, n = ada_w.shape
    bsz = c.shape[0]
    tn = 768
    return pl.pallas_call(
        _ada_kernel,
        grid=(depth, n // tn),
        in_specs=[
            pl.BlockSpec((d, bsz), lambda l, j: (0, 0)),
            pl.BlockSpec((None, d, tn), lambda l, j: (l, 0, j)),
            pl.BlockSpec((None, 1, tn), lambda l, j: (l, 0, j)),
        ],
        out_specs=pl.BlockSpec((None, bsz, tn), lambda l, j: (l, 0, j)),
        out_shape=jax.ShapeDtypeStruct((depth, bsz, n), F32),
        compiler_params=_params("arbitrary", "arbitrary"),
        name="ada_mod",
    )(c.T, ada_w, ada_b.reshape(depth, 1, n))


def _rope(x, cos, sin_signed):
    width = x.shape[1]
    reps = width // cos.shape[1]
    c = jnp.tile(cos, (1, reps))
    s = jnp.tile(sin_signed, (1, reps))
    lane = lax.broadcasted_iota(jnp.int32, x.shape, 1)
    first_half = (lane % HEAD_DIM) < (HEAD_DIM // 2)
    rot = jnp.where(first_half,
                    pltpu.roll(x, width - HEAD_DIM // 2, 1),
                    pltpu.roll(x, HEAD_DIM // 2, 1))
    return x * c + rot * s


def _ones_row_64(vt):
    ones_row = lax.broadcasted_iota(jnp.int32, vt.shape, 0) % LANES == HEAD_DIM
    return jnp.where(ones_row, 1.0, vt)


def _swa_proj_kernel(x_ref, mod_ref, gain_ref, w_ref, wvt_ref, cos_ref, sin_ref, q_ref, k_ref, vt_ref):
    h = _norm_mod(x_ref[...], gain_ref[...], mod_ref[0:1, :], mod_ref[1:2, :]).astype(BF16)
    cos, sin = cos_ref[...], sin_ref[...]
    nq = q_ref.shape[1]
    q = _dot(h, w_ref[:, :nq])
    q_ref[...] = (_rope(q, cos, sin) * (HEAD_DIM ** -0.5 * LOG2E)).astype(BF16)
    k = _dot(h, w_ref[:, nq:])
    k_ref[...] = _rope(k, cos, sin).astype(BF16)
    vt_ref[...] = _ones_row_64(_dot_nt(wvt_ref[...], h)).astype(BF16)


def _swa_proj_call(x, mod, gain, w, wvt, cos, sin, tm=1024):
    bsz, s, d = x.shape
    nq, nkv = D_MODEL, SWA_KV_HEADS * LANES
    row = lambda b, i: (b, i, 0)
    return pl.pallas_call(
        _swa_proj_kernel,
        grid=(bsz, s // tm),
        in_specs=[
            pl.BlockSpec((None, tm, d), row),
            pl.BlockSpec((None, 6, d), lambda b, i: (b, 0, 0)),
            _const_spec((1, d)),
            _const_spec(w.shape),
            _const_spec(wvt.shape),
            pl.BlockSpec((tm, LANES), lambda b, i: (i, 0)),
            pl.BlockSpec((tm, LANES), lambda b, i: (i, 0)),
        ],
        out_specs=[
            pl.BlockSpec((None, tm, nq), row),
            pl.BlockSpec((None, tm, nkv), row),
            pl.BlockSpec((None, nkv, tm), lambda b, i: (b, 0, i)),
        ],
        out_shape=[
            jax.ShapeDtypeStruct((bsz, s, nq), BF16),
            jax.ShapeDtypeStruct((bsz, s, nkv), BF16),
            jax.ShapeDtypeStruct((bsz, nkv, s), BF16),
        ],
        compiler_params=_params("arbitrary", "arbitrary"),
        name="swa_proj",
    )(x, mod, gain, w, wvt, cos, sin)


def _gla_proj_kernel(x_ref, mod_ref, gain_ref, w_ref, wa_ref, wg_ref, bg_ref,
                     q_ref, k_ref, v_ref, r_ref, la_ref):
    h = _norm_mod(x_ref[...], gain_ref[...], mod_ref[0:1, :], mod_ref[1:2, :]).astype(BF16)
    nk = q_ref.shape[1]
    nv = v_ref.shape[1]
    q_ref[...] = _dot(h, w_ref[:, :nk]).astype(BF16)
    k_ref[...] = _dot(h, w_ref[:, nk:2 * nk]).astype(BF16)
    v_ref[...] = _dot(h, w_ref[:, 2 * nk:2 * nk + nv]).astype(BF16)
    r_ref[...] = _dot(h, w_ref[:, 2 * nk + nv:]).astype(BF16)
    a_low = _dot(h, wa_ref[...]).astype(BF16)
    z = _dot(a_low, wg_ref[...]) + bg_ref[...]
    la_ref[...] = _log_sigmoid(z) * (1.0 / GLA_TAU)


def _gla_proj_call(x, mod, gain, w, wa, wg, bg, tm=1024):
    bsz, s, d = x.shape
    nk, nv = GLA_HEADS * GLA_DK, GLA_HEADS * GLA_DV
    row = lambda b, i: (b, i, 0)
    return pl.pallas_call(
        _gla_proj_kernel,
        grid=(bsz, s // tm),
        in_specs=[
            pl.BlockSpec((None, tm, d), row),
            pl.BlockSpec((None, 6, d), lambda b, i: (b, 0, 0)),
            _const_spec((1, d)),
            _const_spec(w.shape),
            _const_spec(wa.shape),
            _const_spec(wg.shape),
            _const_spec(bg.shape),
        ],
        out_specs=[
            pl.BlockSpec((None, tm, nk), row),
            pl.BlockSpec((None, tm, nk), row),
            pl.BlockSpec((None, tm, nv), row),
            pl.BlockSpec((None, tm, nv), row),
            pl.BlockSpec((None, tm, nk), row),
        ],
        out_shape=[
            jax.ShapeDtypeStruct((bsz, s, nk), BF16),
            jax.ShapeDtypeStruct((bsz, s, nk), BF16),
            jax.ShapeDtypeStruct((bsz, s, nv), BF16),
            jax.ShapeDtypeStruct((bsz, s, nv), BF16),
            jax.ShapeDtypeStruct((bsz, s, nk), F32),
        ],
        compiler_params=_params("arbitrary", "arbitrary"),
        name="gla_proj",
    )(x, mod, gain, w, wa, wg, bg)


def _spread_heads(x, extra, out_ref):
    lane = lax.broadcasted_iota(jnp.int32, (x.shape[0], LANES), 1)
    low = lane < HEAD_DIM
    for p in range(x.shape[1] // LANES):
        xs = x[:, p * LANES:(p + 1) * LANES]
        ex = extra[:, p * LANES:(p + 1) * LANES]
        out_ref[:, (2 * p) * LANES:(2 * p + 1) * LANES] = jnp.where(low, xs, ex).astype(out_ref.dtype)
        odd = pltpu.roll(jnp.where(low, ex, xs), HEAD_DIM, 1)
        out_ref[:, (2 * p + 1) * LANES:(2 * p + 2) * LANES] = odd.astype(out_ref.dtype)


def _fox_proj_kernel(x_ref, mod_ref, gain_ref, wq_ref, wk_ref, wvt_ref, wf_ref, bf_ref,
                     pq_ref, pk_ref, hsel_ref, q_ref, k_ref, vt_ref, stats_ref, carry_ref):
    @pl.when(pl.program_id(1) == 0)
    def _():
        carry_ref[...] = jnp.zeros_like(carry_ref)

    h = _norm_mod(x_ref[...], gain_ref[...], mod_ref[0:1, :], mod_ref[1:2, :]).astype(BF16)
    log_f = _log_sigmoid(_dot(h, wf_ref[...]) + bf_ref[...])
    lc = _cumsum_rows(log_f) + carry_ref[...]
    carry_ref[...] = lc[lc.shape[0] - 1:, :]
    lc2 = lc * LOG2E
    hi, mid, lo = _split3(lc2)
    aug = jnp.concatenate([hi, mid, lo, jnp.ones_like(hi)], axis=1)
    qs = _dot(h, wq_ref[...]) * (HEAD_DIM ** -0.5 * LOG2E)
    ks = _dot(h, wk_ref[...])
    _spread_heads(qs, _dot(aug, pq_ref[...]), q_ref)
    _spread_heads(ks, _dot(aug, pk_ref[...]), k_ref)
    qn2 = _dot((qs * qs).astype(BF16), hsel_ref[...])
    kn2 = _dot((ks * ks).astype(BF16), hsel_ref[...])
    tm = lc2.shape[0]
    stats_ref[...] = jnp.concatenate([
        jnp.sqrt(jnp.max(qn2, axis=0, keepdims=True)),
        jnp.sqrt(jnp.max(kn2, axis=0, keepdims=True)),
        jnp.sqrt(jnp.max(qn2 * kn2, axis=0, keepdims=True)),
        lc2[0:1, :], lc2[tm - 1:tm, :],
        jnp.zeros((FOX_STATS - 5, lc2.shape[1]), F32)], axis=0)
    vt = _dot_nt(wvt_ref[...], h).astype(BF16)
    pad = jnp.where(lax.broadcasted_iota(jnp.int32, (HEAD_DIM, vt.shape[1]), 0) == 0,
                    1.0, 0.0).astype(BF16)
    for hd in range(FOX_HEADS):
        vt_ref[hd * LANES:hd * LANES + HEAD_DIM, :] = vt[hd * HEAD_DIM:(hd + 1) * HEAD_DIM, :]
        vt_ref[hd * LANES + HEAD_DIM:(hd + 1) * LANES, :] = pad


def _fox_proj_call(x, mod, gain, wq, wk, wvt, wf, bf, pq, pk, tm=512):
    bsz, s, d = x.shape
    nqk = FOX_HEADS * LANES
    row = lambda b, i: (b, i, 0)
    hsel = jnp.asarray(np.repeat(np.eye(FOX_HEADS, dtype=np.float32), HEAD_DIM, axis=0), BF16)
    return pl.pallas_call(
        _fox_proj_kernel,
        grid=(bsz, s // tm),
        in_specs=[
            pl.BlockSpec((None, tm, d), row),
            pl.BlockSpec((None, 6, d), lambda b, i: (b, 0, 0)),
            _const_spec((1, d)),
            _const_spec(wq.shape),
            _const_spec(wk.shape),
            _const_spec(wvt.shape),
            _const_spec(wf.shape),
            _const_spec(bf.shape),
            _const_spec(pq.shape),
            _const_spec(pk.shape),
            _const_spec(hsel.shape),
        ],
        out_specs=[
            pl.BlockSpec((None, tm, nqk), row),
            pl.BlockSpec((None, tm, nqk), row),
            pl.BlockSpec((None, None, nqk, tm), lambda b, i: (b, i, 0, 0)),
            pl.BlockSpec((None, None, FOX_STATS, FOX_HEADS), lambda b, i: (b, i, 0, 0)),
        ],
        out_shape=[
            jax.ShapeDtypeStruct((bsz, s, nqk), BF16),
            jax.ShapeDtypeStruct((bsz, s, nqk), BF16),
            jax.ShapeDtypeStruct((bsz, s // tm, nqk, tm), BF16),
            jax.ShapeDtypeStruct((bsz, s // tm, FOX_STATS, FOX_HEADS), F32),
        ],
        scratch_shapes=[pltpu.VMEM((1, FOX_HEADS), F32)],
        compiler_params=_params("arbitrary", "arbitrary"),
        name="fox_proj",
    )(x, mod, gain, wq, wk, wvt, wf, bf, pq, pk, hsel)


def _swa_band_bias():
    w, group = SWA_WINDOW, SWA_Q_HEADS // SWA_KV_HEADS
    key = np.arange(2 * w)[:, None]
    qry = np.arange(group * w)[None, :] % w
    dist = (w + qry) - key
    band = (dist >= 0) & (dist < w)
    allowed = np.stack([band & (key >= w), band])
    return jnp.asarray(np.where(allowed, 0.0, NEG_BIG), F32)


def _swa_attn_kernel(sink_ref, bias_ref, q_ref, kc_ref, vtc_ref, o_ref, st_ref, kp_ref, vtp_ref):
    w = SWA_WINDOW
    group = SWA_Q_HEADS // SWA_KV_HEADS
    nq = group * w
    nb = q_ref.shape[0] // w
    lane = lax.broadcasted_iota(jnp.int32, (w, LANES), 1)
    low = lane < HEAD_DIM

    @pl.when(pl.program_id(1) == 0)
    def _():
        kp_ref[...] = jnp.zeros_like(kp_ref)
        vtp_ref[...] = jnp.zeros_like(vtp_ref)

    for u in range(nb):
        rows = slice(u * w, (u + 1) * w)
        for g in range(SWA_KV_HEADS):
            cols = slice(g * LANES, (g + 1) * LANES)
            slabs = []
            for hh in range(group):
                head = g * group + hh
                qs = q_ref[rows, (head // 2) * LANES:(head // 2 + 1) * LANES]
                keep = low if head % 2 == 0 else jnp.logical_not(low)
                slabs.append(jnp.where(keep, qs, jnp.zeros_like(qs)))
            q_stack = jnp.concatenate(slabs, axis=0)
            k_prev = kp_ref[:, cols] if u == 0 else kc_ref[(u - 1) * w:u * w, cols]
            k_both = jnp.concatenate([k_prev, kc_ref[rows, cols]], axis=0)
            st_ref[u * SWA_KV_HEADS + g] = _dot_nt(k_both, q_stack)
    for u in range(nb):
        rows = slice(u * w, (u + 1) * w)
        bias = bias_ref[jnp.minimum(pl.program_id(1), 1)] if u == 0 else bias_ref[1]
        for g in range(SWA_KV_HEADS):
            cols = slice(g * LANES, (g + 1) * LANES)
            st = st_ref[u * SWA_KV_HEADS + g] + bias
            sink = sink_ref[:, g * nq:(g + 1) * nq] * LOG2E
            m = jnp.maximum(jnp.max(st, axis=0, keepdims=True), sink)
            pt = jnp.exp2(st - m).astype(BF16)
            vt_prev = vtp_ref[cols, :] if u == 0 else vtc_ref[cols, (u - 1) * w:u * w]
            vt_both = jnp.concatenate([vt_prev, vtc_ref[cols, rows]], axis=1)
            acc = _dot(vt_both, pt)
            den = acc[HEAD_DIM:HEAD_DIM + 1, :] + jnp.exp2(sink - m)
            ot = acc[:HEAD_DIM, :] / den
            o_t = jnp.concatenate([ot[:, hh * w:(hh + 1) * w] for hh in range(group)], axis=0)
            o_ref[rows, g * group * HEAD_DIM:(g + 1) * group * HEAD_DIM] = o_t.T.astype(BF16)
    kp_ref[...] = kc_ref[(nb - 1) * w:, :]
    vtp_ref[...] = vtc_ref[:, (nb - 1) * w:]


def _swa_attn_call(q, k, vt, sinks, nb=4):
    bsz, s, d = q.shape
    w = SWA_WINDOW
    nkv = k.shape[2]
    sink_row = jnp.repeat(sinks, w).reshape(1, -1)
    bias = _swa_band_bias()
    cur = lambda b, i: (b, i, 0)
    return pl.pallas_call(
        _swa_attn_kernel,
        grid=(bsz, s // (nb * w)),
        in_specs=[
            _const_spec(sink_row.shape),
            _const_spec(bias.shape),
            pl.BlockSpec((None, nb * w, d), cur),
            pl.BlockSpec((None, nb * w, nkv), cur),
            pl.BlockSpec((None, nkv, nb * w), lambda b, i: (b, 0, i)),
        ],
        out_specs=pl.BlockSpec((None, nb * w, d), cur),
        out_shape=jax.ShapeDtypeStruct((bsz, s, d), BF16),
        scratch_shapes=[
            pltpu.VMEM((nb * SWA_KV_HEADS, 2 * w, (SWA_Q_HEADS // SWA_KV_HEADS) * w), F32),
            pltpu.VMEM((w, nkv), BF16),
            pltpu.VMEM((nkv, w), BF16),
        ],
        compiler_params=_params("arbitrary", "arbitrary"),
        name="swa_attn",
    )(sink_row, bias, q, k, vt)


def _chunk_cumsum(x):
    n = x.shape[0]
    row = lax.broadcasted_iota(jnp.int32, (n, n), 0)
    col = lax.broadcasted_iota(jnp.int32, (n, n), 1)
    same_chunk = (row // GLA_CHUNK) == (col // GLA_CHUNK)
    tril = jnp.where(jnp.logical_and(row >= col, same_chunk), 1.0, 0.0).astype(BF16)
    hi, mid, lo = _split3(x)
    return _dot(tril, hi) + _dot(tril, mid) + _dot(tril, lo)


def _gla_intra(q, k, b2):
    c, sub = GLA_CHUNK, GLA_SUB
    col = lax.broadcasted_iota(jnp.int32, (sub, c), 1)
    row = lax.broadcasted_iota(jnp.int32, (sub, c), 0)
    blocks = []
    for i in range(c // sub):
        lo = i * sub
        q_i = q[lo:lo + sub, :]
        b_i = b2[lo:lo + sub, :]
        if i == 0:
            a = jnp.zeros((sub, c), F32)
        else:
            ref = b2[lo - 1:lo, :]
            n = -(-lo // BF16_ROWS) * BF16_ROWS
            q_t = (q_i * jnp.exp2(b_i - ref)).astype(BF16)
            k_t = (k[:n, :] * jnp.exp2(jnp.minimum(ref - b2[:n, :], 0.0))).astype(BF16)
            if n < c:
                k_t = jnp.concatenate([k_t, jnp.zeros((c - n, k_t.shape[1]), BF16)], axis=0)
            a = _dot_nt(q_t, k_t)
        for s in range(lo, lo + sub):
            w = jnp.exp2(b_i - b2[s:s + 1, :])
            val = jnp.sum(q_i * k[s:s + 1, :] * w, axis=1, keepdims=True)
            a = jnp.where(col == s, val, a)
        blocks.append(jnp.where(row + lo >= col, a, 0.0))
    return jnp.concatenate(blocks, axis=0)


def _gla_kernel(q_ref, k_ref, v_ref, r_ref, la_ref, hn_ref, o_ref, state_ref):
    @pl.when(pl.program_id(2) == 0)
    def _():
        state_ref[...] = jnp.zeros_like(state_ref)

    for hh in range(state_ref.shape[0]):
        kq = slice(hh * GLA_DK, (hh + 1) * GLA_DK)
        vv = slice(hh * GLA_DV, (hh + 1) * GLA_DV)
        _gla_head(q_ref.at[:, kq], k_ref.at[:, kq], v_ref.at[:, vv], r_ref.at[:, vv], la_ref.at[:, kq],
                  hn_ref, o_ref.at[:, vv], state_ref.at[hh])


def _gla_head(q_ref, k_ref, v_ref, r_ref, la_ref, hn_ref, o_ref, state_ref):
    c = GLA_CHUNK
    nc = q_ref.shape[0] // c
    rows = [slice(ci * c, (ci + 1) * c) for ci in range(nc)]
    b2_all = _chunk_cumsum(la_ref[...]) * LOG2E
    q_all = q_ref[...].astype(F32) * (GLA_DK ** -0.5)
    k_all = k_ref[...].astype(F32)
    b2 = [b2_all[r] for r in rows]
    q = [q_all[r] for r in rows]
    k = [k_all[r] for r in rows]
    last = [b[c - 1:c, :] for b in b2]
    q_in = [(q[i] * jnp.exp2(b2[i])).astype(BF16) for i in range(nc)]
    k_out = [(k[i] * jnp.exp2(last[i] - b2[i])).astype(BF16) for i in range(nc)]
    kv = [_dot_tn(v_ref[rows[i], :], k_out[i]) for i in range(nc)]
    attn = [_gla_intra(q[i], k[i], b2[i]).astype(BF16) for i in range(nc)]
    intra = [_dot(attn[i], v_ref[rows[i], :]) for i in range(nc)]
    state_t = state_ref[...]
    for i in range(nc):
        o = intra[i] + _dot_nt(q_in[i], state_t.astype(BF16))
        state_t = state_t * jnp.exp2(last[i]) + kv[i]
        r = r_ref[rows[i], :].astype(F32)
        o_ref[rows[i], :] = (_rms(o, hn_ref[...]) * _silu(r)).astype(BF16)
    state_ref[...] = state_t


def _gla_call(q, k, v, r, la, head_norm, tm=256, nh=2):
    bsz, s, _ = q.shape
    dk, dv = GLA_DK, GLA_DV
    blk = lambda b, h, i: (b, i, h)
    return pl.pallas_call(
        _gla_kernel,
        grid=(bsz, GLA_HEADS // nh, s // tm),
        in_specs=[
            pl.BlockSpec((None, tm, nh * dk), blk),
            pl.BlockSpec((None, tm, nh * dk), blk),
            pl.BlockSpec((None, tm, nh * dv), blk),
            pl.BlockSpec((None, tm, nh * dv), blk),
            pl.BlockSpec((None, tm, nh * dk), blk),
            _const_spec((1, dv)),
        ],
        out_specs=pl.BlockSpec((None, tm, nh * dv), blk),
        out_shape=jax.ShapeDtypeStruct((bsz, s, GLA_HEADS * dv), BF16),
        scratch_shapes=[pltpu.VMEM((nh, dv, dk), F32)],
        compiler_params=_params("arbitrary", "arbitrary", "arbitrary"),
        name="gla_mix",
    )(q, k, v, r, la, head_norm)


def _fox_first_live_block(stats_ref, b, first_head, nh, i):
    heads = [first_head + e for e in range(nh)]
    q_max = [FOX_BOUND_SLACK * stats_ref[b, i, Q_MAX, hd] for hd in heads]
    margin = [stats_ref[b, i, LC_FIRST, hd] + FOX_BOUND_SLACK * stats_ref[b, i, QK_MAX, hd]
              - FOX_DEAD_LOG2 for hd in heads]

    def dead(j):
        is_dead = True
        for e, hd in enumerate(heads):
            gap = q_max[e] * stats_ref[b, j, K_MAX, hd] - stats_ref[b, j, LC_LAST, hd] + margin[e]
            is_dead = jnp.logical_and(is_dead, gap <= 0.0)
        return is_dead

    last = jnp.maximum(i - 1, 0)
    return lax.while_loop(lambda j: jnp.logical_and(j < i, dead(jnp.minimum(j, last))),
                          lambda j: j + 1, jnp.int32(0))


def _fox_attn_kernel(stats_ref, q_ref, k_ref, vt_ref, o_ref, m_ref, acc_ref, sa_ref, sb_ref):
    tk = vt_ref.shape[2]
    nh = m_ref.shape[0]
    i = pl.program_id(2)
    j0 = _fox_first_live_block(stats_ref, pl.program_id(0), pl.program_id(1) * nh, nh, i)
    m_ref[...] = jnp.full_like(m_ref, NEG_BIG)
    acc_ref[...] = jnp.zeros_like(acc_ref)

    def scores(j, buf, e):
        lanes = slice(e * LANES, (e + 1) * LANES)
        kj = k_ref[pl.ds(pl.multiple_of(j * tk, tk), tk), lanes]
        buf[e] = _dot_nt(kj, q_ref[:, lanes])

    def accumulate(j, buf, e, masked):
        st = buf[e]
        if masked:
            key = lax.broadcasted_iota(jnp.int32, st.shape, 0)
            qry = lax.broadcasted_iota(jnp.int32, st.shape, 1)
            st = jnp.where(key <= qry, st, NEG_BIG)
        m_old = m_ref[e]
        m_new = jnp.maximum(m_old, jnp.max(st, axis=0, keepdims=True))
        alpha = jnp.exp2(m_old - m_new)
        pt = jnp.exp2(st - m_new).astype(BF16)
        acc_ref[e] = alpha * acc_ref[e] + _dot(vt_ref[j, e * LANES:(e + 1) * LANES, :], pt)
        m_ref[e] = m_new

    def scores_and_accumulate(j_next, buf_next, j_cur, buf_cur):
        for e in range(nh):
            scores(j_next, buf_next, e)
            accumulate(j_cur, buf_cur, e, False)

    for e in range(nh):
        scores(j0, sa_ref, e)
    n_full = i - j0

    def body(t, carry):
        j = j0 + 2 * t
        scores_and_accumulate(j + 1, sb_ref, j, sa_ref)
        scores_and_accumulate(j + 2, sa_ref, j + 1, sb_ref)
        return carry

    lax.fori_loop(0, n_full // 2, body, 0)

    @pl.when(n_full % 2 == 0)
    def _():
        for e in range(nh):
            accumulate(i, sa_ref, e, True)

    @pl.when(n_full % 2 == 1)
    def _():
        scores_and_accumulate(i, sb_ref, i - 1, sa_ref)
        for e in range(nh):
            accumulate(i, sb_ref, e, True)

    outs = []
    for e in range(nh):
        acc = acc_ref[e]
        outs.append(acc[:HEAD_DIM, :] / acc[HEAD_DIM:HEAD_DIM + 1, :])
    o_ref[...] = jnp.concatenate(outs, axis=0).T.astype(BF16)


def _fox_attn_call(q, k, vt, stats, nh=4):
    bsz, s, _ = q.shape
    tk = vt.shape[3]
    tq = tk
    return pl.pallas_call(
        _fox_attn_kernel,
        grid=(bsz, FOX_HEADS // nh, s // tq),
        in_specs=[
            pl.BlockSpec(memory_space=pltpu.SMEM),
            pl.BlockSpec((None, tq, nh * LANES), lambda b, p, i: (b, i, p)),
            pl.BlockSpec((None, s, nh * LANES), lambda b, p, i: (b, 0, p)),
            pl.BlockSpec((None, s // tk, nh * LANES, tk), lambda b, p, i: (b, 0, p, 0)),
        ],
        out_specs=pl.BlockSpec((None, tq, nh * HEAD_DIM), lambda b, p, i: (b, i, p)),
        out_shape=jax.ShapeDtypeStruct((bsz, s, D_MODEL), BF16),
        scratch_shapes=[
            pltpu.VMEM((nh, 1, tq), F32),
            pltpu.VMEM((nh, LANES, tq), F32),
            pltpu.VMEM((nh, tk, tq), F32),
            pltpu.VMEM((nh, tk, tq), F32),
        ],
        compiler_params=_params("arbitrary", "arbitrary", "arbitrary"),
        name="fox_attn",
    )(stats, q, k, vt)


def _post_kernel(x_ref, o_ref, mod_ref, gain_ref, wo_ref, wgu_ref, wd_ref, fn_ref, out_ref,
                 *, ff_chunk, final):
    x1 = x_ref[...] + mod_ref[2:3, :] * _dot(o_ref[...], wo_ref[...])
    h = _norm_mod(x1, gain_ref[...], mod_ref[3:4, :], mod_ref[4:5, :]).astype(BF16)
    acc = jnp.zeros(x1.shape, F32)
    for c0 in range(0, D_FF, ff_chunk):
        g = _dot(h, wgu_ref[:, c0:c0 + ff_chunk])
        u = _dot(h, wgu_ref[:, D_FF + c0:D_FF + c0 + ff_chunk])
        acc = acc + _dot((_silu(g) * u).astype(BF16), wd_ref[c0:c0 + ff_chunk, :])
    x2 = x1 + mod_ref[5:6, :] * acc
    if final:
        x2 = _rms(x2, fn_ref[...])
    out_ref[...] = x2


def _layer_spec(stack, layer):
    return pl.BlockSpec((None,) + stack.shape[1:], lambda *_: (layer, 0, 0),
                        pipeline_mode=pl.Buffered(1))


def _post_call(x, o, mod, gain, wo, wgu_stack, wd_stack, layer, final_norm, final, tm=1024, ff_chunk=256):
    bsz, s, d = x.shape
    row = lambda b, i: (b, i, 0)
    return pl.pallas_call(
        functools.partial(_post_kernel, ff_chunk=ff_chunk, final=final),
        grid=(bsz, s // tm),
        in_specs=[
            pl.BlockSpec((None, tm, d), row),
            pl.BlockSpec((None, tm, d), row),
            pl.BlockSpec((None, 6, d), lambda b, i: (b, 0, 0)),
            _const_spec((1, d)),
            _const_spec(wo.shape),
            _layer_spec(wgu_stack, layer),
            _layer_spec(wd_stack, layer),
            _const_spec((1, d)),
        ],
        out_specs=pl.BlockSpec((None, tm, d), row),
        out_shape=jax.ShapeDtypeStruct((bsz, s, d), F32),
        compiler_params=_params("arbitrary", "arbitrary"),
        name="post_ffn",
    )(x, o, mod, gain, wo, wgu_stack, wd_stack, final_norm)


def _rope_tables(s):
    half = HEAD_DIM // 2
    inv = 1.0 / (ROPE_THETA ** (jnp.arange(0, HEAD_DIM, 2, dtype=F32) / HEAD_DIM))
    ang = jnp.arange(s, dtype=F32)[:, None] * inv[None, :]
    cos, sin = jnp.cos(ang), jnp.sin(ang)
    reps = LANES // HEAD_DIM
    cos_t = jnp.tile(jnp.concatenate([cos, cos], axis=1), (1, reps))
    sin_t = jnp.tile(jnp.concatenate([-sin, sin], axis=1), (1, reps))
    assert half * 2 == HEAD_DIM
    return cos_t, sin_t


def _dup_heads(w, heads):
    w3 = w.reshape(w.shape[0], heads, HEAD_DIM)
    return jnp.concatenate([w3, w3], axis=2).reshape(w.shape[0], heads * LANES)


def _pad_heads(w, heads):
    w3 = w.reshape(w.shape[0], heads, HEAD_DIM)
    return jnp.concatenate([w3, jnp.zeros_like(w3)], axis=2).reshape(w.shape[0], heads * LANES)


def _fox_placement():
    h = FOX_HEADS
    pq = np.zeros((4 * h, h * HEAD_DIM), np.float32)
    pk = np.zeros((4 * h, h * HEAD_DIM), np.float32)
    for head in range(h):
        base = (head // 2) * LANES + (HEAD_DIM if head % 2 == 0 else 0)
        for part in range(3):
            pq[part * h + head, base + part] = 1.0
            pk[3 * h + head, base + part] = 1.0
            pq[3 * h + head, base + 3 + part] = 1.0
            pk[part * h + head, base + 3 + part] = -1.0
    return jnp.asarray(pq, BF16), jnp.asarray(pk, BF16)


def kernel(x, c, ada_w, ada_b, norm_gain, ffn_w_gu, ffn_w_down, swa_w_in, swa_sinks, swa_w_o,
           gla_w_in, gla_w_gate_up, gla_b_gate, gla_head_norm, gla_w_o, fox_w_in, fox_b_f, fox_w_o,
           final_norm):
    bsz, s, d = x.shape
    depth = ada_w.shape[0]
    mod_all = _ada_call(c, ada_w, ada_b).reshape(depth, bsz, 6, d)
    cos_t, sin_t = _rope_tables(s)
    pq, pk = _fox_placement()
    fn = final_norm.reshape(1, d)
    wgu_stack = ffn_w_gu.astype(BF16)
    wd_stack = ffn_w_down.astype(BF16)

    for i in range(depth):
        kind, j = i % N_MIXERS, i // N_MIXERS
        mod = mod_all[i]
        gain1 = norm_gain[i, 0].reshape(1, d)
        gain2 = norm_gain[i, 1].reshape(1, d)
        if kind == 0:
            w = swa_w_in[j]
            nq, nkv = SWA_Q_HEADS * HEAD_DIM, SWA_KV_HEADS * HEAD_DIM
            w_all = jnp.concatenate([w[:, :nq], _dup_heads(w[:, nq:nq + nkv], SWA_KV_HEADS)],
                                    axis=1).astype(BF16)
            wvt = _pad_heads(w[:, nq + nkv:], SWA_KV_HEADS).T.astype(BF16)
            q, k, v = _swa_proj_call(x, mod, gain1, w_all, wvt, cos_t, sin_t)
            o = _swa_attn_call(q, k, v, swa_sinks[j])
            wo = swa_w_o[j]
        elif kind == 1:
            w = gla_w_in[j]
            n_main = 2 * GLA_HEADS * GLA_DK + 2 * GLA_HEADS * GLA_DV
            q, k, v, r, la = _gla_proj_call(
                x, mod, gain1, w[:, :n_main].astype(BF16), w[:, n_main:].astype(BF16),
                gla_w_gate_up[j].astype(BF16), gla_b_gate[j].reshape(1, -1))
            o = _gla_call(q, k, v, r, la, gla_head_norm[j].reshape(1, -1))
            wo = gla_w_o[j]
        else:
            w = fox_w_in[j]
            q, k, v, stats = _fox_proj_call(
                x, mod, gain1,
                w[:, :d].astype(BF16), w[:, d:2 * d].astype(BF16),
                w[:, 2 * d:3 * d].T.astype(BF16), w[:, 3 * d:].astype(BF16),
                fox_b_f[j].reshape(1, -1), pq, pk)
            o = _fox_attn_call(q, k, v, stats)
            wo = fox_w_o[j]
        x = _post_call(x, o, mod, gain2, wo.astype(BF16), wgu_stack, wd_stack, i, fn,
                       final=(i == depth - 1))
    return x
```

```python
import functools

import numpy as np
import jax
import jax.numpy as jnp
from jax import lax
from jax.experimental import pallas as pl
from jax.experimental.pallas import tpu as pltpu

D_MODEL = 1024
HEAD_DIM = 64
RMS_EPS = 1e-6
SWA_Q_HEADS = 16
SWA_KV_HEADS = 4
SWA_WINDOW = 128
ROPE_THETA = 150000.0
GLA_HEADS = 4
GLA_DK = 128
GLA_DV = 256
GLA_RANK = 16
GLA_TAU = 16.0
GLA_CHUNK = 64
GLA_SUB = 8
FOX_HEADS = 16
FOX_STATS = 8
Q_MAX, K_MAX, QK_MAX, LC_FIRST, LC_LAST = range(5)
FOX_DEAD_LOG2 = -160.0
FOX_BOUND_SLACK = 1.02
D_FF = 2816
N_MIXERS = 3

LANES = 128
BF16_ROWS = 16
NEG_BIG = -1e30
LOG2E = 1.4426950408889634
VMEM_LIMIT = 56 * 1024 * 1024

BF16 = jnp.bfloat16
F32 = jnp.float32


def _dot(a, b):
    return jnp.dot(a, b, preferred_element_type=F32)


def _dot_nt(a, b):
    return lax.dot_general(a, b, (((1,), (1,)), ((), ())), preferred_element_type=F32)


def _dot_tn(a, b):
    return lax.dot_general(a, b, (((0,), (0,)), ((), ())), preferred_element_type=F32)


def _split3(x):
    hi = x.astype(BF16)
    r1 = x - hi.astype(F32)
    mid = r1.astype(BF16)
    lo = (r1 - mid.astype(F32)).astype(BF16)
    return hi, mid, lo


def _cumsum_rows(x):
    n = x.shape[0]
    row = lax.broadcasted_iota(jnp.int32, (n, n), 0)
    col = lax.broadcasted_iota(jnp.int32, (n, n), 1)
    tril = jnp.where(row >= col, 1.0, 0.0).astype(BF16)
    hi, mid, lo = _split3(x)
    return _dot(tril, hi) + _dot(tril, mid) + _dot(tril, lo)


def _log_sigmoid(x):
    return jnp.minimum(x, 0.0) - jnp.log(1.0 + jnp.exp(-jnp.abs(x)))


def _silu(x):
    return x * (1.0 / (1.0 + jnp.exp(-x)))


def _rms(x, gain):
    ms = jnp.mean(x * x, axis=-1, keepdims=True)
    return x * lax.rsqrt(ms + RMS_EPS) * gain


def _norm_mod(x, gain, shift, scale):
    return _rms(x, gain) * (1.0 + scale) + shift


def _params(*sem):
    return pltpu.CompilerParams(dimension_semantics=sem, vmem_limit_bytes=VMEM_LIMIT)


def _const_spec(shape):
    nd = len(shape)
    return pl.BlockSpec(shape, lambda *_: (0,) * nd, pipeline_mode=pl.Buffered(1))


def _ada_kernel(ct_ref, w_ref, b_ref, out_ref):
    ca = _silu(ct_ref[...])
    w = w_ref[...]
    for b in range(ct_ref.shape[1]):
        col = ca[:, b:b + 1]
        out_ref[b:b + 1, :] = jnp.sum(col * w, axis=0, keepdims=True) + b_ref[...]


def _ada_call(c, ada_w, ada_b):
    depth, d, n = ada_w.shape
    bsz = c.shape[0]
    tn = 768
    return pl.pallas_call(
        _ada_kernel,
        grid=(depth, n // tn),
        in_specs=[
            pl.BlockSpec((d, bsz), lambda l, j: (0, 0)),
            pl.BlockSpec((None, d, tn), lambda l, j: (l, 0, j)),
            pl.BlockSpec((None, 1, tn), lambda l, j: (l, 0, j)),
        ],
        out_specs=pl.BlockSpec((None, bsz, tn), lambda l, j: (l, 0, j)),
        out_shape=jax.ShapeDtypeStruct((depth, bsz, n), F32),
        compiler_params=_params("arbitrary", "arbitrary"),
        name="ada_mod",
    )(c.T, ada_w, ada_b.reshape(depth, 1, n))


def _rope(x, cos, sin_signed):
    width = x.shape[1]
    reps = width // cos.shape[1]
    c = jnp.tile(cos, (1, reps))
    s = jnp.tile(sin_signed, (1, reps))
    lane = lax.broadcasted_iota(jnp.int32, x.shape, 1)
    first_half = (lane % HEAD_DIM) < (HEAD_DIM // 2)
    rot = jnp.where(first_half,
                    pltpu.roll(x, width - HEAD_DIM // 2, 1),
                    pltpu.roll(x, HEAD_DIM // 2, 1))
    return x * c + rot * s


def _ones_row_64(vt):
    ones_row = lax.broadcasted_iota(jnp.int32, vt.shape, 0) % LANES == HEAD_DIM
    return jnp.where(ones_row, 1.0, vt)


def _swa_proj_kernel(x_ref, mod_ref, gain_ref, w_ref, wvt_ref, cos_ref, sin_ref, q_ref, k_ref, vt_ref):
    h = _norm_mod(x_ref[...], gain_ref[...], mod_ref[0:1, :], mod_ref[1:2, :]).astype(BF16)
    cos, sin = cos_ref[...], sin_ref[...]
    nq = q_ref.shape[1]
    q = _dot(h, w_ref[:, :nq])
    q_ref[...] = (_rope(q, cos, sin) * (HEAD_DIM ** -0.5 * LOG2E)).astype(BF16)
    k = _dot(h, w_ref[:, nq:])
    k_ref[...] = _rope(k, cos, sin).astype(BF16)
    vt_ref[...] = _ones_row_64(_dot_nt(wvt_ref[...], h)).astype(BF16)


def _swa_proj_call(x, mod, gain, w, wvt, cos, sin, tm=1024):
    bsz, s, d = x.shape
    nq, nkv = D_MODEL, SWA_KV_HEADS * LANES
    row = lambda b, i: (b, i, 0)
    return pl.pallas_call(
        _swa_proj_kernel,
        grid=(bsz, s // tm),
        in_specs=[
            pl.BlockSpec((None, tm, d), row),
            pl.BlockSpec((None, 6, d), lambda b, i: (b, 0, 0)),
            _const_spec((1, d)),
            _const_spec(w.shape),
            _const_spec(wvt.shape),
            pl.BlockSpec((tm, LANES), lambda b, i: (i, 0)),
            pl.BlockSpec((tm, LANES), lambda b, i: (i, 0)),
        ],
        out_specs=[
            pl.BlockSpec((None, tm, nq), row),
            pl.BlockSpec((None, tm, nkv), row),
            pl.BlockSpec((None, nkv, tm), lambda b, i: (b, 0, i)),
        ],
        out_shape=[
            jax.ShapeDtypeStruct((bsz, s, nq), BF16),
            jax.ShapeDtypeStruct((bsz, s, nkv), BF16),
            jax.ShapeDtypeStruct((bsz, nkv, s), BF16),
        ],
        compiler_params=_params("arbitrary", "arbitrary"),
        name="swa_proj",
    )(x, mod, gain, w, wvt, cos, sin)


def _gla_proj_kernel(x_ref, mod_ref, gain_ref, w_ref, wa_ref, wg_ref, bg_ref,
                     q_ref, k_ref, v_ref, r_ref, la_ref):
    h = _norm_mod(x_ref[...], gain_ref[...], mod_ref[0:1, :], mod_ref[1:2, :]).astype(BF16)
    nk = q_ref.shape[1]
    nv = v_ref.shape[1]
    q_ref[...] = _dot(h, w_ref[:, :nk]).astype(BF16)
    k_ref[...] = _dot(h, w_ref[:, nk:2 * nk]).astype(BF16)
    v_ref[...] = _dot(h, w_ref[:, 2 * nk:2 * nk + nv]).astype(BF16)
    r_ref[...] = _dot(h, w_ref[:, 2 * nk + nv:]).astype(BF16)
    a_low = _dot(h, wa_ref[...]).astype(BF16)
    z = _dot(a_low, wg_ref[...]) + bg_ref[...]
    la_ref[...] = _log_sigmoid(z) * (1.0 / GLA_TAU)


def _gla_proj_call(x, mod, gain, w, wa, wg, bg, tm=1024):
    bsz, s, d = x.shape
    nk, nv = GLA_HEADS * GLA_DK, GLA_HEADS * GLA_DV
    row = lambda b, i: (b, i, 0)
    return pl.pallas_call(
        _gla_proj_kernel,
        grid=(bsz, s // tm),
        in_specs=[
            pl.BlockSpec((None, tm, d), row),
            pl.BlockSpec((None, 6, d), lambda b, i: (b, 0, 0)),
            _const_spec((1, d)),
            _const_spec(w.shape),
            _const_spec(wa.shape),
            _const_spec(wg.shape),
            _const_spec(bg.shape),
        ],
        out_specs=[
            pl.BlockSpec((None, tm, nk), row),
            pl.BlockSpec((None, tm, nk), row),
            pl.BlockSpec((None, tm, nv), row),
            pl.BlockSpec((None, tm, nv), row),
            pl.BlockSpec((None, tm, nk), row),
        ],
        out_shape=[
            jax.ShapeDtypeStruct((bsz, s, nk), BF16),
            jax.ShapeDtypeStruct((bsz, s, nk), BF16),
            jax.ShapeDtypeStruct((bsz, s, nv), BF16),
            jax.ShapeDtypeStruct((bsz, s, nv), BF16),
            jax.ShapeDtypeStruct((bsz, s, nk), F32),
        ],
        compiler_params=_params("arbitrary", "arbitrary"),
        name="gla_proj",
    )(x, mod, gain, w, wa, wg, bg)


def _spread_heads(x, extra, out_ref):
    lane = lax.broadcasted_iota(jnp.int32, (x.shape[0], LANES), 1)
    low = lane < HEAD_DIM
    for p in range(x.shape[1] // LANES):
        xs = x[:, p * LANES:(p + 1) * LANES]
        ex = extra[:, p * LANES:(p + 1) * LANES]
        out_ref[:, (2 * p) * LANES:(2 * p + 1) * LANES] = jnp.where(low, xs, ex).astype(out_ref.dtype)
        odd = pltpu.roll(jnp.where(low, ex, xs), HEAD_DIM, 1)
        out_ref[:, (2 * p + 1) * LANES:(2 * p + 2) * LANES] = odd.astype(out_ref.dtype)


def _fox_proj_kernel(x_ref, mod_ref, gain_ref, wq_ref, wk_ref, wvt_ref, wf_ref, bf_ref,
                     pq_ref, pk_ref, hsel_ref, q_ref, k_ref, vt_ref, stats_ref, carry_ref):
    @pl.when(pl.program_id(1) == 0)
    def _():
        carry_ref[...] = jnp.zeros_like(carry_ref)

    h = _norm_mod(x_ref[...], gain_ref[...], mod_ref[0:1, :], mod_ref[1:2, :]).astype(BF16)
    log_f = _log_sigmoid(_dot(h, wf_ref[...]) + bf_ref[...])
    lc = _cumsum_rows(log_f) + carry_ref[...]
    carry_ref[...] = lc[lc.shape[0] - 1:, :]
    lc2 = lc * LOG2E
    hi, mid, lo = _split3(lc2)
    aug = jnp.concatenate([hi, mid, lo, jnp.ones_like(hi)], axis=1)
    qs = _dot(h, wq_ref[...]) * (HEAD_DIM ** -0.5 * LOG2E)
    ks = _dot(h, wk_ref[...])
    _spread_heads(qs, _dot(aug, pq_ref[...]), q_ref)
    _spread_heads(ks, _dot(aug, pk_ref[...]), k_ref)
    qn2 = _dot((qs * qs).astype(BF16), hsel_ref[...])
    kn2 = _dot((ks * ks).astype(BF16), hsel_ref[...])
    tm = lc2.shape[0]
    stats_ref[...] = jnp.concatenate([
        jnp.sqrt(jnp.max(qn2, axis=0, keepdims=True)),
        jnp.sqrt(jnp.max(kn2, axis=0, keepdims=True)),
        jnp.sqrt(jnp.max(qn2 * kn2, axis=0, keepdims=True)),
        lc2[0:1, :], lc2[tm - 1:tm, :],
        jnp.zeros((FOX_STATS - 5, lc2.shape[1]), F32)], axis=0)
    vt = _dot_nt(wvt_ref[...], h).astype(BF16)
    pad = jnp.where(lax.broadcasted_iota(jnp.int32, (HEAD_DIM, vt.shape[1]), 0) == 0,
                    1.0, 0.0).astype(BF16)
    for hd in range(FOX_HEADS):
        vt_ref[hd * LANES:hd * LANES + HEAD_DIM, :] = vt[hd * HEAD_DIM:(hd + 1) * HEAD_DIM, :]
        vt_ref[hd * LANES + HEAD_DIM:(hd + 1) * LANES, :] = pad


def _fox_proj_call(x, mod, gain, wq, wk, wvt, wf, bf, pq, pk, tm=512):
    bsz, s, d = x.shape
    nqk = FOX_HEADS * LANES
    row = lambda b, i: (b, i, 0)
    hsel = jnp.asarray(np.repeat(np.eye(FOX_HEADS, dtype=np.float32), HEAD_DIM, axis=0), BF16)
    return pl.pallas_call(
        _fox_proj_kernel,
        grid=(bsz, s // tm),
        in_specs=[
            pl.BlockSpec((None, tm, d), row),
            pl.BlockSpec((None, 6, d), lambda b, i: (b, 0, 0)),
            _const_spec((1, d)),
            _const_spec(wq.shape),
            _const_spec(wk.shape),
            _const_spec(wvt.shape),
            _const_spec(wf.shape),
            _const_spec(bf.shape),
            _const_spec(pq.shape),
            _const_spec(pk.shape),
            _const_spec(hsel.shape),
        ],
        out_specs=[
            pl.BlockSpec((None, tm, nqk), row),
            pl.BlockSpec((None, tm, nqk), row),
            pl.BlockSpec((None, None, nqk, tm), lambda b, i: (b, i, 0, 0)),
            pl.BlockSpec((None, None, FOX_STATS, FOX_HEADS), lambda b, i: (b, i, 0, 0)),
        ],
        out_shape=[
            jax.ShapeDtypeStruct((bsz, s, nqk), BF16),
            jax.ShapeDtypeStruct((bsz, s, nqk), BF16),
            jax.ShapeDtypeStruct((bsz, s // tm, nqk, tm), BF16),
            jax.ShapeDtypeStruct((bsz, s // tm, FOX_STATS, FOX_HEADS), F32),
        ],
        scratch_shapes=[pltpu.VMEM((1, FOX_HEADS), F32)],
        compiler_params=_params("arbitrary", "arbitrary"),
        name="fox_proj",
    )(x, mod, gain, wq, wk, wvt, wf, bf, pq, pk, hsel)


def _swa_band_bias():
    w, group = SWA_WINDOW, SWA_Q_HEADS // SWA_KV_HEADS
    key = np.arange(2 * w)[:, None]
    qry = np.arange(group * w)[None, :] % w
    dist = (w + qry) - key
    band = (dist >= 0) & (dist < w)
    allowed = np.stack([band & (key >= w), band])
    return jnp.asarray(np.where(allowed, 0.0, NEG_BIG), F32)


def _swa_attn_kernel(sink_ref, bias_ref, q_ref, kc_ref, vtc_ref, o_ref, st_ref, kp_ref, vtp_ref):
    w = SWA_WINDOW
    group = SWA_Q_HEADS // SWA_KV_HEADS
    nq = group * w
    nb = q_ref.shape[0] // w
    lane = lax.broadcasted_iota(jnp.int32, (w, LANES), 1)
    low = lane < HEAD_DIM

    @pl.when(pl.program_id(1) == 0)
    def _():
        kp_ref[...] = jnp.zeros_like(kp_ref)
        vtp_ref[...] = jnp.zeros_like(vtp_ref)

    for u in range(nb):
        rows = slice(u * w, (u + 1) * w)
        for g in range(SWA_KV_HEADS):
            cols = slice(g * LANES, (g + 1) * LANES)
            slabs = []
            for hh in range(group):
                head = g * group + hh
                qs = q_ref[rows, (head // 2) * LANES:(head // 2 + 1) * LANES]
                keep = low if head % 2 == 0 else jnp.logical_not(low)
                slabs.append(jnp.where(keep, qs, jnp.zeros_like(qs)))
            q_stack = jnp.concatenate(slabs, axis=0)
            k_prev = kp_ref[:, cols] if u == 0 else kc_ref[(u - 1) * w:u * w, cols]
            k_both = jnp.concatenate([k_prev, kc_ref[rows, cols]], axis=0)
            st_ref[u * SWA_KV_HEADS + g] = _dot_nt(k_both, q_stack)
    for u in range(nb):
        rows = slice(u * w, (u + 1) * w)
        bias = bias_ref[jnp.minimum(pl.program_id(1), 1)] if u == 0 else bias_ref[1]
        for g in range(SWA_KV_HEADS):
            cols = slice(g * LANES, (g + 1) * LANES)
            st = st_ref[u * SWA_KV_HEADS + g] + bias
            sink = sink_ref[:, g * nq:(g + 1) * nq] * LOG2E
            m = jnp.maximum(jnp.max(st, axis=0, keepdims=True), sink)
            pt = jnp.exp2(st - m).astype(BF16)
            vt_prev = vtp_ref[cols, :] if u == 0 else vtc_ref[cols, (u - 1) * w:u * w]
            vt_both = jnp.concatenate([vt_prev, vtc_ref[cols, rows]], axis=1)
            acc = _dot(vt_both, pt)
            den = acc[HEAD_DIM:HEAD_DIM + 1, :] + jnp.exp2(sink - m)
            ot = acc[:HEAD_DIM, :] / den
            o_t = jnp.concatenate([ot[:, hh * w:(hh + 1) * w] for hh in range(group)], axis=0)
            o_ref[rows, g * group * HEAD_DIM:(g + 1) * group * HEAD_DIM] = o_t.T.astype(BF16)
    kp_ref[...] = kc_ref[(nb - 1) * w:, :]
    vtp_ref[...] = vtc_ref[:, (nb - 1) * w:]


def _swa_attn_call(q, k, vt, sinks, nb=4):
    bsz, s, d = q.shape
    w = SWA_WINDOW
    nkv = k.shape[2]
    sink_row = jnp.repeat(sinks, w).reshape(1, -1)
    bias = _swa_band_bias()
    cur = lambda b, i: (b, i, 0)
    return pl.pallas_call(
        _swa_attn_kernel,
        grid=(bsz, s // (nb * w)),
        in_specs=[
            _const_spec(sink_row.shape),
            _const_spec(bias.shape),
            pl.BlockSpec((None, nb * w, d), cur),
            pl.BlockSpec((None, nb * w, nkv), cur),
            pl.BlockSpec((None, nkv, nb * w), lambda b, i: (b, 0, i)),
        ],
        out_specs=pl.BlockSpec((None, nb * w, d), cur),
        out_shape=jax.ShapeDtypeStruct((bsz, s, d), BF16),
        scratch_shapes=[
            pltpu.VMEM((nb * SWA_KV_HEADS, 2 * w, (SWA_Q_HEADS // SWA_KV_HEADS) * w), F32),
            pltpu.VMEM((w, nkv), BF16),
            pltpu.VMEM((nkv, w), BF16),
        ],
        compiler_params=_params("arbitrary", "arbitrary"),
        name="swa_attn",
    )(sink_row, bias, q, k, vt)


def _chunk_cumsum(x):
    n = x.shape[0]
    row = lax.broadcasted_iota(jnp.int32, (n, n), 0)
    col = lax.broadcasted_iota(jnp.int32, (n, n), 1)
    same_chunk = (row // GLA_CHUNK) == (col // GLA_CHUNK)
    tril = jnp.where(jnp.logical_and(row >= col, same_chunk), 1.0, 0.0).astype(BF16)
    hi, mid, lo = _split3(x)
    return _dot(tril, hi) + _dot(tril, mid) + _dot(tril, lo)


def _gla_intra(q, k, b2):
    c, sub = GLA_CHUNK, GLA_SUB
    col = lax.broadcasted_iota(jnp.int32, (sub, c), 1)
    row = lax.broadcasted_iota(jnp.int32, (sub, c), 0)
    blocks = []
    for i in range(c // sub):
        lo = i * sub
        q_i = q[lo:lo + sub, :]
        b_i = b2[lo:lo + sub, :]
        if i == 0:
            a = jnp.zeros((sub, c), F32)
        else:
            ref = b2[lo - 1:lo, :]
            n = -(-lo // BF16_ROWS) * BF16_ROWS
            q_t = (q_i * jnp.exp2(b_i - ref)).astype(BF16)
            k_t = (k[:n, :] * jnp.exp2(jnp.minimum(ref - b2[:n, :], 0.0))).astype(BF16)
            if n < c:
                k_t = jnp.concatenate([k_t, jnp.zeros((c - n, k_t.shape[1]), BF16)], axis=0)
            a = _dot_nt(q_t, k_t)
        for s in range(lo, lo + sub):
            w = jnp.exp2(b_i - b2[s:s + 1, :])
            val = jnp.sum(q_i * k[s:s + 1, :] * w, axis=1, keepdims=True)
            a = jnp.where(col == s, val, a)
        blocks.append(jnp.where(row + lo >= col, a, 0.0))
    return jnp.concatenate(blocks, axis=0)


def _gla_kernel(q_ref, k_ref, v_ref, r_ref, la_ref, hn_ref, o_ref, state_ref):
    @pl.when(pl.program_id(2) == 0)
    def _():
        state_ref[...] = jnp.zeros_like(state_ref)

    for hh in range(state_ref.shape[0]):
        kq = slice(hh * GLA_DK, (hh + 1) * GLA_DK)
        vv = slice(hh * GLA_DV, (hh + 1) * GLA_DV)
        _gla_head(q_ref.at[:, kq], k_ref.at[:, kq], v_ref.at[:, vv], r_ref.at[:, vv], la_ref.at[:, kq],
                  hn_ref, o_ref.at[:, vv], state_ref.at[hh])


def _gla_head(q_ref, k_ref, v_ref, r_ref, la_ref, hn_ref, o_ref, state_ref):
    c = GLA_CHUNK
    nc = q_ref.shape[0] // c
    rows = [slice(ci * c, (ci + 1) * c) for ci in range(nc)]
    b2_all = _chunk_cumsum(la_ref[...]) * LOG2E
    q_all = q_ref[...].astype(F32) * (GLA_DK ** -0.5)
    k_all = k_ref[...].astype(F32)
    b2 = [b2_all[r] for r in rows]
    q = [q_all[r] for r in rows]
    k = [k_all[r] for r in rows]
    last = [b[c - 1:c, :] for b in b2]
    q_in = [(q[i] * jnp.exp2(b2[i])).astype(BF16) for i in range(nc)]
    k_out = [(k[i] * jnp.exp2(last[i] - b2[i])).astype(BF16) for i in range(nc)]
    kv = [_dot_tn(v_ref[rows[i], :], k_out[i]) for i in range(nc)]
    attn = [_gla_intra(q[i], k[i], b2[i]).astype(BF16) for i in range(nc)]
    intra = [_dot(attn[i], v_ref[rows[i], :]) for i in range(nc)]
    state_t = state_ref[...]
    for i in range(nc):
        o = intra[i] + _dot_nt(q_in[i], state_t.astype(BF16))
        state_t = state_t * jnp.exp2(last[i]) + kv[i]
        r = r_ref[rows[i], :].astype(F32)
        o_ref[rows[i], :] = (_rms(o, hn_ref[...]) * _silu(r)).astype(BF16)
    state_ref[...] = state_t


def _gla_call(q, k, v, r, la, head_norm, tm=256, nh=2):
    bsz, s, _ = q.shape
    dk, dv = GLA_DK, GLA_DV
    blk = lambda b, h, i: (b, i, h)
    return pl.pallas_call(
        _gla_kernel,
        grid=(bsz, GLA_HEADS // nh, s // tm),
        in_specs=[
            pl.BlockSpec((None, tm, nh * dk), blk),
            pl.BlockSpec((None, tm, nh * dk), blk),
            pl.BlockSpec((None, tm, nh * dv), blk),
            pl.BlockSpec((None, tm, nh * dv), blk),
            pl.BlockSpec((None, tm, nh * dk), blk),
            _const_spec((1, dv)),
        ],
        out_specs=pl.BlockSpec((None, tm, nh * dv), blk),
        out_shape=jax.ShapeDtypeStruct((bsz, s, GLA_HEADS * dv), BF16),
        scratch_shapes=[pltpu.VMEM((nh, dv, dk), F32)],
        compiler_params=_params("arbitrary", "arbitrary", "arbitrary"),
        name="gla_mix",
    )(q, k, v, r, la, head_norm)


def _fox_first_live_block(stats_ref, b, first_head, nh, i):
    heads = [first_head + e for e in range(nh)]
    q_max = [FOX_BOUND_SLACK * stats_ref[b, i, Q_MAX, hd] for hd in heads]
    margin = [stats_ref[b, i, LC_FIRST, hd] + FOX_BOUND_SLACK * stats_ref[b, i, QK_MAX, hd]
              - FOX_DEAD_LOG2 for hd in heads]

    def dead(j):
        is_dead = True
        for e, hd in enumerate(heads):
            gap = q_max[e] * stats_ref[b, j, K_MAX, hd] - stats_ref[b, j, LC_LAST, hd] + margin[e]
            is_dead = jnp.logical_and(is_dead, gap <= 0.0)
        return is_dead

    last = jnp.maximum(i - 1, 0)
    return lax.while_loop(lambda j: jnp.logical_and(j < i, dead(jnp.minimum(j, last))),
                          lambda j: j + 1, jnp.int32(0))


def _fox_attn_kernel(stats_ref, q_ref, k_ref, vt_ref, o_ref, m_ref, acc_ref, sa_ref, sb_ref):
    tk = vt_ref.shape[2]
    nh = m_ref.shape[0]
    i = pl.program_id(2)
    j0 = _fox_first_live_block(stats_ref, pl.program_id(0), pl.program_id(1) * nh, nh, i)
    m_ref[...] = jnp.full_like(m_ref, NEG_BIG)
    acc_ref[...] = jnp.zeros_like(acc_ref)

    half = tk // 2

    def scores(j, buf, e, diag=False):
        lanes = slice(e * LANES, (e + 1) * LANES)
        row0 = pl.multiple_of(j * tk, tk)
        if not diag:
            buf[e] = _dot_nt(k_ref[pl.ds(row0, tk), lanes], q_ref[:, lanes])
        else:
            buf[e, :half, :] = _dot_nt(k_ref[pl.ds(row0, half), lanes], q_ref[:, lanes])
            buf[e, half:, half:] = _dot_nt(k_ref[pl.ds(row0 + half, half), lanes], q_ref[half:, lanes])

    def online_update(e, st, vt, cols):
        m_old = m_ref[e, :, cols]
        m_new = jnp.maximum(m_old, jnp.max(st, axis=0, keepdims=True))
        alpha = jnp.exp2(m_old - m_new)
        pt = jnp.exp2(st - m_new).astype(BF16)
        acc_ref[e, :, cols] = alpha * acc_ref[e, :, cols] + _dot(vt, pt)
        m_ref[e, :, cols] = m_new

    def accumulate(j, buf, e, masked):
        rows = slice(e * LANES, (e + 1) * LANES)
        if not masked:
            online_update(e, buf[e], vt_ref[j, rows, :], slice(None))
            return
        key = lax.broadcasted_iota(jnp.int32, (half, half), 0)
        qry = lax.broadcasted_iota(jnp.int32, (half, half), 1)
        causal = key <= qry
        st = jnp.concatenate([jnp.where(causal, buf[e, :half, :half], NEG_BIG), buf[e, :half, half:]], axis=1)
        online_update(e, st, vt_ref[j, rows, :half], slice(None))
        online_update(e, jnp.where(causal, buf[e, half:, half:], NEG_BIG), vt_ref[j, rows, half:],
                      slice(half, None))

    def block(j, buf, masked, following):
        for e in range(nh):
            if e + 1 < nh:
                scores(j, buf, e + 1, masked)
            elif following is not None:
                scores(following[0], following[1], 0, following[2])
            accumulate(j, buf, e, masked)

    scores(j0, sa_ref, 0)
    n_full = i - j0

    def body(t, carry):
        j = j0 + 2 * t
        block(j, sa_ref, False, (j + 1, sb_ref, False))
        block(j + 1, sb_ref, False, (j + 2, sa_ref, False))
        return carry

    lax.fori_loop(0, n_full // 2, body, 0)

    @pl.when(n_full % 2 == 0)
    def _():
        block(i, sa_ref, True, None)

    @pl.when(n_full % 2 == 1)
    def _():
        block(i - 1, sa_ref, False, (i, sb_ref, True))
        block(i, sb_ref, True, None)

    outs = []
    for e in range(nh):
        acc = acc_ref[e]
        outs.append(acc[:HEAD_DIM, :] / acc[HEAD_DIM:HEAD_DIM + 1, :])
    o_ref[...] = jnp.concatenate(outs, axis=0).T.astype(BF16)


def _fox_attn_call(q, k, vt, stats, nh=4):
    bsz, s, _ = q.shape
    tk = vt.shape[3]
    tq = tk
    return pl.pallas_call(
        _fox_attn_kernel,
        grid=(bsz, FOX_HEADS // nh, s // tq),
        in_specs=[
            pl.BlockSpec(memory_space=pltpu.SMEM),
            pl.BlockSpec((None, tq, nh * LANES), lambda b, p, i: (b, i, p)),
            pl.BlockSpec((None, s, nh * LANES), lambda b, p, i: (b, 0, p)),
            pl.BlockSpec((None, s // tk, nh * LANES, tk), lambda b, p, i: (b, 0, p, 0)),
        ],
        out_specs=pl.BlockSpec((None, tq, nh * HEAD_DIM), lambda b, p, i: (b, i, p)),
        out_shape=jax.ShapeDtypeStruct((bsz, s, D_MODEL), BF16),
        scratch_shapes=[
            pltpu.VMEM((nh, 1, tq), F32),
            pltpu.VMEM((nh, LANES, tq), F32),
            pltpu.VMEM((nh, tk, tq), F32),
            pltpu.VMEM((nh, tk, tq), F32),
        ],
        compiler_params=_params("arbitrary", "arbitrary", "arbitrary"),
        name="fox_attn",
    )(stats, q, k, vt)


def _post_kernel(x_ref, o_ref, mod_ref, gain_ref, wo_ref, wgu_ref, wd_ref, fn_ref, out_ref,
                 *, ff_chunk, final):
    x1 = x_ref[...] + mod_ref[2:3, :] * _dot(o_ref[...], wo_ref[...])
    h = _norm_mod(x1, gain_ref[...], mod_ref[3:4, :], mod_ref[4:5, :]).astype(BF16)
    acc = jnp.zeros(x1.shape, F32)
    for c0 in range(0, D_FF, ff_chunk):
        g = _dot(h, wgu_ref[:, c0:c0 + ff_chunk])
        u = _dot(h, wgu_ref[:, D_FF + c0:D_FF + c0 + ff_chunk])
        acc = acc + _dot((_silu(g) * u).astype(BF16), wd_ref[c0:c0 + ff_chunk, :])
    x2 = x1 + mod_ref[5:6, :] * acc
    if final:
        x2 = _rms(x2, fn_ref[...])
    out_ref[...] = x2


def _layer_spec(stack, layer):
    return pl.BlockSpec((None,) + stack.shape[1:], lambda *_: (layer, 0, 0),
                        pipeline_mode=pl.Buffered(1))


def _post_call(x, o, mod, gain, wo, wgu_stack, wd_stack, layer, final_norm, final, tm=1024, ff_chunk=256):
    bsz, s, d = x.shape
    row = lambda b, i: (b, i, 0)
    return pl.pallas_call(
        functools.partial(_post_kernel, ff_chunk=ff_chunk, final=final),
        grid=(bsz, s // tm),
        in_specs=[
            pl.BlockSpec((None, tm, d), row),
            pl.BlockSpec((None, tm, d), row),
            pl.BlockSpec((None, 6, d), lambda b, i: (b, 0, 0)),
            _const_spec((1, d)),
            _const_spec(wo.shape),
            _layer_spec(wgu_stack, layer),
            _layer_spec(wd_stack, layer),
            _const_spec((1, d)),
        ],
        out_specs=pl.BlockSpec((None, tm, d), row),
        out_shape=jax.ShapeDtypeStruct((bsz, s, d), F32),
        compiler_params=_params("arbitrary", "arbitrary"),
        name="post_ffn",
    )(x, o, mod, gain, wo, wgu_stack, wd_stack, final_norm)


def _rope_tables(s):
    half = HEAD_DIM // 2
    inv = 1.0 / (ROPE_THETA ** (jnp.arange(0, HEAD_DIM, 2, dtype=F32) / HEAD_DIM))
    ang = jnp.arange(s, dtype=F32)[:, None] * inv[None, :]
    cos, sin = jnp.cos(ang), jnp.sin(ang)
    reps = LANES // HEAD_DIM
    cos_t = jnp.tile(jnp.concatenate([cos, cos], axis=1), (1, reps))
    sin_t = jnp.tile(jnp.concatenate([-sin, sin], axis=1), (1, reps))
    assert half * 2 == HEAD_DIM
    return cos_t, sin_t


def _dup_heads(w, heads):
    w3 = w.reshape(w.shape[0], heads, HEAD_DIM)
    return jnp.concatenate([w3, w3], axis=2).reshape(w.shape[0], heads * LANES)


def _pad_heads(w, heads):
    w3 = w.reshape(w.shape[0], heads, HEAD_DIM)
    return jnp.concatenate([w3, jnp.zeros_like(w3)], axis=2).reshape(w.shape[0], heads * LANES)


def _fox_placement():
    h = FOX_HEADS
    pq = np.zeros((4 * h, h * HEAD_DIM), np.float32)
    pk = np.zeros((4 * h, h * HEAD_DIM), np.float32)
    for head in range(h):
        base = (head // 2) * LANES + (HEAD_DIM if head % 2 == 0 else 0)
        for part in range(3):
            pq[part * h + head, base + part] = 1.0
            pk[3 * h + head, base + part] = 1.0
            pq[3 * h + head, base + 3 + part] = 1.0
            pk[part * h + head, base + 3 + part] = -1.0
    return jnp.asarray(pq, BF16), jnp.asarray(pk, BF16)


def kernel(x, c, ada_w, ada_b, norm_gain, ffn_w_gu, ffn_w_down, swa_w_in, swa_sinks, swa_w_o,
           gla_w_in, gla_w_gate_up, gla_b_gate, gla_head_norm, gla_w_o, fox_w_in, fox_b_f, fox_w_o,
           final_norm):
    bsz, s, d = x.shape
    depth = ada_w.shape[0]
    mod_all = _ada_call(c, ada_w, ada_b).reshape(depth, bsz, 6, d)
    cos_t, sin_t = _rope_tables(s)
    pq, pk = _fox_placement()
    fn = final_norm.reshape(1, d)
    wgu_stack = ffn_w_gu.astype(BF16)
    wd_stack = ffn_w_down.astype(BF16)

    for i in range(depth):
        kind, j = i % N_MIXERS, i // N_MIXERS
        mod = mod_all[i]
        gain1 = norm_gain[i, 0].reshape(1, d)
        gain2 = norm_gain[i, 1].reshape(1, d)
        if kind == 0:
            w = swa_w_in[j]
            nq, nkv = SWA_Q_HEADS * HEAD_DIM, SWA_KV_HEADS * HEAD_DIM
            w_all = jnp.concatenate([w[:, :nq], _dup_heads(w[:, nq:nq + nkv], SWA_KV_HEADS)],
                                    axis=1).astype(BF16)
            wvt = _pad_heads(w[:, nq + nkv:], SWA_KV_HEADS).T.astype(BF16)
            q, k, v = _swa_proj_call(x, mod, gain1, w_all, wvt, cos_t, sin_t)
            o = _swa_attn_call(q, k, v, swa_sinks[j])
            wo = swa_w_o[j]
        elif kind == 1:
            w = gla_w_in[j]
            n_main = 2 * GLA_HEADS * GLA_DK + 2 * GLA_HEADS * GLA_DV
            q, k, v, r, la = _gla_proj_call(
                x, mod, gain1, w[:, :n_main].astype(BF16), w[:, n_main:].astype(BF16),
                gla_w_gate_up[j].astype(BF16), gla_b_gate[j].reshape(1, -1))
            o = _gla_call(q, k, v, r, la, gla_head_norm[j].reshape(1, -1))
            wo = gla_w_o[j]
        else:
            w = fox_w_in[j]
            q, k, v, stats = _fox_proj_call(
                x, mod, gain1,
                w[:, :d].astype(BF16), w[:, d:2 * d].astype(BF16),
                w[:, 2 * d:3 * d].T.astype(BF16), w[:, 3 * d:].astype(BF16),
                fox_b_f[j].reshape(1, -1), pq, pk)
            o = _fox_attn_call(q, k, v, stats)
            wo = fox_w_o[j]
        x = _post_call(x, o, mod, gain2, wo.astype(BF16), wgu_stack, wd_stack, i, fn,
                       final=(i == depth - 1))
    return x
```

```python
import functools

import numpy as np
import jax
import jax.numpy as jnp
from jax import lax
from jax.experimental import pallas as pl
from jax.experimental.pallas import tpu as pltpu

D_MODEL = 1024
HEAD_DIM = 64
RMS_EPS = 1e-6
SWA_Q_HEADS = 16
SWA_KV_HEADS = 4
SWA_WINDOW = 128
ROPE_THETA = 150000.0
GLA_HEADS = 4
GLA_DK = 128
GLA_DV = 256
GLA_RANK = 16
GLA_TAU = 16.0
GLA_CHUNK = 64
GLA_SUB = 8
FOX_HEADS = 16
FOX_STATS = 8
Q_MAX, K_MAX, QK_MAX, LC_FIRST, LC_LAST = range(5)
FOX_DEAD_LOG2 = -160.0
FOX_BOUND_SLACK = 1.02
D_FF = 2816
N_MIXERS = 3

LANES = 128
BF16_ROWS = 16
NEG_BIG = -1e30
LOG2E = 1.4426950408889634
VMEM_LIMIT = 56 * 1024 * 1024

BF16 = jnp.bfloat16
F32 = jnp.float32


def _dot(a, b):
    return jnp.dot(a, b, preferred_element_type=F32)


def _dot_nt(a, b):
    return lax.dot_general(a, b, (((1,), (1,)), ((), ())), preferred_element_type=F32)


def _dot_tn(a, b):
    return lax.dot_general(a, b, (((0,), (0,)), ((), ())), preferred_element_type=F32)


def _split3(x):
    hi = x.astype(BF16)
    r1 = x - hi.astype(F32)
    mid = r1.astype(BF16)
    lo = (r1 - mid.astype(F32)).astype(BF16)
    return hi, mid, lo


def _cumsum_rows(x):
    n = x.shape[0]
    row = lax.broadcasted_iota(jnp.int32, (n, n), 0)
    col = lax.broadcasted_iota(jnp.int32, (n, n), 1)
    tril = jnp.where(row >= col, 1.0, 0.0).astype(BF16)
    hi, mid, lo = _split3(x)
    return _dot(tril, hi) + _dot(tril, mid) + _dot(tril, lo)


def _log_sigmoid(x):
    return jnp.minimum(x, 0.0) - jnp.log(1.0 + jnp.exp(-jnp.abs(x)))


def _silu(x):
    return x * (1.0 / (1.0 + jnp.exp(-x)))


def _rms(x, gain):
    ms = jnp.mean(x * x, axis=-1, keepdims=True)
    return x * lax.rsqrt(ms + RMS_EPS) * gain


def _norm_mod(x, gain, shift, scale):
    return _rms(x, gain) * (1.0 + scale) + shift


def _params(*sem):
    return pltpu.CompilerParams(dimension_semantics=sem, vmem_limit_bytes=VMEM_LIMIT)


def _const_spec(shape):
    nd = len(shape)
    return pl.BlockSpec(shape, lambda *_: (0,) * nd, pipeline_mode=pl.Buffered(1))


def _ada_kernel(ct_ref, w_ref, b_ref, out_ref):
    ca = _silu(ct_ref[...])
    w = w_ref[...]
    for b in range(ct_ref.shape[1]):
        col = ca[:, b:b + 1]
        out_ref[b:b + 1, :] = jnp.sum(col * w, axis=0, keepdims=True) + b_ref[...]


def _ada_call(c, ada_w, ada_b):
    depth, d, n = ada_w.shape
    bsz = c.shape[0]
    tn = 1536
    return pl.pallas_call(
        _ada_kernel,
        grid=(depth, n // tn),
        in_specs=[
            pl.BlockSpec((d, bsz), lambda l, j: (0, 0)),
            pl.BlockSpec((None, d, tn), lambda l, j: (l, 0, j)),
            pl.BlockSpec((None, 1, tn), lambda l, j: (l, 0, j)),
        ],
        out_specs=pl.BlockSpec((None, bsz, tn), lambda l, j: (l, 0, j)),
        out_shape=jax.ShapeDtypeStruct((depth, bsz, n), F32),
        compiler_params=_params("arbitrary", "arbitrary"),
        name="ada_mod",
    )(c.T, ada_w, ada_b.reshape(depth, 1, n))


def _rope(x, cos, sin_signed):
    width = x.shape[1]
    reps = width // cos.shape[1]
    c = jnp.tile(cos, (1, reps))
    s = jnp.tile(sin_signed, (1, reps))
    lane = lax.broadcasted_iota(jnp.int32, x.shape, 1)
    first_half = (lane % HEAD_DIM) < (HEAD_DIM // 2)
    rot = jnp.where(first_half,
                    pltpu.roll(x, width - HEAD_DIM // 2, 1),
                    pltpu.roll(x, HEAD_DIM // 2, 1))
    return x * c + rot * s


def _ones_row_64(vt):
    ones_row = lax.broadcasted_iota(jnp.int32, vt.shape, 0) % LANES == HEAD_DIM
    return jnp.where(ones_row, 1.0, vt)


def _swa_proj_kernel(x_ref, mod_ref, gain_ref, w_ref, wvt_ref, cos_ref, sin_ref, q_ref, k_ref, vt_ref):
    h = _norm_mod(x_ref[...], gain_ref[...], mod_ref[0:1, :], mod_ref[1:2, :]).astype(BF16)
    cos, sin = cos_ref[...], sin_ref[...]
    nq = q_ref.shape[1]
    q = _dot(h, w_ref[:, :nq])
    q_ref[...] = (_rope(q, cos, sin) * (HEAD_DIM ** -0.5 * LOG2E)).astype(BF16)
    k = _dot(h, w_ref[:, nq:])
    k_ref[...] = _rope(k, cos, sin).astype(BF16)
    vt_ref[...] = _ones_row_64(_dot_nt(wvt_ref[...], h)).astype(BF16)


def _swa_proj_call(x, mod, gain, w, wvt, cos, sin, tm=1024):
    bsz, s, d = x.shape
    nq, nkv = D_MODEL, SWA_KV_HEADS * LANES
    row = lambda b, i: (b, i, 0)
    return pl.pallas_call(
        _swa_proj_kernel,
        grid=(bsz, s // tm),
        in_specs=[
            pl.BlockSpec((None, tm, d), row),
            pl.BlockSpec((None, 6, d), lambda b, i: (b, 0, 0)),
            _const_spec((1, d)),
            _const_spec(w.shape),
            _const_spec(wvt.shape),
            pl.BlockSpec((tm, LANES), lambda b, i: (i, 0)),
            pl.BlockSpec((tm, LANES), lambda b, i: (i, 0)),
        ],
        out_specs=[
            pl.BlockSpec((None, tm, nq), row),
            pl.BlockSpec((None, tm, nkv), row),
            pl.BlockSpec((None, nkv, tm), lambda b, i: (b, 0, i)),
        ],
        out_shape=[
            jax.ShapeDtypeStruct((bsz, s, nq), BF16),
            jax.ShapeDtypeStruct((bsz, s, nkv), BF16),
            jax.ShapeDtypeStruct((bsz, nkv, s), BF16),
        ],
        compiler_params=_params("arbitrary", "arbitrary"),
        name="swa_proj",
    )(x, mod, gain, w, wvt, cos, sin)


def _gla_proj_kernel(x_ref, mod_ref, gain_ref, w_ref, wa_ref, wg_ref, bg_ref,
                     q_ref, k_ref, v_ref, r_ref, la_ref):
    h = _norm_mod(x_ref[...], gain_ref[...], mod_ref[0:1, :], mod_ref[1:2, :]).astype(BF16)
    nk = q_ref.shape[1]
    nv = v_ref.shape[1]
    q_ref[...] = _dot(h, w_ref[:, :nk]).astype(BF16)
    k_ref[...] = _dot(h, w_ref[:, nk:2 * nk]).astype(BF16)
    v_ref[...] = _dot(h, w_ref[:, 2 * nk:2 * nk + nv]).astype(BF16)
    r_ref[...] = _dot(h, w_ref[:, 2 * nk + nv:]).astype(BF16)
    a_low = _dot(h, wa_ref[...]).astype(BF16)
    z = _dot(a_low, wg_ref[...]) + bg_ref[...]
    la_ref[...] = _log_sigmoid(z) * (1.0 / GLA_TAU)


def _gla_proj_call(x, mod, gain, w, wa, wg, bg, tm=1024):
    bsz, s, d = x.shape
    nk, nv = GLA_HEADS * GLA_DK, GLA_HEADS * GLA_DV
    row = lambda b, i: (b, i, 0)
    return pl.pallas_call(
        _gla_proj_kernel,
        grid=(bsz, s // tm),
        in_specs=[
            pl.BlockSpec((None, tm, d), row),
            pl.BlockSpec((None, 6, d), lambda b, i: (b, 0, 0)),
            _const_spec((1, d)),
            _const_spec(w.shape),
            _const_spec(wa.shape),
            _const_spec(wg.shape),
            _const_spec(bg.shape),
        ],
        out_specs=[
            pl.BlockSpec((None, tm, nk), row),
            pl.BlockSpec((None, tm, nk), row),
            pl.BlockSpec((None, tm, nv), row),
            pl.BlockSpec((None, tm, nv), row),
            pl.BlockSpec((None, tm, nk), row),
        ],
        out_shape=[
            jax.ShapeDtypeStruct((bsz, s, nk), BF16),
            jax.ShapeDtypeStruct((bsz, s, nk), BF16),
            jax.ShapeDtypeStruct((bsz, s, nv), BF16),
            jax.ShapeDtypeStruct((bsz, s, nv), BF16),
            jax.ShapeDtypeStruct((bsz, s, nk), F32),
        ],
        compiler_params=_params("arbitrary", "arbitrary"),
        name="gla_proj",
    )(x, mod, gain, w, wa, wg, bg)


def _spread_heads(x, extra, out_ref):
    lane = lax.broadcasted_iota(jnp.int32, (x.shape[0], LANES), 1)
    low = lane < HEAD_DIM
    for p in range(x.shape[1] // LANES):
        xs = x[:, p * LANES:(p + 1) * LANES]
        ex = extra[:, p * LANES:(p + 1) * LANES]
        out_ref[:, (2 * p) * LANES:(2 * p + 1) * LANES] = jnp.where(low, xs, ex).astype(out_ref.dtype)
        odd = pltpu.roll(jnp.where(low, ex, xs), HEAD_DIM, 1)
        out_ref[:, (2 * p + 1) * LANES:(2 * p + 2) * LANES] = odd.astype(out_ref.dtype)


def _fox_proj_kernel(x_ref, mod_ref, gain_ref, wq_ref, wk_ref, wvt_ref, wf_ref, bf_ref,
                     pq_ref, pk_ref, hsel_ref, q_ref, k_ref, vt_ref, stats_ref, carry_ref):
    @pl.when(pl.program_id(1) == 0)
    def _():
        carry_ref[...] = jnp.zeros_like(carry_ref)

    h = _norm_mod(x_ref[...], gain_ref[...], mod_ref[0:1, :], mod_ref[1:2, :]).astype(BF16)
    log_f = _log_sigmoid(_dot(h, wf_ref[...]) + bf_ref[...])
    lc = _cumsum_rows(log_f) + carry_ref[...]
    carry_ref[...] = lc[lc.shape[0] - 1:, :]
    lc2 = lc * LOG2E
    hi, mid, lo = _split3(lc2)
    aug = jnp.concatenate([hi, mid, lo, jnp.ones_like(hi)], axis=1)
    qs = _dot(h, wq_ref[...]) * (HEAD_DIM ** -0.5 * LOG2E)
    ks = _dot(h, wk_ref[...])
    _spread_heads(qs, _dot(aug, pq_ref[...]), q_ref)
    _spread_heads(ks, _dot(aug, pk_ref[...]), k_ref)
    qn2 = _dot((qs * qs).astype(BF16), hsel_ref[...])
    kn2 = _dot((ks * ks).astype(BF16), hsel_ref[...])
    tm = lc2.shape[0]
    stats_ref[...] = jnp.concatenate([
        jnp.sqrt(jnp.max(qn2, axis=0, keepdims=True)),
        jnp.sqrt(jnp.max(kn2, axis=0, keepdims=True)),
        jnp.sqrt(jnp.max(qn2 * kn2, axis=0, keepdims=True)),
        lc2[0:1, :], lc2[tm - 1:tm, :],
        jnp.zeros((FOX_STATS - 5, lc2.shape[1]), F32)], axis=0)
    vt = _dot_nt(wvt_ref[...], h).astype(BF16)
    pad = jnp.where(lax.broadcasted_iota(jnp.int32, (HEAD_DIM, vt.shape[1]), 0) == 0,
                    1.0, 0.0).astype(BF16)
    for hd in range(FOX_HEADS):
        vt_ref[hd * LANES:hd * LANES + HEAD_DIM, :] = vt[hd * HEAD_DIM:(hd + 1) * HEAD_DIM, :]
        vt_ref[hd * LANES + HEAD_DIM:(hd + 1) * LANES, :] = pad


def _fox_proj_call(x, mod, gain, wq, wk, wvt, wf, bf, pq, pk, tm=512):
    bsz, s, d = x.shape
    nqk = FOX_HEADS * LANES
    row = lambda b, i: (b, i, 0)
    hsel = jnp.asarray(np.repeat(np.eye(FOX_HEADS, dtype=np.float32), HEAD_DIM, axis=0), BF16)
    return pl.pallas_call(
        _fox_proj_kernel,
        grid=(bsz, s // tm),
        in_specs=[
            pl.BlockSpec((None, tm, d), row),
            pl.BlockSpec((None, 6, d), lambda b, i: (b, 0, 0)),
            _const_spec((1, d)),
            _const_spec(wq.shape),
            _const_spec(wk.shape),
            _const_spec(wvt.shape),
            _const_spec(wf.shape),
            _const_spec(bf.shape),
            _const_spec(pq.shape),
            _const_spec(pk.shape),
            _const_spec(hsel.shape),
        ],
        out_specs=[
            pl.BlockSpec((None, tm, nqk), row),
            pl.BlockSpec((None, tm, nqk), row),
            pl.BlockSpec((None, None, nqk, tm), lambda b, i: (b, i, 0, 0)),
            pl.BlockSpec((None, None, FOX_STATS, FOX_HEADS), lambda b, i: (b, i, 0, 0)),
        ],
        out_shape=[
            jax.ShapeDtypeStruct((bsz, s, nqk), BF16),
            jax.ShapeDtypeStruct((bsz, s, nqk), BF16),
            jax.ShapeDtypeStruct((bsz, s // tm, nqk, tm), BF16),
            jax.ShapeDtypeStruct((bsz, s // tm, FOX_STATS, FOX_HEADS), F32),
        ],
        scratch_shapes=[pltpu.VMEM((1, FOX_HEADS), F32)],
        compiler_params=_params("arbitrary", "arbitrary"),
        name="fox_proj",
    )(x, mod, gain, wq, wk, wvt, wf, bf, pq, pk, hsel)


def _swa_band_bias():
    w, group = SWA_WINDOW, SWA_Q_HEADS // SWA_KV_HEADS
    key = np.arange(2 * w)[:, None]
    qry = np.arange(group * w)[None, :] % w
    dist = (w + qry) - key
    band = (dist >= 0) & (dist < w)
    allowed = np.stack([band & (key >= w), band])
    return jnp.asarray(np.where(allowed, 0.0, NEG_BIG), F32)


def _swa_attn_kernel(sink_ref, bias_ref, q_ref, kc_ref, vtc_ref, o_ref, st_ref, kp_ref, vtp_ref):
    w = SWA_WINDOW
    group = SWA_Q_HEADS // SWA_KV_HEADS
    nq = group * w
    nb = q_ref.shape[0] // w
    lane = lax.broadcasted_iota(jnp.int32, (w, LANES), 1)
    low = lane < HEAD_DIM

    @pl.when(pl.program_id(1) == 0)
    def _():
        kp_ref[...] = jnp.zeros_like(kp_ref)
        vtp_ref[...] = jnp.zeros_like(vtp_ref)

    for u in range(nb):
        rows = slice(u * w, (u + 1) * w)
        for g in range(SWA_KV_HEADS):
            cols = slice(g * LANES, (g + 1) * LANES)
            slabs = []
            for hh in range(group):
                head = g * group + hh
                qs = q_ref[rows, (head // 2) * LANES:(head // 2 + 1) * LANES]
                keep = low if head % 2 == 0 else jnp.logical_not(low)
                slabs.append(jnp.where(keep, qs, jnp.zeros_like(qs)))
            q_stack = jnp.concatenate(slabs, axis=0)
            k_prev = kp_ref[:, cols] if u == 0 else kc_ref[(u - 1) * w:u * w, cols]
            k_both = jnp.concatenate([k_prev, kc_ref[rows, cols]], axis=0)
            st_ref[u * SWA_KV_HEADS + g] = _dot_nt(k_both, q_stack)
    for u in range(nb):
        rows = slice(u * w, (u + 1) * w)
        bias = bias_ref[jnp.minimum(pl.program_id(1), 1)] if u == 0 else bias_ref[1]
        for g in range(SWA_KV_HEADS):
            cols = slice(g * LANES, (g + 1) * LANES)
            st = st_ref[u * SWA_KV_HEADS + g] + bias
            sink = sink_ref[:, g * nq:(g + 1) * nq] * LOG2E
            m = jnp.maximum(jnp.max(st, axis=0, keepdims=True), sink)
            pt = jnp.exp2(st - m).astype(BF16)
            vt_prev = vtp_ref[cols, :] if u == 0 else vtc_ref[cols, (u - 1) * w:u * w]
            vt_both = jnp.concatenate([vt_prev, vtc_ref[cols, rows]], axis=1)
            acc = _dot(vt_both, pt)
            den = acc[HEAD_DIM:HEAD_DIM + 1, :] + jnp.exp2(sink - m)
            ot = acc[:HEAD_DIM, :] / den
            o_t = jnp.concatenate([ot[:, hh * w:(hh + 1) * w] for hh in range(group)], axis=0)
            o_ref[rows, g * group * HEAD_DIM:(g + 1) * group * HEAD_DIM] = o_t.T.astype(BF16)
    kp_ref[...] = kc_ref[(nb - 1) * w:, :]
    vtp_ref[...] = vtc_ref[:, (nb - 1) * w:]


def _swa_attn_call(q, k, vt, sinks, nb=4):
    bsz, s, d = q.shape
    w = SWA_WINDOW
    nkv = k.shape[2]
    sink_row = jnp.repeat(sinks, w).reshape(1, -1)
    bias = _swa_band_bias()
    cur = lambda b, i: (b, i, 0)
    return pl.pallas_call(
        _swa_attn_kernel,
        grid=(bsz, s // (nb * w)),
        in_specs=[
            _const_spec(sink_row.shape),
            _const_spec(bias.shape),
            pl.BlockSpec((None, nb * w, d), cur),
            pl.BlockSpec((None, nb * w, nkv), cur),
            pl.BlockSpec((None, nkv, nb * w), lambda b, i: (b, 0, i)),
        ],
        out_specs=pl.BlockSpec((None, nb * w, d), cur),
        out_shape=jax.ShapeDtypeStruct((bsz, s, d), BF16),
        scratch_shapes=[
            pltpu.VMEM((nb * SWA_KV_HEADS, 2 * w, (SWA_Q_HEADS // SWA_KV_HEADS) * w), F32),
            pltpu.VMEM((w, nkv), BF16),
            pltpu.VMEM((nkv, w), BF16),
        ],
        compiler_params=_params("arbitrary", "arbitrary"),
        name="swa_attn",
    )(sink_row, bias, q, k, vt)


def _chunk_cumsum(x):
    n = x.shape[0]
    row = lax.broadcasted_iota(jnp.int32, (n, n), 0)
    col = lax.broadcasted_iota(jnp.int32, (n, n), 1)
    same_chunk = (row // GLA_CHUNK) == (col // GLA_CHUNK)
    tril = jnp.where(jnp.logical_and(row >= col, same_chunk), 1.0, 0.0).astype(BF16)
    hi, mid, lo = _split3(x)
    return _dot(tril, hi) + _dot(tril, mid) + _dot(tril, lo)


def _gla_intra(q, k, b2):
    c, sub = GLA_CHUNK, GLA_SUB
    col = lax.broadcasted_iota(jnp.int32, (sub, c), 1)
    row = lax.broadcasted_iota(jnp.int32, (sub, c), 0)
    blocks = []
    for i in range(c // sub):
        lo = i * sub
        q_i = q[lo:lo + sub, :]
        b_i = b2[lo:lo + sub, :]
        if i == 0:
            a = jnp.zeros((sub, c), F32)
        else:
            ref = b2[lo - 1:lo, :]
            n = -(-lo // BF16_ROWS) * BF16_ROWS
            q_t = (q_i * jnp.exp2(b_i - ref)).astype(BF16)
            k_t = (k[:n, :] * jnp.exp2(jnp.minimum(ref - b2[:n, :], 0.0))).astype(BF16)
            if n < c:
                k_t = jnp.concatenate([k_t, jnp.zeros((c - n, k_t.shape[1]), BF16)], axis=0)
            a = _dot_nt(q_t, k_t)
        for s in range(lo, lo + sub):
            w = jnp.exp2(b_i - b2[s:s + 1, :])
            val = jnp.sum(q_i * k[s:s + 1, :] * w, axis=1, keepdims=True)
            a = jnp.where(col == s, val, a)
        blocks.append(jnp.where(row + lo >= col, a, 0.0))
    return jnp.concatenate(blocks, axis=0)


def _gla_kernel(q_ref, k_ref, v_ref, r_ref, la_ref, hn_ref, o_ref, state_ref):
    @pl.when(pl.program_id(2) == 0)
    def _():
        state_ref[...] = jnp.zeros_like(state_ref)

    for hh in range(state_ref.shape[0]):
        kq = slice(hh * GLA_DK, (hh + 1) * GLA_DK)
        vv = slice(hh * GLA_DV, (hh + 1) * GLA_DV)
        _gla_head(q_ref.at[:, kq], k_ref.at[:, kq], v_ref.at[:, vv], r_ref.at[:, vv], la_ref.at[:, kq],
                  hn_ref, o_ref.at[:, vv], state_ref.at[hh])


def _gla_head(q_ref, k_ref, v_ref, r_ref, la_ref, hn_ref, o_ref, state_ref):
    c = GLA_CHUNK
    nc = q_ref.shape[0] // c
    rows = [slice(ci * c, (ci + 1) * c) for ci in range(nc)]
    b2_all = _chunk_cumsum(la_ref[...]) * LOG2E
    q_all = q_ref[...].astype(F32) * (GLA_DK ** -0.5)
    k_all = k_ref[...].astype(F32)
    b2 = [b2_all[r] for r in rows]
    q = [q_all[r] for r in rows]
    k = [k_all[r] for r in rows]
    last = [b[c - 1:c, :] for b in b2]
    q_in = [(q[i] * jnp.exp2(b2[i])).astype(BF16) for i in range(nc)]
    k_out = [(k[i] * jnp.exp2(last[i] - b2[i])).astype(BF16) for i in range(nc)]
    kv = [_dot_tn(v_ref[rows[i], :], k_out[i]) for i in range(nc)]
    attn = [_gla_intra(q[i], k[i], b2[i]).astype(BF16) for i in range(nc)]
    intra = [_dot(attn[i], v_ref[rows[i], :]) for i in range(nc)]
    state_t = state_ref[...]
    for i in range(nc):
        o = intra[i] + _dot_nt(q_in[i], state_t.astype(BF16))
        state_t = state_t * jnp.exp2(last[i]) + kv[i]
        r = r_ref[rows[i], :].astype(F32)
        o_ref[rows[i], :] = (_rms(o, hn_ref[...]) * _silu(r)).astype(BF16)
    state_ref[...] = state_t


def _gla_call(q, k, v, r, la, head_norm, tm=256, nh=2):
    bsz, s, _ = q.shape
    dk, dv = GLA_DK, GLA_DV
    blk = lambda b, h, i: (b, i, h)
    return pl.pallas_call(
        _gla_kernel,
        grid=(bsz, GLA_HEADS // nh, s // tm),
        in_specs=[
            pl.BlockSpec((None, tm, nh * dk), blk),
            pl.BlockSpec((None, tm, nh * dk), blk),
            pl.BlockSpec((None, tm, nh * dv), blk),
            pl.BlockSpec((None, tm, nh * dv), blk),
            pl.BlockSpec((None, tm, nh * dk), blk),
            _const_spec((1, dv)),
        ],
        out_specs=pl.BlockSpec((None, tm, nh * dv), blk),
        out_shape=jax.ShapeDtypeStruct((bsz, s, GLA_HEADS * dv), BF16),
        scratch_shapes=[pltpu.VMEM((nh, dv, dk), F32)],
        compiler_params=_params("arbitrary", "arbitrary", "arbitrary"),
        name="gla_mix",
    )(q, k, v, r, la, head_norm)


def _fox_first_live_block(stats_ref, b, first_head, nh, i):
    heads = [first_head + e for e in range(nh)]
    q_max = [FOX_BOUND_SLACK * stats_ref[b, i, Q_MAX, hd] for hd in heads]
    margin = [stats_ref[b, i, LC_FIRST, hd] + FOX_BOUND_SLACK * stats_ref[b, i, QK_MAX, hd]
              - FOX_DEAD_LOG2 for hd in heads]

    def dead(j):
        is_dead = True
        for e, hd in enumerate(heads):
            gap = q_max[e] * stats_ref[b, j, K_MAX, hd] - stats_ref[b, j, LC_LAST, hd] + margin[e]
            is_dead = jnp.logical_and(is_dead, gap <= 0.0)
        return is_dead

    last = jnp.maximum(i - 1, 0)
    return lax.while_loop(lambda j: jnp.logical_and(j < i, dead(jnp.minimum(j, last))),
                          lambda j: j + 1, jnp.int32(0))


def _fox_attn_kernel(stats_ref, q_ref, k_ref, vt_ref, o_ref, m_ref, acc_ref, sa_ref, sb_ref):
    tk = vt_ref.shape[2]
    nh = m_ref.shape[0]
    i = pl.program_id(2)
    j0 = _fox_first_live_block(stats_ref, pl.program_id(0), pl.program_id(1) * nh, nh, i)
    m_ref[...] = jnp.full_like(m_ref, NEG_BIG)
    acc_ref[...] = jnp.zeros_like(acc_ref)

    half = tk // 2

    def scores(j, buf, e, diag=False):
        lanes = slice(e * LANES, (e + 1) * LANES)
        row0 = pl.multiple_of(j * tk, tk)
        if not diag:
            buf[e] = _dot_nt(k_ref[pl.ds(row0, tk), lanes], q_ref[:, lanes])
        else:
            buf[e, :half, :] = _dot_nt(k_ref[pl.ds(row0, half), lanes], q_ref[:, lanes])
            buf[e, half:, half:] = _dot_nt(k_ref[pl.ds(row0 + half, half), lanes], q_ref[half:, lanes])

    def online_update(e, st, vt, cols):
        m_old = m_ref[e, :, cols]
        m_new = jnp.maximum(m_old, jnp.max(st, axis=0, keepdims=True))
        alpha = jnp.exp2(m_old - m_new)
        pt = jnp.exp2(st - m_new).astype(BF16)
        acc_ref[e, :, cols] = alpha * acc_ref[e, :, cols] + _dot(vt, pt)
        m_ref[e, :, cols] = m_new

    def accumulate(j, buf, e, masked):
        rows = slice(e * LANES, (e + 1) * LANES)
        if not masked:
            online_update(e, buf[e], vt_ref[j, rows, :], slice(None))
            return
        key = lax.broadcasted_iota(jnp.int32, (half, half), 0)
        qry = lax.broadcasted_iota(jnp.int32, (half, half), 1)
        causal = key <= qry
        st = jnp.concatenate([jnp.where(causal, buf[e, :half, :half], NEG_BIG), buf[e, :half, half:]], axis=1)
        online_update(e, st, vt_ref[j, rows, :half], slice(None))
        online_update(e, jnp.where(causal, buf[e, half:, half:], NEG_BIG), vt_ref[j, rows, half:],
                      slice(half, None))

    def block(j, buf, masked, following):
        for e in range(nh):
            if e + 1 < nh:
                scores(j, buf, e + 1, masked)
            elif following is not None:
                scores(following[0], following[1], 0, following[2])
            accumulate(j, buf, e, masked)

    scores(j0, sa_ref, 0)
    n_full = i - j0

    def body(t, carry):
        j = j0 + 2 * t
        block(j, sa_ref, False, (j + 1, sb_ref, False))
        block(j + 1, sb_ref, False, (j + 2, sa_ref, False))
        return carry

    lax.fori_loop(0, n_full // 2, body, 0)

    @pl.when(n_full % 2 == 0)
    def _():
        block(i, sa_ref, True, None)

    @pl.when(n_full % 2 == 1)
    def _():
        block(i - 1, sa_ref, False, (i, sb_ref, True))
        block(i, sb_ref, True, None)

    outs = []
    for e in range(nh):
        acc = acc_ref[e]
        outs.append(acc[:HEAD_DIM, :] / acc[HEAD_DIM:HEAD_DIM + 1, :])
    o_ref[...] = jnp.concatenate(outs, axis=0).T.astype(BF16)


def _fox_attn_call(q, k, vt, stats, nh=4):
    bsz, s, _ = q.shape
    tk = vt.shape[3]
    tq = tk
    return pl.pallas_call(
        _fox_attn_kernel,
        grid=(bsz, FOX_HEADS // nh, s // tq),
        in_specs=[
            pl.BlockSpec(memory_space=pltpu.SMEM),
            pl.BlockSpec((None, tq, nh * LANES), lambda b, p, i: (b, i, p)),
            pl.BlockSpec((None, s, nh * LANES), lambda b, p, i: (b, 0, p)),
            pl.BlockSpec((None, s // tk, nh * LANES, tk), lambda b, p, i: (b, 0, p, 0)),
        ],
        out_specs=pl.BlockSpec((None, tq, nh * HEAD_DIM), lambda b, p, i: (b, i, p)),
        out_shape=jax.ShapeDtypeStruct((bsz, s, D_MODEL), BF16),
        scratch_shapes=[
            pltpu.VMEM((nh, 1, tq), F32),
            pltpu.VMEM((nh, LANES, tq), F32),
            pltpu.VMEM((nh, tk, tq), F32),
            pltpu.VMEM((nh, tk, tq), F32),
        ],
        compiler_params=_params("arbitrary", "arbitrary", "arbitrary"),
        name="fox_attn",
    )(stats, q, k, vt)


def _post_kernel(x_ref, o_ref, mod_ref, gain_ref, wo_ref, wgu_ref, wd_ref, fn_ref, out_ref,
                 *, ff_chunk, final):
    x1 = x_ref[...] + mod_ref[2:3, :] * _dot(o_ref[...], wo_ref[...])
    h = _norm_mod(x1, gain_ref[...], mod_ref[3:4, :], mod_ref[4:5, :]).astype(BF16)
    acc = jnp.zeros(x1.shape, F32)
    for c0 in range(0, D_FF, ff_chunk):
        g = _dot(h, wgu_ref[:, c0:c0 + ff_chunk])
        u = _dot(h, wgu_ref[:, D_FF + c0:D_FF + c0 + ff_chunk])
        acc = acc + _dot((_silu(g) * u).astype(BF16), wd_ref[c0:c0 + ff_chunk, :])
    x2 = x1 + mod_ref[5:6, :] * acc
    if final:
        x2 = _rms(x2, fn_ref[...])
    out_ref[...] = x2


def _layer_spec(stack, layer):
    return pl.BlockSpec((None,) + stack.shape[1:], lambda *_: (layer, 0, 0),
                        pipeline_mode=pl.Buffered(1))


def _post_call(x, o, mod, gain, wo, wgu_stack, wd_stack, layer, final_norm, final, tm=1024, ff_chunk=256):
    bsz, s, d = x.shape
    row = lambda b, i: (b, i, 0)
    return pl.pallas_call(
        functools.partial(_post_kernel, ff_chunk=ff_chunk, final=final),
        grid=(bsz, s // tm),
        in_specs=[
            pl.BlockSpec((None, tm, d), row),
            pl.BlockSpec((None, tm, d), row),
            pl.BlockSpec((None, 6, d), lambda b, i: (b, 0, 0)),
            _const_spec((1, d)),
            _const_spec(wo.shape),
            _layer_spec(wgu_stack, layer),
            _layer_spec(wd_stack, layer),
            _const_spec((1, d)),
        ],
        out_specs=pl.BlockSpec((None, tm, d), row),
        out_shape=jax.ShapeDtypeStruct((bsz, s, d), F32),
        compiler_params=_params("arbitrary", "arbitrary"),
        name="post_ffn",
    )(x, o, mod, gain, wo, wgu_stack, wd_stack, final_norm)


def _rope_tables(s):
    half = HEAD_DIM // 2
    inv = 1.0 / (ROPE_THETA ** (jnp.arange(0, HEAD_DIM, 2, dtype=F32) / HEAD_DIM))
    ang = jnp.arange(s, dtype=F32)[:, None] * inv[None, :]
    cos, sin = jnp.cos(ang), jnp.sin(ang)
    reps = LANES // HEAD_DIM
    cos_t = jnp.tile(jnp.concatenate([cos, cos], axis=1), (1, reps))
    sin_t = jnp.tile(jnp.concatenate([-sin, sin], axis=1), (1, reps))
    assert half * 2 == HEAD_DIM
    return cos_t, sin_t


def _dup_heads(w, heads):
    w3 = w.reshape(w.shape[0], heads, HEAD_DIM)
    return jnp.concatenate([w3, w3], axis=2).reshape(w.shape[0], heads * LANES)


def _pad_heads(w, heads):
    w3 = w.reshape(w.shape[0], heads, HEAD_DIM)
    return jnp.concatenate([w3, jnp.zeros_like(w3)], axis=2).reshape(w.shape[0], heads * LANES)


def _fox_placement():
    h = FOX_HEADS
    pq = np.zeros((4 * h, h * HEAD_DIM), np.float32)
    pk = np.zeros((4 * h, h * HEAD_DIM), np.float32)
    for head in range(h):
        base = (head // 2) * LANES + (HEAD_DIM if head % 2 == 0 else 0)
        for part in range(3):
            pq[part * h + head, base + part] = 1.0
            pk[3 * h + head, base + part] = 1.0
            pq[3 * h + head, base + 3 + part] = 1.0
            pk[part * h + head, base + 3 + part] = -1.0
    return jnp.asarray(pq, BF16), jnp.asarray(pk, BF16)


def kernel(x, c, ada_w, ada_b, norm_gain, ffn_w_gu, ffn_w_down, swa_w_in, swa_sinks, swa_w_o,
           gla_w_in, gla_w_gate_up, gla_b_gate, gla_head_norm, gla_w_o, fox_w_in, fox_b_f, fox_w_o,
           final_norm):
    bsz, s, d = x.shape
    depth = ada_w.shape[0]
    mod_all = _ada_call(c, ada_w, ada_b).reshape(depth, bsz, 6, d)
    cos_t, sin_t = _rope_tables(s)
    pq, pk = _fox_placement()
    fn = final_norm.reshape(1, d)
    wgu_stack = ffn_w_gu.astype(BF16)
    wd_stack = ffn_w_down.astype(BF16)

    for i in range(depth):
        kind, j = i % N_MIXERS, i // N_MIXERS
        mod = mod_all[i]
        gain1 = norm_gain[i, 0].reshape(1, d)
        gain2 = norm_gain[i, 1].reshape(1, d)
        if kind == 0:
            w = swa_w_in[j]
            nq, nkv = SWA_Q_HEADS * HEAD_DIM, SWA_KV_HEADS * HEAD_DIM
            w_all = jnp.concatenate([w[:, :nq], _dup_heads(w[:, nq:nq + nkv], SWA_KV_HEADS)],
                                    axis=1).astype(BF16)
            wvt = _pad_heads(w[:, nq + nkv:], SWA_KV_HEADS).T.astype(BF16)
            q, k, v = _swa_proj_call(x, mod, gain1, w_all, wvt, cos_t, sin_t)
            o = _swa_attn_call(q, k, v, swa_sinks[j])
            wo = swa_w_o[j]
        elif kind == 1:
            w = gla_w_in[j]
            n_main = 2 * GLA_HEADS * GLA_DK + 2 * GLA_HEADS * GLA_DV
            q, k, v, r, la = _gla_proj_call(
                x, mod, gain1, w[:, :n_main].astype(BF16), w[:, n_main:].astype(BF16),
                gla_w_gate_up[j].astype(BF16), gla_b_gate[j].reshape(1, -1))
            o = _gla_call(q, k, v, r, la, gla_head_norm[j].reshape(1, -1))
            wo = gla_w_o[j]
        else:
            w = fox_w_in[j]
            order = jnp.argsort(fox_b_f[j])
            by_head = lambda m: jnp.take(m.reshape(d, FOX_HEADS, HEAD_DIM), order, axis=1).reshape(d, d)
            q, k, v, stats = _fox_proj_call(
                x, mod, gain1,
                by_head(w[:, :d]).astype(BF16), by_head(w[:, d:2 * d]).astype(BF16),
                by_head(w[:, 2 * d:3 * d]).T.astype(BF16), jnp.take(w[:, 3 * d:], order, axis=1).astype(BF16),
                jnp.take(fox_b_f[j], order).reshape(1, -1), pq, pk)
            o = _fox_attn_call(q, k, v, stats)
            wo = jnp.take(fox_w_o[j].reshape(FOX_HEADS, HEAD_DIM, d), order, axis=0).reshape(d, d)
        x = _post_call(x, o, mod, gain2, wo.astype(BF16), wgu_stack, wd_stack, i, fn,
                       final=(i == depth - 1))
    return x
```

```python
import functools

import numpy as np
import jax
import jax.numpy as jnp
from jax import lax
from jax.experimental import pallas as pl
from jax.experimental.pallas import tpu as pltpu

D_MODEL = 1024
HEAD_DIM = 64
RMS_EPS = 1e-6
SWA_Q_HEADS = 16
SWA_KV_HEADS = 4
SWA_WINDOW = 128
ROPE_THETA = 150000.0
GLA_HEADS = 4
GLA_DK = 128
GLA_DV = 256
GLA_RANK = 16
GLA_TAU = 16.0
GLA_CHUNK = 64
GLA_SUB = 8
FOX_HEADS = 16
FOX_STATS = 8
Q_MAX, K_MAX, QK_MAX, LC_FIRST, LC_LAST = range(5)
FOX_DEAD_LOG2 = -160.0
FOX_BOUND_SLACK = 1.02
D_FF = 2816
N_MIXERS = 3

LANES = 128
BF16_ROWS = 16
NEG_BIG = -1e30
LOG2E = 1.4426950408889634
VMEM_LIMIT = 56 * 1024 * 1024

BF16 = jnp.bfloat16
F32 = jnp.float32


def _dot(a, b):
    return jnp.dot(a, b, preferred_element_type=F32)


def _dot_nt(a, b):
    return lax.dot_general(a, b, (((1,), (1,)), ((), ())), preferred_element_type=F32)


def _dot_tn(a, b):
    return lax.dot_general(a, b, (((0,), (0,)), ((), ())), preferred_element_type=F32)


def _split3(x):
    hi = x.astype(BF16)
    r1 = x - hi.astype(F32)
    mid = r1.astype(BF16)
    lo = (r1 - mid.astype(F32)).astype(BF16)
    return hi, mid, lo


def _cumsum_rows(x):
    n = x.shape[0]
    row = lax.broadcasted_iota(jnp.int32, (n, n), 0)
    col = lax.broadcasted_iota(jnp.int32, (n, n), 1)
    tril = jnp.where(row >= col, 1.0, 0.0).astype(BF16)
    hi, mid, lo = _split3(x)
    return _dot(tril, hi) + _dot(tril, mid) + _dot(tril, lo)


def _log_sigmoid(x):
    return jnp.minimum(x, 0.0) - jnp.log(1.0 + jnp.exp(-jnp.abs(x)))


def _silu(x):
    return x * (1.0 / (1.0 + jnp.exp(-x)))


def _rms(x, gain):
    ms = jnp.mean(x * x, axis=-1, keepdims=True)
    return x * lax.rsqrt(ms + RMS_EPS) * gain


def _norm_mod(x, gain, shift, scale):
    return _rms(x, gain) * (1.0 + scale) + shift


def _params(*sem):
    return pltpu.CompilerParams(dimension_semantics=sem, vmem_limit_bytes=VMEM_LIMIT)


def _const_spec(shape):
    nd = len(shape)
    return pl.BlockSpec(shape, lambda *_: (0,) * nd, pipeline_mode=pl.Buffered(1))


def _ada_kernel(ct_ref, w_ref, b_ref, out_ref):
    ca = _silu(ct_ref[...])
    w = w_ref[...]
    for b in range(ct_ref.shape[1]):
        col = ca[:, b:b + 1]
        out_ref[b:b + 1, :] = jnp.sum(col * w, axis=0, keepdims=True) + b_ref[...]


def _ada_call(c, ada_w, ada_b):
    depth, d, n = ada_w.shape
    bsz = c.shape[0]
    tn = 1536
    return pl.pallas_call(
        _ada_kernel,
        grid=(depth, n // tn),
        in_specs=[
            pl.BlockSpec((d, bsz), lambda l, j: (0, 0)),
            pl.BlockSpec((None, d, tn), lambda l, j: (l, 0, j)),
            pl.BlockSpec((None, 1, tn), lambda l, j: (l, 0, j)),
        ],
        out_specs=pl.BlockSpec((None, bsz, tn), lambda l, j: (l, 0, j)),
        out_shape=jax.ShapeDtypeStruct((depth, bsz, n), F32),
        compiler_params=_params("arbitrary", "arbitrary"),
        name="ada_mod",
    )(c.T, ada_w, ada_b.reshape(depth, 1, n))


def _rope(x, cos, sin_signed):
    width = x.shape[1]
    reps = width // cos.shape[1]
    c = jnp.tile(cos, (1, reps))
    s = jnp.tile(sin_signed, (1, reps))
    lane = lax.broadcasted_iota(jnp.int32, x.shape, 1)
    first_half = (lane % HEAD_DIM) < (HEAD_DIM // 2)
    rot = jnp.where(first_half,
                    pltpu.roll(x, width - HEAD_DIM // 2, 1),
                    pltpu.roll(x, HEAD_DIM // 2, 1))
    return x * c + rot * s


def _ones_row_64(vt):
    ones_row = lax.broadcasted_iota(jnp.int32, vt.shape, 0) % LANES == HEAD_DIM
    return jnp.where(ones_row, 1.0, vt)


def _swa_proj_kernel(x_ref, mod_ref, gain_ref, w_ref, wvt_ref, cos_ref, sin_ref, q_ref, k_ref, vt_ref):
    h = _norm_mod(x_ref[...], gain_ref[...], mod_ref[0:1, :], mod_ref[1:2, :]).astype(BF16)
    cos, sin = cos_ref[...], sin_ref[...]
    nq = q_ref.shape[1]
    q = _dot(h, w_ref[:, :nq])
    q_ref[...] = (_rope(q, cos, sin) * (HEAD_DIM ** -0.5 * LOG2E)).astype(BF16)
    k = _dot(h, w_ref[:, nq:])
    k_ref[...] = _rope(k, cos, sin).astype(BF16)
    vt_ref[...] = _ones_row_64(_dot_nt(wvt_ref[...], h)).astype(BF16)


def _swa_proj_call(x, mod, gain, w, wvt, cos, sin, tm=1024):
    bsz, s, d = x.shape
    nq, nkv = D_MODEL, SWA_KV_HEADS * LANES
    row = lambda b, i: (b, i, 0)
    return pl.pallas_call(
        _swa_proj_kernel,
        grid=(bsz, s // tm),
        in_specs=[
            pl.BlockSpec((None, tm, d), row),
            pl.BlockSpec((None, 6, d), lambda b, i: (b, 0, 0)),
            _const_spec((1, d)),
            _const_spec(w.shape),
            _const_spec(wvt.shape),
            pl.BlockSpec((tm, LANES), lambda b, i: (i, 0)),
            pl.BlockSpec((tm, LANES), lambda b, i: (i, 0)),
        ],
        out_specs=[
            pl.BlockSpec((None, tm, nq), row),
            pl.BlockSpec((None, tm, nkv), row),
            pl.BlockSpec((None, nkv, tm), lambda b, i: (b, 0, i)),
        ],
        out_shape=[
            jax.ShapeDtypeStruct((bsz, s, nq), BF16),
            jax.ShapeDtypeStruct((bsz, s, nkv), BF16),
            jax.ShapeDtypeStruct((bsz, nkv, s), BF16),
        ],
        compiler_params=_params("arbitrary", "arbitrary"),
        name="swa_proj",
    )(x, mod, gain, w, wvt, cos, sin)


def _gla_proj_kernel(x_ref, mod_ref, gain_ref, w_ref, wa_ref, wg_ref, bg_ref,
                     q_ref, k_ref, v_ref, r_ref, la_ref):
    h = _norm_mod(x_ref[...], gain_ref[...], mod_ref[0:1, :], mod_ref[1:2, :]).astype(BF16)
    nk = q_ref.shape[1]
    nv = v_ref.shape[1]
    q_ref[...] = _dot(h, w_ref[:, :nk]).astype(BF16)
    k_ref[...] = _dot(h, w_ref[:, nk:2 * nk]).astype(BF16)
    v_ref[...] = _dot(h, w_ref[:, 2 * nk:2 * nk + nv]).astype(BF16)
    r_ref[...] = _dot(h, w_ref[:, 2 * nk + nv:]).astype(BF16)
    a_low = _dot(h, wa_ref[...]).astype(BF16)
    z = _dot(a_low, wg_ref[...]) + bg_ref[...]
    la_ref[...] = _log_sigmoid(z) * (1.0 / GLA_TAU)


def _gla_proj_call(x, mod, gain, w, wa, wg, bg, tm=1024):
    bsz, s, d = x.shape
    nk, nv = GLA_HEADS * GLA_DK, GLA_HEADS * GLA_DV
    row = lambda b, i: (b, i, 0)
    return pl.pallas_call(
        _gla_proj_kernel,
        grid=(bsz, s // tm),
        in_specs=[
            pl.BlockSpec((None, tm, d), row),
            pl.BlockSpec((None, 6, d), lambda b, i: (b, 0, 0)),
            _const_spec((1, d)),
            _const_spec(w.shape),
            _const_spec(wa.shape),
            _const_spec(wg.shape),
            _const_spec(bg.shape),
        ],
        out_specs=[
            pl.BlockSpec((None, tm, nk), row),
            pl.BlockSpec((None, tm, nk), row),
            pl.BlockSpec((None, tm, nv), row),
            pl.BlockSpec((None, tm, nv), row),
            pl.BlockSpec((None, tm, nk), row),
        ],
        out_shape=[
            jax.ShapeDtypeStruct((bsz, s, nk), BF16),
            jax.ShapeDtypeStruct((bsz, s, nk), BF16),
            jax.ShapeDtypeStruct((bsz, s, nv), BF16),
            jax.ShapeDtypeStruct((bsz, s, nv), BF16),
            jax.ShapeDtypeStruct((bsz, s, nk), F32),
        ],
        compiler_params=_params("arbitrary", "arbitrary"),
        name="gla_proj",
    )(x, mod, gain, w, wa, wg, bg)


def _spread_heads(x, extra, out_ref):
    lane = lax.broadcasted_iota(jnp.int32, (x.shape[0], LANES), 1)
    low = lane < HEAD_DIM
    for p in range(x.shape[1] // LANES):
        xs = x[:, p * LANES:(p + 1) * LANES]
        ex = extra[:, p * LANES:(p + 1) * LANES]
        out_ref[:, (2 * p) * LANES:(2 * p + 1) * LANES] = jnp.where(low, xs, ex).astype(out_ref.dtype)
        odd = pltpu.roll(jnp.where(low, ex, xs), HEAD_DIM, 1)
        out_ref[:, (2 * p + 1) * LANES:(2 * p + 2) * LANES] = odd.astype(out_ref.dtype)


def _fox_proj_kernel(x_ref, mod_ref, gain_ref, wq_ref, wk_ref, wvt_ref, wf_ref, bf_ref,
                     pq_ref, pk_ref, hsel_ref, q_ref, k_ref, vt_ref, stats_ref, carry_ref):
    @pl.when(pl.program_id(1) == 0)
    def _():
        carry_ref[...] = jnp.zeros_like(carry_ref)

    h = _norm_mod(x_ref[...], gain_ref[...], mod_ref[0:1, :], mod_ref[1:2, :]).astype(BF16)
    log_f = _log_sigmoid(_dot(h, wf_ref[...]) + bf_ref[...])
    lc = _cumsum_rows(log_f) + carry_ref[...]
    carry_ref[...] = lc[lc.shape[0] - 1:, :]
    lc2 = lc * LOG2E
    hi, mid, lo = _split3(lc2)
    aug = jnp.concatenate([hi, mid, lo, jnp.ones_like(hi)], axis=1)
    qs = _dot(h, wq_ref[...]) * (HEAD_DIM ** -0.5 * LOG2E)
    ks = _dot(h, wk_ref[...])
    _spread_heads(qs, _dot(aug, pq_ref[...]), q_ref)
    _spread_heads(ks, _dot(aug, pk_ref[...]), k_ref)
    qn2 = _dot((qs * qs).astype(BF16), hsel_ref[...])
    kn2 = _dot((ks * ks).astype(BF16), hsel_ref[...])
    tm = lc2.shape[0]
    stats_ref[...] = jnp.concatenate([
        jnp.sqrt(jnp.max(qn2, axis=0, keepdims=True)),
        jnp.sqrt(jnp.max(kn2, axis=0, keepdims=True)),
        jnp.sqrt(jnp.max(qn2 * kn2, axis=0, keepdims=True)),
        lc2[0:1, :], lc2[tm - 1:tm, :],
        jnp.zeros((FOX_STATS - 5, lc2.shape[1]), F32)], axis=0)
    vt = _dot_nt(wvt_ref[...], h).astype(BF16)
    pad = jnp.where(lax.broadcasted_iota(jnp.int32, (HEAD_DIM, vt.shape[1]), 0) == 0,
                    1.0, 0.0).astype(BF16)
    for hd in range(FOX_HEADS):
        vt_ref[hd * LANES:hd * LANES + HEAD_DIM, :] = vt[hd * HEAD_DIM:(hd + 1) * HEAD_DIM, :]
        vt_ref[hd * LANES + HEAD_DIM:(hd + 1) * LANES, :] = pad


def _fox_proj_call(x, mod, gain, wq, wk, wvt, wf, bf, pq, pk, tm=512):
    bsz, s, d = x.shape
    nqk = FOX_HEADS * LANES
    row = lambda b, i: (b, i, 0)
    hsel = jnp.asarray(np.repeat(np.eye(FOX_HEADS, dtype=np.float32), HEAD_DIM, axis=0), BF16)
    return pl.pallas_call(
        _fox_proj_kernel,
        grid=(bsz, s // tm),
        in_specs=[
            pl.BlockSpec((None, tm, d), row),
            pl.BlockSpec((None, 6, d), lambda b, i: (b, 0, 0)),
            _const_spec((1, d)),
            _const_spec(wq.shape),
            _const_spec(wk.shape),
            _const_spec(wvt.shape),
            _const_spec(wf.shape),
            _const_spec(bf.shape),
            _const_spec(pq.shape),
            _const_spec(pk.shape),
            _const_spec(hsel.shape),
        ],
        out_specs=[
            pl.BlockSpec((None, tm, nqk), row),
            pl.BlockSpec((None, tm, nqk), row),
            pl.BlockSpec((None, None, nqk, tm), lambda b, i: (b, i, 0, 0)),
            pl.BlockSpec((None, None, FOX_STATS, FOX_HEADS), lambda b, i: (b, i, 0, 0)),
        ],
        out_shape=[
            jax.ShapeDtypeStruct((bsz, s, nqk), BF16),
            jax.ShapeDtypeStruct((bsz, s, nqk), BF16),
            jax.ShapeDtypeStruct((bsz, s // tm, nqk, tm), BF16),
            jax.ShapeDtypeStruct((bsz, s // tm, FOX_STATS, FOX_HEADS), F32),
        ],
        scratch_shapes=[pltpu.VMEM((1, FOX_HEADS), F32)],
        compiler_params=_params("arbitrary", "arbitrary"),
        name="fox_proj",
    )(x, mod, gain, wq, wk, wvt, wf, bf, pq, pk, hsel)


def _swa_band_bias():
    w, group = SWA_WINDOW, SWA_Q_HEADS // SWA_KV_HEADS
    key = np.arange(2 * w)[:, None]
    qry = np.arange(group * w)[None, :] % w
    dist = (w + qry) - key
    band = (dist >= 0) & (dist < w)
    allowed = np.stack([band & (key >= w), band])
    return jnp.asarray(np.where(allowed, 0.0, NEG_BIG), F32)


def _swa_attn_kernel(sink_ref, bias_ref, q_ref, kc_ref, vtc_ref, o_ref, st_ref, kp_ref, vtp_ref):
    w = SWA_WINDOW
    group = SWA_Q_HEADS // SWA_KV_HEADS
    nq = group * w
    nb = q_ref.shape[0] // w
    lane = lax.broadcasted_iota(jnp.int32, (w, LANES), 1)
    low = lane < HEAD_DIM

    @pl.when(pl.program_id(1) == 0)
    def _():
        kp_ref[...] = jnp.zeros_like(kp_ref)
        vtp_ref[...] = jnp.zeros_like(vtp_ref)

    for u in range(nb):
        rows = slice(u * w, (u + 1) * w)
        for g in range(SWA_KV_HEADS):
            cols = slice(g * LANES, (g + 1) * LANES)
            slabs = []
            for hh in range(group):
                head = g * group + hh
                qs = q_ref[rows, (head // 2) * LANES:(head // 2 + 1) * LANES]
                keep = low if head % 2 == 0 else jnp.logical_not(low)
                slabs.append(jnp.where(keep, qs, jnp.zeros_like(qs)))
            q_stack = jnp.concatenate(slabs, axis=0)
            k_prev = kp_ref[:, cols] if u == 0 else kc_ref[(u - 1) * w:u * w, cols]
            k_both = jnp.concatenate([k_prev, kc_ref[rows, cols]], axis=0)
            st_ref[u * SWA_KV_HEADS + g] = _dot_nt(k_both, q_stack)
    for u in range(nb):
        rows = slice(u * w, (u + 1) * w)
        bias = bias_ref[jnp.minimum(pl.program_id(1), 1)] if u == 0 else bias_ref[1]
        for g in range(SWA_KV_HEADS):
            cols = slice(g * LANES, (g + 1) * LANES)
            st = st_ref[u * SWA_KV_HEADS + g] + bias
            sink = sink_ref[:, g * nq:(g + 1) * nq] * LOG2E
            m = jnp.maximum(jnp.max(st, axis=0, keepdims=True), sink)
            pt = jnp.exp2(st - m).astype(BF16)
            vt_prev = vtp_ref[cols, :] if u == 0 else vtc_ref[cols, (u - 1) * w:u * w]
            vt_both = jnp.concatenate([vt_prev, vtc_ref[cols, rows]], axis=1)
            acc = _dot(vt_both, pt)
            den = acc[HEAD_DIM:HEAD_DIM + 1, :] + jnp.exp2(sink - m)
            ot = acc[:HEAD_DIM, :] / den
            o_t = jnp.concatenate([ot[:, hh * w:(hh + 1) * w] for hh in range(group)], axis=0)
            o_ref[rows, g * group * HEAD_DIM:(g + 1) * group * HEAD_DIM] = o_t.T.astype(BF16)
    kp_ref[...] = kc_ref[(nb - 1) * w:, :]
    vtp_ref[...] = vtc_ref[:, (nb - 1) * w:]


def _swa_attn_call(q, k, vt, sinks, nb=4):
    bsz, s, d = q.shape
    w = SWA_WINDOW
    nkv = k.shape[2]
    sink_row = jnp.repeat(sinks, w).reshape(1, -1)
    bias = _swa_band_bias()
    cur = lambda b, i: (b, i, 0)
    return pl.pallas_call(
        _swa_attn_kernel,
        grid=(bsz, s // (nb * w)),
        in_specs=[
            _const_spec(sink_row.shape),
            _const_spec(bias.shape),
            pl.BlockSpec((None, nb * w, d), cur),
            pl.BlockSpec((None, nb * w, nkv), cur),
            pl.BlockSpec((None, nkv, nb * w), lambda b, i: (b, 0, i)),
        ],
        out_specs=pl.BlockSpec((None, nb * w, d), cur),
        out_shape=jax.ShapeDtypeStruct((bsz, s, d), BF16),
        scratch_shapes=[
            pltpu.VMEM((nb * SWA_KV_HEADS, 2 * w, (SWA_Q_HEADS // SWA_KV_HEADS) * w), F32),
            pltpu.VMEM((w, nkv), BF16),
            pltpu.VMEM((nkv, w), BF16),
        ],
        compiler_params=_params("arbitrary", "arbitrary"),
        name="swa_attn",
    )(sink_row, bias, q, k, vt)


def _chunk_cumsum(x):
    n = x.shape[0]
    row = lax.broadcasted_iota(jnp.int32, (n, n), 0)
    col = lax.broadcasted_iota(jnp.int32, (n, n), 1)
    same_chunk = (row // GLA_CHUNK) == (col // GLA_CHUNK)
    tril = jnp.where(jnp.logical_and(row >= col, same_chunk), 1.0, 0.0).astype(BF16)
    hi, mid, lo = _split3(x)
    return _dot(tril, hi) + _dot(tril, mid) + _dot(tril, lo)


def _gla_intra(q, k, b2, k_rows, b_rows, base):
    c, sub = GLA_CHUNK, GLA_SUB
    col = lax.broadcasted_iota(jnp.int32, (sub, c), 1)
    row = lax.broadcasted_iota(jnp.int32, (sub, c), 0)
    blocks = []
    for i in range(c // sub):
        lo = i * sub
        q_i = q[lo:lo + sub, :]
        b_i = b2[lo:lo + sub, :]
        if i == 0:
            a = jnp.zeros((sub, c), F32)
        else:
            ref = b2[lo - 1:lo, :]
            n = -(-lo // BF16_ROWS) * BF16_ROWS
            q_t = (q_i * jnp.exp2(b_i - ref)).astype(BF16)
            k_t = (k[:n, :] * jnp.exp2(jnp.minimum(ref - b2[:n, :], 0.0))).astype(BF16)
            if n < c:
                k_t = jnp.concatenate([k_t, jnp.zeros((c - n, k_t.shape[1]), BF16)], axis=0)
            a = _dot_nt(q_t, k_t)
        for s in range(lo, lo + sub):
            w = jnp.exp2(b_i - b_rows[base + s:base + s + 1, :])
            val = jnp.sum(q_i * k_rows[base + s:base + s + 1, :] * w, axis=1, keepdims=True)
            a = jnp.where(col == s, val, a)
        blocks.append(jnp.where(row + lo >= col, a, 0.0))
    return jnp.concatenate(blocks, axis=0)


def _gla_kernel(q_ref, k_ref, v_ref, r_ref, la_ref, hn_ref, o_ref, state_ref, krow_ref, brow_ref):
    @pl.when(pl.program_id(2) == 0)
    def _():
        state_ref[...] = jnp.zeros_like(state_ref)

    for hh in range(state_ref.shape[0]):
        kq = slice(hh * GLA_DK, (hh + 1) * GLA_DK)
        vv = slice(hh * GLA_DV, (hh + 1) * GLA_DV)
        _gla_head(q_ref.at[:, kq], k_ref.at[:, kq], v_ref.at[:, vv], r_ref.at[:, vv], la_ref.at[:, kq],
                  hn_ref, o_ref.at[:, vv], state_ref.at[hh], krow_ref.at[hh], brow_ref.at[hh])


def _gla_head(q_ref, k_ref, v_ref, r_ref, la_ref, hn_ref, o_ref, state_ref, krow_ref, brow_ref):
    c = GLA_CHUNK
    nc = q_ref.shape[0] // c
    rows = [slice(ci * c, (ci + 1) * c) for ci in range(nc)]
    b2_all = _chunk_cumsum(la_ref[...]) * LOG2E
    q_all = q_ref[...].astype(F32) * (GLA_DK ** -0.5)
    k_all = k_ref[...].astype(F32)
    krow_ref[...] = k_all
    brow_ref[...] = b2_all
    b2 = [b2_all[r] for r in rows]
    q = [q_all[r] for r in rows]
    k = [k_all[r] for r in rows]
    last = [b[c - 1:c, :] for b in b2]
    q_in = [(q[i] * jnp.exp2(b2[i])).astype(BF16) for i in range(nc)]
    k_out = [(k[i] * jnp.exp2(last[i] - b2[i])).astype(BF16) for i in range(nc)]
    kv = [_dot_tn(v_ref[rows[i], :], k_out[i]) for i in range(nc)]
    attn = [_gla_intra(q[i], k[i], b2[i], krow_ref, brow_ref, i * c).astype(BF16) for i in range(nc)]
    intra = [_dot(attn[i], v_ref[rows[i], :]) for i in range(nc)]
    state_t = state_ref[...]
    for i in range(nc):
        o = intra[i] + _dot_nt(q_in[i], state_t.astype(BF16))
        state_t = state_t * jnp.exp2(last[i]) + kv[i]
        r = r_ref[rows[i], :].astype(F32)
        o_ref[rows[i], :] = (_rms(o, hn_ref[...]) * _silu(r)).astype(BF16)
    state_ref[...] = state_t


def _gla_call(q, k, v, r, la, head_norm, tm=256, nh=2):
    bsz, s, _ = q.shape
    dk, dv = GLA_DK, GLA_DV
    blk = lambda b, h, i: (b, i, h)
    return pl.pallas_call(
        _gla_kernel,
        grid=(bsz, GLA_HEADS // nh, s // tm),
        in_specs=[
            pl.BlockSpec((None, tm, nh * dk), blk),
            pl.BlockSpec((None, tm, nh * dk), blk),
            pl.BlockSpec((None, tm, nh * dv), blk),
            pl.BlockSpec((None, tm, nh * dv), blk),
            pl.BlockSpec((None, tm, nh * dk), blk),
            _const_spec((1, dv)),
        ],
        out_specs=pl.BlockSpec((None, tm, nh * dv), blk),
        out_shape=jax.ShapeDtypeStruct((bsz, s, GLA_HEADS * dv), BF16),
        scratch_shapes=[pltpu.VMEM((nh, dv, dk), F32), pltpu.VMEM((nh, tm, dk), F32),
                        pltpu.VMEM((nh, tm, dk), F32)],
        compiler_params=_params("arbitrary", "arbitrary", "arbitrary"),
        name="gla_mix",
    )(q, k, v, r, la, head_norm)


def _fox_first_live_block(stats_ref, b, first_head, nh, i):
    heads = [first_head + e for e in range(nh)]
    q_max = [FOX_BOUND_SLACK * stats_ref[b, i, Q_MAX, hd] for hd in heads]
    margin = [stats_ref[b, i, LC_FIRST, hd] + FOX_BOUND_SLACK * stats_ref[b, i, QK_MAX, hd]
              - FOX_DEAD_LOG2 for hd in heads]

    def dead(j):
        is_dead = True
        for e, hd in enumerate(heads):
            gap = q_max[e] * stats_ref[b, j, K_MAX, hd] - stats_ref[b, j, LC_LAST, hd] + margin[e]
            is_dead = jnp.logical_and(is_dead, gap <= 0.0)
        return is_dead

    last = jnp.maximum(i - 1, 0)
    return lax.while_loop(lambda j: jnp.logical_and(j < i, dead(jnp.minimum(j, last))),
                          lambda j: j + 1, jnp.int32(0))


def _fox_attn_kernel(stats_ref, q_ref, k_ref, vt_ref, o_ref, m_ref, acc_ref, sa_ref, sb_ref):
    tk = vt_ref.shape[2]
    nh = m_ref.shape[0]
    i = pl.program_id(2)
    j0 = _fox_first_live_block(stats_ref, pl.program_id(0), pl.program_id(1) * nh, nh, i)
    m_ref[...] = jnp.full_like(m_ref, NEG_BIG)
    acc_ref[...] = jnp.zeros_like(acc_ref)

    half = tk // 2

    def scores(j, buf, e, diag=False):
        lanes = slice(e * LANES, (e + 1) * LANES)
        row0 = pl.multiple_of(j * tk, tk)
        if not diag:
            buf[e] = _dot_nt(k_ref[pl.ds(row0, tk), lanes], q_ref[:, lanes])
        else:
            buf[e, :half, :] = _dot_nt(k_ref[pl.ds(row0, half), lanes], q_ref[:, lanes])
            buf[e, half:, half:] = _dot_nt(k_ref[pl.ds(row0 + half, half), lanes], q_ref[half:, lanes])

    def online_update(e, st, vt, cols):
        m_old = m_ref[e, :, cols]
        m_new = jnp.maximum(m_old, jnp.max(st, axis=0, keepdims=True))
        alpha = jnp.exp2(m_old - m_new)
        pt = jnp.exp2(st - m_new).astype(BF16)
        acc_ref[e, :, cols] = alpha * acc_ref[e, :, cols] + _dot(vt, pt)
        m_ref[e, :, cols] = m_new

    def accumulate(j, buf, e, masked):
        rows = slice(e * LANES, (e + 1) * LANES)
        if not masked:
            online_update(e, buf[e], vt_ref[j, rows, :], slice(None))
            return
        key = lax.broadcasted_iota(jnp.int32, (half, half), 0)
        qry = lax.broadcasted_iota(jnp.int32, (half, half), 1)
        causal = key <= qry
        st = jnp.concatenate([jnp.where(causal, buf[e, :half, :half], NEG_BIG), buf[e, :half, half:]], axis=1)
        online_update(e, st, vt_ref[j, rows, :half], slice(None))
        online_update(e, jnp.where(causal, buf[e, half:, half:], NEG_BIG), vt_ref[j, rows, half:],
                      slice(half, None))

    def block(j, buf, masked, following):
        for e in range(nh):
            if e + 1 < nh:
                scores(j, buf, e + 1, masked)
            elif following is not None:
                scores(following[0], following[1], 0, following[2])
            accumulate(j, buf, e, masked)

    scores(j0, sa_ref, 0)
    n_full = i - j0

    def body(t, carry):
        j = j0 + 2 * t
        block(j, sa_ref, False, (j + 1, sb_ref, False))
        block(j + 1, sb_ref, False, (j + 2, sa_ref, False))
        return carry

    lax.fori_loop(0, n_full // 2, body, 0)

    @pl.when(n_full % 2 == 0)
    def _():
        block(i, sa_ref, True, None)

    @pl.when(n_full % 2 == 1)
    def _():
        block(i - 1, sa_ref, False, (i, sb_ref, True))
        block(i, sb_ref, True, None)

    outs = []
    for e in range(nh):
        acc = acc_ref[e]
        outs.append(acc[:HEAD_DIM, :] / acc[HEAD_DIM:HEAD_DIM + 1, :])
    o_ref[...] = jnp.concatenate(outs, axis=0).T.astype(BF16)


def _fox_attn_call(q, k, vt, stats, nh=4):
    bsz, s, _ = q.shape
    tk = vt.shape[3]
    tq = tk
    return pl.pallas_call(
        _fox_attn_kernel,
        grid=(bsz, FOX_HEADS // nh, s // tq),
        in_specs=[
            pl.BlockSpec(memory_space=pltpu.SMEM),
            pl.BlockSpec((None, tq, nh * LANES), lambda b, p, i: (b, i, p)),
            pl.BlockSpec((None, s, nh * LANES), lambda b, p, i: (b, 0, p)),
            pl.BlockSpec((None, s // tk, nh * LANES, tk), lambda b, p, i: (b, 0, p, 0)),
        ],
        out_specs=pl.BlockSpec((None, tq, nh * HEAD_DIM), lambda b, p, i: (b, i, p)),
        out_shape=jax.ShapeDtypeStruct((bsz, s, D_MODEL), BF16),
        scratch_shapes=[
            pltpu.VMEM((nh, 1, tq), F32),
            pltpu.VMEM((nh, LANES, tq), F32),
            pltpu.VMEM((nh, tk, tq), F32),
            pltpu.VMEM((nh, tk, tq), F32),
        ],
        compiler_params=_params("arbitrary", "arbitrary", "arbitrary"),
        name="fox_attn",
    )(stats, q, k, vt)


def _post_kernel(x_ref, o_ref, mod_ref, gain_ref, wo_ref, wgu_ref, wd_ref, fn_ref, out_ref,
                 *, ff_chunk, final):
    x1 = x_ref[...] + mod_ref[2:3, :] * _dot(o_ref[...], wo_ref[...])
    h = _norm_mod(x1, gain_ref[...], mod_ref[3:4, :], mod_ref[4:5, :]).astype(BF16)
    acc = jnp.zeros(x1.shape, F32)
    for c0 in range(0, D_FF, ff_chunk):
        g = _dot(h, wgu_ref[:, c0:c0 + ff_chunk])
        u = _dot(h, wgu_ref[:, D_FF + c0:D_FF + c0 + ff_chunk])
        acc = acc + _dot((_silu(g) * u).astype(BF16), wd_ref[c0:c0 + ff_chunk, :])
    x2 = x1 + mod_ref[5:6, :] * acc
    if final:
        x2 = _rms(x2, fn_ref[...])
    out_ref[...] = x2


def _layer_spec(stack, layer):
    return pl.BlockSpec((None,) + stack.shape[1:], lambda *_: (layer, 0, 0),
                        pipeline_mode=pl.Buffered(1))


def _post_call(x, o, mod, gain, wo, wgu_stack, wd_stack, layer, final_norm, final, tm=1024, ff_chunk=256):
    bsz, s, d = x.shape
    row = lambda b, i: (b, i, 0)
    return pl.pallas_call(
        functools.partial(_post_kernel, ff_chunk=ff_chunk, final=final),
        grid=(bsz, s // tm),
        in_specs=[
            pl.BlockSpec((None, tm, d), row),
            pl.BlockSpec((None, tm, d), row),
            pl.BlockSpec((None, 6, d), lambda b, i: (b, 0, 0)),
            _const_spec((1, d)),
            _const_spec(wo.shape),
            _layer_spec(wgu_stack, layer),
            _layer_spec(wd_stack, layer),
            _const_spec((1, d)),
        ],
        out_specs=pl.BlockSpec((None, tm, d), row),
        out_shape=jax.ShapeDtypeStruct((bsz, s, d), F32),
        compiler_params=_params("arbitrary", "arbitrary"),
        name="post_ffn",
    )(x, o, mod, gain, wo, wgu_stack, wd_stack, final_norm)


def _rope_tables(s):
    half = HEAD_DIM // 2
    inv = 1.0 / (ROPE_THETA ** (jnp.arange(0, HEAD_DIM, 2, dtype=F32) / HEAD_DIM))
    ang = jnp.arange(s, dtype=F32)[:, None] * inv[None, :]
    cos, sin = jnp.cos(ang), jnp.sin(ang)
    reps = LANES // HEAD_DIM
    cos_t = jnp.tile(jnp.concatenate([cos, cos], axis=1), (1, reps))
    sin_t = jnp.tile(jnp.concatenate([-sin, sin], axis=1), (1, reps))
    assert half * 2 == HEAD_DIM
    return cos_t, sin_t


def _dup_heads(w, heads):
    w3 = w.reshape(w.shape[0], heads, HEAD_DIM)
    return jnp.concatenate([w3, w3], axis=2).reshape(w.shape[0], heads * LANES)


def _pad_heads(w, heads):
    w3 = w.reshape(w.shape[0], heads, HEAD_DIM)
    return jnp.concatenate([w3, jnp.zeros_like(w3)], axis=2).reshape(w.shape[0], heads * LANES)


def _fox_placement():
    h = FOX_HEADS
    pq = np.zeros((4 * h, h * HEAD_DIM), np.float32)
    pk = np.zeros((4 * h, h * HEAD_DIM), np.float32)
    for head in range(h):
        base = (head // 2) * LANES + (HEAD_DIM if head % 2 == 0 else 0)
        for part in range(3):
            pq[part * h + head, base + part] = 1.0
            pk[3 * h + head, base + part] = 1.0
            pq[3 * h + head, base + 3 + part] = 1.0
            pk[part * h + head, base + 3 + part] = -1.0
    return jnp.asarray(pq, BF16), jnp.asarray(pk, BF16)


def kernel(x, c, ada_w, ada_b, norm_gain, ffn_w_gu, ffn_w_down, swa_w_in, swa_sinks, swa_w_o,
           gla_w_in, gla_w_gate_up, gla_b_gate, gla_head_norm, gla_w_o, fox_w_in, fox_b_f, fox_w_o,
           final_norm):
    bsz, s, d = x.shape
    depth = ada_w.shape[0]
    mod_all = _ada_call(c, ada_w, ada_b).reshape(depth, bsz, 6, d)
    cos_t, sin_t = _rope_tables(s)
    pq, pk = _fox_placement()
    fn = final_norm.reshape(1, d)
    wgu_stack = ffn_w_gu.astype(BF16)
    wd_stack = ffn_w_down.astype(BF16)

    for i in range(depth):
        kind, j = i % N_MIXERS, i // N_MIXERS
        mod = mod_all[i]
        gain1 = norm_gain[i, 0].reshape(1, d)
        gain2 = norm_gain[i, 1].reshape(1, d)
        if kind == 0:
            w = swa_w_in[j]
            nq, nkv = SWA_Q_HEADS * HEAD_DIM, SWA_KV_HEADS * HEAD_DIM
            w_all = jnp.concatenate([w[:, :nq], _dup_heads(w[:, nq:nq + nkv], SWA_KV_HEADS)],
                                    axis=1).astype(BF16)
            wvt = _pad_heads(w[:, nq + nkv:], SWA_KV_HEADS).T.astype(BF16)
            q, k, v = _swa_proj_call(x, mod, gain1, w_all, wvt, cos_t, sin_t)
            o = _swa_attn_call(q, k, v, swa_sinks[j])
            wo = swa_w_o[j]
        elif kind == 1:
            w = gla_w_in[j]
            n_main = 2 * GLA_HEADS * GLA_DK + 2 * GLA_HEADS * GLA_DV
            q, k, v, r, la = _gla_proj_call(
                x, mod, gain1, w[:, :n_main].astype(BF16), w[:, n_main:].astype(BF16),
                gla_w_gate_up[j].astype(BF16), gla_b_gate[j].reshape(1, -1))
            o = _gla_call(q, k, v, r, la, gla_head_norm[j].reshape(1, -1))
            wo = gla_w_o[j]
        else:
            w = fox_w_in[j]
            order = jnp.argsort(fox_b_f[j])
            by_head = lambda m: jnp.take(m.reshape(d, FOX_HEADS, HEAD_DIM), order, axis=1).reshape(d, d)
            q, k, v, stats = _fox_proj_call(
                x, mod, gain1,
                by_head(w[:, :d]).astype(BF16), by_head(w[:, d:2 * d]).astype(BF16),
                by_head(w[:, 2 * d:3 * d]).T.astype(BF16), jnp.take(w[:, 3 * d:], order, axis=1).astype(BF16),
                jnp.take(fox_b_f[j], order).reshape(1, -1), pq, pk)
            o = _fox_attn_call(q, k, v, stats)
            wo = jnp.take(fox_w_o[j].reshape(FOX_HEADS, HEAD_DIM, d), order, axis=0).reshape(d, d)
        x = _post_call(x, o, mod, gain2, wo.astype(BF16), wgu_stack, wd_stack, i, fn,
                       final=(i == depth - 1))
    return x
```

```python
import functools

import numpy as np
import jax
import jax.numpy as jnp
from jax import lax
from jax.experimental import pallas as pl
from jax.experimental.pallas import tpu as pltpu

D_MODEL = 1024
HEAD_DIM = 64
RMS_EPS = 1e-6
SWA_Q_HEADS = 16
SWA_KV_HEADS = 4
SWA_WINDOW = 128
SWA_SCORE_SLOTS = 4
ROPE_THETA = 150000.0
GLA_HEADS = 4
GLA_DK = 128
GLA_DV = 256
GLA_RANK = 16
GLA_TAU = 16.0
GLA_CHUNK = 64
GLA_SUB = 8
FOX_HEADS = 16
FOX_STATS = 8
Q_MAX, K_MAX, QK_MAX, LC_FIRST, LC_LAST = range(5)
FOX_DEAD_LOG2 = -160.0
FOX_BOUND_SLACK = 1.02
D_FF = 2816
N_MIXERS = 3

LANES = 128
BF16_ROWS = 16
NEG_BIG = -1e30
LOG2E = 1.4426950408889634
VMEM_LIMIT = 56 * 1024 * 1024

BF16 = jnp.bfloat16
F32 = jnp.float32


def _dot(a, b):
    return jnp.dot(a, b, preferred_element_type=F32)


def _dot_nt(a, b):
    return lax.dot_general(a, b, (((1,), (1,)), ((), ())), preferred_element_type=F32)


def _dot_tn(a, b):
    return lax.dot_general(a, b, (((0,), (0,)), ((), ())), preferred_element_type=F32)


def _split3(x):
    hi = x.astype(BF16)
    r1 = x - hi.astype(F32)
    mid = r1.astype(BF16)
    lo = (r1 - mid.astype(F32)).astype(BF16)
    return hi, mid, lo


def _cumsum_rows(x):
    n = x.shape[0]
    row = lax.broadcasted_iota(jnp.int32, (n, n), 0)
    col = lax.broadcasted_iota(jnp.int32, (n, n), 1)
    tril = jnp.where(row >= col, 1.0, 0.0).astype(BF16)
    w = x.shape[1]
    sums = _dot(tril, jnp.concatenate(_split3(x), axis=1))
    return sums[:, :w] + sums[:, w:2 * w] + sums[:, 2 * w:]


def _log_sigmoid(x):
    return jnp.minimum(x, 0.0) - jnp.log(1.0 + jnp.exp(-jnp.abs(x)))


def _silu(x):
    return x * (1.0 / (1.0 + jnp.exp(-x)))


def _rms(x, gain):
    ms = jnp.mean(x * x, axis=-1, keepdims=True)
    return x * lax.rsqrt(ms + RMS_EPS) * gain


def _norm_mod(x, gain, shift, scale):
    return _rms(x, gain) * (1.0 + scale) + shift


def _params(*sem):
    return pltpu.CompilerParams(dimension_semantics=sem, vmem_limit_bytes=VMEM_LIMIT)


def _const_spec(shape):
    nd = len(shape)
    return pl.BlockSpec(shape, lambda *_: (0,) * nd, pipeline_mode=pl.Buffered(1))


def _ada_kernel(ct_ref, w_ref, b_ref, out_ref):
    ca = _silu(ct_ref[...])
    w = w_ref[...]
    for b in range(ct_ref.shape[1]):
        col = ca[:, b:b + 1]
        out_ref[b:b + 1, :] = jnp.sum(col * w, axis=0, keepdims=True) + b_ref[...]


def _ada_call(c, ada_w, ada_b):
    depth, d, n = ada_w.shape
    bsz = c.shape[0]
    tn = 1536
    return pl.pallas_call(
        _ada_kernel,
        grid=(depth, n // tn),
        in_specs=[
            pl.BlockSpec((d, bsz), lambda l, j: (0, 0)),
            pl.BlockSpec((None, d, tn), lambda l, j: (l, 0, j)),
            pl.BlockSpec((None, 1, tn), lambda l, j: (l, 0, j)),
        ],
        out_specs=pl.BlockSpec((None, bsz, tn), lambda l, j: (l, 0, j)),
        out_shape=jax.ShapeDtypeStruct((depth, bsz, n), F32),
        compiler_params=_params("arbitrary", "arbitrary"),
        name="ada_mod",
    )(c.T, ada_w, ada_b.reshape(depth, 1, n))


def _rope(x, cos, sin_signed):
    width = x.shape[1]
    reps = width // cos.shape[1]
    c = jnp.tile(cos, (1, reps))
    s = jnp.tile(sin_signed, (1, reps))
    lane = lax.broadcasted_iota(jnp.int32, x.shape, 1)
    first_half = (lane % HEAD_DIM) < (HEAD_DIM // 2)
    rot = jnp.where(first_half,
                    pltpu.roll(x, width - HEAD_DIM // 2, 1),
                    pltpu.roll(x, HEAD_DIM // 2, 1))
    return x * c + rot * s


def _ones_row_64(vt):
    ones_row = lax.broadcasted_iota(jnp.int32, vt.shape, 0) % LANES == HEAD_DIM
    return jnp.where(ones_row, 1.0, vt)


def _swa_proj_kernel(x_ref, mod_ref, gain_ref, w_ref, wvt_ref, cos_ref, sin_ref, q_ref, k_ref, vt_ref):
    h = _norm_mod(x_ref[...], gain_ref[...], mod_ref[0:1, :], mod_ref[1:2, :]).astype(BF16)
    cos, sin = cos_ref[...], sin_ref[...]
    nq = q_ref.shape[1]
    q = _dot(h, w_ref[:, :nq])
    q_ref[...] = (_rope(q, cos, sin) * (HEAD_DIM ** -0.5 * LOG2E)).astype(BF16)
    k = _dot(h, w_ref[:, nq:])
    k_ref[...] = _rope(k, cos, sin).astype(BF16)
    vt_ref[...] = _ones_row_64(_dot_nt(wvt_ref[...], h)).astype(BF16)


def _swa_proj_call(x, mod, gain, w, wvt, cos, sin, tm=1024):
    bsz, s, d = x.shape
    nq, nkv = D_MODEL, SWA_KV_HEADS * LANES
    row = lambda b, i: (b, i, 0)
    return pl.pallas_call(
        _swa_proj_kernel,
        grid=(bsz, s // tm),
        in_specs=[
            pl.BlockSpec((None, tm, d), row),
            pl.BlockSpec((None, 6, d), lambda b, i: (b, 0, 0)),
            _const_spec((1, d)),
            _const_spec(w.shape),
            _const_spec(wvt.shape),
            pl.BlockSpec((tm, LANES), lambda b, i: (i, 0)),
            pl.BlockSpec((tm, LANES), lambda b, i: (i, 0)),
        ],
        out_specs=[
            pl.BlockSpec((None, tm, nq), row),
            pl.BlockSpec((None, tm, nkv), row),
            pl.BlockSpec((None, nkv, tm), lambda b, i: (b, 0, i)),
        ],
        out_shape=[
            jax.ShapeDtypeStruct((bsz, s, nq), BF16),
            jax.ShapeDtypeStruct((bsz, s, nkv), BF16),
            jax.ShapeDtypeStruct((bsz, nkv, s), BF16),
        ],
        compiler_params=_params("arbitrary", "arbitrary"),
        name="swa_proj",
    )(x, mod, gain, w, wvt, cos, sin)


def _gla_proj_kernel(x_ref, mod_ref, gain_ref, w_ref, wa_ref, wg_ref, bg_ref,
                     q_ref, k_ref, v_ref, r_ref, la_ref):
    h = _norm_mod(x_ref[...], gain_ref[...], mod_ref[0:1, :], mod_ref[1:2, :]).astype(BF16)
    nk = q_ref.shape[1]
    nv = v_ref.shape[1]
    q_ref[...] = _dot(h, w_ref[:, :nk]).astype(BF16)
    k_ref[...] = _dot(h, w_ref[:, nk:2 * nk]).astype(BF16)
    v_ref[...] = _dot(h, w_ref[:, 2 * nk:2 * nk + nv]).astype(BF16)
    r_ref[...] = _dot(h, w_ref[:, 2 * nk + nv:]).astype(BF16)
    a_low = _dot(h, wa_ref[...]).astype(BF16)
    z = _dot(a_low, wg_ref[...]) + bg_ref[...]
    la_ref[...] = _log_sigmoid(z) * (1.0 / GLA_TAU)


def _gla_proj_call(x, mod, gain, w, wa, wg, bg, tm=1024):
    bsz, s, d = x.shape
    nk, nv = GLA_HEADS * GLA_DK, GLA_HEADS * GLA_DV
    row = lambda b, i: (b, i, 0)
    return pl.pallas_call(
        _gla_proj_kernel,
        grid=(bsz, s // tm),
        in_specs=[
            pl.BlockSpec((None, tm, d), row),
            pl.BlockSpec((None, 6, d), lambda b, i: (b, 0, 0)),
            _const_spec((1, d)),
            _const_spec(w.shape),
            _const_spec(wa.shape),
            _const_spec(wg.shape),
            _const_spec(bg.shape),
        ],
        out_specs=[
            pl.BlockSpec((None, tm, nk), row),
            pl.BlockSpec((None, tm, nk), row),
            pl.BlockSpec((None, tm, nv), row),
            pl.BlockSpec((None, tm, nv), row),
            pl.BlockSpec((None, tm, nk), row),
        ],
        out_shape=[
            jax.ShapeDtypeStruct((bsz, s, nk), BF16),
            jax.ShapeDtypeStruct((bsz, s, nk), BF16),
            jax.ShapeDtypeStruct((bsz, s, nv), BF16),
            jax.ShapeDtypeStruct((bsz, s, nv), BF16),
            jax.ShapeDtypeStruct((bsz, s, nk), F32),
        ],
        compiler_params=_params("arbitrary", "arbitrary"),
        name="gla_proj",
    )(x, mod, gain, w, wa, wg, bg)


def _spread_heads(x, extra, out_ref):
    lane = lax.broadcasted_iota(jnp.int32, (x.shape[0], LANES), 1)
    low = lane < HEAD_DIM
    for p in range(x.shape[1] // LANES):
        xs = x[:, p * LANES:(p + 1) * LANES]
        ex = extra[:, p * LANES:(p + 1) * LANES]
        out_ref[:, (2 * p) * LANES:(2 * p + 1) * LANES] = jnp.where(low, xs, ex).astype(out_ref.dtype)
        odd = pltpu.roll(jnp.where(low, ex, xs), HEAD_DIM, 1)
        out_ref[:, (2 * p + 1) * LANES:(2 * p + 2) * LANES] = odd.astype(out_ref.dtype)


def _fox_proj_kernel(x_ref, mod_ref, gain_ref, wq_ref, wk_ref, wvt_ref, wf_ref, bf_ref,
                     pq_ref, pk_ref, hsel_ref, q_ref, k_ref, vt_ref, stats_ref, carry_ref):
    @pl.when(pl.program_id(1) == 0)
    def _():
        carry_ref[...] = jnp.zeros_like(carry_ref)

    h = _norm_mod(x_ref[...], gain_ref[...], mod_ref[0:1, :], mod_ref[1:2, :]).astype(BF16)
    log_f = _log_sigmoid(_dot(h, wf_ref[...]) + bf_ref[...])
    lc = _cumsum_rows(log_f) + carry_ref[...]
    carry_ref[...] = lc[lc.shape[0] - 1:, :]
    lc2 = lc * LOG2E
    hi, mid, lo = _split3(lc2)
    aug = jnp.concatenate([hi, mid, lo, jnp.ones_like(hi)], axis=1)
    qs = _dot(h, wq_ref[...]) * (HEAD_DIM ** -0.5 * LOG2E)
    ks = _dot(h, wk_ref[...])
    _spread_heads(qs, _dot(aug, pq_ref[...]), q_ref)
    _spread_heads(ks, _dot(aug, pk_ref[...]), k_ref)
    qn2 = _dot((qs * qs).astype(BF16), hsel_ref[...])
    kn2 = _dot((ks * ks).astype(BF16), hsel_ref[...])
    tm = lc2.shape[0]
    stats_ref[...] = jnp.concatenate([
        jnp.sqrt(jnp.max(qn2, axis=0, keepdims=True)),
        jnp.sqrt(jnp.max(kn2, axis=0, keepdims=True)),
        jnp.sqrt(jnp.max(qn2 * kn2, axis=0, keepdims=True)),
        lc2[0:1, :], lc2[tm - 1:tm, :],
        jnp.zeros((FOX_STATS - 5, lc2.shape[1]), F32)], axis=0)
    vt = _dot_nt(wvt_ref[...], h).astype(BF16)
    pad = jnp.where(lax.broadcasted_iota(jnp.int32, (HEAD_DIM, vt.shape[1]), 0) == 0,
                    1.0, 0.0).astype(BF16)
    for hd in range(FOX_HEADS):
        vt_ref[hd * LANES:hd * LANES + HEAD_DIM, :] = vt[hd * HEAD_DIM:(hd + 1) * HEAD_DIM, :]
        vt_ref[hd * LANES + HEAD_DIM:(hd + 1) * LANES, :] = pad


def _fox_proj_call(x, mod, gain, wq, wk, wvt, wf, bf, pq, pk, tm=512):
    bsz, s, d = x.shape
    nqk = FOX_HEADS * LANES
    row = lambda b, i: (b, i, 0)
    hsel = jnp.asarray(np.repeat(np.eye(FOX_HEADS, dtype=np.float32), HEAD_DIM, axis=0), BF16)
    return pl.pallas_call(
        _fox_proj_kernel,
        grid=(bsz, s // tm),
        in_specs=[
            pl.BlockSpec((None, tm, d), row),
            pl.BlockSpec((None, 6, d), lambda b, i: (b, 0, 0)),
            _const_spec((1, d)),
            _const_spec(wq.shape),
            _const_spec(wk.shape),
            _const_spec(wvt.shape),
            _const_spec(wf.shape),
            _const_spec(bf.shape),
            _const_spec(pq.shape),
            _const_spec(pk.shape),
            _const_spec(hsel.shape),
        ],
        out_specs=[
            pl.BlockSpec((None, tm, nqk), row),
            pl.BlockSpec((None, tm, nqk), row),
            pl.BlockSpec((None, None, nqk, tm), lambda b, i: (b, i, 0, 0)),
            pl.BlockSpec((None, None, FOX_STATS, FOX_HEADS), lambda b, i: (b, i, 0, 0)),
        ],
        out_shape=[
            jax.ShapeDtypeStruct((bsz, s, nqk), BF16),
            jax.ShapeDtypeStruct((bsz, s, nqk), BF16),
            jax.ShapeDtypeStruct((bsz, s // tm, nqk, tm), BF16),
            jax.ShapeDtypeStruct((bsz, s // tm, FOX_STATS, FOX_HEADS), F32),
        ],
        scratch_shapes=[pltpu.VMEM((1, FOX_HEADS), F32)],
        compiler_params=_params("arbitrary", "arbitrary"),
        name="fox_proj",
    )(x, mod, gain, wq, wk, wvt, wf, bf, pq, pk, hsel)


def _swa_band_bias():
    w, group = SWA_WINDOW, SWA_Q_HEADS // SWA_KV_HEADS
    key = np.arange(2 * w)[:, None]
    qry = np.arange(group * w)[None, :] % w
    dist = (w + qry) - key
    band = (dist >= 0) & (dist < w)
    allowed = np.stack([band & (key >= w), band])
    return jnp.asarray(np.where(allowed, 0.0, NEG_BIG), F32)


def _swa_attn_kernel(sink_ref, bias_ref, q_ref, kc_ref, vtc_ref, o_ref, st_ref, kp_ref, vtp_ref):
    w = SWA_WINDOW
    group = SWA_Q_HEADS // SWA_KV_HEADS
    nq = group * w
    nb = q_ref.shape[0] // w
    lane = lax.broadcasted_iota(jnp.int32, (w, LANES), 1)
    low = lane < HEAD_DIM

    @pl.when(pl.program_id(1) == 0)
    def _():
        kp_ref[...] = jnp.zeros_like(kp_ref)
        vtp_ref[...] = jnp.zeros_like(vtp_ref)

    def scores(u, g, slot):
        rows = slice(u * w, (u + 1) * w)
        cols = slice(g * LANES, (g + 1) * LANES)
        slabs = []
        for hh in range(group):
            head = g * group + hh
            qs = q_ref[rows, (head // 2) * LANES:(head // 2 + 1) * LANES]
            keep = low if head % 2 == 0 else jnp.logical_not(low)
            slabs.append(jnp.where(keep, qs, jnp.zeros_like(qs)))
        q_stack = jnp.concatenate(slabs, axis=0)
        k_prev = kp_ref[:, cols] if u == 0 else kc_ref[(u - 1) * w:u * w, cols]
        k_both = jnp.concatenate([k_prev, kc_ref[rows, cols]], axis=0)
        st_ref[slot] = _dot_nt(k_both, q_stack)

    def finish(u, g, slot):
        rows = slice(u * w, (u + 1) * w)
        cols = slice(g * LANES, (g + 1) * LANES)
        bias = bias_ref[jnp.minimum(pl.program_id(1), 1)] if u == 0 else bias_ref[1]
        st = st_ref[slot] + bias
        sink = sink_ref[:, g * nq:(g + 1) * nq] * LOG2E
        m = jnp.maximum(jnp.max(st, axis=0, keepdims=True), sink)
        pt = jnp.exp2(st - m).astype(BF16)
        vt_prev = vtp_ref[cols, :] if u == 0 else vtc_ref[cols, (u - 1) * w:u * w]
        vt_both = jnp.concatenate([vt_prev, vtc_ref[cols, rows]], axis=1)
        acc = _dot(vt_both, pt)
        den = acc[HEAD_DIM:HEAD_DIM + 1, :] + jnp.exp2(sink - m)
        ot = acc[:HEAD_DIM, :] / den
        o_t = jnp.concatenate([ot[:, hh * w:(hh + 1) * w] for hh in range(group)], axis=0)
        o_ref[rows, g * group * HEAD_DIM:(g + 1) * group * HEAD_DIM] = o_t.T.astype(BF16)

    items = [(u, g) for u in range(nb) for g in range(SWA_KV_HEADS)]
    slots = st_ref.shape[0]
    ahead = slots - 1
    for n in range(min(ahead, len(items))):
        scores(*items[n], n % slots)
    for n, item in enumerate(items):
        if n + ahead < len(items):
            scores(*items[n + ahead], (n + ahead) % slots)
        finish(*item, n % slots)
    kp_ref[...] = kc_ref[(nb - 1) * w:, :]
    vtp_ref[...] = vtc_ref[:, (nb - 1) * w:]


def _swa_attn_call(q, k, vt, sinks, nb=4):
    bsz, s, d = q.shape
    w = SWA_WINDOW
    nkv = k.shape[2]
    sink_row = jnp.repeat(sinks, w).reshape(1, -1)
    bias = _swa_band_bias()
    cur = lambda b, i: (b, i, 0)
    return pl.pallas_call(
        _swa_attn_kernel,
        grid=(bsz, s // (nb * w)),
        in_specs=[
            _const_spec(sink_row.shape),
            _const_spec(bias.shape),
            pl.BlockSpec((None, nb * w, d), cur),
            pl.BlockSpec((None, nb * w, nkv), cur),
            pl.BlockSpec((None, nkv, nb * w), lambda b, i: (b, 0, i)),
        ],
        out_specs=pl.BlockSpec((None, nb * w, d), cur),
        out_shape=jax.ShapeDtypeStruct((bsz, s, d), BF16),
        scratch_shapes=[
            pltpu.VMEM((SWA_SCORE_SLOTS, 2 * w, (SWA_Q_HEADS // SWA_KV_HEADS) * w), F32),
            pltpu.VMEM((w, nkv), BF16),
            pltpu.VMEM((nkv, w), BF16),
        ],
        compiler_params=_params("arbitrary", "arbitrary"),
        name="swa_attn",
    )(sink_row, bias, q, k, vt)


def _chunk_cumsum(x):
    n = x.shape[0]
    row = lax.broadcasted_iota(jnp.int32, (n, n), 0)
    col = lax.broadcasted_iota(jnp.int32, (n, n), 1)
    same_chunk = (row // GLA_CHUNK) == (col // GLA_CHUNK)
    tril = jnp.where(jnp.logical_and(row >= col, same_chunk), 1.0, 0.0).astype(BF16)
    hi, mid, lo = _split3(x)
    return _dot(tril, hi) + _dot(tril, mid) + _dot(tril, lo)


def _gla_intra(q, k, b2, k_rows, b_rows, base):
    c, sub = GLA_CHUNK, GLA_SUB
    col = lax.broadcasted_iota(jnp.int32, (sub, c), 1)
    row = lax.broadcasted_iota(jnp.int32, (sub, c), 0)
    blocks = []
    for i in range(c // sub):
        lo = i * sub
        q_i = q[lo:lo + sub, :]
        b_i = b2[lo:lo + sub, :]
        if i == 0:
            a = jnp.zeros((sub, c), F32)
        else:
            ref = b2[lo - 1:lo, :]
            n = -(-lo // BF16_ROWS) * BF16_ROWS
            q_t = (q_i * jnp.exp2(b_i - ref)).astype(BF16)
            k_t = (k[:n, :] * jnp.exp2(jnp.minimum(ref - b2[:n, :], 0.0))).astype(BF16)
            if n < c:
                k_t = jnp.concatenate([k_t, jnp.zeros((c - n, k_t.shape[1]), BF16)], axis=0)
            a = _dot_nt(q_t, k_t)
        for s in range(lo, lo + sub):
            w = jnp.exp2(b_i - b_rows[base + s:base + s + 1, :])
            val = jnp.sum(q_i * k_rows[base + s:base + s + 1, :] * w, axis=1, keepdims=True)
            a = jnp.where(col == s, val, a)
        blocks.append(jnp.where(row + lo >= col, a, 0.0))
    return jnp.concatenate(blocks, axis=0)


def _gla_kernel(q_ref, k_ref, v_ref, r_ref, la_ref, hn_ref, o_ref, state_ref, krow_ref, brow_ref):
    @pl.when(pl.program_id(2) == 0)
    def _():
        state_ref[...] = jnp.zeros_like(state_ref)

    for hh in range(state_ref.shape[0]):
        kq = slice(hh * GLA_DK, (hh + 1) * GLA_DK)
        vv = slice(hh * GLA_DV, (hh + 1) * GLA_DV)
        _gla_head(q_ref.at[:, kq], k_ref.at[:, kq], v_ref.at[:, vv], r_ref.at[:, vv], la_ref.at[:, kq],
                  hn_ref, o_ref.at[:, vv], state_ref.at[hh], krow_ref.at[hh], brow_ref.at[hh])


def _gla_head(q_ref, k_ref, v_ref, r_ref, la_ref, hn_ref, o_ref, state_ref, krow_ref, brow_ref):
    c = GLA_CHUNK
    nc = q_ref.shape[0] // c
    rows = [slice(ci * c, (ci + 1) * c) for ci in range(nc)]
    b2_all = _chunk_cumsum(la_ref[...]) * LOG2E
    q_all = q_ref[...].astype(F32) * (GLA_DK ** -0.5)
    k_all = k_ref[...].astype(F32)
    krow_ref[...] = k_all
    brow_ref[...] = b2_all
    b2 = [b2_all[r] for r in rows]
    q = [q_all[r] for r in rows]
    k = [k_all[r] for r in rows]
    last = [b[c - 1:c, :] for b in b2]
    q_in = [(q[i] * jnp.exp2(b2[i])).astype(BF16) for i in range(nc)]
    k_out = [(k[i] * jnp.exp2(last[i] - b2[i])).astype(BF16) for i in range(nc)]
    kv = [_dot_tn(v_ref[rows[i], :], k_out[i]) for i in range(nc)]
    attn = [_gla_intra(q[i], k[i], b2[i], krow_ref, brow_ref, i * c).astype(BF16) for i in range(nc)]
    intra = [_dot(attn[i], v_ref[rows[i], :]) for i in range(nc)]
    state_t = state_ref[...]
    for i in range(nc):
        o = intra[i] + _dot_nt(q_in[i], state_t.astype(BF16))
        state_t = state_t * jnp.exp2(last[i]) + kv[i]
        r = r_ref[rows[i], :].astype(F32)
        o_ref[rows[i], :] = (_rms(o, hn_ref[...]) * _silu(r)).astype(BF16)
    state_ref[...] = state_t


def _gla_call(q, k, v, r, la, head_norm, tm=256, nh=2):
    bsz, s, _ = q.shape
    dk, dv = GLA_DK, GLA_DV
    blk = lambda b, h, i: (b, i, h)
    return pl.pallas_call(
        _gla_kernel,
        grid=(bsz, GLA_HEADS // nh, s // tm),
        in_specs=[
            pl.BlockSpec((None, tm, nh * dk), blk),
            pl.BlockSpec((None, tm, nh * dk), blk),
            pl.BlockSpec((None, tm, nh * dv), blk),
            pl.BlockSpec((None, tm, nh * dv), blk),
            pl.BlockSpec((None, tm, nh * dk), blk),
            _const_spec((1, dv)),
        ],
        out_specs=pl.BlockSpec((None, tm, nh * dv), blk),
        out_shape=jax.ShapeDtypeStruct((bsz, s, GLA_HEADS * dv), BF16),
        scratch_shapes=[pltpu.VMEM((nh, dv, dk), F32), pltpu.VMEM((nh, tm, dk), F32),
                        pltpu.VMEM((nh, tm, dk), F32)],
        compiler_params=_params("arbitrary", "arbitrary", "arbitrary"),
        name="gla_mix",
    )(q, k, v, r, la, head_norm)


def _fox_first_live_block(stats_ref, b, first_head, nh, i):
    heads = [first_head + e for e in range(nh)]
    q_max = [FOX_BOUND_SLACK * stats_ref[b, i, Q_MAX, hd] for hd in heads]
    margin = [stats_ref[b, i, LC_FIRST, hd] + FOX_BOUND_SLACK * stats_ref[b, i, QK_MAX, hd]
              - FOX_DEAD_LOG2 for hd in heads]

    def dead(j):
        is_dead = True
        for e, hd in enumerate(heads):
            gap = q_max[e] * stats_ref[b, j, K_MAX, hd] - stats_ref[b, j, LC_LAST, hd] + margin[e]
            is_dead = jnp.logical_and(is_dead, gap <= 0.0)
        return is_dead

    last = jnp.maximum(i - 1, 0)
    return lax.while_loop(lambda j: jnp.logical_and(j < i, dead(jnp.minimum(j, last))),
                          lambda j: j + 1, jnp.int32(0))


def _fox_attn_kernel(stats_ref, q_ref, k_ref, vt_ref, o_ref, m_ref, acc_ref, sa_ref, sb_ref):
    tk = vt_ref.shape[2]
    nh = m_ref.shape[0]
    i = pl.program_id(2)
    j0 = _fox_first_live_block(stats_ref, pl.program_id(0), pl.program_id(1) * nh, nh, i)
    m_ref[...] = jnp.full_like(m_ref, NEG_BIG)
    acc_ref[...] = jnp.zeros_like(acc_ref)

    half = tk // 2

    def scores(j, buf, e, diag=False):
        lanes = slice(e * LANES, (e + 1) * LANES)
        row0 = pl.multiple_of(j * tk, tk)
        if not diag:
            buf[e] = _dot_nt(k_ref[pl.ds(row0, tk), lanes], q_ref[:, lanes])
        else:
            buf[e, :half, :] = _dot_nt(k_ref[pl.ds(row0, half), lanes], q_ref[:, lanes])
            buf[e, half:, half:] = _dot_nt(k_ref[pl.ds(row0 + half, half), lanes], q_ref[half:, lanes])

    def online_update(e, st, vt, cols):
        m_old = m_ref[e, :, cols]
        m_new = jnp.maximum(m_old, jnp.max(st, axis=0, keepdims=True))
        alpha = jnp.exp2(m_old - m_new)
        pt = jnp.exp2(st - m_new).astype(BF16)
        acc_ref[e, :, cols] = alpha * acc_ref[e, :, cols] + _dot(vt, pt)
        m_ref[e, :, cols] = m_new

    def accumulate(j, buf, e, masked):
        rows = slice(e * LANES, (e + 1) * LANES)
        if not masked:
            online_update(e, buf[e], vt_ref[j, rows, :], slice(None))
            return
        key = lax.broadcasted_iota(jnp.int32, (half, half), 0)
        qry = lax.broadcasted_iota(jnp.int32, (half, half), 1)
        causal = key <= qry
        st = jnp.concatenate([jnp.where(causal, buf[e, :half, :half], NEG_BIG), buf[e, :half, half:]], axis=1)
        online_update(e, st, vt_ref[j, rows, :half], slice(None))
        online_update(e, jnp.where(causal, buf[e, half:, half:], NEG_BIG), vt_ref[j, rows, half:],
                      slice(half, None))

    def block(j, buf, masked, following):
        for e in range(nh):
            if e + 1 < nh:
                scores(j, buf, e + 1, masked)
            elif following is not None:
                scores(following[0], following[1], 0, following[2])
            accumulate(j, buf, e, masked)

    scores(j0, sa_ref, 0)
    n_full = i - j0

    def body(t, carry):
        j = j0 + 2 * t
        block(j, sa_ref, False, (j + 1, sb_ref, False))
        block(j + 1, sb_ref, False, (j + 2, sa_ref, False))
        return carry

    lax.fori_loop(0, n_full // 2, body, 0)

    @pl.when(n_full % 2 == 0)
    def _():
        block(i, sa_ref, True, None)

    @pl.when(n_full % 2 == 1)
    def _():
        block(i - 1, sa_ref, False, (i, sb_ref, True))
        block(i, sb_ref, True, None)

    outs = []
    for e in range(nh):
        acc = acc_ref[e]
        outs.append(acc[:HEAD_DIM, :] / acc[HEAD_DIM:HEAD_DIM + 1, :])
    o_ref[...] = jnp.concatenate(outs, axis=0).T.astype(BF16)


def _fox_attn_call(q, k, vt, stats, nh=4):
    bsz, s, _ = q.shape
    tk = vt.shape[3]
    tq = tk
    return pl.pallas_call(
        _fox_attn_kernel,
        grid=(bsz, FOX_HEADS // nh, s // tq),
        in_specs=[
            pl.BlockSpec(memory_space=pltpu.SMEM),
            pl.BlockSpec((None, tq, nh * LANES), lambda b, p, i: (b, i, p)),
            pl.BlockSpec((None, s, nh * LANES), lambda b, p, i: (b, 0, p)),
            pl.BlockSpec((None, s // tk, nh * LANES, tk), lambda b, p, i: (b, 0, p, 0)),
        ],
        out_specs=pl.BlockSpec((None, tq, nh * HEAD_DIM), lambda b, p, i: (b, i, p)),
        out_shape=jax.ShapeDtypeStruct((bsz, s, D_MODEL), BF16),
        scratch_shapes=[
            pltpu.VMEM((nh, 1, tq), F32),
            pltpu.VMEM((nh, LANES, tq), F32),
            pltpu.VMEM((nh, tk, tq), F32),
            pltpu.VMEM((nh, tk, tq), F32),
        ],
        compiler_params=_params("arbitrary", "arbitrary", "arbitrary"),
        name="fox_attn",
    )(stats, q, k, vt)


def _post_kernel(x_ref, o_ref, mod_ref, gain_ref, wo_ref, wgu_ref, wd_ref, fn_ref, out_ref,
                 *, ff_chunk, final):
    x1 = x_ref[...] + mod_ref[2:3, :] * _dot(o_ref[...], wo_ref[...])
    h = _norm_mod(x1, gain_ref[...], mod_ref[3:4, :], mod_ref[4:5, :]).astype(BF16)
    acc = jnp.zeros(x1.shape, F32)
    for c0 in range(0, D_FF, ff_chunk):
        g = _dot(h, wgu_ref[:, c0:c0 + ff_chunk])
        u = _dot(h, wgu_ref[:, D_FF + c0:D_FF + c0 + ff_chunk])
        acc = acc + _dot((_silu(g) * u).astype(BF16), wd_ref[c0:c0 + ff_chunk, :])
    x2 = x1 + mod_ref[5:6, :] * acc
    if final:
        x2 = _rms(x2, fn_ref[...])
    out_ref[...] = x2


def _layer_spec(stack, layer):
    return pl.BlockSpec((None,) + stack.shape[1:], lambda *_: (layer, 0, 0),
                        pipeline_mode=pl.Buffered(1))


def _post_call(x, o, mod, gain, wo, wgu_stack, wd_stack, layer, final_norm, final, tm=1024, ff_chunk=256):
    bsz, s, d = x.shape
    row = lambda b, i: (b, i, 0)
    return pl.pallas_call(
        functools.partial(_post_kernel, ff_chunk=ff_chunk, final=final),
        grid=(bsz, s // tm),
        in_specs=[
            pl.BlockSpec((None, tm, d), row),
            pl.BlockSpec((None, tm, d), row),
            pl.BlockSpec((None, 6, d), lambda b, i: (b, 0, 0)),
            _const_spec((1, d)),
            _const_spec(wo.shape),
            _layer_spec(wgu_stack, layer),
            _layer_spec(wd_stack, layer),
            _const_spec((1, d)),
        ],
        out_specs=pl.BlockSpec((None, tm, d), row),
        out_shape=jax.ShapeDtypeStruct((bsz, s, d), F32),
        compiler_params=_params("arbitrary", "arbitrary"),
        name="post_ffn",
    )(x, o, mod, gain, wo, wgu_stack, wd_stack, final_norm)


def _rope_tables(s):
    half = HEAD_DIM // 2
    inv = 1.0 / (ROPE_THETA ** (jnp.arange(0, HEAD_DIM, 2, dtype=F32) / HEAD_DIM))
    ang = jnp.arange(s, dtype=F32)[:, None] * inv[None, :]
    cos, sin = jnp.cos(ang), jnp.sin(ang)
    reps = LANES // HEAD_DIM
    cos_t = jnp.tile(jnp.concatenate([cos, cos], axis=1), (1, reps))
    sin_t = jnp.tile(jnp.concatenate([-sin, sin], axis=1), (1, reps))
    assert half * 2 == HEAD_DIM
    return cos_t, sin_t


def _dup_heads(w, heads):
    w3 = w.reshape(w.shape[0], heads, HEAD_DIM)
    return jnp.concatenate([w3, w3], axis=2).reshape(w.shape[0], heads * LANES)


def _pad_heads(w, heads):
    w3 = w.reshape(w.shape[0], heads, HEAD_DIM)
    return jnp.concatenate([w3, jnp.zeros_like(w3)], axis=2).reshape(w.shape[0], heads * LANES)


def _fox_placement():
    h = FOX_HEADS
    pq = np.zeros((4 * h, h * HEAD_DIM), np.float32)
    pk = np.zeros((4 * h, h * HEAD_DIM), np.float32)
    for head in range(h):
        base = (head // 2) * LANES + (HEAD_DIM if head % 2 == 0 else 0)
        for part in range(3):
            pq[part * h + head, base + part] = 1.0
            pk[3 * h + head, base + part] = 1.0
            pq[3 * h + head, base + 3 + part] = 1.0
            pk[part * h + head, base + 3 + part] = -1.0
    return jnp.asarray(pq, BF16), jnp.asarray(pk, BF16)


def kernel(x, c, ada_w, ada_b, norm_gain, ffn_w_gu, ffn_w_down, swa_w_in, swa_sinks, swa_w_o,
           gla_w_in, gla_w_gate_up, gla_b_gate, gla_head_norm, gla_w_o, fox_w_in, fox_b_f, fox_w_o,
           final_norm):
    bsz, s, d = x.shape
    depth = ada_w.shape[0]
    mod_all = _ada_call(c, ada_w, ada_b).reshape(depth, bsz, 6, d)
    cos_t, sin_t = _rope_tables(s)
    pq, pk = _fox_placement()
    fn = final_norm.reshape(1, d)
    wgu_stack = ffn_w_gu.astype(BF16)
    wd_stack = ffn_w_down.astype(BF16)

    for i in range(depth):
        kind, j = i % N_MIXERS, i // N_MIXERS
        mod = mod_all[i]
        gain1 = norm_gain[i, 0].reshape(1, d)
        gain2 = norm_gain[i, 1].reshape(1, d)
        if kind == 0:
            w = swa_w_in[j]
            nq, nkv = SWA_Q_HEADS * HEAD_DIM, SWA_KV_HEADS * HEAD_DIM
            w_all = jnp.concatenate([w[:, :nq], _dup_heads(w[:, nq:nq + nkv], SWA_KV_HEADS)],
                                    axis=1).astype(BF16)
            wvt = _pad_heads(w[:, nq + nkv:], SWA_KV_HEADS).T.astype(BF16)
            q, k, v = _swa_proj_call(x, mod, gain1, w_all, wvt, cos_t, sin_t)
            o = _swa_attn_call(q, k, v, swa_sinks[j])
            wo = swa_w_o[j]
        elif kind == 1:
            w = gla_w_in[j]
            n_main = 2 * GLA_HEADS * GLA_DK + 2 * GLA_HEADS * GLA_DV
            q, k, v, r, la = _gla_proj_call(
                x, mod, gain1, w[:, :n_main].astype(BF16), w[:, n_main:].astype(BF16),
                gla_w_gate_up[j].astype(BF16), gla_b_gate[j].reshape(1, -1))
            o = _gla_call(q, k, v, r, la, gla_head_norm[j].reshape(1, -1))
            wo = gla_w_o[j]
        else:
            w = fox_w_in[j]
            order = jnp.argsort(fox_b_f[j])
            by_head = lambda m: jnp.take(m.reshape(d, FOX_HEADS, HEAD_DIM), order, axis=1).reshape(d, d)
            q, k, v, stats = _fox_proj_call(
                x, mod, gain1,
                by_head(w[:, :d]).astype(BF16), by_head(w[:, d:2 * d]).astype(BF16),
                by_head(w[:, 2 * d:3 * d]).T.astype(BF16), jnp.take(w[:, 3 * d:], order, axis=1).astype(BF16),
                jnp.take(fox_b_f[j], order).reshape(1, -1), pq, pk)
            o = _fox_attn_call(q, k, v, stats)
            wo = jnp.take(fox_w_o[j].reshape(FOX_HEADS, HEAD_DIM, d), order, axis=0).reshape(d, d)
        x = _post_call(x, o, mod, gain2, wo.astype(BF16), wgu_stack, wd_stack, i, fn,
                       final=(i == depth - 1))
    return x
```

```python
import functools
import math

import numpy as np
import jax
import jax.numpy as jnp
from jax import lax
from jax.experimental import pallas as pl
from jax.experimental.pallas import tpu as pltpu

D_MODEL = 1024
HEAD_DIM = 64
RMS_EPS = 1e-6
SWA_Q_HEADS = 16
SWA_KV_HEADS = 4
SWA_WINDOW = 128
SWA_SCORE_SLOTS = 4
ROPE_THETA = 150000.0
GLA_HEADS = 4
GLA_DK = 128
GLA_DV = 256
GLA_RANK = 16
GLA_TAU = 16.0
GLA_CHUNK = 64
GLA_SUB = 8
FOX_HEADS = 16
FOX_STATS = 8
Q_MAX, K_MAX, QK_MAX, LC_FIRST, LC_LAST = range(5)
FOX_DEAD_LOG2 = -160.0
FOX_BOUND_SLACK = 1.02
D_FF = 2816
N_MIXERS = 3

LANES = 128
BF16_ROWS = 16
NEG_BIG = -1e30
LOG2E = 1.4426950408889634
VMEM_LIMIT = 56 * 1024 * 1024

BF16 = jnp.bfloat16
F32 = jnp.float32


def _dot(a, b):
    return jnp.dot(a, b, preferred_element_type=F32)


def _dot_nt(a, b):
    return lax.dot_general(a, b, (((1,), (1,)), ((), ())), preferred_element_type=F32)


def _dot_tn(a, b):
    return lax.dot_general(a, b, (((0,), (0,)), ((), ())), preferred_element_type=F32)


def _split3(x):
    hi = x.astype(BF16)
    r1 = x - hi.astype(F32)
    mid = r1.astype(BF16)
    lo = (r1 - mid.astype(F32)).astype(BF16)
    return hi, mid, lo


def _cumsum_rows(x):
    n = x.shape[0]
    row = lax.broadcasted_iota(jnp.int32, (n, n), 0)
    col = lax.broadcasted_iota(jnp.int32, (n, n), 1)
    tril = jnp.where(row >= col, 1.0, 0.0).astype(BF16)
    w = x.shape[1]
    sums = _dot(tril, jnp.concatenate(_split3(x), axis=1))
    return sums[:, :w] + sums[:, w:2 * w] + sums[:, 2 * w:]


def _log_sigmoid(x):
    return jnp.minimum(x, 0.0) - jnp.log(1.0 + jnp.exp(-jnp.abs(x)))


def _silu(x):
    return x * (1.0 / (1.0 + jnp.exp(-x)))


def _rms(x, gain):
    ms = jnp.mean(x * x, axis=-1, keepdims=True)
    return x * lax.rsqrt(ms + RMS_EPS) * gain


def _norm_mod(x, gain, shift, scale):
    return _rms(x, gain) * (1.0 + scale) + shift


def _params(*sem):
    return pltpu.CompilerParams(dimension_semantics=sem, vmem_limit_bytes=VMEM_LIMIT)


def _const_spec(shape):
    nd = len(shape)
    return pl.BlockSpec(shape, lambda *_: (0,) * nd, pipeline_mode=pl.Buffered(1))


def _ada_kernel(ct_ref, w_ref, b_ref, out_ref):
    ca = _silu(ct_ref[...])
    w = w_ref[...]
    for b in range(ct_ref.shape[1]):
        col = ca[:, b:b + 1]
        out_ref[b:b + 1, :] = jnp.sum(col * w, axis=0, keepdims=True) + b_ref[...]


def _ada_call(c, ada_w, ada_b):
    depth, d, n = ada_w.shape
    bsz = c.shape[0]
    tn = 1536
    return pl.pallas_call(
        _ada_kernel,
        grid=(depth, n // tn),
        in_specs=[
            pl.BlockSpec((d, bsz), lambda l, j: (0, 0)),
            pl.BlockSpec((None, d, tn), lambda l, j: (l, 0, j)),
            pl.BlockSpec((None, 1, tn), lambda l, j: (l, 0, j)),
        ],
        out_specs=pl.BlockSpec((None, bsz, tn), lambda l, j: (l, 0, j)),
        out_shape=jax.ShapeDtypeStruct((depth, bsz, n), F32),
        compiler_params=_params("arbitrary", "arbitrary"),
        name="ada_mod",
    )(c.T, ada_w, ada_b.reshape(depth, 1, n))


def _rope(x, cos, sin_signed):
    width = x.shape[1]
    reps = width // cos.shape[1]
    c = jnp.tile(cos, (1, reps))
    s = jnp.tile(sin_signed, (1, reps))
    lane = lax.broadcasted_iota(jnp.int32, x.shape, 1)
    first_half = (lane % HEAD_DIM) < (HEAD_DIM // 2)
    rot = jnp.where(first_half,
                    pltpu.roll(x, width - HEAD_DIM // 2, 1),
                    pltpu.roll(x, HEAD_DIM // 2, 1))
    return x * c + rot * s


def _ones_row_64(vt):
    ones_row = lax.broadcasted_iota(jnp.int32, vt.shape, 0) % LANES == HEAD_DIM
    return jnp.where(ones_row, 1.0, vt)


def _swa_proj_kernel(x_ref, mod_ref, gain_ref, w_ref, wvt_ref, cos_ref, sin_ref, q_ref, k_ref, vt_ref):
    h = _norm_mod(x_ref[...], gain_ref[...], mod_ref[0:1, :], mod_ref[1:2, :]).astype(BF16)
    cos, sin = cos_ref[...], sin_ref[...]
    nq = q_ref.shape[1]
    q = _dot(h, w_ref[:, :nq])
    q_ref[...] = (_rope(q, cos, sin) * (HEAD_DIM ** -0.5 * LOG2E)).astype(BF16)
    k = _dot(h, w_ref[:, nq:])
    k_ref[...] = _rope(k, cos, sin).astype(BF16)
    vt_ref[...] = _ones_row_64(_dot_nt(wvt_ref[...], h)).astype(BF16)


def _swa_proj_call(x, mod, gain, w, wvt, cos, sin, tm=1024):
    bsz, s, d = x.shape
    nq, nkv = D_MODEL, SWA_KV_HEADS * LANES
    row = lambda b, i: (b, i, 0)
    return pl.pallas_call(
        _swa_proj_kernel,
        grid=(bsz, s // tm),
        in_specs=[
            pl.BlockSpec((None, tm, d), row),
            pl.BlockSpec((None, 6, d), lambda b, i: (b, 0, 0)),
            _const_spec((1, d)),
            _const_spec(w.shape),
            _const_spec(wvt.shape),
            pl.BlockSpec((tm, LANES), lambda b, i: (i, 0)),
            pl.BlockSpec((tm, LANES), lambda b, i: (i, 0)),
        ],
        out_specs=[
            pl.BlockSpec((None, tm, nq), row),
            pl.BlockSpec((None, tm, nkv), row),
            pl.BlockSpec((None, nkv, tm), lambda b, i: (b, 0, i)),
        ],
        out_shape=[
            jax.ShapeDtypeStruct((bsz, s, nq), BF16),
            jax.ShapeDtypeStruct((bsz, s, nkv), BF16),
            jax.ShapeDtypeStruct((bsz, nkv, s), BF16),
        ],
        compiler_params=_params("arbitrary", "arbitrary"),
        name="swa_proj",
    )(x, mod, gain, w, wvt, cos, sin)


def _gla_proj_kernel(x_ref, mod_ref, gain_ref, w_ref, wa_ref, wg_ref, bg_ref,
                     q_ref, k_ref, v_ref, r_ref, la_ref):
    h = _norm_mod(x_ref[...], gain_ref[...], mod_ref[0:1, :], mod_ref[1:2, :]).astype(BF16)
    nk = q_ref.shape[1]
    nv = v_ref.shape[1]
    q_ref[...] = _dot(h, w_ref[:, :nk]).astype(BF16)
    k_ref[...] = _dot(h, w_ref[:, nk:2 * nk]).astype(BF16)
    v_ref[...] = _dot(h, w_ref[:, 2 * nk:2 * nk + nv]).astype(BF16)
    r_ref[...] = _dot(h, w_ref[:, 2 * nk + nv:]).astype(BF16)
    a_low = _dot(h, wa_ref[...]).astype(BF16)
    z = _dot(a_low, wg_ref[...]) + bg_ref[...]
    la_ref[...] = _log_sigmoid(z) * (1.0 / GLA_TAU)


def _gla_proj_call(x, mod, gain, w, wa, wg, bg, tm=1024):
    bsz, s, d = x.shape
    nk, nv = GLA_HEADS * GLA_DK, GLA_HEADS * GLA_DV
    row = lambda b, i: (b, i, 0)
    return pl.pallas_call(
        _gla_proj_kernel,
        grid=(bsz, s // tm),
        in_specs=[
            pl.BlockSpec((None, tm, d), row),
            pl.BlockSpec((None, 6, d), lambda b, i: (b, 0, 0)),
            _const_spec((1, d)),
            _const_spec(w.shape),
            _const_spec(wa.shape),
            _const_spec(wg.shape),
            _const_spec(bg.shape),
        ],
        out_specs=[
            pl.BlockSpec((None, tm, nk), row),
            pl.BlockSpec((None, tm, nk), row),
            pl.BlockSpec((None, tm, nv), row),
            pl.BlockSpec((None, tm, nv), row),
            pl.BlockSpec((None, tm, nk), row),
        ],
        out_shape=[
            jax.ShapeDtypeStruct((bsz, s, nk), BF16),
            jax.ShapeDtypeStruct((bsz, s, nk), BF16),
            jax.ShapeDtypeStruct((bsz, s, nv), BF16),
            jax.ShapeDtypeStruct((bsz, s, nv), BF16),
            jax.ShapeDtypeStruct((bsz, s, nk), F32),
        ],
        compiler_params=_params("arbitrary", "arbitrary"),
        name="gla_proj",
    )(x, mod, gain, w, wa, wg, bg)


def _spread_heads(x, extra, out_ref):
    lane = lax.broadcasted_iota(jnp.int32, (x.shape[0], LANES), 1)
    low = lane < HEAD_DIM
    for p in range(x.shape[1] // LANES):
        xs = x[:, p * LANES:(p + 1) * LANES]
        ex = extra[:, p * LANES:(p + 1) * LANES]
        out_ref[:, (2 * p) * LANES:(2 * p + 1) * LANES] = jnp.where(low, xs, ex).astype(out_ref.dtype)
        odd = pltpu.roll(jnp.where(low, ex, xs), HEAD_DIM, 1)
        out_ref[:, (2 * p + 1) * LANES:(2 * p + 2) * LANES] = odd.astype(out_ref.dtype)


def _fox_proj_kernel(x_ref, mod_ref, gain_ref, wq_ref, wk_ref, wvt_ref, wf_ref, bf_ref,
                     pq_ref, pk_ref, hsel_ref, q_ref, k_ref, vt_ref, stats_ref, carry_ref):
    @pl.when(pl.program_id(1) == 0)
    def _():
        carry_ref[...] = jnp.zeros_like(carry_ref)

    h = _norm_mod(x_ref[...], gain_ref[...], mod_ref[0:1, :], mod_ref[1:2, :]).astype(BF16)
    log_f = _log_sigmoid(_dot(h, wf_ref[...]) + bf_ref[...])
    lc = _cumsum_rows(log_f) + carry_ref[...]
    carry_ref[...] = lc[lc.shape[0] - 1:, :]
    lc2 = lc * LOG2E
    hi, mid, lo = _split3(lc2)
    aug = jnp.concatenate([hi, mid, lo, jnp.ones_like(hi)], axis=1)
    qs = _dot(h, wq_ref[...]) * (HEAD_DIM ** -0.5 * LOG2E)
    ks = _dot(h, wk_ref[...])
    _spread_heads(qs, _dot(aug, pq_ref[...]), q_ref)
    _spread_heads(ks, _dot(aug, pk_ref[...]), k_ref)
    qn2 = _dot((qs * qs).astype(BF16), hsel_ref[...])
    kn2 = _dot((ks * ks).astype(BF16), hsel_ref[...])
    tm = lc2.shape[0]
    stats_ref[...] = jnp.concatenate([
        jnp.sqrt(jnp.max(qn2, axis=0, keepdims=True)),
        jnp.sqrt(jnp.max(kn2, axis=0, keepdims=True)),
        jnp.sqrt(jnp.max(qn2 * kn2, axis=0, keepdims=True)),
        lc2[0:1, :], lc2[tm - 1:tm, :],
        jnp.zeros((FOX_STATS - 5, lc2.shape[1]), F32)], axis=0)
    vt = _dot_nt(wvt_ref[...], h).astype(BF16)
    pad = jnp.where(lax.broadcasted_iota(jnp.int32, (HEAD_DIM, vt.shape[1]), 0) == 0,
                    1.0, 0.0).astype(BF16)
    for hd in range(FOX_HEADS):
        vt_ref[hd * LANES:hd * LANES + HEAD_DIM, :] = vt[hd * HEAD_DIM:(hd + 1) * HEAD_DIM, :]
        vt_ref[hd * LANES + HEAD_DIM:(hd + 1) * LANES, :] = pad


def _fox_proj_call(x, mod, gain, wq, wk, wvt, wf, bf, pq, pk, tm=512):
    bsz, s, d = x.shape
    nqk = FOX_HEADS * LANES
    row = lambda b, i: (b, i, 0)
    hsel = jnp.asarray(np.repeat(np.eye(FOX_HEADS, dtype=np.float32), HEAD_DIM, axis=0), BF16)
    return pl.pallas_call(
        _fox_proj_kernel,
        grid=(bsz, s // tm),
        in_specs=[
            pl.BlockSpec((None, tm, d), row),
            pl.BlockSpec((None, 6, d), lambda b, i: (b, 0, 0)),
            _const_spec((1, d)),
            _const_spec(wq.shape),
            _const_spec(wk.shape),
            _const_spec(wvt.shape),
            _const_spec(wf.shape),
            _const_spec(bf.shape),
            _const_spec(pq.shape),
            _const_spec(pk.shape),
            _const_spec(hsel.shape),
        ],
        out_specs=[
            pl.BlockSpec((None, tm, nqk), row),
            pl.BlockSpec((None, tm, nqk), row),
            pl.BlockSpec((None, None, nqk, tm), lambda b, i: (b, i, 0, 0)),
            pl.BlockSpec((None, None, FOX_STATS, FOX_HEADS), lambda b, i: (b, i, 0, 0)),
        ],
        out_shape=[
            jax.ShapeDtypeStruct((bsz, s, nqk), BF16),
            jax.ShapeDtypeStruct((bsz, s, nqk), BF16),
            jax.ShapeDtypeStruct((bsz, s // tm, nqk, tm), BF16),
            jax.ShapeDtypeStruct((bsz, s // tm, FOX_STATS, FOX_HEADS), F32),
        ],
        scratch_shapes=[pltpu.VMEM((1, FOX_HEADS), F32)],
        compiler_params=_params("arbitrary", "arbitrary"),
        name="fox_proj",
    )(x, mod, gain, wq, wk, wvt, wf, bf, pq, pk, hsel)


FFN_CAST_BLOCKS = 16


def _mixer_call(kernel_fn, grid, in_specs, out_spec, out_shape, scratch_shapes, name, args,
                wgu_stack, wd_stack, layer):
    n_in = len(in_specs)
    steps = math.prod(grid)
    blocks = math.gcd(steps, FFN_CAST_BLOCKS)
    per = steps // blocks

    def linear(*ids):
        idx = ids[0]
        for extent, i in zip(grid[1:], ids[1:]):
            idx = idx * extent + i
        return idx

    def body(*refs):
        ins, (gu32, d32) = refs[:n_in], refs[n_in:n_in + 2]
        out, (gu16, d16), scratch = refs[n_in + 2], refs[n_in + 3:n_in + 5], refs[n_in + 5:]

        @pl.when(linear(*[pl.program_id(a) for a in range(len(grid))]) % per == 0)
        def _():
            gu16[...] = gu32[...].astype(BF16)
            d16[...] = d32[...].astype(BF16)

        kernel_fn(*ins, out, *scratch)

    def weight_specs(stack):
        rows, cols = stack.shape[1] // blocks, stack.shape[2]
        return (pl.BlockSpec((None, rows, cols), lambda *ids: (layer, linear(*ids) // per, 0)),
                pl.BlockSpec((rows, cols), lambda *ids: (linear(*ids) // per, 0)),
                jax.ShapeDtypeStruct(stack.shape[1:], BF16))

    gu_in, gu_out, gu_shape = weight_specs(wgu_stack)
    d_in, d_out, d_shape = weight_specs(wd_stack)
    return pl.pallas_call(
        body,
        grid=grid,
        in_specs=list(in_specs) + [gu_in, d_in],
        out_specs=[out_spec, gu_out, d_out],
        out_shape=[out_shape, gu_shape, d_shape],
        scratch_shapes=scratch_shapes,
        compiler_params=_params(*["arbitrary"] * len(grid)),
        name=name,
    )(*args, wgu_stack, wd_stack)


def _swa_band_bias():
    w, group = SWA_WINDOW, SWA_Q_HEADS // SWA_KV_HEADS
    key = np.arange(2 * w)[:, None]
    qry = np.arange(group * w)[None, :] % w
    dist = (w + qry) - key
    band = (dist >= 0) & (dist < w)
    allowed = np.stack([band & (key >= w), band])
    return jnp.asarray(np.where(allowed, 0.0, NEG_BIG), F32)


def _swa_attn_kernel(sink_ref, bias_ref, q_ref, kc_ref, vtc_ref, o_ref, st_ref, kp_ref, vtp_ref):
    w = SWA_WINDOW
    group = SWA_Q_HEADS // SWA_KV_HEADS
    nq = group * w
    nb = q_ref.shape[0] // w
    lane = lax.broadcasted_iota(jnp.int32, (w, LANES), 1)
    low = lane < HEAD_DIM

    @pl.when(pl.program_id(1) == 0)
    def _():
        kp_ref[...] = jnp.zeros_like(kp_ref)
        vtp_ref[...] = jnp.zeros_like(vtp_ref)

    def scores(u, g, slot):
        rows = slice(u * w, (u + 1) * w)
        cols = slice(g * LANES, (g + 1) * LANES)
        slabs = []
        for hh in range(group):
            head = g * group + hh
            qs = q_ref[rows, (head // 2) * LANES:(head // 2 + 1) * LANES]
            keep = low if head % 2 == 0 else jnp.logical_not(low)
            slabs.append(jnp.where(keep, qs, jnp.zeros_like(qs)))
        q_stack = jnp.concatenate(slabs, axis=0)
        k_prev = kp_ref[:, cols] if u == 0 else kc_ref[(u - 1) * w:u * w, cols]
        k_both = jnp.concatenate([k_prev, kc_ref[rows, cols]], axis=0)
        st_ref[slot] = _dot_nt(k_both, q_stack)

    def finish(u, g, slot):
        rows = slice(u * w, (u + 1) * w)
        cols = slice(g * LANES, (g + 1) * LANES)
        bias = bias_ref[jnp.minimum(pl.program_id(1), 1)] if u == 0 else bias_ref[1]
        st = st_ref[slot] + bias
        sink = sink_ref[:, g * nq:(g + 1) * nq] * LOG2E
        m = jnp.maximum(jnp.max(st, axis=0, keepdims=True), sink)
        pt = jnp.exp2(st - m).astype(BF16)
        vt_prev = vtp_ref[cols, :] if u == 0 else vtc_ref[cols, (u - 1) * w:u * w]
        vt_both = jnp.concatenate([vt_prev, vtc_ref[cols, rows]], axis=1)
        acc = _dot(vt_both, pt)
        den = acc[HEAD_DIM:HEAD_DIM + 1, :] + jnp.exp2(sink - m)
        ot = acc[:HEAD_DIM, :] / den
        o_t = jnp.concatenate([ot[:, hh * w:(hh + 1) * w] for hh in range(group)], axis=0)
        o_ref[rows, g * group * HEAD_DIM:(g + 1) * group * HEAD_DIM] = o_t.T.astype(BF16)

    items = [(u, g) for u in range(nb) for g in range(SWA_KV_HEADS)]
    slots = st_ref.shape[0]
    ahead = slots - 1
    for n in range(min(ahead, len(items))):
        scores(*items[n], n % slots)
    for n, item in enumerate(items):
        if n + ahead < len(items):
            scores(*items[n + ahead], (n + ahead) % slots)
        finish(*item, n % slots)
    kp_ref[...] = kc_ref[(nb - 1) * w:, :]
    vtp_ref[...] = vtc_ref[:, (nb - 1) * w:]


def _swa_attn_call(q, k, vt, sinks, ffn_weights, nb=4):
    bsz, s, d = q.shape
    w = SWA_WINDOW
    nkv = k.shape[2]
    sink_row = jnp.repeat(sinks, w).reshape(1, -1)
    bias = _swa_band_bias()
    cur = lambda b, i: (b, i, 0)
    return _mixer_call(
        _swa_attn_kernel,
        grid=(bsz, s // (nb * w)),
        in_specs=[
            _const_spec(sink_row.shape),
            _const_spec(bias.shape),
            pl.BlockSpec((None, nb * w, d), cur),
            pl.BlockSpec((None, nb * w, nkv), cur),
            pl.BlockSpec((None, nkv, nb * w), lambda b, i: (b, 0, i)),
        ],
        out_spec=pl.BlockSpec((None, nb * w, d), cur),
        out_shape=jax.ShapeDtypeStruct((bsz, s, d), BF16),
        scratch_shapes=[
            pltpu.VMEM((SWA_SCORE_SLOTS, 2 * w, (SWA_Q_HEADS // SWA_KV_HEADS) * w), F32),
            pltpu.VMEM((w, nkv), BF16),
            pltpu.VMEM((nkv, w), BF16),
        ],
        name="swa_attn", args=(sink_row, bias, q, k, vt), **ffn_weights)


def _chunk_cumsum(x):
    n = x.shape[0]
    row = lax.broadcasted_iota(jnp.int32, (n, n), 0)
    col = lax.broadcasted_iota(jnp.int32, (n, n), 1)
    same_chunk = (row // GLA_CHUNK) == (col // GLA_CHUNK)
    tril = jnp.where(jnp.logical_and(row >= col, same_chunk), 1.0, 0.0).astype(BF16)
    hi, mid, lo = _split3(x)
    return _dot(tril, hi) + _dot(tril, mid) + _dot(tril, lo)


def _gla_intra(q, k, b2, k_rows, b_rows, base):
    c, sub = GLA_CHUNK, GLA_SUB
    col = lax.broadcasted_iota(jnp.int32, (sub, c), 1)
    row = lax.broadcasted_iota(jnp.int32, (sub, c), 0)
    blocks = []
    for i in range(c // sub):
        lo = i * sub
        q_i = q[lo:lo + sub, :]
        b_i = b2[lo:lo + sub, :]
        if i == 0:
            a = jnp.zeros((sub, c), F32)
        else:
            ref = b2[lo - 1:lo, :]
            n = -(-lo // BF16_ROWS) * BF16_ROWS
            q_t = (q_i * jnp.exp2(b_i - ref)).astype(BF16)
            k_t = (k[:n, :] * jnp.exp2(jnp.minimum(ref - b2[:n, :], 0.0))).astype(BF16)
            if n < c:
                k_t = jnp.concatenate([k_t, jnp.zeros((c - n, k_t.shape[1]), BF16)], axis=0)
            a = _dot_nt(q_t, k_t)
        for s in range(lo, lo + sub):
            w = jnp.exp2(b_i - b_rows[base + s:base + s + 1, :])
            val = jnp.sum(q_i * k_rows[base + s:base + s + 1, :] * w, axis=1, keepdims=True)
            a = jnp.where(col == s, val, a)
        blocks.append(jnp.where(row + lo >= col, a, 0.0))
    return jnp.concatenate(blocks, axis=0)


def _gla_kernel(q_ref, k_ref, v_ref, r_ref, la_ref, hn_ref, o_ref, state_ref, krow_ref, brow_ref):
    @pl.when(pl.program_id(2) == 0)
    def _():
        state_ref[...] = jnp.zeros_like(state_ref)

    for hh in range(state_ref.shape[0]):
        kq = slice(hh * GLA_DK, (hh + 1) * GLA_DK)
        vv = slice(hh * GLA_DV, (hh + 1) * GLA_DV)
        _gla_head(q_ref.at[:, kq], k_ref.at[:, kq], v_ref.at[:, vv], r_ref.at[:, vv], la_ref.at[:, kq],
                  hn_ref, o_ref.at[:, vv], state_ref.at[hh], krow_ref.at[hh], brow_ref.at[hh])


def _gla_head(q_ref, k_ref, v_ref, r_ref, la_ref, hn_ref, o_ref, state_ref, krow_ref, brow_ref):
    c = GLA_CHUNK
    nc = q_ref.shape[0] // c
    rows = [slice(ci * c, (ci + 1) * c) for ci in range(nc)]
    b2_all = _chunk_cumsum(la_ref[...]) * LOG2E
    q_all = q_ref[...].astype(F32) * (GLA_DK ** -0.5)
    k_all = k_ref[...].astype(F32)
    krow_ref[...] = k_all
    brow_ref[...] = b2_all
    b2 = [b2_all[r] for r in rows]
    q = [q_all[r] for r in rows]
    k = [k_all[r] for r in rows]
    last = [b[c - 1:c, :] for b in b2]
    q_in = [(q[i] * jnp.exp2(b2[i])).astype(BF16) for i in range(nc)]
    k_out = [(k[i] * jnp.exp2(last[i] - b2[i])).astype(BF16) for i in range(nc)]
    kv = [_dot_tn(v_ref[rows[i], :], k_out[i]) for i in range(nc)]
    attn = [_gla_intra(q[i], k[i], b2[i], krow_ref, brow_ref, i * c).astype(BF16) for i in range(nc)]
    intra = [_dot(attn[i], v_ref[rows[i], :]) for i in range(nc)]
    state_t = state_ref[...]
    for i in range(nc):
        o = intra[i] + _dot_nt(q_in[i], state_t.astype(BF16))
        state_t = state_t * jnp.exp2(last[i]) + kv[i]
        r = r_ref[rows[i], :].astype(F32)
        o_ref[rows[i], :] = (_rms(o, hn_ref[...]) * _silu(r)).astype(BF16)
    state_ref[...] = state_t


def _gla_call(q, k, v, r, la, head_norm, ffn_weights, tm=256, nh=2):
    bsz, s, _ = q.shape
    dk, dv = GLA_DK, GLA_DV
    blk = lambda b, h, i: (b, i, h)
    return _mixer_call(
        _gla_kernel,
        grid=(bsz, GLA_HEADS // nh, s // tm),
        in_specs=[
            pl.BlockSpec((None, tm, nh * dk), blk),
            pl.BlockSpec((None, tm, nh * dk), blk),
            pl.BlockSpec((None, tm, nh * dv), blk),
            pl.BlockSpec((None, tm, nh * dv), blk),
            pl.BlockSpec((None, tm, nh * dk), blk),
            _const_spec((1, dv)),
        ],
        out_spec=pl.BlockSpec((None, tm, nh * dv), blk),
        out_shape=jax.ShapeDtypeStruct((bsz, s, GLA_HEADS * dv), BF16),
        scratch_shapes=[pltpu.VMEM((nh, dv, dk), F32), pltpu.VMEM((nh, tm, dk), F32),
                        pltpu.VMEM((nh, tm, dk), F32)],
        name="gla_mix", args=(q, k, v, r, la, head_norm), **ffn_weights)


def _fox_first_live_block(stats_ref, b, first_head, nh, i):
    heads = [first_head + e for e in range(nh)]
    q_max = [FOX_BOUND_SLACK * stats_ref[b, i, Q_MAX, hd] for hd in heads]
    margin = [stats_ref[b, i, LC_FIRST, hd] + FOX_BOUND_SLACK * stats_ref[b, i, QK_MAX, hd]
              - FOX_DEAD_LOG2 for hd in heads]

    def dead(j):
        is_dead = True
        for e, hd in enumerate(heads):
            gap = q_max[e] * stats_ref[b, j, K_MAX, hd] - stats_ref[b, j, LC_LAST, hd] + margin[e]
            is_dead = jnp.logical_and(is_dead, gap <= 0.0)
        return is_dead

    last = jnp.maximum(i - 1, 0)
    return lax.while_loop(lambda j: jnp.logical_and(j < i, dead(jnp.minimum(j, last))),
                          lambda j: j + 1, jnp.int32(0))


def _fox_attn_kernel(stats_ref, q_ref, k_ref, vt_ref, o_ref, m_ref, acc_ref, sa_ref, sb_ref):
    tk = vt_ref.shape[2]
    nh = m_ref.shape[0]
    i = pl.program_id(2)
    j0 = _fox_first_live_block(stats_ref, pl.program_id(0), pl.program_id(1) * nh, nh, i)
    m_ref[...] = jnp.full_like(m_ref, NEG_BIG)
    acc_ref[...] = jnp.zeros_like(acc_ref)

    half = tk // 2

    def scores(j, buf, e, diag=False):
        lanes = slice(e * LANES, (e + 1) * LANES)
        row0 = pl.multiple_of(j * tk, tk)
        if not diag:
            buf[e] = _dot_nt(k_ref[pl.ds(row0, tk), lanes], q_ref[:, lanes])
        else:
            buf[e, :half, :] = _dot_nt(k_ref[pl.ds(row0, half), lanes], q_ref[:, lanes])
            buf[e, half:, half:] = _dot_nt(k_ref[pl.ds(row0 + half, half), lanes], q_ref[half:, lanes])

    def online_update(e, st, vt, cols):
        m_old = m_ref[e, :, cols]
        m_new = jnp.maximum(m_old, jnp.max(st, axis=0, keepdims=True))
        alpha = jnp.exp2(m_old - m_new)
        pt = jnp.exp2(st - m_new).astype(BF16)
        acc_ref[e, :, cols] = alpha * acc_ref[e, :, cols] + _dot(vt, pt)
        m_ref[e, :, cols] = m_new

    def accumulate(j, buf, e, masked):
        rows = slice(e * LANES, (e + 1) * LANES)
        if not masked:
            online_update(e, buf[e], vt_ref[j, rows, :], slice(None))
            return
        key = lax.broadcasted_iota(jnp.int32, (half, half), 0)
        qry = lax.broadcasted_iota(jnp.int32, (half, half), 1)
        causal = key <= qry
        st = jnp.concatenate([jnp.where(causal, buf[e, :half, :half], NEG_BIG), buf[e, :half, half:]], axis=1)
        online_update(e, st, vt_ref[j, rows, :half], slice(None))
        online_update(e, jnp.where(causal, buf[e, half:, half:], NEG_BIG), vt_ref[j, rows, half:],
                      slice(half, None))

    def block(j, buf, masked, following):
        for e in range(nh):
            if e + 1 < nh:
                scores(j, buf, e + 1, masked)
            elif following is not None:
                scores(following[0], following[1], 0, following[2])
            accumulate(j, buf, e, masked)

    scores(j0, sa_ref, 0)
    n_full = i - j0

    def body(t, carry):
        j = j0 + 2 * t
        block(j, sa_ref, False, (j + 1, sb_ref, False))
        block(j + 1, sb_ref, False, (j + 2, sa_ref, False))
        return carry

    lax.fori_loop(0, n_full // 2, body, 0)

    @pl.when(n_full % 2 == 0)
    def _():
        block(i, sa_ref, True, None)

    @pl.when(n_full % 2 == 1)
    def _():
        block(i - 1, sa_ref, False, (i, sb_ref, True))
        block(i, sb_ref, True, None)

    outs = []
    for e in range(nh):
        acc = acc_ref[e]
        outs.append(acc[:HEAD_DIM, :] / acc[HEAD_DIM:HEAD_DIM + 1, :])
    o_ref[...] = jnp.concatenate(outs, axis=0).T.astype(BF16)


def _fox_attn_call(q, k, vt, stats, ffn_weights, nh=4):
    bsz, s, _ = q.shape
    tk = vt.shape[3]
    tq = tk
    return _mixer_call(
        _fox_attn_kernel,
        grid=(bsz, FOX_HEADS // nh, s // tq),
        in_specs=[
            pl.BlockSpec(memory_space=pltpu.SMEM),
            pl.BlockSpec((None, tq, nh * LANES), lambda b, p, i: (b, i, p)),
            pl.BlockSpec((None, s, nh * LANES), lambda b, p, i: (b, 0, p)),
            pl.BlockSpec((None, s // tk, nh * LANES, tk), lambda b, p, i: (b, 0, p, 0)),
        ],
        out_spec=pl.BlockSpec((None, tq, nh * HEAD_DIM), lambda b, p, i: (b, i, p)),
        out_shape=jax.ShapeDtypeStruct((bsz, s, D_MODEL), BF16),
        scratch_shapes=[
            pltpu.VMEM((nh, 1, tq), F32),
            pltpu.VMEM((nh, LANES, tq), F32),
            pltpu.VMEM((nh, tk, tq), F32),
            pltpu.VMEM((nh, tk, tq), F32),
        ],
        name="fox_attn", args=(stats, q, k, vt), **ffn_weights)


def _post_kernel(x_ref, o_ref, mod_ref, gain_ref, wo_ref, wgu_ref, wd_ref, fn_ref, out_ref,
                 *, ff_chunk, final):
    x1 = x_ref[...] + mod_ref[2:3, :] * _dot(o_ref[...], wo_ref[...])
    h = _norm_mod(x1, gain_ref[...], mod_ref[3:4, :], mod_ref[4:5, :]).astype(BF16)
    acc = jnp.zeros(x1.shape, F32)
    for c0 in range(0, D_FF, ff_chunk):
        g = _dot(h, wgu_ref[:, c0:c0 + ff_chunk])
        u = _dot(h, wgu_ref[:, D_FF + c0:D_FF + c0 + ff_chunk])
        acc = acc + _dot((_silu(g) * u).astype(BF16), wd_ref[c0:c0 + ff_chunk, :])
    x2 = x1 + mod_ref[5:6, :] * acc
    if final:
        x2 = _rms(x2, fn_ref[...])
    out_ref[...] = x2


def _post_call(x, o, mod, gain, wo, wgu, wd, final_norm, final, tm=1024, ff_chunk=256):
    bsz, s, d = x.shape
    row = lambda b, i: (b, i, 0)
    return pl.pallas_call(
        functools.partial(_post_kernel, ff_chunk=ff_chunk, final=final),
        grid=(bsz, s // tm),
        in_specs=[
            pl.BlockSpec((None, tm, d), row),
            pl.BlockSpec((None, tm, d), row),
            pl.BlockSpec((None, 6, d), lambda b, i: (b, 0, 0)),
            _const_spec((1, d)),
            _const_spec(wo.shape),
            _const_spec(wgu.shape),
            _const_spec(wd.shape),
            _const_spec((1, d)),
        ],
        out_specs=pl.BlockSpec((None, tm, d), row),
        out_shape=jax.ShapeDtypeStruct((bsz, s, d), F32),
        compiler_params=_params("arbitrary", "arbitrary"),
        name="post_ffn",
    )(x, o, mod, gain, wo, wgu, wd, final_norm)


def _rope_tables(s):
    half = HEAD_DIM // 2
    inv = 1.0 / (ROPE_THETA ** (jnp.arange(0, HEAD_DIM, 2, dtype=F32) / HEAD_DIM))
    ang = jnp.arange(s, dtype=F32)[:, None] * inv[None, :]
    cos, sin = jnp.cos(ang), jnp.sin(ang)
    reps = LANES // HEAD_DIM
    cos_t = jnp.tile(jnp.concatenate([cos, cos], axis=1), (1, reps))
    sin_t = jnp.tile(jnp.concatenate([-sin, sin], axis=1), (1, reps))
    assert half * 2 == HEAD_DIM
    return cos_t, sin_t


def _dup_heads(w, heads):
    w3 = w.reshape(w.shape[0], heads, HEAD_DIM)
    return jnp.concatenate([w3, w3], axis=2).reshape(w.shape[0], heads * LANES)


def _pad_heads(w, heads):
    w3 = w.reshape(w.shape[0], heads, HEAD_DIM)
    return jnp.concatenate([w3, jnp.zeros_like(w3)], axis=2).reshape(w.shape[0], heads * LANES)


def _fox_placement():
    h = FOX_HEADS
    pq = np.zeros((4 * h, h * HEAD_DIM), np.float32)
    pk = np.zeros((4 * h, h * HEAD_DIM), np.float32)
    for head in range(h):
        base = (head // 2) * LANES + (HEAD_DIM if head % 2 == 0 else 0)
        for part in range(3):
            pq[part * h + head, base + part] = 1.0
            pk[3 * h + head, base + part] = 1.0
            pq[3 * h + head, base + 3 + part] = 1.0
            pk[part * h + head, base + 3 + part] = -1.0
    return jnp.asarray(pq, BF16), jnp.asarray(pk, BF16)


def kernel(x, c, ada_w, ada_b, norm_gain, ffn_w_gu, ffn_w_down, swa_w_in, swa_sinks, swa_w_o,
           gla_w_in, gla_w_gate_up, gla_b_gate, gla_head_norm, gla_w_o, fox_w_in, fox_b_f, fox_w_o,
           final_norm):
    bsz, s, d = x.shape
    depth = ada_w.shape[0]
    mod_all = _ada_call(c, ada_w, ada_b).reshape(depth, bsz, 6, d)
    cos_t, sin_t = _rope_tables(s)
    pq, pk = _fox_placement()
    fn = final_norm.reshape(1, d)

    for i in range(depth):
        kind, j = i % N_MIXERS, i // N_MIXERS
        mod = mod_all[i]
        gain1 = norm_gain[i, 0].reshape(1, d)
        gain2 = norm_gain[i, 1].reshape(1, d)
        ffn_weights = dict(wgu_stack=ffn_w_gu, wd_stack=ffn_w_down, layer=i)
        if kind == 0:
            w = swa_w_in[j]
            nq, nkv = SWA_Q_HEADS * HEAD_DIM, SWA_KV_HEADS * HEAD_DIM
            w_all = jnp.concatenate([w[:, :nq], _dup_heads(w[:, nq:nq + nkv], SWA_KV_HEADS)],
                                    axis=1).astype(BF16)
            wvt = _pad_heads(w[:, nq + nkv:], SWA_KV_HEADS).T.astype(BF16)
            q, k, v = _swa_proj_call(x, mod, gain1, w_all, wvt, cos_t, sin_t)
            o, wgu, wd = _swa_attn_call(q, k, v, swa_sinks[j], ffn_weights)
            wo = swa_w_o[j]
        elif kind == 1:
            w = gla_w_in[j]
            n_main = 2 * GLA_HEADS * GLA_DK + 2 * GLA_HEADS * GLA_DV
            q, k, v, r, la = _gla_proj_call(
                x, mod, gain1, w[:, :n_main].astype(BF16), w[:, n_main:].astype(BF16),
                gla_w_gate_up[j].astype(BF16), gla_b_gate[j].reshape(1, -1))
            o, wgu, wd = _gla_call(q, k, v, r, la, gla_head_norm[j].reshape(1, -1), ffn_weights)
            wo = gla_w_o[j]
        else:
            w = fox_w_in[j]
            order = jnp.argsort(fox_b_f[j])
            by_head = lambda m: jnp.take(m.reshape(d, FOX_HEADS, HEAD_DIM), order, axis=1).reshape(d, d)
            q, k, v, stats = _fox_proj_call(
                x, mod, gain1,
                by_head(w[:, :d]).astype(BF16), by_head(w[:, d:2 * d]).astype(BF16),
                by_head(w[:, 2 * d:3 * d]).T.astype(BF16), jnp.take(w[:, 3 * d:], order, axis=1).astype(BF16),
                jnp.take(fox_b_f[j], order).reshape(1, -1), pq, pk)
            o, wgu, wd = _fox_attn_call(q, k, v, stats, ffn_weights)
            wo = jnp.take(fox_w_o[j].reshape(FOX_HEADS, HEAD_DIM, d), order, axis=0).reshape(d, d)
        x = _post_call(x, o, mod, gain2, wo.astype(BF16), wgu, wd, fn, final=(i == depth - 1))
    return x
```

```python
import functools
import math

import numpy as np
import jax
import jax.numpy as jnp
from jax import lax
from jax.experimental import pallas as pl
from jax.experimental.pallas import tpu as pltpu

D_MODEL = 1024
HEAD_DIM = 64
RMS_EPS = 1e-6
SWA_Q_HEADS = 16
SWA_KV_HEADS = 4
SWA_WINDOW = 128
SWA_SCORE_SLOTS = 4
ROPE_THETA = 150000.0
GLA_HEADS = 4
GLA_DK = 128
GLA_DV = 256
GLA_RANK = 16
GLA_TAU = 16.0
GLA_CHUNK = 64
GLA_SUB = 8
FOX_HEADS = 16
FOX_STATS = 8
Q_MAX, K_MAX, QK_MAX, LC_FIRST, LC_LAST = range(5)
FOX_DEAD_LOG2 = -160.0
FOX_BOUND_SLACK = 1.02
D_FF = 2816
N_MIXERS = 3

LANES = 128
BF16_ROWS = 16
NEG_BIG = -1e30
LOG2E = 1.4426950408889634
VMEM_LIMIT = 56 * 1024 * 1024

BF16 = jnp.bfloat16
F32 = jnp.float32


def _dot(a, b):
    return jnp.dot(a, b, preferred_element_type=F32)


def _dot_nt(a, b):
    return lax.dot_general(a, b, (((1,), (1,)), ((), ())), preferred_element_type=F32)


def _dot_tn(a, b):
    return lax.dot_general(a, b, (((0,), (0,)), ((), ())), preferred_element_type=F32)


def _split3(x):
    hi = x.astype(BF16)
    r1 = x - hi.astype(F32)
    mid = r1.astype(BF16)
    lo = (r1 - mid.astype(F32)).astype(BF16)
    return hi, mid, lo


def _cumsum_rows(x):
    n = x.shape[0]
    row = lax.broadcasted_iota(jnp.int32, (n, n), 0)
    col = lax.broadcasted_iota(jnp.int32, (n, n), 1)
    tril = jnp.where(row >= col, 1.0, 0.0).astype(BF16)
    w = x.shape[1]
    sums = _dot(tril, jnp.concatenate(_split3(x), axis=1))
    return sums[:, :w] + sums[:, w:2 * w] + sums[:, 2 * w:]


def _log_sigmoid(x):
    return jnp.minimum(x, 0.0) - jnp.log(1.0 + jnp.exp(-jnp.abs(x)))


def _silu(x):
    return x * (1.0 / (1.0 + jnp.exp(-x)))


def _rms(x, gain):
    ms = jnp.mean(x * x, axis=-1, keepdims=True)
    return x * lax.rsqrt(ms + RMS_EPS) * gain


def _norm_mod(x, gain, shift, scale):
    return _rms(x, gain) * (1.0 + scale) + shift


def _params(*sem):
    return pltpu.CompilerParams(dimension_semantics=sem, vmem_limit_bytes=VMEM_LIMIT)


def _const_spec(shape):
    nd = len(shape)
    return pl.BlockSpec(shape, lambda *_: (0,) * nd, pipeline_mode=pl.Buffered(1))


def _ada_kernel(ct_ref, w_ref, b_ref, out_ref):
    ca = _silu(ct_ref[...])
    w = w_ref[...]
    for b in range(ct_ref.shape[1]):
        col = ca[:, b:b + 1]
        out_ref[b:b + 1, :] = jnp.sum(col * w, axis=0, keepdims=True) + b_ref[...]


def _ada_call(c, ada_w, ada_b):
    depth, d, n = ada_w.shape
    bsz = c.shape[0]
    tn = 1536
    return pl.pallas_call(
        _ada_kernel,
        grid=(depth, n // tn),
        in_specs=[
            pl.BlockSpec((d, bsz), lambda l, j: (0, 0)),
            pl.BlockSpec((None, d, tn), lambda l, j: (l, 0, j)),
            pl.BlockSpec((None, 1, tn), lambda l, j: (l, 0, j)),
        ],
        out_specs=pl.BlockSpec((None, bsz, tn), lambda l, j: (l, 0, j)),
        out_shape=jax.ShapeDtypeStruct((depth, bsz, n), F32),
        compiler_params=_params("arbitrary", "arbitrary"),
        name="ada_mod",
    )(c.T, ada_w, ada_b.reshape(depth, 1, n))


def _rope(x, cos, sin_signed):
    width = x.shape[1]
    reps = width // cos.shape[1]
    c = jnp.tile(cos, (1, reps))
    s = jnp.tile(sin_signed, (1, reps))
    lane = lax.broadcasted_iota(jnp.int32, x.shape, 1)
    first_half = (lane % HEAD_DIM) < (HEAD_DIM // 2)
    rot = jnp.where(first_half,
                    pltpu.roll(x, width - HEAD_DIM // 2, 1),
                    pltpu.roll(x, HEAD_DIM // 2, 1))
    return x * c + rot * s


def _ones_row_64(vt):
    ones_row = lax.broadcasted_iota(jnp.int32, vt.shape, 0) % LANES == HEAD_DIM
    return jnp.where(ones_row, 1.0, vt)


def _swa_proj_kernel(x_ref, mod_ref, gain_ref, w_ref, wvt_ref, cos_ref, sin_ref, q_ref, k_ref, vt_ref):
    h = _norm_mod(x_ref[...], gain_ref[...], mod_ref[0:1, :], mod_ref[1:2, :]).astype(BF16)
    cos, sin = cos_ref[...], sin_ref[...]
    nq = q_ref.shape[1]
    q = _dot(h, w_ref[:, :nq])
    q_ref[...] = (_rope(q, cos, sin) * (HEAD_DIM ** -0.5 * LOG2E)).astype(BF16)
    k = _dot(h, w_ref[:, nq:])
    k_ref[...] = _rope(k, cos, sin).astype(BF16)
    vt_ref[...] = _ones_row_64(_dot_nt(wvt_ref[...], h)).astype(BF16)


def _swa_proj_call(x, mod, gain, w, wvt, cos, sin, ffn_weights, tm=1024):
    bsz, s, d = x.shape
    nq, nkv = D_MODEL, SWA_KV_HEADS * LANES
    row = lambda b, i: (b, i, 0)
    return _call_with_ffn_cast(
        _swa_proj_kernel,
        grid=(bsz, s // tm),
        in_specs=[
            pl.BlockSpec((None, tm, d), row),
            pl.BlockSpec((None, 6, d), lambda b, i: (b, 0, 0)),
            _const_spec((1, d)),
            _const_spec(w.shape),
            _const_spec(wvt.shape),
            pl.BlockSpec((tm, LANES), lambda b, i: (i, 0)),
            pl.BlockSpec((tm, LANES), lambda b, i: (i, 0)),
        ],
        out_specs=[
            pl.BlockSpec((None, tm, nq), row),
            pl.BlockSpec((None, tm, nkv), row),
            pl.BlockSpec((None, nkv, tm), lambda b, i: (b, 0, i)),
        ],
        out_shape=[
            jax.ShapeDtypeStruct((bsz, s, nq), BF16),
            jax.ShapeDtypeStruct((bsz, s, nkv), BF16),
            jax.ShapeDtypeStruct((bsz, nkv, s), BF16),
        ],
        scratch_shapes=[], name="swa_proj", args=(x, mod, gain, w, wvt, cos, sin), ffn_weights=ffn_weights)


def _gla_proj_kernel(x_ref, mod_ref, gain_ref, w_ref, wa_ref, wg_ref, bg_ref,
                     q_ref, k_ref, v_ref, r_ref, la_ref):
    h = _norm_mod(x_ref[...], gain_ref[...], mod_ref[0:1, :], mod_ref[1:2, :]).astype(BF16)
    nk = q_ref.shape[1]
    nv = v_ref.shape[1]
    q_ref[...] = _dot(h, w_ref[:, :nk]).astype(BF16)
    k_ref[...] = _dot(h, w_ref[:, nk:2 * nk]).astype(BF16)
    v_ref[...] = _dot(h, w_ref[:, 2 * nk:2 * nk + nv]).astype(BF16)
    r_ref[...] = _dot(h, w_ref[:, 2 * nk + nv:]).astype(BF16)
    a_low = _dot(h, wa_ref[...]).astype(BF16)
    z = _dot(a_low, wg_ref[...]) + bg_ref[...]
    la_ref[...] = _log_sigmoid(z) * (1.0 / GLA_TAU)


def _gla_proj_call(x, mod, gain, w, wa, wg, bg, ffn_weights, tm=1024):
    bsz, s, d = x.shape
    nk, nv = GLA_HEADS * GLA_DK, GLA_HEADS * GLA_DV
    row = lambda b, i: (b, i, 0)
    return _call_with_ffn_cast(
        _gla_proj_kernel,
        grid=(bsz, s // tm),
        in_specs=[
            pl.BlockSpec((None, tm, d), row),
            pl.BlockSpec((None, 6, d), lambda b, i: (b, 0, 0)),
            _const_spec((1, d)),
            _const_spec(w.shape),
            _const_spec(wa.shape),
            _const_spec(wg.shape),
            _const_spec(bg.shape),
        ],
        out_specs=[
            pl.BlockSpec((None, tm, nk), row),
            pl.BlockSpec((None, tm, nk), row),
            pl.BlockSpec((None, tm, nv), row),
            pl.BlockSpec((None, tm, nv), row),
            pl.BlockSpec((None, tm, nk), row),
        ],
        out_shape=[
            jax.ShapeDtypeStruct((bsz, s, nk), BF16),
            jax.ShapeDtypeStruct((bsz, s, nk), BF16),
            jax.ShapeDtypeStruct((bsz, s, nv), BF16),
            jax.ShapeDtypeStruct((bsz, s, nv), BF16),
            jax.ShapeDtypeStruct((bsz, s, nk), F32),
        ],
        scratch_shapes=[], name="gla_proj", args=(x, mod, gain, w, wa, wg, bg), ffn_weights=ffn_weights)


def _spread_heads(x, extra, out_ref):
    lane = lax.broadcasted_iota(jnp.int32, (x.shape[0], LANES), 1)
    low = lane < HEAD_DIM
    for p in range(x.shape[1] // LANES):
        xs = x[:, p * LANES:(p + 1) * LANES]
        ex = extra[:, p * LANES:(p + 1) * LANES]
        out_ref[:, (2 * p) * LANES:(2 * p + 1) * LANES] = jnp.where(low, xs, ex).astype(out_ref.dtype)
        odd = pltpu.roll(jnp.where(low, ex, xs), HEAD_DIM, 1)
        out_ref[:, (2 * p + 1) * LANES:(2 * p + 2) * LANES] = odd.astype(out_ref.dtype)


def _fox_proj_kernel(x_ref, mod_ref, gain_ref, wq_ref, wk_ref, wvt_ref, wf_ref, bf_ref,
                     pq_ref, pk_ref, hsel_ref, q_ref, k_ref, vt_ref, stats_ref, carry_ref):
    @pl.when(pl.program_id(1) == 0)
    def _():
        carry_ref[...] = jnp.zeros_like(carry_ref)

    h = _norm_mod(x_ref[...], gain_ref[...], mod_ref[0:1, :], mod_ref[1:2, :]).astype(BF16)
    log_f = _log_sigmoid(_dot(h, wf_ref[...]) + bf_ref[...])
    lc = _cumsum_rows(log_f) + carry_ref[...]
    carry_ref[...] = lc[lc.shape[0] - 1:, :]
    lc2 = lc * LOG2E
    hi, mid, lo = _split3(lc2)
    aug = jnp.concatenate([hi, mid, lo, jnp.ones_like(hi)], axis=1)
    qs = _dot(h, wq_ref[...]) * (HEAD_DIM ** -0.5 * LOG2E)
    ks = _dot(h, wk_ref[...])
    _spread_heads(qs, _dot(aug, pq_ref[...]), q_ref)
    _spread_heads(ks, _dot(aug, pk_ref[...]), k_ref)
    qn2 = _dot((qs * qs).astype(BF16), hsel_ref[...])
    kn2 = _dot((ks * ks).astype(BF16), hsel_ref[...])
    tm = lc2.shape[0]
    stats_ref[...] = jnp.concatenate([
        jnp.sqrt(jnp.max(qn2, axis=0, keepdims=True)),
        jnp.sqrt(jnp.max(kn2, axis=0, keepdims=True)),
        jnp.sqrt(jnp.max(qn2 * kn2, axis=0, keepdims=True)),
        lc2[0:1, :], lc2[tm - 1:tm, :],
        jnp.zeros((FOX_STATS - 5, lc2.shape[1]), F32)], axis=0)
    vt = _dot_nt(wvt_ref[...], h).astype(BF16)
    pad = jnp.where(lax.broadcasted_iota(jnp.int32, (HEAD_DIM, vt.shape[1]), 0) == 0,
                    1.0, 0.0).astype(BF16)
    for hd in range(FOX_HEADS):
        vt_ref[hd * LANES:hd * LANES + HEAD_DIM, :] = vt[hd * HEAD_DIM:(hd + 1) * HEAD_DIM, :]
        vt_ref[hd * LANES + HEAD_DIM:(hd + 1) * LANES, :] = pad


def _fox_proj_call(x, mod, gain, wq, wk, wvt, wf, bf, pq, pk, tm=512):
    bsz, s, d = x.shape
    nqk = FOX_HEADS * LANES
    row = lambda b, i: (b, i, 0)
    hsel = jnp.asarray(np.repeat(np.eye(FOX_HEADS, dtype=np.float32), HEAD_DIM, axis=0), BF16)
    return pl.pallas_call(
        _fox_proj_kernel,
        grid=(bsz, s // tm),
        in_specs=[
            pl.BlockSpec((None, tm, d), row),
            pl.BlockSpec((None, 6, d), lambda b, i: (b, 0, 0)),
            _const_spec((1, d)),
            _const_spec(wq.shape),
            _const_spec(wk.shape),
            _const_spec(wvt.shape),
            _const_spec(wf.shape),
            _const_spec(bf.shape),
            _const_spec(pq.shape),
            _const_spec(pk.shape),
            _const_spec(hsel.shape),
        ],
        out_specs=[
            pl.BlockSpec((None, tm, nqk), row),
            pl.BlockSpec((None, tm, nqk), row),
            pl.BlockSpec((None, None, nqk, tm), lambda b, i: (b, i, 0, 0)),
            pl.BlockSpec((None, None, FOX_STATS, FOX_HEADS), lambda b, i: (b, i, 0, 0)),
        ],
        out_shape=[
            jax.ShapeDtypeStruct((bsz, s, nqk), BF16),
            jax.ShapeDtypeStruct((bsz, s, nqk), BF16),
            jax.ShapeDtypeStruct((bsz, s // tm, nqk, tm), BF16),
            jax.ShapeDtypeStruct((bsz, s // tm, FOX_STATS, FOX_HEADS), F32),
        ],
        scratch_shapes=[pltpu.VMEM((1, FOX_HEADS), F32)],
        compiler_params=_params("arbitrary", "arbitrary"),
        name="fox_proj",
    )(x, mod, gain, wq, wk, wvt, wf, bf, pq, pk, hsel)


FFN_CAST_BLOCKS = 16


def _call_with_ffn_cast(kernel_fn, grid, in_specs, out_specs, out_shape, scratch_shapes, name, args,
                        ffn_weights):
    params = _params(*["arbitrary"] * len(grid))
    if ffn_weights is None:
        return pl.pallas_call(kernel_fn, grid=grid, in_specs=in_specs, out_specs=out_specs,
                              out_shape=out_shape, scratch_shapes=scratch_shapes,
                              compiler_params=params, name=name)(*args)
    wgu_stack, wd_stack, layer = ffn_weights
    n_in, n_out = len(in_specs), len(out_specs)
    steps = math.prod(grid)
    blocks = math.gcd(steps, FFN_CAST_BLOCKS)
    per = steps // blocks

    def linear(*ids):
        idx = ids[0]
        for extent, i in zip(grid[1:], ids[1:]):
            idx = idx * extent + i
        return idx

    def body(*refs):
        ins, (gu32, d32) = refs[:n_in], refs[n_in:n_in + 2]
        outs = refs[n_in + 2:n_in + 2 + n_out]
        (gu16, d16), scratch = refs[n_in + 2 + n_out:n_in + 4 + n_out], refs[n_in + 4 + n_out:]

        def cast():
            gu16[...] = gu32[...].astype(BF16)
            d16[...] = d32[...].astype(BF16)

        if per == 1:
            cast()
        else:
            pl.when(linear(*[pl.program_id(a) for a in range(len(grid))]) % per == 0)(cast)
        kernel_fn(*ins, *outs, *scratch)

    def weight_specs(stack):
        rows, cols = stack.shape[1] // blocks, stack.shape[2]
        return (pl.BlockSpec((None, rows, cols), lambda *ids: (layer, linear(*ids) // per, 0)),
                pl.BlockSpec((rows, cols), lambda *ids: (linear(*ids) // per, 0)),
                jax.ShapeDtypeStruct(stack.shape[1:], BF16))

    gu_in, gu_out, gu_shape = weight_specs(wgu_stack)
    d_in, d_out, d_shape = weight_specs(wd_stack)
    return pl.pallas_call(
        body,
        grid=grid,
        in_specs=list(in_specs) + [gu_in, d_in],
        out_specs=list(out_specs) + [gu_out, d_out],
        out_shape=list(out_shape) + [gu_shape, d_shape],
        scratch_shapes=scratch_shapes,
        compiler_params=params,
        name=name,
    )(*args, wgu_stack, wd_stack)


def _swa_band_bias():
    w, group = SWA_WINDOW, SWA_Q_HEADS // SWA_KV_HEADS
    key = np.arange(2 * w)[:, None]
    qry = np.arange(group * w)[None, :] % w
    dist = (w + qry) - key
    band = (dist >= 0) & (dist < w)
    allowed = np.stack([band & (key >= w), band])
    return jnp.asarray(np.where(allowed, 0.0, NEG_BIG), F32)


def _swa_attn_kernel(sink_ref, bias_ref, q_ref, kc_ref, vtc_ref, o_ref, st_ref, kp_ref, vtp_ref):
    w = SWA_WINDOW
    group = SWA_Q_HEADS // SWA_KV_HEADS
    nq = group * w
    nb = q_ref.shape[0] // w
    lane = lax.broadcasted_iota(jnp.int32, (w, LANES), 1)
    low = lane < HEAD_DIM

    @pl.when(pl.program_id(1) == 0)
    def _():
        kp_ref[...] = jnp.zeros_like(kp_ref)
        vtp_ref[...] = jnp.zeros_like(vtp_ref)

    def scores(u, g, slot):
        rows = slice(u * w, (u + 1) * w)
        cols = slice(g * LANES, (g + 1) * LANES)
        slabs = []
        for hh in range(group):
            head = g * group + hh
            qs = q_ref[rows, (head // 2) * LANES:(head // 2 + 1) * LANES]
            keep = low if head % 2 == 0 else jnp.logical_not(low)
            slabs.append(jnp.where(keep, qs, jnp.zeros_like(qs)))
        q_stack = jnp.concatenate(slabs, axis=0)
        k_prev = kp_ref[:, cols] if u == 0 else kc_ref[(u - 1) * w:u * w, cols]
        k_both = jnp.concatenate([k_prev, kc_ref[rows, cols]], axis=0)
        st_ref[slot] = _dot_nt(k_both, q_stack)

    def finish(u, g, slot):
        rows = slice(u * w, (u + 1) * w)
        cols = slice(g * LANES, (g + 1) * LANES)
        bias = bias_ref[jnp.minimum(pl.program_id(1), 1)] if u == 0 else bias_ref[1]
        st = st_ref[slot] + bias
        sink = sink_ref[:, g * nq:(g + 1) * nq] * LOG2E
        m = jnp.maximum(jnp.max(st, axis=0, keepdims=True), sink)
        pt = jnp.exp2(st - m).astype(BF16)
        vt_prev = vtp_ref[cols, :] if u == 0 else vtc_ref[cols, (u - 1) * w:u * w]
        vt_both = jnp.concatenate([vt_prev, vtc_ref[cols, rows]], axis=1)
        acc = _dot(vt_both, pt)
        den = acc[HEAD_DIM:HEAD_DIM + 1, :] + jnp.exp2(sink - m)
        ot = acc[:HEAD_DIM, :] / den
        o_t = jnp.concatenate([ot[:, hh * w:(hh + 1) * w] for hh in range(group)], axis=0)
        o_ref[rows, g * group * HEAD_DIM:(g + 1) * group * HEAD_DIM] = o_t.T.astype(BF16)

    items = [(u, g) for u in range(nb) for g in range(SWA_KV_HEADS)]
    slots = st_ref.shape[0]
    ahead = slots - 1
    for n in range(min(ahead, len(items))):
        scores(*items[n], n % slots)
    for n, item in enumerate(items):
        if n + ahead < len(items):
            scores(*items[n + ahead], (n + ahead) % slots)
        finish(*item, n % slots)
    kp_ref[...] = kc_ref[(nb - 1) * w:, :]
    vtp_ref[...] = vtc_ref[:, (nb - 1) * w:]


def _swa_attn_call(q, k, vt, sinks, ffn_weights, nb=4):
    bsz, s, d = q.shape
    w = SWA_WINDOW
    nkv = k.shape[2]
    sink_row = jnp.repeat(sinks, w).reshape(1, -1)
    bias = _swa_band_bias()
    cur = lambda b, i: (b, i, 0)
    return _call_with_ffn_cast(
        _swa_attn_kernel,
        grid=(bsz, s // (nb * w)),
        in_specs=[
            _const_spec(sink_row.shape),
            _const_spec(bias.shape),
            pl.BlockSpec((None, nb * w, d), cur),
            pl.BlockSpec((None, nb * w, nkv), cur),
            pl.BlockSpec((None, nkv, nb * w), lambda b, i: (b, 0, i)),
        ],
        out_specs=[pl.BlockSpec((None, nb * w, d), cur)],
        out_shape=[jax.ShapeDtypeStruct((bsz, s, d), BF16)],
        scratch_shapes=[
            pltpu.VMEM((SWA_SCORE_SLOTS, 2 * w, (SWA_Q_HEADS // SWA_KV_HEADS) * w), F32),
            pltpu.VMEM((w, nkv), BF16),
            pltpu.VMEM((nkv, w), BF16),
        ],
        name="swa_attn", args=(sink_row, bias, q, k, vt), ffn_weights=ffn_weights)


def _chunk_cumsum(x):
    n = x.shape[0]
    row = lax.broadcasted_iota(jnp.int32, (n, n), 0)
    col = lax.broadcasted_iota(jnp.int32, (n, n), 1)
    same_chunk = (row // GLA_CHUNK) == (col // GLA_CHUNK)
    tril = jnp.where(jnp.logical_and(row >= col, same_chunk), 1.0, 0.0).astype(BF16)
    hi, mid, lo = _split3(x)
    return _dot(tril, hi) + _dot(tril, mid) + _dot(tril, lo)


def _gla_intra(q, k, b2, k_rows, b_rows, base):
    c, sub = GLA_CHUNK, GLA_SUB
    col = lax.broadcasted_iota(jnp.int32, (sub, c), 1)
    row = lax.broadcasted_iota(jnp.int32, (sub, c), 0)
    blocks = []
    for i in range(c // sub):
        lo = i * sub
        q_i = q[lo:lo + sub, :]
        b_i = b2[lo:lo + sub, :]
        if i == 0:
            a = jnp.zeros((sub, c), F32)
        else:
            ref = b2[lo - 1:lo, :]
            n = -(-lo // BF16_ROWS) * BF16_ROWS
            q_t = (q_i * jnp.exp2(b_i - ref)).astype(BF16)
            k_t = (k[:n, :] * jnp.exp2(jnp.minimum(ref - b2[:n, :], 0.0))).astype(BF16)
            if n < c:
                k_t = jnp.concatenate([k_t, jnp.zeros((c - n, k_t.shape[1]), BF16)], axis=0)
            a = _dot_nt(q_t, k_t)
        for s in range(lo, lo + sub):
            w = jnp.exp2(b_i - b_rows[base + s:base + s + 1, :])
            val = jnp.sum(q_i * k_rows[base + s:base + s + 1, :] * w, axis=1, keepdims=True)
            a = jnp.where(col == s, val, a)
        blocks.append(jnp.where(row + lo >= col, a, 0.0))
    return jnp.concatenate(blocks, axis=0)


def _gla_kernel(q_ref, k_ref, v_ref, r_ref, la_ref, hn_ref, o_ref, state_ref, krow_ref, brow_ref):
    @pl.when(pl.program_id(2) == 0)
    def _():
        state_ref[...] = jnp.zeros_like(state_ref)

    for hh in range(state_ref.shape[0]):
        kq = slice(hh * GLA_DK, (hh + 1) * GLA_DK)
        vv = slice(hh * GLA_DV, (hh + 1) * GLA_DV)
        _gla_head(q_ref.at[:, kq], k_ref.at[:, kq], v_ref.at[:, vv], r_ref.at[:, vv], la_ref.at[:, kq],
                  hn_ref, o_ref.at[:, vv], state_ref.at[hh], krow_ref.at[hh], brow_ref.at[hh])


def _gla_head(q_ref, k_ref, v_ref, r_ref, la_ref, hn_ref, o_ref, state_ref, krow_ref, brow_ref):
    c = GLA_CHUNK
    nc = q_ref.shape[0] // c
    rows = [slice(ci * c, (ci + 1) * c) for ci in range(nc)]
    b2_all = _chunk_cumsum(la_ref[...]) * LOG2E
    q_all = q_ref[...].astype(F32) * (GLA_DK ** -0.5)
    k_all = k_ref[...].astype(F32)
    krow_ref[...] = k_all
    brow_ref[...] = b2_all
    b2 = [b2_all[r] for r in rows]
    q = [q_all[r] for r in rows]
    k = [k_all[r] for r in rows]
    last = [b[c - 1:c, :] for b in b2]
    q_in = [(q[i] * jnp.exp2(b2[i])).astype(BF16) for i in range(nc)]
    k_out = [(k[i] * jnp.exp2(last[i] - b2[i])).astype(BF16) for i in range(nc)]
    kv = [_dot_tn(v_ref[rows[i], :], k_out[i]) for i in range(nc)]
    attn = [_gla_intra(q[i], k[i], b2[i], krow_ref, brow_ref, i * c).astype(BF16) for i in range(nc)]
    intra = [_dot(attn[i], v_ref[rows[i], :]) for i in range(nc)]
    state_t = state_ref[...]
    for i in range(nc):
        o = intra[i] + _dot_nt(q_in[i], state_t.astype(BF16))
        state_t = state_t * jnp.exp2(last[i]) + kv[i]
        r = r_ref[rows[i], :].astype(F32)
        o_ref[rows[i], :] = (_rms(o, hn_ref[...]) * _silu(r)).astype(BF16)
    state_ref[...] = state_t


def _gla_call(q, k, v, r, la, head_norm, ffn_weights, tm=256, nh=2):
    bsz, s, _ = q.shape
    dk, dv = GLA_DK, GLA_DV
    blk = lambda b, h, i: (b, i, h)
    return _call_with_ffn_cast(
        _gla_kernel,
        grid=(bsz, GLA_HEADS // nh, s // tm),
        in_specs=[
            pl.BlockSpec((None, tm, nh * dk), blk),
            pl.BlockSpec((None, tm, nh * dk), blk),
            pl.BlockSpec((None, tm, nh * dv), blk),
            pl.BlockSpec((None, tm, nh * dv), blk),
            pl.BlockSpec((None, tm, nh * dk), blk),
            _const_spec((1, dv)),
        ],
        out_specs=[pl.BlockSpec((None, tm, nh * dv), blk)],
        out_shape=[jax.ShapeDtypeStruct((bsz, s, GLA_HEADS * dv), BF16)],
        scratch_shapes=[pltpu.VMEM((nh, dv, dk), F32), pltpu.VMEM((nh, tm, dk), F32),
                        pltpu.VMEM((nh, tm, dk), F32)],
        name="gla_mix", args=(q, k, v, r, la, head_norm), ffn_weights=ffn_weights)


def _fox_first_live_block(stats_ref, b, first_head, nh, i):
    heads = [first_head + e for e in range(nh)]
    q_max = [FOX_BOUND_SLACK * stats_ref[b, i, Q_MAX, hd] for hd in heads]
    margin = [stats_ref[b, i, LC_FIRST, hd] + FOX_BOUND_SLACK * stats_ref[b, i, QK_MAX, hd]
              - FOX_DEAD_LOG2 for hd in heads]

    def dead(j):
        is_dead = True
        for e, hd in enumerate(heads):
            gap = q_max[e] * stats_ref[b, j, K_MAX, hd] - stats_ref[b, j, LC_LAST, hd] + margin[e]
            is_dead = jnp.logical_and(is_dead, gap <= 0.0)
        return is_dead

    last = jnp.maximum(i - 1, 0)
    return lax.while_loop(lambda j: jnp.logical_and(j < i, dead(jnp.minimum(j, last))),
                          lambda j: j + 1, jnp.int32(0))


def _fox_attn_kernel(stats_ref, q_ref, k_ref, vt_ref, o_ref, m_ref, acc_ref, sa_ref, sb_ref):
    tk = vt_ref.shape[2]
    nh = m_ref.shape[0]
    i = pl.program_id(2)
    j0 = _fox_first_live_block(stats_ref, pl.program_id(0), pl.program_id(1) * nh, nh, i)
    m_ref[...] = jnp.full_like(m_ref, NEG_BIG)
    acc_ref[...] = jnp.zeros_like(acc_ref)

    half = tk // 2

    def scores(j, buf, e, diag=False):
        lanes = slice(e * LANES, (e + 1) * LANES)
        row0 = pl.multiple_of(j * tk, tk)
        if not diag:
            buf[e] = _dot_nt(k_ref[pl.ds(row0, tk), lanes], q_ref[:, lanes])
        else:
            buf[e, :half, :] = _dot_nt(k_ref[pl.ds(row0, half), lanes], q_ref[:, lanes])
            buf[e, half:, half:] = _dot_nt(k_ref[pl.ds(row0 + half, half), lanes], q_ref[half:, lanes])

    def online_update(e, st, vt, cols):
        m_old = m_ref[e, :, cols]
        m_new = jnp.maximum(m_old, jnp.max(st, axis=0, keepdims=True))
        alpha = jnp.exp2(m_old - m_new)
        pt = jnp.exp2(st - m_new).astype(BF16)
        acc_ref[e, :, cols] = alpha * acc_ref[e, :, cols] + _dot(vt, pt)
        m_ref[e, :, cols] = m_new

    def accumulate(j, buf, e, masked):
        rows = slice(e * LANES, (e + 1) * LANES)
        if not masked:
            online_update(e, buf[e], vt_ref[j, rows, :], slice(None))
            return
        key = lax.broadcasted_iota(jnp.int32, (half, half), 0)
        qry = lax.broadcasted_iota(jnp.int32, (half, half), 1)
        causal = key <= qry
        st = jnp.concatenate([jnp.where(causal, buf[e, :half, :half], NEG_BIG), buf[e, :half, half:]], axis=1)
        online_update(e, st, vt_ref[j, rows, :half], slice(None))
        online_update(e, jnp.where(causal, buf[e, half:, half:], NEG_BIG), vt_ref[j, rows, half:],
                      slice(half, None))

    def block(j, buf, masked, following):
        for e in range(nh):
            if e + 1 < nh:
                scores(j, buf, e + 1, masked)
            elif following is not None:
                scores(following[0], following[1], 0, following[2])
            accumulate(j, buf, e, masked)

    scores(j0, sa_ref, 0)
    n_full = i - j0

    def body(t, carry):
        j = j0 + 2 * t
        block(j, sa_ref, False, (j + 1, sb_ref, False))
        block(j + 1, sb_ref, False, (j + 2, sa_ref, False))
        return carry

    lax.fori_loop(0, n_full // 2, body, 0)

    @pl.when(n_full % 2 == 0)
    def _():
        block(i, sa_ref, True, None)

    @pl.when(n_full % 2 == 1)
    def _():
        block(i - 1, sa_ref, False, (i, sb_ref, True))
        block(i, sb_ref, True, None)

    outs = []
    for e in range(nh):
        acc = acc_ref[e]
        outs.append(acc[:HEAD_DIM, :] / acc[HEAD_DIM:HEAD_DIM + 1, :])
    o_ref[...] = jnp.concatenate(outs, axis=0).T.astype(BF16)


def _fox_attn_call(q, k, vt, stats, ffn_weights, nh=4):
    bsz, s, _ = q.shape
    tk = vt.shape[3]
    tq = tk
    return _call_with_ffn_cast(
        _fox_attn_kernel,
        grid=(bsz, FOX_HEADS // nh, s // tq),
        in_specs=[
            pl.BlockSpec(memory_space=pltpu.SMEM),
            pl.BlockSpec((None, tq, nh * LANES), lambda b, p, i: (b, i, p)),
            pl.BlockSpec((None, s, nh * LANES), lambda b, p, i: (b, 0, p)),
            pl.BlockSpec((None, s // tk, nh * LANES, tk), lambda b, p, i: (b, 0, p, 0)),
        ],
        out_specs=[pl.BlockSpec((None, tq, nh * HEAD_DIM), lambda b, p, i: (b, i, p))],
        out_shape=[jax.ShapeDtypeStruct((bsz, s, D_MODEL), BF16)],
        scratch_shapes=[
            pltpu.VMEM((nh, 1, tq), F32),
            pltpu.VMEM((nh, LANES, tq), F32),
            pltpu.VMEM((nh, tk, tq), F32),
            pltpu.VMEM((nh, tk, tq), F32),
        ],
        name="fox_attn", args=(stats, q, k, vt), ffn_weights=ffn_weights)


def _post_kernel(x_ref, o_ref, mod_ref, gain_ref, wo_ref, wgu_ref, wd_ref, fn_ref, out_ref,
                 *, ff_chunk, final):
    x1 = x_ref[...] + mod_ref[2:3, :] * _dot(o_ref[...], wo_ref[...])
    h = _norm_mod(x1, gain_ref[...], mod_ref[3:4, :], mod_ref[4:5, :]).astype(BF16)
    acc = jnp.zeros(x1.shape, F32)
    for c0 in range(0, D_FF, ff_chunk):
        g = _dot(h, wgu_ref[:, c0:c0 + ff_chunk])
        u = _dot(h, wgu_ref[:, D_FF + c0:D_FF + c0 + ff_chunk])
        acc = acc + _dot((_silu(g) * u).astype(BF16), wd_ref[c0:c0 + ff_chunk, :])
    x2 = x1 + mod_ref[5:6, :] * acc
    if final:
        x2 = _rms(x2, fn_ref[...])
    out_ref[...] = x2


def _post_call(x, o, mod, gain, wo, wgu, wd, final_norm, final, tm=1024, ff_chunk=256):
    bsz, s, d = x.shape
    row = lambda b, i: (b, i, 0)
    return pl.pallas_call(
        functools.partial(_post_kernel, ff_chunk=ff_chunk, final=final),
        grid=(bsz, s // tm),
        in_specs=[
            pl.BlockSpec((None, tm, d), row),
            pl.BlockSpec((None, tm, d), row),
            pl.BlockSpec((None, 6, d), lambda b, i: (b, 0, 0)),
            _const_spec((1, d)),
            _const_spec(wo.shape),
            _const_spec(wgu.shape),
            _const_spec(wd.shape),
            _const_spec((1, d)),
        ],
        out_specs=pl.BlockSpec((None, tm, d), row),
        out_shape=jax.ShapeDtypeStruct((bsz, s, d), F32),
        compiler_params=_params("arbitrary", "arbitrary"),
        name="post_ffn",
    )(x, o, mod, gain, wo, wgu, wd, final_norm)


def _rope_tables(s):
    half = HEAD_DIM // 2
    inv = 1.0 / (ROPE_THETA ** (jnp.arange(0, HEAD_DIM, 2, dtype=F32) / HEAD_DIM))
    ang = jnp.arange(s, dtype=F32)[:, None] * inv[None, :]
    cos, sin = jnp.cos(ang), jnp.sin(ang)
    reps = LANES // HEAD_DIM
    cos_t = jnp.tile(jnp.concatenate([cos, cos], axis=1), (1, reps))
    sin_t = jnp.tile(jnp.concatenate([-sin, sin], axis=1), (1, reps))
    assert half * 2 == HEAD_DIM
    return cos_t, sin_t


def _dup_heads(w, heads):
    w3 = w.reshape(w.shape[0], heads, HEAD_DIM)
    return jnp.concatenate([w3, w3], axis=2).reshape(w.shape[0], heads * LANES)


def _pad_heads(w, heads):
    w3 = w.reshape(w.shape[0], heads, HEAD_DIM)
    return jnp.concatenate([w3, jnp.zeros_like(w3)], axis=2).reshape(w.shape[0], heads * LANES)


def _fox_placement():
    h = FOX_HEADS
    pq = np.zeros((4 * h, h * HEAD_DIM), np.float32)
    pk = np.zeros((4 * h, h * HEAD_DIM), np.float32)
    for head in range(h):
        base = (head // 2) * LANES + (HEAD_DIM if head % 2 == 0 else 0)
        for part in range(3):
            pq[part * h + head, base + part] = 1.0
            pk[3 * h + head, base + part] = 1.0
            pq[3 * h + head, base + 3 + part] = 1.0
            pk[part * h + head, base + 3 + part] = -1.0
    return jnp.asarray(pq, BF16), jnp.asarray(pk, BF16)


def kernel(x, c, ada_w, ada_b, norm_gain, ffn_w_gu, ffn_w_down, swa_w_in, swa_sinks, swa_w_o,
           gla_w_in, gla_w_gate_up, gla_b_gate, gla_head_norm, gla_w_o, fox_w_in, fox_b_f, fox_w_o,
           final_norm):
    bsz, s, d = x.shape
    depth = ada_w.shape[0]
    mod_all = _ada_call(c, ada_w, ada_b).reshape(depth, bsz, 6, d)
    cos_t, sin_t = _rope_tables(s)
    pq, pk = _fox_placement()
    fn = final_norm.reshape(1, d)

    for i in range(depth):
        kind, j = i % N_MIXERS, i // N_MIXERS
        mod = mod_all[i]
        gain1 = norm_gain[i, 0].reshape(1, d)
        gain2 = norm_gain[i, 1].reshape(1, d)
        ffn_weights = (ffn_w_gu, ffn_w_down, i)
        if kind == 0:
            w = swa_w_in[j]
            nq, nkv = SWA_Q_HEADS * HEAD_DIM, SWA_KV_HEADS * HEAD_DIM
            w_all = jnp.concatenate([w[:, :nq], _dup_heads(w[:, nq:nq + nkv], SWA_KV_HEADS)],
                                    axis=1).astype(BF16)
            wvt = _pad_heads(w[:, nq + nkv:], SWA_KV_HEADS).T.astype(BF16)
            q, k, v, wgu, wd = _swa_proj_call(x, mod, gain1, w_all, wvt, cos_t, sin_t, ffn_weights)
            o, = _swa_attn_call(q, k, v, swa_sinks[j], None)
            wo = swa_w_o[j]
        elif kind == 1:
            w = gla_w_in[j]
            n_main = 2 * GLA_HEADS * GLA_DK + 2 * GLA_HEADS * GLA_DV
            q, k, v, r, la, wgu, wd = _gla_proj_call(
                x, mod, gain1, w[:, :n_main].astype(BF16), w[:, n_main:].astype(BF16),
                gla_w_gate_up[j].astype(BF16), gla_b_gate[j].reshape(1, -1), ffn_weights)
            o, = _gla_call(q, k, v, r, la, gla_head_norm[j].reshape(1, -1), None)
            wo = gla_w_o[j]
        else:
            w = fox_w_in[j]
            order = jnp.argsort(fox_b_f[j])
            by_head = lambda m: jnp.take(m.reshape(d, FOX_HEADS, HEAD_DIM), order, axis=1).reshape(d, d)
            q, k, v, stats = _fox_proj_call(
                x, mod, gain1,
                by_head(w[:, :d]).astype(BF16), by_head(w[:, d:2 * d]).astype(BF16),
                by_head(w[:, 2 * d:3 * d]).T.astype(BF16), jnp.take(w[:, 3 * d:], order, axis=1).astype(BF16),
                jnp.take(fox_b_f[j], order).reshape(1, -1), pq, pk)
            o, wgu, wd = _fox_attn_call(q, k, v, stats, ffn_weights)
            wo = jnp.take(fox_w_o[j].reshape(FOX_HEADS, HEAD_DIM, d), order, axis=0).reshape(d, d)
        x = _post_call(x, o, mod, gain2, wo.astype(BF16), wgu, wd, fn, final=(i == depth - 1))
    return x
```

```python
import functools
import math

import numpy as np
import jax
import jax.numpy as jnp
from jax import lax
from jax.experimental import pallas as pl
from jax.experimental.pallas import tpu as pltpu

D_MODEL = 1024
HEAD_DIM = 64
RMS_EPS = 1e-6
SWA_Q_HEADS = 16
SWA_KV_HEADS = 4
SWA_WINDOW = 128
SWA_SCORE_SLOTS = 4
ROPE_THETA = 150000.0
GLA_HEADS = 4
GLA_DK = 128
GLA_DV = 256
GLA_RANK = 16
GLA_TAU = 16.0
GLA_CHUNK = 64
GLA_SUB = 8
FOX_HEADS = 16
FOX_STATS = 8
Q_MAX, K_MAX, QK_MAX, LC_FIRST, LC_LAST = range(5)
FOX_DEAD_LOG2 = -160.0
FOX_BOUND_SLACK = 1.02
D_FF = 2816
N_MIXERS = 3

LANES = 128
BF16_ROWS = 16
NEG_BIG = -1e30
LOG2E = 1.4426950408889634
VMEM_LIMIT = 56 * 1024 * 1024

BF16 = jnp.bfloat16
F32 = jnp.float32


def _dot(a, b):
    return jnp.dot(a, b, preferred_element_type=F32)


def _dot_nt(a, b):
    return lax.dot_general(a, b, (((1,), (1,)), ((), ())), preferred_element_type=F32)


def _dot_tn(a, b):
    return lax.dot_general(a, b, (((0,), (0,)), ((), ())), preferred_element_type=F32)


def _split3(x):
    hi = x.astype(BF16)
    r1 = x - hi.astype(F32)
    mid = r1.astype(BF16)
    lo = (r1 - mid.astype(F32)).astype(BF16)
    return hi, mid, lo


def _cumsum_rows(x):
    n = x.shape[0]
    row = lax.broadcasted_iota(jnp.int32, (n, n), 0)
    col = lax.broadcasted_iota(jnp.int32, (n, n), 1)
    tril = jnp.where(row >= col, 1.0, 0.0).astype(BF16)
    w = x.shape[1]
    sums = _dot(tril, jnp.concatenate(_split3(x), axis=1))
    return sums[:, :w] + sums[:, w:2 * w] + sums[:, 2 * w:]


def _log_sigmoid(x):
    return jnp.minimum(x, 0.0) - jnp.log(1.0 + jnp.exp(-jnp.abs(x)))


def _silu(x):
    return x * (1.0 / (1.0 + jnp.exp(-x)))


def _rms(x, gain):
    ms = jnp.mean(x * x, axis=-1, keepdims=True)
    return x * lax.rsqrt(ms + RMS_EPS) * gain


def _norm_mod(x, gain, shift, scale):
    return _rms(x, gain) * (1.0 + scale) + shift


def _params(*sem):
    return pltpu.CompilerParams(dimension_semantics=sem, vmem_limit_bytes=VMEM_LIMIT)


def _const_spec(shape):
    nd = len(shape)
    return pl.BlockSpec(shape, lambda *_: (0,) * nd, pipeline_mode=pl.Buffered(1))


def _ada_kernel(ct_ref, w_ref, b_ref, out_ref):
    ca = _silu(ct_ref[...])
    w = w_ref[...]
    for b in range(ct_ref.shape[1]):
        col = ca[:, b:b + 1]
        out_ref[b:b + 1, :] = jnp.sum(col * w, axis=0, keepdims=True) + b_ref[...]


def _ada_call(c, ada_w, ada_b):
    depth, d, n = ada_w.shape
    bsz = c.shape[0]
    tn = 1536
    return pl.pallas_call(
        _ada_kernel,
        grid=(depth, n // tn),
        in_specs=[
            pl.BlockSpec((d, bsz), lambda l, j: (0, 0)),
            pl.BlockSpec((None, d, tn), lambda l, j: (l, 0, j)),
            pl.BlockSpec((None, 1, tn), lambda l, j: (l, 0, j)),
        ],
        out_specs=pl.BlockSpec((None, bsz, tn), lambda l, j: (l, 0, j)),
        out_shape=jax.ShapeDtypeStruct((depth, bsz, n), F32),
        compiler_params=_params("arbitrary", "arbitrary"),
        name="ada_mod",
    )(c.T, ada_w, ada_b.reshape(depth, 1, n))


def _rope(x, cos, sin_signed):
    width = x.shape[1]
    reps = width // cos.shape[1]
    c = jnp.tile(cos, (1, reps))
    s = jnp.tile(sin_signed, (1, reps))
    lane = lax.broadcasted_iota(jnp.int32, x.shape, 1)
    first_half = (lane % HEAD_DIM) < (HEAD_DIM // 2)
    rot = jnp.where(first_half,
                    pltpu.roll(x, width - HEAD_DIM // 2, 1),
                    pltpu.roll(x, HEAD_DIM // 2, 1))
    return x * c + rot * s


def _ones_row_64(vt):
    ones_row = lax.broadcasted_iota(jnp.int32, vt.shape, 0) % LANES == HEAD_DIM
    return jnp.where(ones_row, 1.0, vt)


def _swa_proj_kernel(x_ref, mod_ref, gain_ref, w_ref, wvt_ref, cos_ref, sin_ref, q_ref, k_ref, vt_ref):
    h = _norm_mod(x_ref[...], gain_ref[...], mod_ref[0:1, :], mod_ref[1:2, :]).astype(BF16)
    cos, sin = cos_ref[...], sin_ref[...]
    nq = q_ref.shape[1]
    q = _dot(h, w_ref[:, :nq])
    q_ref[...] = (_rope(q, cos, sin) * (HEAD_DIM ** -0.5 * LOG2E)).astype(BF16)
    k = _dot(h, w_ref[:, nq:])
    k_ref[...] = _rope(k, cos, sin).astype(BF16)
    vt_ref[...] = _ones_row_64(_dot_nt(wvt_ref[...], h)).astype(BF16)


def _swa_proj_call(x, mod, gain, w, wvt, cos, sin, ffn_weights, tm=1024):
    bsz, s, d = x.shape
    nq, nkv = D_MODEL, SWA_KV_HEADS * LANES
    row = lambda b, i: (b, i, 0)
    return _call_with_ffn_cast(
        _swa_proj_kernel,
        grid=(bsz, s // tm),
        in_specs=[
            pl.BlockSpec((None, tm, d), row),
            pl.BlockSpec((None, 6, d), lambda b, i: (b, 0, 0)),
            _const_spec((1, d)),
            _const_spec(w.shape),
            _const_spec(wvt.shape),
            pl.BlockSpec((tm, LANES), lambda b, i: (i, 0)),
            pl.BlockSpec((tm, LANES), lambda b, i: (i, 0)),
        ],
        out_specs=[
            pl.BlockSpec((None, tm, nq), row),
            pl.BlockSpec((None, tm, nkv), row),
            pl.BlockSpec((None, nkv, tm), lambda b, i: (b, 0, i)),
        ],
        out_shape=[
            jax.ShapeDtypeStruct((bsz, s, nq), BF16),
            jax.ShapeDtypeStruct((bsz, s, nkv), BF16),
            jax.ShapeDtypeStruct((bsz, nkv, s), BF16),
        ],
        scratch_shapes=[], name="swa_proj", args=(x, mod, gain, w, wvt, cos, sin), ffn_weights=ffn_weights)


def _gla_proj_kernel(x_ref, mod_ref, gain_ref, w_ref, wa_ref, wg_ref, bg_ref,
                     q_ref, k_ref, v_ref, r_ref, la_ref):
    h = _norm_mod(x_ref[...], gain_ref[...], mod_ref[0:1, :], mod_ref[1:2, :]).astype(BF16)
    nk = q_ref.shape[1]
    nv = v_ref.shape[1]
    q_ref[...] = _dot(h, w_ref[:, :nk]).astype(BF16)
    k_ref[...] = _dot(h, w_ref[:, nk:2 * nk]).astype(BF16)
    v_ref[...] = _dot(h, w_ref[:, 2 * nk:2 * nk + nv]).astype(BF16)
    r_ref[...] = _dot(h, w_ref[:, 2 * nk + nv:]).astype(BF16)
    a_low = _dot(h, wa_ref[...]).astype(BF16)
    z = _dot(a_low, wg_ref[...]) + bg_ref[...]
    la_ref[...] = _log_sigmoid(z) * (1.0 / GLA_TAU)


def _gla_proj_call(x, mod, gain, w, wa, wg, bg, ffn_weights, tm=1024):
    bsz, s, d = x.shape
    nk, nv = GLA_HEADS * GLA_DK, GLA_HEADS * GLA_DV
    row = lambda b, i: (b, i, 0)
    return _call_with_ffn_cast(
        _gla_proj_kernel,
        grid=(bsz, s // tm),
        in_specs=[
            pl.BlockSpec((None, tm, d), row),
            pl.BlockSpec((None, 6, d), lambda b, i: (b, 0, 0)),
            _const_spec((1, d)),
            _const_spec(w.shape),
            _const_spec(wa.shape),
            _const_spec(wg.shape),
            _const_spec(bg.shape),
        ],
        out_specs=[
            pl.BlockSpec((None, tm, nk), row),
            pl.BlockSpec((None, tm, nk), row),
            pl.BlockSpec((None, tm, nv), row),
            pl.BlockSpec((None, tm, nv), row),
            pl.BlockSpec((None, tm, nk), row),
        ],
        out_shape=[
            jax.ShapeDtypeStruct((bsz, s, nk), BF16),
            jax.ShapeDtypeStruct((bsz, s, nk), BF16),
            jax.ShapeDtypeStruct((bsz, s, nv), BF16),
            jax.ShapeDtypeStruct((bsz, s, nv), BF16),
            jax.ShapeDtypeStruct((bsz, s, nk), F32),
        ],
        scratch_shapes=[], name="gla_proj", args=(x, mod, gain, w, wa, wg, bg), ffn_weights=ffn_weights)


def _spread_heads(x, extra, out_ref):
    lane = lax.broadcasted_iota(jnp.int32, (x.shape[0], LANES), 1)
    low = lane < HEAD_DIM
    for p in range(x.shape[1] // LANES):
        xs = x[:, p * LANES:(p + 1) * LANES]
        ex = extra[:, p * LANES:(p + 1) * LANES]
        out_ref[:, (2 * p) * LANES:(2 * p + 1) * LANES] = jnp.where(low, xs, ex).astype(out_ref.dtype)
        odd = pltpu.roll(jnp.where(low, ex, xs), HEAD_DIM, 1)
        out_ref[:, (2 * p + 1) * LANES:(2 * p + 2) * LANES] = odd.astype(out_ref.dtype)


def _fox_proj_kernel(x_ref, mod_ref, gain_ref, wq_ref, wk_ref, wvt_ref, wf_ref, bf_ref,
                     pq_ref, pk_ref, hsel_ref, q_ref, k_ref, vt_ref, stats_ref, carry_ref):
    @pl.when(pl.program_id(1) == 0)
    def _():
        carry_ref[...] = jnp.zeros_like(carry_ref)

    h = _norm_mod(x_ref[...], gain_ref[...], mod_ref[0:1, :], mod_ref[1:2, :]).astype(BF16)
    log_f = _log_sigmoid(_dot(h, wf_ref[...]) + bf_ref[...])
    lc = _cumsum_rows(log_f) + carry_ref[...]
    carry_ref[...] = lc[lc.shape[0] - 1:, :]
    lc2 = lc * LOG2E
    hi, mid, lo = _split3(lc2)
    aug = jnp.concatenate([hi, mid, lo, jnp.ones_like(hi)], axis=1)
    qs = _dot(h, wq_ref[...]) * (HEAD_DIM ** -0.5 * LOG2E)
    ks = _dot(h, wk_ref[...])
    _spread_heads(qs, _dot(aug, pq_ref[...]), q_ref)
    _spread_heads(ks, _dot(aug, pk_ref[...]), k_ref)
    qn2 = _dot((qs * qs).astype(BF16), hsel_ref[...])
    kn2 = _dot((ks * ks).astype(BF16), hsel_ref[...])
    tm = lc2.shape[0]
    stats_ref[...] = jnp.concatenate([
        jnp.sqrt(jnp.max(qn2, axis=0, keepdims=True)),
        jnp.sqrt(jnp.max(kn2, axis=0, keepdims=True)),
        jnp.sqrt(jnp.max(qn2 * kn2, axis=0, keepdims=True)),
        lc2[0:1, :], lc2[tm - 1:tm, :],
        jnp.zeros((FOX_STATS - 5, lc2.shape[1]), F32)], axis=0)
    vt = _dot_nt(wvt_ref[...], h).astype(BF16)
    pad = jnp.where(lax.broadcasted_iota(jnp.int32, (HEAD_DIM, vt.shape[1]), 0) == 0,
                    1.0, 0.0).astype(BF16)
    for hd in range(FOX_HEADS):
        vt_ref[hd * LANES:hd * LANES + HEAD_DIM, :] = vt[hd * HEAD_DIM:(hd + 1) * HEAD_DIM, :]
        vt_ref[hd * LANES + HEAD_DIM:(hd + 1) * LANES, :] = pad


def _fox_proj_call(x, mod, gain, wq, wk, wvt, wf, bf, pq, pk, tm=512):
    bsz, s, d = x.shape
    nqk = FOX_HEADS * LANES
    row = lambda b, i: (b, i, 0)
    hsel = jnp.asarray(np.repeat(np.eye(FOX_HEADS, dtype=np.float32), HEAD_DIM, axis=0), BF16)
    return pl.pallas_call(
        _fox_proj_kernel,
        grid=(bsz, s // tm),
        in_specs=[
            pl.BlockSpec((None, tm, d), row),
            pl.BlockSpec((None, 6, d), lambda b, i: (b, 0, 0)),
            _const_spec((1, d)),
            _const_spec(wq.shape),
            _const_spec(wk.shape),
            _const_spec(wvt.shape),
            _const_spec(wf.shape),
            _const_spec(bf.shape),
            _const_spec(pq.shape),
            _const_spec(pk.shape),
            _const_spec(hsel.shape),
        ],
        out_specs=[
            pl.BlockSpec((None, tm, nqk), row),
            pl.BlockSpec((None, tm, nqk), row),
            pl.BlockSpec((None, None, nqk, tm), lambda b, i: (b, i, 0, 0)),
            pl.BlockSpec((None, None, FOX_STATS, FOX_HEADS), lambda b, i: (b, i, 0, 0)),
        ],
        out_shape=[
            jax.ShapeDtypeStruct((bsz, s, nqk), BF16),
            jax.ShapeDtypeStruct((bsz, s, nqk), BF16),
            jax.ShapeDtypeStruct((bsz, s // tm, nqk, tm), BF16),
            jax.ShapeDtypeStruct((bsz, s // tm, FOX_STATS, FOX_HEADS), F32),
        ],
        scratch_shapes=[pltpu.VMEM((1, FOX_HEADS), F32)],
        compiler_params=_params("arbitrary", "arbitrary"),
        name="fox_proj",
    )(x, mod, gain, wq, wk, wvt, wf, bf, pq, pk, hsel)


FFN_CAST_BLOCKS = 16


def _call_with_ffn_cast(kernel_fn, grid, in_specs, out_specs, out_shape, scratch_shapes, name, args,
                        ffn_weights):
    params = _params(*["arbitrary"] * len(grid))
    if ffn_weights is None:
        return pl.pallas_call(kernel_fn, grid=grid, in_specs=in_specs, out_specs=out_specs,
                              out_shape=out_shape, scratch_shapes=scratch_shapes,
                              compiler_params=params, name=name)(*args)
    wgu_stack, wd_stack, layer = ffn_weights
    n_in, n_out = len(in_specs), len(out_specs)
    steps = math.prod(grid)
    blocks = math.gcd(steps, FFN_CAST_BLOCKS)
    per = steps // blocks

    def linear(*ids):
        idx = ids[0]
        for extent, i in zip(grid[1:], ids[1:]):
            idx = idx * extent + i
        return idx

    def body(*refs):
        ins, (gu32, d32) = refs[:n_in], refs[n_in:n_in + 2]
        outs = refs[n_in + 2:n_in + 2 + n_out]
        (gu16, d16), scratch = refs[n_in + 2 + n_out:n_in + 4 + n_out], refs[n_in + 4 + n_out:]

        def cast():
            gu16[...] = gu32[...].astype(BF16)
            d16[...] = d32[...].astype(BF16)

        if per == 1:
            cast()
        else:
            pl.when(linear(*[pl.program_id(a) for a in range(len(grid))]) % per == 0)(cast)
        kernel_fn(*ins, *outs, *scratch)

    def weight_specs(stack):
        rows, cols = stack.shape[1] // blocks, stack.shape[2]
        return (pl.BlockSpec((None, rows, cols), lambda *ids: (layer, linear(*ids) // per, 0)),
                pl.BlockSpec((rows, cols), lambda *ids: (linear(*ids) // per, 0)),
                jax.ShapeDtypeStruct(stack.shape[1:], BF16))

    gu_in, gu_out, gu_shape = weight_specs(wgu_stack)
    d_in, d_out, d_shape = weight_specs(wd_stack)
    return pl.pallas_call(
        body,
        grid=grid,
        in_specs=list(in_specs) + [gu_in, d_in],
        out_specs=list(out_specs) + [gu_out, d_out],
        out_shape=list(out_shape) + [gu_shape, d_shape],
        scratch_shapes=scratch_shapes,
        compiler_params=params,
        name=name,
    )(*args, wgu_stack, wd_stack)


def _swa_band_bias():
    w, group = SWA_WINDOW, SWA_Q_HEADS // SWA_KV_HEADS
    key = np.arange(2 * w)[:, None]
    qry = np.arange(group * w)[None, :] % w
    dist = (w + qry) - key
    band = (dist >= 0) & (dist < w)
    allowed = np.stack([band & (key >= w), band])
    return jnp.asarray(np.where(allowed, 0.0, NEG_BIG), F32)


def _swa_attn_kernel(sink_ref, bias_ref, q_ref, kc_ref, vtc_ref, o_ref, st_ref, kp_ref, vtp_ref):
    w = SWA_WINDOW
    group = SWA_Q_HEADS // SWA_KV_HEADS
    nq = group * w
    nb = q_ref.shape[0] // w
    lane = lax.broadcasted_iota(jnp.int32, (w, LANES), 1)
    low = lane < HEAD_DIM

    @pl.when(pl.program_id(1) == 0)
    def _():
        kp_ref[...] = jnp.zeros_like(kp_ref)
        vtp_ref[...] = jnp.zeros_like(vtp_ref)

    def scores(u, g, slot):
        rows = slice(u * w, (u + 1) * w)
        cols = slice(g * LANES, (g + 1) * LANES)
        slabs = []
        for hh in range(group):
            head = g * group + hh
            qs = q_ref[rows, (head // 2) * LANES:(head // 2 + 1) * LANES]
            keep = low if head % 2 == 0 else jnp.logical_not(low)
            slabs.append(jnp.where(keep, qs, jnp.zeros_like(qs)))
        q_stack = jnp.concatenate(slabs, axis=0)
        k_prev = kp_ref[:, cols] if u == 0 else kc_ref[(u - 1) * w:u * w, cols]
        k_both = jnp.concatenate([k_prev, kc_ref[rows, cols]], axis=0)
        st_ref[slot] = _dot_nt(k_both, q_stack)

    def finish(u, g, slot):
        rows = slice(u * w, (u + 1) * w)
        cols = slice(g * LANES, (g + 1) * LANES)
        bias = bias_ref[jnp.minimum(pl.program_id(1), 1)] if u == 0 else bias_ref[1]
        st = st_ref[slot] + bias
        sink = sink_ref[:, g * nq:(g + 1) * nq] * LOG2E
        m = jnp.maximum(jnp.max(st, axis=0, keepdims=True), sink)
        pt = jnp.exp2(st - m).astype(BF16)
        vt_prev = vtp_ref[cols, :] if u == 0 else vtc_ref[cols, (u - 1) * w:u * w]
        vt_both = jnp.concatenate([vt_prev, vtc_ref[cols, rows]], axis=1)
        acc = _dot(vt_both, pt)
        den = acc[HEAD_DIM:HEAD_DIM + 1, :] + jnp.exp2(sink - m)
        ot = acc[:HEAD_DIM, :] / den
        o_t = jnp.concatenate([ot[:, hh * w:(hh + 1) * w] for hh in range(group)], axis=0)
        o_ref[rows, g * group * HEAD_DIM:(g + 1) * group * HEAD_DIM] = o_t.T.astype(BF16)

    items = [(u, g) for u in range(nb) for g in range(SWA_KV_HEADS)]
    slots = st_ref.shape[0]
    ahead = slots - 1
    for n in range(min(ahead, len(items))):
        scores(*items[n], n % slots)
    for n, item in enumerate(items):
        if n + ahead < len(items):
            scores(*items[n + ahead], (n + ahead) % slots)
        finish(*item, n % slots)
    kp_ref[...] = kc_ref[(nb - 1) * w:, :]
    vtp_ref[...] = vtc_ref[:, (nb - 1) * w:]


def _swa_attn_call(q, k, vt, sinks, ffn_weights, nb=4):
    bsz, s, d = q.shape
    w = SWA_WINDOW
    nkv = k.shape[2]
    sink_row = jnp.repeat(sinks, w).reshape(1, -1)
    bias = _swa_band_bias()
    cur = lambda b, i: (b, i, 0)
    return _call_with_ffn_cast(
        _swa_attn_kernel,
        grid=(bsz, s // (nb * w)),
        in_specs=[
            _const_spec(sink_row.shape),
            _const_spec(bias.shape),
            pl.BlockSpec((None, nb * w, d), cur),
            pl.BlockSpec((None, nb * w, nkv), cur),
            pl.BlockSpec((None, nkv, nb * w), lambda b, i: (b, 0, i)),
        ],
        out_specs=[pl.BlockSpec((None, nb * w, d), cur)],
        out_shape=[jax.ShapeDtypeStruct((bsz, s, d), BF16)],
        scratch_shapes=[
            pltpu.VMEM((SWA_SCORE_SLOTS, 2 * w, (SWA_Q_HEADS // SWA_KV_HEADS) * w), F32),
            pltpu.VMEM((w, nkv), BF16),
            pltpu.VMEM((nkv, w), BF16),
        ],
        name="swa_attn", args=(sink_row, bias, q, k, vt), ffn_weights=ffn_weights)


def _chunk_cumsum(x):
    n = x.shape[0]
    row = lax.broadcasted_iota(jnp.int32, (n, n), 0)
    col = lax.broadcasted_iota(jnp.int32, (n, n), 1)
    same_chunk = (row // GLA_CHUNK) == (col // GLA_CHUNK)
    tril = jnp.where(jnp.logical_and(row >= col, same_chunk), 1.0, 0.0).astype(BF16)
    hi, mid, lo = _split3(x)
    return _dot(tril, hi) + _dot(tril, mid) + _dot(tril, lo)


def _gla_intra(q, k, b2, k_rows, b_rows, base):
    c, sub = GLA_CHUNK, GLA_SUB
    col = lax.broadcasted_iota(jnp.int32, (sub, c), 1)
    row = lax.broadcasted_iota(jnp.int32, (sub, c), 0)
    blocks = []
    for i in range(c // sub):
        lo = i * sub
        q_i = q[lo:lo + sub, :]
        b_i = b2[lo:lo + sub, :]
        if i == 0:
            a = jnp.zeros((sub, c), F32)
        else:
            ref = b2[lo - 1:lo, :]
            n = -(-lo // BF16_ROWS) * BF16_ROWS
            q_t = (q_i * jnp.exp2(b_i - ref)).astype(BF16)
            k_t = (k[:n, :] * jnp.exp2(jnp.minimum(ref - b2[:n, :], 0.0))).astype(BF16)
            if n < c:
                k_t = jnp.concatenate([k_t, jnp.zeros((c - n, k_t.shape[1]), BF16)], axis=0)
            a = _dot_nt(q_t, k_t)
        for s in range(lo, lo + sub):
            w = jnp.exp2(b_i - b_rows[base + s:base + s + 1, :])
            val = jnp.sum(q_i * k_rows[base + s:base + s + 1, :] * w, axis=1, keepdims=True)
            a = jnp.where(col == s, val, a)
        blocks.append(jnp.where(row + lo >= col, a, 0.0))
    return jnp.concatenate(blocks, axis=0)


def _gla_kernel(q_ref, k_ref, v_ref, r_ref, la_ref, hn_ref, o_ref, state_ref, krow_ref, brow_ref):
    @pl.when(pl.program_id(2) == 0)
    def _():
        state_ref[...] = jnp.zeros_like(state_ref)

    for hh in range(state_ref.shape[0]):
        kq = slice(hh * GLA_DK, (hh + 1) * GLA_DK)
        vv = slice(hh * GLA_DV, (hh + 1) * GLA_DV)
        _gla_head(q_ref.at[:, kq], k_ref.at[:, kq], v_ref.at[:, vv], r_ref.at[:, vv], la_ref.at[:, kq],
                  hn_ref, o_ref.at[:, vv], state_ref.at[hh], krow_ref.at[hh], brow_ref.at[hh])


def _gla_head(q_ref, k_ref, v_ref, r_ref, la_ref, hn_ref, o_ref, state_ref, krow_ref, brow_ref):
    c = GLA_CHUNK
    nc = q_ref.shape[0] // c
    rows = [slice(ci * c, (ci + 1) * c) for ci in range(nc)]
    b2_all = _chunk_cumsum(la_ref[...]) * LOG2E
    q_all = q_ref[...].astype(F32) * (GLA_DK ** -0.5)
    k_all = k_ref[...].astype(F32)
    krow_ref[...] = k_all
    brow_ref[...] = b2_all
    b2 = [b2_all[r] for r in rows]
    q = [q_all[r] for r in rows]
    k = [k_all[r] for r in rows]
    last = [b[c - 1:c, :] for b in b2]
    q_in = [(q[i] * jnp.exp2(b2[i])).astype(BF16) for i in range(nc)]
    k_out = [(k[i] * jnp.exp2(last[i] - b2[i])).astype(BF16) for i in range(nc)]
    kv = [_dot_tn(v_ref[rows[i], :], k_out[i]) for i in range(nc)]
    attn = [_gla_intra(q[i], k[i], b2[i], krow_ref, brow_ref, i * c).astype(BF16) for i in range(nc)]
    intra = [_dot(attn[i], v_ref[rows[i], :]) for i in range(nc)]
    state_t = state_ref[...]
    for i in range(nc):
        o = intra[i] + _dot_nt(q_in[i], state_t.astype(BF16))
        state_t = state_t * jnp.exp2(last[i]) + kv[i]
        r = r_ref[rows[i], :].astype(F32)
        o_ref[rows[i], :] = (_rms(o, hn_ref[...]) * _silu(r)).astype(BF16)
    state_ref[...] = state_t


def _gla_call(q, k, v, r, la, head_norm, ffn_weights, tm=256, nh=4):
    bsz, s, _ = q.shape
    dk, dv = GLA_DK, GLA_DV
    blk = lambda b, h, i: (b, i, h)
    return _call_with_ffn_cast(
        _gla_kernel,
        grid=(bsz, GLA_HEADS // nh, s // tm),
        in_specs=[
            pl.BlockSpec((None, tm, nh * dk), blk),
            pl.BlockSpec((None, tm, nh * dk), blk),
            pl.BlockSpec((None, tm, nh * dv), blk),
            pl.BlockSpec((None, tm, nh * dv), blk),
            pl.BlockSpec((None, tm, nh * dk), blk),
            _const_spec((1, dv)),
        ],
        out_specs=[pl.BlockSpec((None, tm, nh * dv), blk)],
        out_shape=[jax.ShapeDtypeStruct((bsz, s, GLA_HEADS * dv), BF16)],
        scratch_shapes=[pltpu.VMEM((nh, dv, dk), F32), pltpu.VMEM((nh, tm, dk), F32),
                        pltpu.VMEM((nh, tm, dk), F32)],
        name="gla_mix", args=(q, k, v, r, la, head_norm), ffn_weights=ffn_weights)


def _fox_first_live_block(stats_ref, b, first_head, nh, i):
    heads = [first_head + e for e in range(nh)]
    q_max = [FOX_BOUND_SLACK * stats_ref[b, i, Q_MAX, hd] for hd in heads]
    margin = [stats_ref[b, i, LC_FIRST, hd] + FOX_BOUND_SLACK * stats_ref[b, i, QK_MAX, hd]
              - FOX_DEAD_LOG2 for hd in heads]

    def dead(j):
        is_dead = True
        for e, hd in enumerate(heads):
            gap = q_max[e] * stats_ref[b, j, K_MAX, hd] - stats_ref[b, j, LC_LAST, hd] + margin[e]
            is_dead = jnp.logical_and(is_dead, gap <= 0.0)
        return is_dead

    last = jnp.maximum(i - 1, 0)
    return lax.while_loop(lambda j: jnp.logical_and(j < i, dead(jnp.minimum(j, last))),
                          lambda j: j + 1, jnp.int32(0))


def _fox_attn_kernel(stats_ref, q_ref, k_ref, vt_ref, o_ref, m_ref, acc_ref, sa_ref, sb_ref):
    tk = vt_ref.shape[2]
    nh = m_ref.shape[0]
    i = pl.program_id(2)
    j0 = _fox_first_live_block(stats_ref, pl.program_id(0), pl.program_id(1) * nh, nh, i)
    m_ref[...] = jnp.full_like(m_ref, NEG_BIG)
    acc_ref[...] = jnp.zeros_like(acc_ref)

    half = tk // 2

    def scores(j, buf, e, diag=False):
        lanes = slice(e * LANES, (e + 1) * LANES)
        row0 = pl.multiple_of(j * tk, tk)
        if not diag:
            buf[e] = _dot_nt(k_ref[pl.ds(row0, tk), lanes], q_ref[:, lanes])
        else:
            buf[e, :half, :] = _dot_nt(k_ref[pl.ds(row0, half), lanes], q_ref[:, lanes])
            buf[e, half:, half:] = _dot_nt(k_ref[pl.ds(row0 + half, half), lanes], q_ref[half:, lanes])

    def online_update(e, st, vt, cols):
        m_old = m_ref[e, :, cols]
        m_new = jnp.maximum(m_old, jnp.max(st, axis=0, keepdims=True))
        alpha = jnp.exp2(m_old - m_new)
        pt = jnp.exp2(st - m_new).astype(BF16)
        acc_ref[e, :, cols] = alpha * acc_ref[e, :, cols] + _dot(vt, pt)
        m_ref[e, :, cols] = m_new

    def accumulate(j, buf, e, masked):
        rows = slice(e * LANES, (e + 1) * LANES)
        if not masked:
            online_update(e, buf[e], vt_ref[j, rows, :], slice(None))
            return
        key = lax.broadcasted_iota(jnp.int32, (half, half), 0)
        qry = lax.broadcasted_iota(jnp.int32, (half, half), 1)
        causal = key <= qry
        st = jnp.concatenate([jnp.where(causal, buf[e, :half, :half], NEG_BIG), buf[e, :half, half:]], axis=1)
        online_update(e, st, vt_ref[j, rows, :half], slice(None))
        online_update(e, jnp.where(causal, buf[e, half:, half:], NEG_BIG), vt_ref[j, rows, half:],
                      slice(half, None))

    def block(j, buf, masked, following):
        for e in range(nh):
            if e + 1 < nh:
                scores(j, buf, e + 1, masked)
            elif following is not None:
                scores(following[0], following[1], 0, following[2])
            accumulate(j, buf, e, masked)

    scores(j0, sa_ref, 0)
    n_full = i - j0

    def body(t, carry):
        j = j0 + 2 * t
        block(j, sa_ref, False, (j + 1, sb_ref, False))
        block(j + 1, sb_ref, False, (j + 2, sa_ref, False))
        return carry

    lax.fori_loop(0, n_full // 2, body, 0)

    @pl.when(n_full % 2 == 0)
    def _():
        block(i, sa_ref, True, None)

    @pl.when(n_full % 2 == 1)
    def _():
        block(i - 1, sa_ref, False, (i, sb_ref, True))
        block(i, sb_ref, True, None)

    outs = []
    for e in range(nh):
        acc = acc_ref[e]
        outs.append(acc[:HEAD_DIM, :] / acc[HEAD_DIM:HEAD_DIM + 1, :])
    o_ref[...] = jnp.concatenate(outs, axis=0).T.astype(BF16)


def _fox_attn_call(q, k, vt, stats, ffn_weights, nh=4):
    bsz, s, _ = q.shape
    tk = vt.shape[3]
    tq = tk
    return _call_with_ffn_cast(
        _fox_attn_kernel,
        grid=(bsz, FOX_HEADS // nh, s // tq),
        in_specs=[
            pl.BlockSpec(memory_space=pltpu.SMEM),
            pl.BlockSpec((None, tq, nh * LANES), lambda b, p, i: (b, i, p)),
            pl.BlockSpec((None, s, nh * LANES), lambda b, p, i: (b, 0, p)),
            pl.BlockSpec((None, s // tk, nh * LANES, tk), lambda b, p, i: (b, 0, p, 0)),
        ],
        out_specs=[pl.BlockSpec((None, tq, nh * HEAD_DIM), lambda b, p, i: (b, i, p))],
        out_shape=[jax.ShapeDtypeStruct((bsz, s, D_MODEL), BF16)],
        scratch_shapes=[
            pltpu.VMEM((nh, 1, tq), F32),
            pltpu.VMEM((nh, LANES, tq), F32),
            pltpu.VMEM((nh, tk, tq), F32),
            pltpu.VMEM((nh, tk, tq), F32),
        ],
        name="fox_attn", args=(stats, q, k, vt), ffn_weights=ffn_weights)


def _post_kernel(x_ref, o_ref, mod_ref, gain_ref, wo_ref, wgu_ref, wd_ref, fn_ref, out_ref,
                 *, ff_chunk, final):
    x1 = x_ref[...] + mod_ref[2:3, :] * _dot(o_ref[...], wo_ref[...])
    h = _norm_mod(x1, gain_ref[...], mod_ref[3:4, :], mod_ref[4:5, :]).astype(BF16)
    acc = jnp.zeros(x1.shape, F32)
    for c0 in range(0, D_FF, ff_chunk):
        g = _dot(h, wgu_ref[:, c0:c0 + ff_chunk])
        u = _dot(h, wgu_ref[:, D_FF + c0:D_FF + c0 + ff_chunk])
        acc = acc + _dot((_silu(g) * u).astype(BF16), wd_ref[c0:c0 + ff_chunk, :])
    x2 = x1 + mod_ref[5:6, :] * acc
    if final:
        x2 = _rms(x2, fn_ref[...])
    out_ref[...] = x2


def _post_call(x, o, mod, gain, wo, wgu, wd, final_norm, final, tm=1024, ff_chunk=256):
    bsz, s, d = x.shape
    row = lambda b, i: (b, i, 0)
    return pl.pallas_call(
        functools.partial(_post_kernel, ff_chunk=ff_chunk, final=final),
        grid=(bsz, s // tm),
        in_specs=[
            pl.BlockSpec((None, tm, d), row),
            pl.BlockSpec((None, tm, d), row),
            pl.BlockSpec((None, 6, d), lambda b, i: (b, 0, 0)),
            _const_spec((1, d)),
            _const_spec(wo.shape),
            _const_spec(wgu.shape),
            _const_spec(wd.shape),
            _const_spec((1, d)),
        ],
        out_specs=pl.BlockSpec((None, tm, d), row),
        out_shape=jax.ShapeDtypeStruct((bsz, s, d), F32),
        compiler_params=_params("arbitrary", "arbitrary"),
        name="post_ffn",
    )(x, o, mod, gain, wo, wgu, wd, final_norm)


def _rope_tables(s):
    half = HEAD_DIM // 2
    inv = 1.0 / (ROPE_THETA ** (jnp.arange(0, HEAD_DIM, 2, dtype=F32) / HEAD_DIM))
    ang = jnp.arange(s, dtype=F32)[:, None] * inv[None, :]
    cos, sin = jnp.cos(ang), jnp.sin(ang)
    reps = LANES // HEAD_DIM
    cos_t = jnp.tile(jnp.concatenate([cos, cos], axis=1), (1, reps))
    sin_t = jnp.tile(jnp.concatenate([-sin, sin], axis=1), (1, reps))
    assert half * 2 == HEAD_DIM
    return cos_t, sin_t


def _dup_heads(w, heads):
    w3 = w.reshape(w.shape[0], heads, HEAD_DIM)
    return jnp.concatenate([w3, w3], axis=2).reshape(w.shape[0], heads * LANES)


def _pad_heads(w, heads):
    w3 = w.reshape(w.shape[0], heads, HEAD_DIM)
    return jnp.concatenate([w3, jnp.zeros_like(w3)], axis=2).reshape(w.shape[0], heads * LANES)


def _fox_placement():
    h = FOX_HEADS
    pq = np.zeros((4 * h, h * HEAD_DIM), np.float32)
    pk = np.zeros((4 * h, h * HEAD_DIM), np.float32)
    for head in range(h):
        base = (head // 2) * LANES + (HEAD_DIM if head % 2 == 0 else 0)
        for part in range(3):
            pq[part * h + head, base + part] = 1.0
            pk[3 * h + head, base + part] = 1.0
            pq[3 * h + head, base + 3 + part] = 1.0
            pk[part * h + head, base + 3 + part] = -1.0
    return jnp.asarray(pq, BF16), jnp.asarray(pk, BF16)


def kernel(x, c, ada_w, ada_b, norm_gain, ffn_w_gu, ffn_w_down, swa_w_in, swa_sinks, swa_w_o,
           gla_w_in, gla_w_gate_up, gla_b_gate, gla_head_norm, gla_w_o, fox_w_in, fox_b_f, fox_w_o,
           final_norm):
    bsz, s, d = x.shape
    depth = ada_w.shape[0]
    mod_all = _ada_call(c, ada_w, ada_b).reshape(depth, bsz, 6, d)
    cos_t, sin_t = _rope_tables(s)
    pq, pk = _fox_placement()
    fn = final_norm.reshape(1, d)

    for i in range(depth):
        kind, j = i % N_MIXERS, i // N_MIXERS
        mod = mod_all[i]
        gain1 = norm_gain[i, 0].reshape(1, d)
        gain2 = norm_gain[i, 1].reshape(1, d)
        ffn_weights = (ffn_w_gu, ffn_w_down, i)
        if kind == 0:
            w = swa_w_in[j]
            nq, nkv = SWA_Q_HEADS * HEAD_DIM, SWA_KV_HEADS * HEAD_DIM
            w_all = jnp.concatenate([w[:, :nq], _dup_heads(w[:, nq:nq + nkv], SWA_KV_HEADS)],
                                    axis=1).astype(BF16)
            wvt = _pad_heads(w[:, nq + nkv:], SWA_KV_HEADS).T.astype(BF16)
            q, k, v, wgu, wd = _swa_proj_call(x, mod, gain1, w_all, wvt, cos_t, sin_t, ffn_weights)
            o, = _swa_attn_call(q, k, v, swa_sinks[j], None)
            wo = swa_w_o[j]
        elif kind == 1:
            w = gla_w_in[j]
            n_main = 2 * GLA_HEADS * GLA_DK + 2 * GLA_HEADS * GLA_DV
            q, k, v, r, la, wgu, wd = _gla_proj_call(
                x, mod, gain1, w[:, :n_main].astype(BF16), w[:, n_main:].astype(BF16),
                gla_w_gate_up[j].astype(BF16), gla_b_gate[j].reshape(1, -1), ffn_weights)
            o, = _gla_call(q, k, v, r, la, gla_head_norm[j].reshape(1, -1), None)
            wo = gla_w_o[j]
        else:
            w = fox_w_in[j]
            order = jnp.argsort(fox_b_f[j])
            by_head = lambda m: jnp.take(m.reshape(d, FOX_HEADS, HEAD_DIM), order, axis=1).reshape(d, d)
            q, k, v, stats = _fox_proj_call(
                x, mod, gain1,
                by_head(w[:, :d]).astype(BF16), by_head(w[:, d:2 * d]).astype(BF16),
                by_head(w[:, 2 * d:3 * d]).T.astype(BF16), jnp.take(w[:, 3 * d:], order, axis=1).astype(BF16),
                jnp.take(fox_b_f[j], order).reshape(1, -1), pq, pk)
            o, wgu, wd = _fox_attn_call(q, k, v, stats, ffn_weights)
            wo = jnp.take(fox_w_o[j].reshape(FOX_HEADS, HEAD_DIM, d), order, axis=0).reshape(d, d)
        x = _post_call(x, o, mod, gain2, wo.astype(BF16), wgu, wd, fn, final=(i == depth - 1))
    return x
```

```python
import functools
import math

import numpy as np
import jax
import jax.numpy as jnp
from jax import lax
from jax.experimental import pallas as pl
from jax.experimental.pallas import tpu as pltpu

D_MODEL = 1024
HEAD_DIM = 64
RMS_EPS = 1e-6
SWA_Q_HEADS = 16
SWA_KV_HEADS = 4
SWA_WINDOW = 128
SWA_SCORE_SLOTS = 3
ROPE_THETA = 150000.0
GLA_HEADS = 4
GLA_DK = 128
GLA_DV = 256
GLA_RANK = 16
GLA_TAU = 16.0
GLA_CHUNK = 64
GLA_SUB = 8
FOX_HEADS = 16
FOX_STATS = 8
Q_MAX, K_MAX, QK_MAX, LC_FIRST, LC_LAST = range(5)
FOX_DEAD_LOG2 = -160.0
FOX_BOUND_SLACK = 1.02
D_FF = 2816
N_MIXERS = 3

LANES = 128
BF16_ROWS = 16
NEG_BIG = -1e30
LOG2E = 1.4426950408889634
VMEM_LIMIT = 56 * 1024 * 1024

BF16 = jnp.bfloat16
F32 = jnp.float32


def _dot(a, b):
    return jnp.dot(a, b, preferred_element_type=F32)


def _dot_nt(a, b):
    return lax.dot_general(a, b, (((1,), (1,)), ((), ())), preferred_element_type=F32)


def _dot_tn(a, b):
    return lax.dot_general(a, b, (((0,), (0,)), ((), ())), preferred_element_type=F32)


def _split3(x):
    hi = x.astype(BF16)
    r1 = x - hi.astype(F32)
    mid = r1.astype(BF16)
    lo = (r1 - mid.astype(F32)).astype(BF16)
    return hi, mid, lo


def _cumsum_rows(x):
    n = x.shape[0]
    row = lax.broadcasted_iota(jnp.int32, (n, n), 0)
    col = lax.broadcasted_iota(jnp.int32, (n, n), 1)
    tril = jnp.where(row >= col, 1.0, 0.0).astype(BF16)
    w = x.shape[1]
    sums = _dot(tril, jnp.concatenate(_split3(x), axis=1))
    return sums[:, :w] + sums[:, w:2 * w] + sums[:, 2 * w:]


def _log_sigmoid(x):
    return jnp.minimum(x, 0.0) - jnp.log(1.0 + jnp.exp(-jnp.abs(x)))


def _silu(x):
    return x * (1.0 / (1.0 + jnp.exp(-x)))


def _rms(x, gain):
    ms = jnp.mean(x * x, axis=-1, keepdims=True)
    return x * lax.rsqrt(ms + RMS_EPS) * gain


def _norm_mod(x, gain, shift, scale):
    return _rms(x, gain) * (1.0 + scale) + shift


def _params(*sem):
    return pltpu.CompilerParams(dimension_semantics=sem, vmem_limit_bytes=VMEM_LIMIT)


def _const_spec(shape):
    nd = len(shape)
    return pl.BlockSpec(shape, lambda *_: (0,) * nd, pipeline_mode=pl.Buffered(1))


def _ada_kernel(ct_ref, w_ref, b_ref, out_ref):
    ca = _silu(ct_ref[...])
    w = w_ref[...]
    for b in range(ct_ref.shape[1]):
        col = ca[:, b:b + 1]
        out_ref[b:b + 1, :] = jnp.sum(col * w, axis=0, keepdims=True) + b_ref[...]


def _ada_call(c, ada_w, ada_b):
    depth, d, n = ada_w.shape
    bsz = c.shape[0]
    tn = 1536
    return pl.pallas_call(
        _ada_kernel,
        grid=(depth, n // tn),
        in_specs=[
            pl.BlockSpec((d, bsz), lambda l, j: (0, 0)),
            pl.BlockSpec((None, d, tn), lambda l, j: (l, 0, j)),
            pl.BlockSpec((None, 1, tn), lambda l, j: (l, 0, j)),
        ],
        out_specs=pl.BlockSpec((None, bsz, tn), lambda l, j: (l, 0, j)),
        out_shape=jax.ShapeDtypeStruct((depth, bsz, n), F32),
        compiler_params=_params("arbitrary", "arbitrary"),
        name="ada_mod",
    )(c.T, ada_w, ada_b.reshape(depth, 1, n))


def _rope(x, cos, sin_signed):
    width = x.shape[1]
    reps = width // cos.shape[1]
    c = jnp.tile(cos, (1, reps))
    s = jnp.tile(sin_signed, (1, reps))
    lane = lax.broadcasted_iota(jnp.int32, x.shape, 1)
    first_half = (lane % HEAD_DIM) < (HEAD_DIM // 2)
    rot = jnp.where(first_half,
                    pltpu.roll(x, width - HEAD_DIM // 2, 1),
                    pltpu.roll(x, HEAD_DIM // 2, 1))
    return x * c + rot * s


def _ones_row_64(vt):
    ones_row = lax.broadcasted_iota(jnp.int32, vt.shape, 0) % LANES == HEAD_DIM
    return jnp.where(ones_row, 1.0, vt)


def _swa_proj_kernel(x_ref, mod_ref, gain_ref, w_ref, wvt_ref, cos_ref, sin_ref, q_ref, k_ref, vt_ref):
    h = _norm_mod(x_ref[...], gain_ref[...], mod_ref[0:1, :], mod_ref[1:2, :]).astype(BF16)
    cos, sin = cos_ref[...], sin_ref[...]
    nq = q_ref.shape[1]
    q = _dot(h, w_ref[:, :nq])
    q_ref[...] = (_rope(q, cos, sin) * (HEAD_DIM ** -0.5 * LOG2E)).astype(BF16)
    k = _dot(h, w_ref[:, nq:])
    k_ref[...] = _rope(k, cos, sin).astype(BF16)
    vt_ref[...] = _ones_row_64(_dot_nt(wvt_ref[...], h)).astype(BF16)


def _swa_proj_call(x, mod, gain, w, wvt, cos, sin, ffn_weights, tm=1024):
    bsz, s, d = x.shape
    nq, nkv = D_MODEL, SWA_KV_HEADS * LANES
    row = lambda b, i: (b, i, 0)
    return _call_with_ffn_cast(
        _swa_proj_kernel,
        grid=(bsz, s // tm),
        in_specs=[
            pl.BlockSpec((None, tm, d), row),
            pl.BlockSpec((None, 6, d), lambda b, i: (b, 0, 0)),
            _const_spec((1, d)),
            _const_spec(w.shape),
            _const_spec(wvt.shape),
            pl.BlockSpec((tm, LANES), lambda b, i: (i, 0)),
            pl.BlockSpec((tm, LANES), lambda b, i: (i, 0)),
        ],
        out_specs=[
            pl.BlockSpec((None, tm, nq), row),
            pl.BlockSpec((None, tm, nkv), row),
            pl.BlockSpec((None, nkv, tm), lambda b, i: (b, 0, i)),
        ],
        out_shape=[
            jax.ShapeDtypeStruct((bsz, s, nq), BF16),
            jax.ShapeDtypeStruct((bsz, s, nkv), BF16),
            jax.ShapeDtypeStruct((bsz, nkv, s), BF16),
        ],
        scratch_shapes=[], name="swa_proj", args=(x, mod, gain, w, wvt, cos, sin), ffn_weights=ffn_weights)


def _gla_proj_kernel(x_ref, mod_ref, gain_ref, w_ref, wa_ref, wg_ref, bg_ref,
                     q_ref, k_ref, v_ref, r_ref, la_ref):
    h = _norm_mod(x_ref[...], gain_ref[...], mod_ref[0:1, :], mod_ref[1:2, :]).astype(BF16)
    nk = q_ref.shape[1]
    nv = v_ref.shape[1]
    q_ref[...] = _dot(h, w_ref[:, :nk]).astype(BF16)
    k_ref[...] = _dot(h, w_ref[:, nk:2 * nk]).astype(BF16)
    v_ref[...] = _dot(h, w_ref[:, 2 * nk:2 * nk + nv]).astype(BF16)
    r_ref[...] = _dot(h, w_ref[:, 2 * nk + nv:]).astype(BF16)
    a_low = _dot(h, wa_ref[...]).astype(BF16)
    z = _dot(a_low, wg_ref[...]) + bg_ref[...]
    la_ref[...] = _log_sigmoid(z) * (1.0 / GLA_TAU)


def _gla_proj_call(x, mod, gain, w, wa, wg, bg, ffn_weights, tm=1024):
    bsz, s, d = x.shape
    nk, nv = GLA_HEADS * GLA_DK, GLA_HEADS * GLA_DV
    row = lambda b, i: (b, i, 0)
    return _call_with_ffn_cast(
        _gla_proj_kernel,
        grid=(bsz, s // tm),
        in_specs=[
            pl.BlockSpec((None, tm, d), row),
            pl.BlockSpec((None, 6, d), lambda b, i: (b, 0, 0)),
            _const_spec((1, d)),
            _const_spec(w.shape),
            _const_spec(wa.shape),
            _const_spec(wg.shape),
            _const_spec(bg.shape),
        ],
        out_specs=[
            pl.BlockSpec((None, tm, nk), row),
            pl.BlockSpec((None, tm, nk), row),
            pl.BlockSpec((None, tm, nv), row),
            pl.BlockSpec((None, tm, nv), row),
            pl.BlockSpec((None, tm, nk), row),
        ],
        out_shape=[
            jax.ShapeDtypeStruct((bsz, s, nk), BF16),
            jax.ShapeDtypeStruct((bsz, s, nk), BF16),
            jax.ShapeDtypeStruct((bsz, s, nv), BF16),
            jax.ShapeDtypeStruct((bsz, s, nv), BF16),
            jax.ShapeDtypeStruct((bsz, s, nk), F32),
        ],
        scratch_shapes=[], name="gla_proj", args=(x, mod, gain, w, wa, wg, bg), ffn_weights=ffn_weights)


def _spread_heads(x, extra, out_ref):
    lane = lax.broadcasted_iota(jnp.int32, (x.shape[0], LANES), 1)
    low = lane < HEAD_DIM
    for p in range(x.shape[1] // LANES):
        xs = x[:, p * LANES:(p + 1) * LANES]
        ex = extra[:, p * LANES:(p + 1) * LANES]
        out_ref[:, (2 * p) * LANES:(2 * p + 1) * LANES] = jnp.where(low, xs, ex).astype(out_ref.dtype)
        odd = pltpu.roll(jnp.where(low, ex, xs), HEAD_DIM, 1)
        out_ref[:, (2 * p + 1) * LANES:(2 * p + 2) * LANES] = odd.astype(out_ref.dtype)


def _fox_proj_kernel(x_ref, mod_ref, gain_ref, wq_ref, wk_ref, wvt_ref, wf_ref, bf_ref,
                     pq_ref, pk_ref, hsel_ref, q_ref, k_ref, vt_ref, stats_ref, carry_ref):
    @pl.when(pl.program_id(1) == 0)
    def _():
        carry_ref[...] = jnp.zeros_like(carry_ref)

    h = _norm_mod(x_ref[...], gain_ref[...], mod_ref[0:1, :], mod_ref[1:2, :]).astype(BF16)
    log_f = _log_sigmoid(_dot(h, wf_ref[...]) + bf_ref[...])
    lc = _cumsum_rows(log_f) + carry_ref[...]
    carry_ref[...] = lc[lc.shape[0] - 1:, :]
    lc2 = lc * LOG2E
    hi, mid, lo = _split3(lc2)
    aug = jnp.concatenate([hi, mid, lo, jnp.ones_like(hi)], axis=1)
    qs = _dot(h, wq_ref[...]) * (HEAD_DIM ** -0.5 * LOG2E)
    ks = _dot(h, wk_ref[...])
    _spread_heads(qs, _dot(aug, pq_ref[...]), q_ref)
    _spread_heads(ks, _dot(aug, pk_ref[...]), k_ref)
    qn2 = _dot((qs * qs).astype(BF16), hsel_ref[...])
    kn2 = _dot((ks * ks).astype(BF16), hsel_ref[...])
    tm = lc2.shape[0]
    stats_ref[...] = jnp.concatenate([
        jnp.sqrt(jnp.max(qn2, axis=0, keepdims=True)),
        jnp.sqrt(jnp.max(kn2, axis=0, keepdims=True)),
        jnp.sqrt(jnp.max(qn2 * kn2, axis=0, keepdims=True)),
        lc2[0:1, :], lc2[tm - 1:tm, :],
        jnp.zeros((FOX_STATS - 5, lc2.shape[1]), F32)], axis=0)
    vt = _dot_nt(wvt_ref[...], h).astype(BF16)
    pad = jnp.where(lax.broadcasted_iota(jnp.int32, (HEAD_DIM, vt.shape[1]), 0) == 0,
                    1.0, 0.0).astype(BF16)
    for hd in range(FOX_HEADS):
        vt_ref[hd * LANES:hd * LANES + HEAD_DIM, :] = vt[hd * HEAD_DIM:(hd + 1) * HEAD_DIM, :]
        vt_ref[hd * LANES + HEAD_DIM:(hd + 1) * LANES, :] = pad


def _fox_proj_call(x, mod, gain, wq, wk, wvt, wf, bf, pq, pk, tm=512):
    bsz, s, d = x.shape
    nqk = FOX_HEADS * LANES
    row = lambda b, i: (b, i, 0)
    hsel = jnp.asarray(np.repeat(np.eye(FOX_HEADS, dtype=np.float32), HEAD_DIM, axis=0), BF16)
    return pl.pallas_call(
        _fox_proj_kernel,
        grid=(bsz, s // tm),
        in_specs=[
            pl.BlockSpec((None, tm, d), row),
            pl.BlockSpec((None, 6, d), lambda b, i: (b, 0, 0)),
            _const_spec((1, d)),
            _const_spec(wq.shape),
            _const_spec(wk.shape),
            _const_spec(wvt.shape),
            _const_spec(wf.shape),
            _const_spec(bf.shape),
            _const_spec(pq.shape),
            _const_spec(pk.shape),
            _const_spec(hsel.shape),
        ],
        out_specs=[
            pl.BlockSpec((None, tm, nqk), row),
            pl.BlockSpec((None, tm, nqk), row),
            pl.BlockSpec((None, None, nqk, tm), lambda b, i: (b, i, 0, 0)),
            pl.BlockSpec((None, None, FOX_STATS, FOX_HEADS), lambda b, i: (b, i, 0, 0)),
        ],
        out_shape=[
            jax.ShapeDtypeStruct((bsz, s, nqk), BF16),
            jax.ShapeDtypeStruct((bsz, s, nqk), BF16),
            jax.ShapeDtypeStruct((bsz, s // tm, nqk, tm), BF16),
            jax.ShapeDtypeStruct((bsz, s // tm, FOX_STATS, FOX_HEADS), F32),
        ],
        scratch_shapes=[pltpu.VMEM((1, FOX_HEADS), F32)],
        compiler_params=_params("arbitrary", "arbitrary"),
        name="fox_proj",
    )(x, mod, gain, wq, wk, wvt, wf, bf, pq, pk, hsel)


FFN_CAST_BLOCKS = 16


def _call_with_ffn_cast(kernel_fn, grid, in_specs, out_specs, out_shape, scratch_shapes, name, args,
                        ffn_weights):
    params = _params(*["arbitrary"] * len(grid))
    if ffn_weights is None:
        return pl.pallas_call(kernel_fn, grid=grid, in_specs=in_specs, out_specs=out_specs,
                              out_shape=out_shape, scratch_shapes=scratch_shapes,
                              compiler_params=params, name=name)(*args)
    wgu_stack, wd_stack, layer = ffn_weights
    n_in, n_out = len(in_specs), len(out_specs)
    steps = math.prod(grid)
    blocks = math.gcd(steps, FFN_CAST_BLOCKS)
    per = steps // blocks

    def linear(*ids):
        idx = ids[0]
        for extent, i in zip(grid[1:], ids[1:]):
            idx = idx * extent + i
        return idx

    def body(*refs):
        ins, (gu32, d32) = refs[:n_in], refs[n_in:n_in + 2]
        outs = refs[n_in + 2:n_in + 2 + n_out]
        (gu16, d16), scratch = refs[n_in + 2 + n_out:n_in + 4 + n_out], refs[n_in + 4 + n_out:]

        def cast():
            gu16[...] = gu32[...].astype(BF16)
            d16[...] = d32[...].astype(BF16)

        if per == 1:
            cast()
        else:
            pl.when(linear(*[pl.program_id(a) for a in range(len(grid))]) % per == 0)(cast)
        kernel_fn(*ins, *outs, *scratch)

    def weight_specs(stack):
        rows, cols = stack.shape[1] // blocks, stack.shape[2]
        return (pl.BlockSpec((None, rows, cols), lambda *ids: (layer, linear(*ids) // per, 0)),
                pl.BlockSpec((rows, cols), lambda *ids: (linear(*ids) // per, 0)),
                jax.ShapeDtypeStruct(stack.shape[1:], BF16))

    gu_in, gu_out, gu_shape = weight_specs(wgu_stack)
    d_in, d_out, d_shape = weight_specs(wd_stack)
    return pl.pallas_call(
        body,
        grid=grid,
        in_specs=list(in_specs) + [gu_in, d_in],
        out_specs=list(out_specs) + [gu_out, d_out],
        out_shape=list(out_shape) + [gu_shape, d_shape],
        scratch_shapes=scratch_shapes,
        compiler_params=params,
        name=name,
    )(*args, wgu_stack, wd_stack)


def _swa_band_bias():
    w, group = SWA_WINDOW, SWA_Q_HEADS // SWA_KV_HEADS
    key = np.arange(2 * w)[:, None]
    qry = np.arange(group * w)[None, :] % w
    dist = (w + qry) - key
    band = (dist >= 0) & (dist < w)
    allowed = np.stack([band & (key >= w), band])
    return jnp.asarray(np.where(allowed, 0.0, NEG_BIG), F32)


def _swa_attn_kernel(sink_ref, bias_ref, q_ref, kc_ref, vtc_ref, o_ref, st_ref, kp_ref, vtp_ref):
    w = SWA_WINDOW
    group = SWA_Q_HEADS // SWA_KV_HEADS
    nq = group * w
    nb = q_ref.shape[0] // w
    lane = lax.broadcasted_iota(jnp.int32, (w, LANES), 1)
    low = lane < HEAD_DIM

    @pl.when(pl.program_id(1) == 0)
    def _():
        kp_ref[...] = jnp.zeros_like(kp_ref)
        vtp_ref[...] = jnp.zeros_like(vtp_ref)

    def scores(u, g, slot):
        rows = slice(u * w, (u + 1) * w)
        cols = slice(g * LANES, (g + 1) * LANES)
        slabs = []
        for hh in range(group):
            head = g * group + hh
            qs = q_ref[rows, (head // 2) * LANES:(head // 2 + 1) * LANES]
            keep = low if head % 2 == 0 else jnp.logical_not(low)
            slabs.append(jnp.where(keep, qs, jnp.zeros_like(qs)))
        q_stack = jnp.concatenate(slabs, axis=0)
        k_prev = kp_ref[:, cols] if u == 0 else kc_ref[(u - 1) * w:u * w, cols]
        k_both = jnp.concatenate([k_prev, kc_ref[rows, cols]], axis=0)
        st_ref[slot] = _dot_nt(k_both, q_stack)

    def finish(u, g, slot):
        rows = slice(u * w, (u + 1) * w)
        cols = slice(g * LANES, (g + 1) * LANES)
        bias = bias_ref[jnp.minimum(pl.program_id(1), 1)] if u == 0 else bias_ref[1]
        st = st_ref[slot] + bias
        sink = sink_ref[:, g * nq:(g + 1) * nq] * LOG2E
        m = jnp.maximum(jnp.max(st, axis=0, keepdims=True), sink)
        pt = jnp.exp2(st - m).astype(BF16)
        vt_prev = vtp_ref[cols, :] if u == 0 else vtc_ref[cols, (u - 1) * w:u * w]
        vt_both = jnp.concatenate([vt_prev, vtc_ref[cols, rows]], axis=1)
        acc = _dot(vt_both, pt)
        den = acc[HEAD_DIM:HEAD_DIM + 1, :] + jnp.exp2(sink - m)
        ot = acc[:HEAD_DIM, :] / den
        o_t = jnp.concatenate([ot[:, hh * w:(hh + 1) * w] for hh in range(group)], axis=0)
        o_ref[rows, g * group * HEAD_DIM:(g + 1) * group * HEAD_DIM] = o_t.T.astype(BF16)

    items = [(u, g) for u in range(nb) for g in range(SWA_KV_HEADS)]
    slots = st_ref.shape[0]
    ahead = slots - 1
    for n in range(min(ahead, len(items))):
        scores(*items[n], n % slots)
    for n, item in enumerate(items):
        if n + ahead < len(items):
            scores(*items[n + ahead], (n + ahead) % slots)
        finish(*item, n % slots)
    kp_ref[...] = kc_ref[(nb - 1) * w:, :]
    vtp_ref[...] = vtc_ref[:, (nb - 1) * w:]


def _swa_attn_call(q, k, vt, sinks, ffn_weights, nb=4):
    bsz, s, d = q.shape
    w = SWA_WINDOW
    nkv = k.shape[2]
    sink_row = jnp.repeat(sinks, w).reshape(1, -1)
    bias = _swa_band_bias()
    cur = lambda b, i: (b, i, 0)
    return _call_with_ffn_cast(
        _swa_attn_kernel,
        grid=(bsz, s // (nb * w)),
        in_specs=[
            _const_spec(sink_row.shape),
            _const_spec(bias.shape),
            pl.BlockSpec((None, nb * w, d), cur),
            pl.BlockSpec((None, nb * w, nkv), cur),
            pl.BlockSpec((None, nkv, nb * w), lambda b, i: (b, 0, i)),
        ],
        out_specs=[pl.BlockSpec((None, nb * w, d), cur)],
        out_shape=[jax.ShapeDtypeStruct((bsz, s, d), BF16)],
        scratch_shapes=[
            pltpu.VMEM((SWA_SCORE_SLOTS, 2 * w, (SWA_Q_HEADS // SWA_KV_HEADS) * w), F32),
            pltpu.VMEM((w, nkv), BF16),
            pltpu.VMEM((nkv, w), BF16),
        ],
        name="swa_attn", args=(sink_row, bias, q, k, vt), ffn_weights=ffn_weights)


def _chunk_cumsum(x):
    n = x.shape[0]
    row = lax.broadcasted_iota(jnp.int32, (n, n), 0)
    col = lax.broadcasted_iota(jnp.int32, (n, n), 1)
    same_chunk = (row // GLA_CHUNK) == (col // GLA_CHUNK)
    tril = jnp.where(jnp.logical_and(row >= col, same_chunk), 1.0, 0.0).astype(BF16)
    hi, mid, lo = _split3(x)
    return _dot(tril, hi) + _dot(tril, mid) + _dot(tril, lo)


def _gla_intra(q, k, b2, k_rows, b_rows, base):
    c, sub = GLA_CHUNK, GLA_SUB
    col = lax.broadcasted_iota(jnp.int32, (sub, c), 1)
    row = lax.broadcasted_iota(jnp.int32, (sub, c), 0)
    blocks = []
    for i in range(c // sub):
        lo = i * sub
        q_i = q[lo:lo + sub, :]
        b_i = b2[lo:lo + sub, :]
        if i == 0:
            a = jnp.zeros((sub, c), F32)
        else:
            ref = b2[lo - 1:lo, :]
            n = -(-lo // BF16_ROWS) * BF16_ROWS
            q_t = (q_i * jnp.exp2(b_i - ref)).astype(BF16)
            k_t = (k[:n, :] * jnp.exp2(jnp.minimum(ref - b2[:n, :], 0.0))).astype(BF16)
            if n < c:
                k_t = jnp.concatenate([k_t, jnp.zeros((c - n, k_t.shape[1]), BF16)], axis=0)
            a = _dot_nt(q_t, k_t)
        for s in range(lo, lo + sub):
            w = jnp.exp2(b_i - b_rows[base + s:base + s + 1, :])
            val = jnp.sum(q_i * k_rows[base + s:base + s + 1, :] * w, axis=1, keepdims=True)
            a = jnp.where(col == s, val, a)
        blocks.append(jnp.where(row + lo >= col, a, 0.0))
    return jnp.concatenate(blocks, axis=0)


def _gla_kernel(q_ref, k_ref, v_ref, r_ref, la_ref, hn_ref, o_ref, state_ref, krow_ref, brow_ref):
    @pl.when(pl.program_id(2) == 0)
    def _():
        state_ref[...] = jnp.zeros_like(state_ref)

    for hh in range(state_ref.shape[0]):
        kq = slice(hh * GLA_DK, (hh + 1) * GLA_DK)
        vv = slice(hh * GLA_DV, (hh + 1) * GLA_DV)
        _gla_head(q_ref.at[:, kq], k_ref.at[:, kq], v_ref.at[:, vv], r_ref.at[:, vv], la_ref.at[:, kq],
                  hn_ref, o_ref.at[:, vv], state_ref.at[hh], krow_ref.at[hh], brow_ref.at[hh])


def _gla_head(q_ref, k_ref, v_ref, r_ref, la_ref, hn_ref, o_ref, state_ref, krow_ref, brow_ref):
    c = GLA_CHUNK
    nc = q_ref.shape[0] // c
    rows = [slice(ci * c, (ci + 1) * c) for ci in range(nc)]
    b2_all = _chunk_cumsum(la_ref[...]) * LOG2E
    q_all = q_ref[...].astype(F32) * (GLA_DK ** -0.5)
    k_all = k_ref[...].astype(F32)
    krow_ref[...] = k_all
    brow_ref[...] = b2_all
    b2 = [b2_all[r] for r in rows]
    q = [q_all[r] for r in rows]
    k = [k_all[r] for r in rows]
    last = [b[c - 1:c, :] for b in b2]
    q_in = [(q[i] * jnp.exp2(b2[i])).astype(BF16) for i in range(nc)]
    k_out = [(k[i] * jnp.exp2(last[i] - b2[i])).astype(BF16) for i in range(nc)]
    kv = [_dot_tn(v_ref[rows[i], :], k_out[i]) for i in range(nc)]
    attn = [_gla_intra(q[i], k[i], b2[i], krow_ref, brow_ref, i * c).astype(BF16) for i in range(nc)]
    intra = [_dot(attn[i], v_ref[rows[i], :]) for i in range(nc)]
    state_t = state_ref[...]
    for i in range(nc):
        o = intra[i] + _dot_nt(q_in[i], state_t.astype(BF16))
        state_t = state_t * jnp.exp2(last[i]) + kv[i]
        r = r_ref[rows[i], :].astype(F32)
        o_ref[rows[i], :] = (_rms(o, hn_ref[...]) * _silu(r)).astype(BF16)
    state_ref[...] = state_t


def _gla_call(q, k, v, r, la, head_norm, ffn_weights, tm=256, nh=4):
    bsz, s, _ = q.shape
    dk, dv = GLA_DK, GLA_DV
    blk = lambda b, h, i: (b, i, h)
    return _call_with_ffn_cast(
        _gla_kernel,
        grid=(bsz, GLA_HEADS // nh, s // tm),
        in_specs=[
            pl.BlockSpec((None, tm, nh * dk), blk),
            pl.BlockSpec((None, tm, nh * dk), blk),
            pl.BlockSpec((None, tm, nh * dv), blk),
            pl.BlockSpec((None, tm, nh * dv), blk),
            pl.BlockSpec((None, tm, nh * dk), blk),
            _const_spec((1, dv)),
        ],
        out_specs=[pl.BlockSpec((None, tm, nh * dv), blk)],
        out_shape=[jax.ShapeDtypeStruct((bsz, s, GLA_HEADS * dv), BF16)],
        scratch_shapes=[pltpu.VMEM((nh, dv, dk), F32), pltpu.VMEM((nh, tm, dk), F32),
                        pltpu.VMEM((nh, tm, dk), F32)],
        name="gla_mix", args=(q, k, v, r, la, head_norm), ffn_weights=ffn_weights)


def _fox_first_live_block(stats_ref, b, first_head, nh, i):
    heads = [first_head + e for e in range(nh)]
    q_max = [FOX_BOUND_SLACK * stats_ref[b, i, Q_MAX, hd] for hd in heads]
    margin = [stats_ref[b, i, LC_FIRST, hd] + FOX_BOUND_SLACK * stats_ref[b, i, QK_MAX, hd]
              - FOX_DEAD_LOG2 for hd in heads]

    def dead(j):
        is_dead = True
        for e, hd in enumerate(heads):
            gap = q_max[e] * stats_ref[b, j, K_MAX, hd] - stats_ref[b, j, LC_LAST, hd] + margin[e]
            is_dead = jnp.logical_and(is_dead, gap <= 0.0)
        return is_dead

    last = jnp.maximum(i - 1, 0)
    return lax.while_loop(lambda j: jnp.logical_and(j < i, dead(jnp.minimum(j, last))),
                          lambda j: j + 1, jnp.int32(0))


def _fox_attn_kernel(stats_ref, q_ref, k_ref, vt_ref, o_ref, m_ref, acc_ref, sa_ref, sb_ref):
    tk = vt_ref.shape[2]
    nh = m_ref.shape[0]
    i = pl.program_id(2)
    j0 = _fox_first_live_block(stats_ref, pl.program_id(0), pl.program_id(1) * nh, nh, i)
    m_ref[...] = jnp.full_like(m_ref, NEG_BIG)
    acc_ref[...] = jnp.zeros_like(acc_ref)

    half = tk // 2

    def scores(j, buf, e, diag=False):
        lanes = slice(e * LANES, (e + 1) * LANES)
        row0 = pl.multiple_of(j * tk, tk)
        if not diag:
            buf[e] = _dot_nt(k_ref[pl.ds(row0, tk), lanes], q_ref[:, lanes])
        else:
            buf[e, :half, :] = _dot_nt(k_ref[pl.ds(row0, half), lanes], q_ref[:, lanes])
            buf[e, half:, half:] = _dot_nt(k_ref[pl.ds(row0 + half, half), lanes], q_ref[half:, lanes])

    def online_update(e, st, vt, cols):
        m_old = m_ref[e, :, cols]
        m_new = jnp.maximum(m_old, jnp.max(st, axis=0, keepdims=True))
        alpha = jnp.exp2(m_old - m_new)
        pt = jnp.exp2(st - m_new).astype(BF16)
        acc_ref[e, :, cols] = alpha * acc_ref[e, :, cols] + _dot(vt, pt)
        m_ref[e, :, cols] = m_new

    def accumulate(j, buf, e, masked):
        rows = slice(e * LANES, (e + 1) * LANES)
        if not masked:
            online_update(e, buf[e], vt_ref[j, rows, :], slice(None))
            return
        key = lax.broadcasted_iota(jnp.int32, (half, half), 0)
        qry = lax.broadcasted_iota(jnp.int32, (half, half), 1)
        causal = key <= qry
        st = jnp.concatenate([jnp.where(causal, buf[e, :half, :half], NEG_BIG), buf[e, :half, half:]], axis=1)
        online_update(e, st, vt_ref[j, rows, :half], slice(None))
        online_update(e, jnp.where(causal, buf[e, half:, half:], NEG_BIG), vt_ref[j, rows, half:],
                      slice(half, None))

    def block(j, buf, masked, following):
        for e in range(nh):
            if e + 1 < nh:
                scores(j, buf, e + 1, masked)
            elif following is not None:
                scores(following[0], following[1], 0, following[2])
            accumulate(j, buf, e, masked)

    scores(j0, sa_ref, 0)
    n_full = i - j0

    def body(t, carry):
        j = j0 + 2 * t
        block(j, sa_ref, False, (j + 1, sb_ref, False))
        block(j + 1, sb_ref, False, (j + 2, sa_ref, False))
        return carry

    lax.fori_loop(0, n_full // 2, body, 0)

    @pl.when(n_full % 2 == 0)
    def _():
        block(i, sa_ref, True, None)

    @pl.when(n_full % 2 == 1)
    def _():
        block(i - 1, sa_ref, False, (i, sb_ref, True))
        block(i, sb_ref, True, None)

    outs = []
    for e in range(nh):
        acc = acc_ref[e]
        outs.append(acc[:HEAD_DIM, :] / acc[HEAD_DIM:HEAD_DIM + 1, :])
    o_ref[...] = jnp.concatenate(outs, axis=0).T.astype(BF16)


def _fox_attn_call(q, k, vt, stats, ffn_weights, nh=4):
    bsz, s, _ = q.shape
    tk = vt.shape[3]
    tq = tk
    return _call_with_ffn_cast(
        _fox_attn_kernel,
        grid=(bsz, FOX_HEADS // nh, s // tq),
        in_specs=[
            pl.BlockSpec(memory_space=pltpu.SMEM),
            pl.BlockSpec((None, tq, nh * LANES), lambda b, p, i: (b, i, p)),
            pl.BlockSpec((None, s, nh * LANES), lambda b, p, i: (b, 0, p)),
            pl.BlockSpec((None, s // tk, nh * LANES, tk), lambda b, p, i: (b, 0, p, 0)),
        ],
        out_specs=[pl.BlockSpec((None, tq, nh * HEAD_DIM), lambda b, p, i: (b, i, p))],
        out_shape=[jax.ShapeDtypeStruct((bsz, s, D_MODEL), BF16)],
        scratch_shapes=[
            pltpu.VMEM((nh, 1, tq), F32),
            pltpu.VMEM((nh, LANES, tq), F32),
            pltpu.VMEM((nh, tk, tq), F32),
            pltpu.VMEM((nh, tk, tq), F32),
        ],
        name="fox_attn", args=(stats, q, k, vt), ffn_weights=ffn_weights)


def _post_kernel(x_ref, o_ref, mod_ref, gain_ref, wo_ref, wgu_ref, wd_ref, fn_ref, out_ref,
                 *, ff_chunk, final):
    x1 = x_ref[...] + mod_ref[2:3, :] * _dot(o_ref[...], wo_ref[...])
    h = _norm_mod(x1, gain_ref[...], mod_ref[3:4, :], mod_ref[4:5, :]).astype(BF16)
    acc = jnp.zeros(x1.shape, F32)
    for c0 in range(0, D_FF, ff_chunk):
        g = _dot(h, wgu_ref[:, c0:c0 + ff_chunk])
        u = _dot(h, wgu_ref[:, D_FF + c0:D_FF + c0 + ff_chunk])
        acc = acc + _dot((_silu(g) * u).astype(BF16), wd_ref[c0:c0 + ff_chunk, :])
    x2 = x1 + mod_ref[5:6, :] * acc
    if final:
        x2 = _rms(x2, fn_ref[...])
    out_ref[...] = x2


def _post_call(x, o, mod, gain, wo, wgu, wd, final_norm, final, tm=1024, ff_chunk=256):
    bsz, s, d = x.shape
    row = lambda b, i: (b, i, 0)
    return pl.pallas_call(
        functools.partial(_post_kernel, ff_chunk=ff_chunk, final=final),
        grid=(bsz, s // tm),
        in_specs=[
            pl.BlockSpec((None, tm, d), row),
            pl.BlockSpec((None, tm, d), row),
            pl.BlockSpec((None, 6, d), lambda b, i: (b, 0, 0)),
            _const_spec((1, d)),
            _const_spec(wo.shape),
            _const_spec(wgu.shape),
            _const_spec(wd.shape),
            _const_spec((1, d)),
        ],
        out_specs=pl.BlockSpec((None, tm, d), row),
        out_shape=jax.ShapeDtypeStruct((bsz, s, d), F32),
        compiler_params=_params("arbitrary", "arbitrary"),
        name="post_ffn",
    )(x, o, mod, gain, wo, wgu, wd, final_norm)


def _rope_tables(s):
    half = HEAD_DIM // 2
    inv = 1.0 / (ROPE_THETA ** (jnp.arange(0, HEAD_DIM, 2, dtype=F32) / HEAD_DIM))
    ang = jnp.arange(s, dtype=F32)[:, None] * inv[None, :]
    cos, sin = jnp.cos(ang), jnp.sin(ang)
    reps = LANES // HEAD_DIM
    cos_t = jnp.tile(jnp.concatenate([cos, cos], axis=1), (1, reps))
    sin_t = jnp.tile(jnp.concatenate([-sin, sin], axis=1), (1, reps))
    assert half * 2 == HEAD_DIM
    return cos_t, sin_t


def _dup_heads(w, heads):
    w3 = w.reshape(w.shape[0], heads, HEAD_DIM)
    return jnp.concatenate([w3, w3], axis=2).reshape(w.shape[0], heads * LANES)


def _pad_heads(w, heads):
    w3 = w.reshape(w.shape[0], heads, HEAD_DIM)
    return jnp.concatenate([w3, jnp.zeros_like(w3)], axis=2).reshape(w.shape[0], heads * LANES)


def _fox_placement():
    h = FOX_HEADS
    pq = np.zeros((4 * h, h * HEAD_DIM), np.float32)
    pk = np.zeros((4 * h, h * HEAD_DIM), np.float32)
    for head in range(h):
        base = (head // 2) * LANES + (HEAD_DIM if head % 2 == 0 else 0)
        for part in range(3):
            pq[part * h + head, base + part] = 1.0
            pk[3 * h + head, base + part] = 1.0
            pq[3 * h + head, base + 3 + part] = 1.0
            pk[part * h + head, base + 3 + part] = -1.0
    return jnp.asarray(pq, BF16), jnp.asarray(pk, BF16)


def kernel(x, c, ada_w, ada_b, norm_gain, ffn_w_gu, ffn_w_down, swa_w_in, swa_sinks, swa_w_o,
           gla_w_in, gla_w_gate_up, gla_b_gate, gla_head_norm, gla_w_o, fox_w_in, fox_b_f, fox_w_o,
           final_norm):
    bsz, s, d = x.shape
    depth = ada_w.shape[0]
    mod_all = _ada_call(c, ada_w, ada_b).reshape(depth, bsz, 6, d)
    cos_t, sin_t = _rope_tables(s)
    pq, pk = _fox_placement()
    fn = final_norm.reshape(1, d)

    for i in range(depth):
        kind, j = i % N_MIXERS, i // N_MIXERS
        mod = mod_all[i]
        gain1 = norm_gain[i, 0].reshape(1, d)
        gain2 = norm_gain[i, 1].reshape(1, d)
        ffn_weights = (ffn_w_gu, ffn_w_down, i)
        if kind == 0:
            w = swa_w_in[j]
            nq, nkv = SWA_Q_HEADS * HEAD_DIM, SWA_KV_HEADS * HEAD_DIM
            w_all = jnp.concatenate([w[:, :nq], _dup_heads(w[:, nq:nq + nkv], SWA_KV_HEADS)],
                                    axis=1).astype(BF16)
            wvt = _pad_heads(w[:, nq + nkv:], SWA_KV_HEADS).T.astype(BF16)
            q, k, v, wgu, wd = _swa_proj_call(x, mod, gain1, w_all, wvt, cos_t, sin_t, ffn_weights)
            o, = _swa_attn_call(q, k, v, swa_sinks[j], None)
            wo = swa_w_o[j]
        elif kind == 1:
            w = gla_w_in[j]
            n_main = 2 * GLA_HEADS * GLA_DK + 2 * GLA_HEADS * GLA_DV
            q, k, v, r, la, wgu, wd = _gla_proj_call(
                x, mod, gain1, w[:, :n_main].astype(BF16), w[:, n_main:].astype(BF16),
                gla_w_gate_up[j].astype(BF16), gla_b_gate[j].reshape(1, -1), ffn_weights)
            o, = _gla_call(q, k, v, r, la, gla_head_norm[j].reshape(1, -1), None)
            wo = gla_w_o[j]
        else:
            w = fox_w_in[j]
            order = jnp.argsort(fox_b_f[j])
            by_head = lambda m: jnp.take(m.reshape(d, FOX_HEADS, HEAD_DIM), order, axis=1).reshape(d, d)
            q, k, v, stats = _fox_proj_call(
                x, mod, gain1,
                by_head(w[:, :d]).astype(BF16), by_head(w[:, d:2 * d]).astype(BF16),
                by_head(w[:, 2 * d:3 * d]).T.astype(BF16), jnp.take(w[:, 3 * d:], order, axis=1).astype(BF16),
                jnp.take(fox_b_f[j], order).reshape(1, -1), pq, pk)
            o, wgu, wd = _fox_attn_call(q, k, v, stats, ffn_weights)
            wo = jnp.take(fox_w_o[j].reshape(FOX_HEADS, HEAD_DIM, d), order, axis=0).reshape(d, d)
        x = _post_call(x, o, mod, gain2, wo.astype(BF16), wgu, wd, fn, final=(i == depth - 1))
    return x
```

```python
import functools
import math

import numpy as np
import jax
import jax.numpy as jnp
from jax import lax
from jax.experimental import pallas as pl
from jax.experimental.pallas import tpu as pltpu

D_MODEL = 1024
HEAD_DIM = 64
RMS_EPS = 1e-6
SWA_Q_HEADS = 16
SWA_KV_HEADS = 4
SWA_WINDOW = 128
SWA_SCORE_SLOTS = 3
ROPE_THETA = 150000.0
GLA_HEADS = 4
GLA_DK = 128
GLA_DV = 256
GLA_RANK = 16
GLA_TAU = 16.0
GLA_CHUNK = 64
GLA_SUB = 8
FOX_HEADS = 16
FOX_STATS = 8
Q_MAX, K_MAX, QK_MAX, LC_FIRST, LC_LAST = range(5)
FOX_DEAD_LOG2 = -160.0
FOX_BOUND_SLACK = 1.02
D_FF = 2816
N_MIXERS = 3

LANES = 128
BF16_ROWS = 16
NEG_BIG = -1e30
LOG2E = 1.4426950408889634
VMEM_LIMIT = 56 * 1024 * 1024

BF16 = jnp.bfloat16
F32 = jnp.float32


def _dot(a, b):
    return jnp.dot(a, b, preferred_element_type=F32)


def _dot_nt(a, b):
    return lax.dot_general(a, b, (((1,), (1,)), ((), ())), preferred_element_type=F32)


def _dot_tn(a, b):
    return lax.dot_general(a, b, (((0,), (0,)), ((), ())), preferred_element_type=F32)


def _split3(x):
    hi = x.astype(BF16)
    r1 = x - hi.astype(F32)
    mid = r1.astype(BF16)
    lo = (r1 - mid.astype(F32)).astype(BF16)
    return hi, mid, lo


def _cumsum_rows(x):
    n = x.shape[0]
    row = lax.broadcasted_iota(jnp.int32, (n, n), 0)
    col = lax.broadcasted_iota(jnp.int32, (n, n), 1)
    tril = jnp.where(row >= col, 1.0, 0.0).astype(BF16)
    w = x.shape[1]
    sums = _dot(tril, jnp.concatenate(_split3(x), axis=1))
    return sums[:, :w] + sums[:, w:2 * w] + sums[:, 2 * w:]


def _log_sigmoid(x):
    return jnp.minimum(x, 0.0) - jnp.log(1.0 + jnp.exp(-jnp.abs(x)))


def _silu(x):
    return x * (1.0 / (1.0 + jnp.exp(-x)))


def _rms(x, gain):
    ms = jnp.mean(x * x, axis=-1, keepdims=True)
    return x * lax.rsqrt(ms + RMS_EPS) * gain


def _norm_mod(x, gain, shift, scale):
    return _rms(x, gain) * (1.0 + scale) + shift


def _params(*sem):
    return pltpu.CompilerParams(dimension_semantics=sem, vmem_limit_bytes=VMEM_LIMIT)


def _const_spec(shape):
    nd = len(shape)
    return pl.BlockSpec(shape, lambda *_: (0,) * nd, pipeline_mode=pl.Buffered(1))


def _ada_kernel(ct_ref, w_ref, b_ref, out_ref):
    ca = _silu(ct_ref[...])
    w = w_ref[...]
    for b in range(ct_ref.shape[1]):
        col = ca[:, b:b + 1]
        out_ref[b:b + 1, :] = jnp.sum(col * w, axis=0, keepdims=True) + b_ref[...]


def _ada_call(c, ada_w, ada_b):
    depth, d, n = ada_w.shape
    bsz = c.shape[0]
    tn = 1536
    return pl.pallas_call(
        _ada_kernel,
        grid=(depth, n // tn),
        in_specs=[
            pl.BlockSpec((d, bsz), lambda l, j: (0, 0)),
            pl.BlockSpec((None, d, tn), lambda l, j: (l, 0, j)),
            pl.BlockSpec((None, 1, tn), lambda l, j: (l, 0, j)),
        ],
        out_specs=pl.BlockSpec((None, bsz, tn), lambda l, j: (l, 0, j)),
        out_shape=jax.ShapeDtypeStruct((depth, bsz, n), F32),
        compiler_params=_params("arbitrary", "arbitrary"),
        name="ada_mod",
    )(c.T, ada_w, ada_b.reshape(depth, 1, n))


def _rope(x, cos, sin_signed):
    width = x.shape[1]
    reps = width // cos.shape[1]
    c = jnp.tile(cos, (1, reps))
    s = jnp.tile(sin_signed, (1, reps))
    lane = lax.broadcasted_iota(jnp.int32, x.shape, 1)
    first_half = (lane % HEAD_DIM) < (HEAD_DIM // 2)
    rot = jnp.where(first_half,
                    pltpu.roll(x, width - HEAD_DIM // 2, 1),
                    pltpu.roll(x, HEAD_DIM // 2, 1))
    return x * c + rot * s


def _ones_row_64(vt):
    ones_row = lax.broadcasted_iota(jnp.int32, vt.shape, 0) % LANES == HEAD_DIM
    return jnp.where(ones_row, 1.0, vt)


def _swa_proj_kernel(x_ref, mod_ref, gain_ref, w_ref, wvt_ref, cos_ref, sin_ref, q_ref, k_ref, vt_ref):
    h = _norm_mod(x_ref[...], gain_ref[...], mod_ref[0:1, :], mod_ref[1:2, :]).astype(BF16)
    cos, sin = cos_ref[...], sin_ref[...]
    nq = q_ref.shape[1]
    q = _dot(h, w_ref[:, :nq])
    q_ref[...] = (_rope(q, cos, sin) * (HEAD_DIM ** -0.5 * LOG2E)).astype(BF16)
    k = _dot(h, w_ref[:, nq:])
    k_ref[...] = _rope(k, cos, sin).astype(BF16)
    vt_ref[...] = _ones_row_64(_dot_nt(wvt_ref[...], h)).astype(BF16)


def _swa_proj_call(x, mod, gain, w, wvt, cos, sin, ffn_weights, tm=1024):
    bsz, s, d = x.shape
    nq, nkv = D_MODEL, SWA_KV_HEADS * LANES
    row = lambda b, i: (b, i, 0)
    return _call_with_ffn_cast(
        _swa_proj_kernel,
        grid=(bsz, s // tm),
        in_specs=[
            pl.BlockSpec((None, tm, d), row),
            pl.BlockSpec((None, 6, d), lambda b, i: (b, 0, 0)),
            _const_spec((1, d)),
            _const_spec(w.shape),
            _const_spec(wvt.shape),
            pl.BlockSpec((tm, LANES), lambda b, i: (i, 0)),
            pl.BlockSpec((tm, LANES), lambda b, i: (i, 0)),
        ],
        out_specs=[
            pl.BlockSpec((None, tm, nq), row),
            pl.BlockSpec((None, tm, nkv), row),
            pl.BlockSpec((None, nkv, tm), lambda b, i: (b, 0, i)),
        ],
        out_shape=[
            jax.ShapeDtypeStruct((bsz, s, nq), BF16),
            jax.ShapeDtypeStruct((bsz, s, nkv), BF16),
            jax.ShapeDtypeStruct((bsz, nkv, s), BF16),
        ],
        scratch_shapes=[], name="swa_proj", args=(x, mod, gain, w, wvt, cos, sin), ffn_weights=ffn_weights)


def _gla_proj_kernel(x_ref, mod_ref, gain_ref, w_ref, wa_ref, wg_ref, bg_ref,
                     q_ref, k_ref, v_ref, r_ref, la_ref):
    h = _norm_mod(x_ref[...], gain_ref[...], mod_ref[0:1, :], mod_ref[1:2, :]).astype(BF16)
    nk = q_ref.shape[1]
    nv = v_ref.shape[1]
    q_ref[...] = _dot(h, w_ref[:, :nk]).astype(BF16)
    k_ref[...] = _dot(h, w_ref[:, nk:2 * nk]).astype(BF16)
    v_ref[...] = _dot(h, w_ref[:, 2 * nk:2 * nk + nv]).astype(BF16)
    r_ref[...] = _dot(h, w_ref[:, 2 * nk + nv:]).astype(BF16)
    a_low = _dot(h, wa_ref[...]).astype(BF16)
    z = _dot(a_low, wg_ref[...]) + bg_ref[...]
    la_ref[...] = _log_sigmoid(z) * (1.0 / GLA_TAU)


def _gla_proj_call(x, mod, gain, w, wa, wg, bg, ffn_weights, tm=1024):
    bsz, s, d = x.shape
    nk, nv = GLA_HEADS * GLA_DK, GLA_HEADS * GLA_DV
    row = lambda b, i: (b, i, 0)
    return _call_with_ffn_cast(
        _gla_proj_kernel,
        grid=(bsz, s // tm),
        in_specs=[
            pl.BlockSpec((None, tm, d), row),
            pl.BlockSpec((None, 6, d), lambda b, i: (b, 0, 0)),
            _const_spec((1, d)),
            _const_spec(w.shape),
            _const_spec(wa.shape),
            _const_spec(wg.shape),
            _const_spec(bg.shape),
        ],
        out_specs=[
            pl.BlockSpec((None, tm, nk), row),
            pl.BlockSpec((None, tm, nk), row),
            pl.BlockSpec((None, tm, nv), row),
            pl.BlockSpec((None, tm, nv), row),
            pl.BlockSpec((None, tm, nk), row),
        ],
        out_shape=[
            jax.ShapeDtypeStruct((bsz, s, nk), BF16),
            jax.ShapeDtypeStruct((bsz, s, nk), BF16),
            jax.ShapeDtypeStruct((bsz, s, nv), BF16),
            jax.ShapeDtypeStruct((bsz, s, nv), BF16),
            jax.ShapeDtypeStruct((bsz, s, nk), F32),
        ],
        scratch_shapes=[], name="gla_proj", args=(x, mod, gain, w, wa, wg, bg), ffn_weights=ffn_weights)


def _spread_heads(x, extra, out_ref):
    lane = lax.broadcasted_iota(jnp.int32, (x.shape[0], LANES), 1)
    low = lane < HEAD_DIM
    for p in range(x.shape[1] // LANES):
        xs = x[:, p * LANES:(p + 1) * LANES]
        ex = extra[:, p * LANES:(p + 1) * LANES]
        out_ref[:, (2 * p) * LANES:(2 * p + 1) * LANES] = jnp.where(low, xs, ex).astype(out_ref.dtype)
        odd = pltpu.roll(jnp.where(low, ex, xs), HEAD_DIM, 1)
        out_ref[:, (2 * p + 1) * LANES:(2 * p + 2) * LANES] = odd.astype(out_ref.dtype)


def _fox_proj_kernel(x_ref, mod_ref, gain_ref, wq_ref, wk_ref, wvt_ref, wf_ref, bf_ref,
                     pq_ref, pk_ref, hsel_ref, q_ref, k_ref, vt_ref, stats_ref, carry_ref):
    @pl.when(pl.program_id(1) == 0)
    def _():
        carry_ref[...] = jnp.zeros_like(carry_ref)

    h = _norm_mod(x_ref[...], gain_ref[...], mod_ref[0:1, :], mod_ref[1:2, :]).astype(BF16)
    log_f = _log_sigmoid(_dot(h, wf_ref[...]) + bf_ref[...])
    lc = _cumsum_rows(log_f) + carry_ref[...]
    carry_ref[...] = lc[lc.shape[0] - 1:, :]
    lc2 = lc * LOG2E
    hi, mid, lo = _split3(lc2)
    aug = jnp.concatenate([hi, mid, lo, jnp.ones_like(hi)], axis=1)
    qs = _dot(h, wq_ref[...]) * (HEAD_DIM ** -0.5 * LOG2E)
    ks = _dot(h, wk_ref[...])
    _spread_heads(qs, _dot(aug, pq_ref[...]), q_ref)
    _spread_heads(ks, _dot(aug, pk_ref[...]), k_ref)
    qn2 = _dot((qs * qs).astype(BF16), hsel_ref[...])
    kn2 = _dot((ks * ks).astype(BF16), hsel_ref[...])
    tm = lc2.shape[0]
    stats_ref[...] = jnp.concatenate([
        jnp.sqrt(jnp.max(qn2, axis=0, keepdims=True)),
        jnp.sqrt(jnp.max(kn2, axis=0, keepdims=True)),
        jnp.sqrt(jnp.max(qn2 * kn2, axis=0, keepdims=True)),
        lc2[0:1, :], lc2[tm - 1:tm, :],
        jnp.zeros((FOX_STATS - 5, lc2.shape[1]), F32)], axis=0)
    vt = _dot_nt(wvt_ref[...], h).astype(BF16)
    pad = jnp.where(lax.broadcasted_iota(jnp.int32, (HEAD_DIM, vt.shape[1]), 0) == 0,
                    1.0, 0.0).astype(BF16)
    for hd in range(FOX_HEADS):
        vt_ref[hd * LANES:hd * LANES + HEAD_DIM, :] = vt[hd * HEAD_DIM:(hd + 1) * HEAD_DIM, :]
        vt_ref[hd * LANES + HEAD_DIM:(hd + 1) * LANES, :] = pad


def _fox_proj_call(x, mod, gain, wq, wk, wvt, wf, bf, pq, pk, tm=512):
    bsz, s, d = x.shape
    nqk = FOX_HEADS * LANES
    row = lambda b, i: (b, i, 0)
    hsel = jnp.asarray(np.repeat(np.eye(FOX_HEADS, dtype=np.float32), HEAD_DIM, axis=0), BF16)
    return pl.pallas_call(
        _fox_proj_kernel,
        grid=(bsz, s // tm),
        in_specs=[
            pl.BlockSpec((None, tm, d), row),
            pl.BlockSpec((None, 6, d), lambda b, i: (b, 0, 0)),
            _const_spec((1, d)),
            _const_spec(wq.shape),
            _const_spec(wk.shape),
            _const_spec(wvt.shape),
            _const_spec(wf.shape),
            _const_spec(bf.shape),
            _const_spec(pq.shape),
            _const_spec(pk.shape),
            _const_spec(hsel.shape),
        ],
        out_specs=[
            pl.BlockSpec((None, tm, nqk), row),
            pl.BlockSpec((None, tm, nqk), row),
            pl.BlockSpec((None, None, nqk, tm), lambda b, i: (b, i, 0, 0)),
            pl.BlockSpec((None, None, FOX_STATS, FOX_HEADS), lambda b, i: (b, i, 0, 0)),
        ],
        out_shape=[
            jax.ShapeDtypeStruct((bsz, s, nqk), BF16),
            jax.ShapeDtypeStruct((bsz, s, nqk), BF16),
            jax.ShapeDtypeStruct((bsz, s // tm, nqk, tm), BF16),
            jax.ShapeDtypeStruct((bsz, s // tm, FOX_STATS, FOX_HEADS), F32),
        ],
        scratch_shapes=[pltpu.VMEM((1, FOX_HEADS), F32)],
        compiler_params=_params("arbitrary", "arbitrary"),
        name="fox_proj",
    )(x, mod, gain, wq, wk, wvt, wf, bf, pq, pk, hsel)


FFN_CAST_BLOCKS = 16


def _call_with_ffn_cast(kernel_fn, grid, in_specs, out_specs, out_shape, scratch_shapes, name, args,
                        ffn_weights):
    params = _params(*["arbitrary"] * len(grid))
    if ffn_weights is None:
        return pl.pallas_call(kernel_fn, grid=grid, in_specs=in_specs, out_specs=out_specs,
                              out_shape=out_shape, scratch_shapes=scratch_shapes,
                              compiler_params=params, name=name)(*args)
    wgu_stack, wd_stack, layer = ffn_weights
    n_in, n_out = len(in_specs), len(out_specs)
    steps = math.prod(grid)
    blocks = math.gcd(steps, FFN_CAST_BLOCKS)
    per = steps // blocks

    def linear(*ids):
        idx = ids[0]
        for extent, i in zip(grid[1:], ids[1:]):
            idx = idx * extent + i
        return idx

    def body(*refs):
        ins, (gu32, d32) = refs[:n_in], refs[n_in:n_in + 2]
        outs = refs[n_in + 2:n_in + 2 + n_out]
        (gu16, d16), scratch = refs[n_in + 2 + n_out:n_in + 4 + n_out], refs[n_in + 4 + n_out:]

        def cast():
            gu16[...] = gu32[...].astype(BF16)
            d16[...] = d32[...].astype(BF16)

        if per == 1:
            cast()
        else:
            pl.when(linear(*[pl.program_id(a) for a in range(len(grid))]) % per == 0)(cast)
        kernel_fn(*ins, *outs, *scratch)

    def weight_specs(stack):
        rows, cols = stack.shape[1] // blocks, stack.shape[2]
        return (pl.BlockSpec((None, rows, cols), lambda *ids: (layer, linear(*ids) // per, 0)),
                pl.BlockSpec((rows, cols), lambda *ids: (linear(*ids) // per, 0)),
                jax.ShapeDtypeStruct(stack.shape[1:], BF16))

    gu_in, gu_out, gu_shape = weight_specs(wgu_stack)
    d_in, d_out, d_shape = weight_specs(wd_stack)
    return pl.pallas_call(
        body,
        grid=grid,
        in_specs=list(in_specs) + [gu_in, d_in],
        out_specs=list(out_specs) + [gu_out, d_out],
        out_shape=list(out_shape) + [gu_shape, d_shape],
        scratch_shapes=scratch_shapes,
        compiler_params=params,
        name=name,
    )(*args, wgu_stack, wd_stack)


def _swa_band_bias():
    w, group = SWA_WINDOW, SWA_Q_HEADS // SWA_KV_HEADS
    key = np.arange(2 * w)[:, None]
    qry = np.arange(group * w)[None, :] % w
    dist = (w + qry) - key
    band = (dist >= 0) & (dist < w)
    allowed = np.stack([band & (key >= w), band])
    return jnp.asarray(np.where(allowed, 0.0, NEG_BIG), F32)


def _swa_attn_kernel(sink_ref, bias_ref, q_ref, kc_ref, vtc_ref, o_ref, st_ref, kp_ref, vtp_ref):
    w = SWA_WINDOW
    group = SWA_Q_HEADS // SWA_KV_HEADS
    nq = group * w
    nb = q_ref.shape[0] // w
    lane = lax.broadcasted_iota(jnp.int32, (w, LANES), 1)
    low = lane < HEAD_DIM

    @pl.when(pl.program_id(1) == 0)
    def _():
        kp_ref[...] = jnp.zeros_like(kp_ref)
        vtp_ref[...] = jnp.zeros_like(vtp_ref)

    def scores(u, g, slot):
        rows = slice(u * w, (u + 1) * w)
        cols = slice(g * LANES, (g + 1) * LANES)
        slabs = []
        for hh in range(group):
            head = g * group + hh
            qs = q_ref[rows, (head // 2) * LANES:(head // 2 + 1) * LANES]
            keep = low if head % 2 == 0 else jnp.logical_not(low)
            slabs.append(jnp.where(keep, qs, jnp.zeros_like(qs)))
        q_stack = jnp.concatenate(slabs, axis=0)
        k_prev = kp_ref[:, cols] if u == 0 else kc_ref[(u - 1) * w:u * w, cols]
        k_both = jnp.concatenate([k_prev, kc_ref[rows, cols]], axis=0)
        st_ref[slot] = _dot_nt(k_both, q_stack)

    def finish(u, g, slot):
        rows = slice(u * w, (u + 1) * w)
        cols = slice(g * LANES, (g + 1) * LANES)
        bias = bias_ref[jnp.minimum(pl.program_id(1), 1)] if u == 0 else bias_ref[1]
        sink = sink_ref[:, g * nq:(g + 1) * nq] * LOG2E
        ms, pts = [], []
        for hh in range(group):
            qc = slice(hh * w, (hh + 1) * w)
            st = st_ref[slot, :, qc] + bias[:, qc]
            m_h = jnp.maximum(jnp.max(st, axis=0, keepdims=True), sink[:, qc])
            ms.append(m_h)
            pts.append(jnp.exp2(st - m_h).astype(BF16))
        m = jnp.concatenate(ms, axis=1)
        pt = jnp.concatenate(pts, axis=1)
        vt_prev = vtp_ref[cols, :] if u == 0 else vtc_ref[cols, (u - 1) * w:u * w]
        vt_both = jnp.concatenate([vt_prev, vtc_ref[cols, rows]], axis=1)
        acc = _dot(vt_both, pt)
        den = acc[HEAD_DIM:HEAD_DIM + 1, :] + jnp.exp2(sink - m)
        ot = acc[:HEAD_DIM, :] / den
        o_t = jnp.concatenate([ot[:, hh * w:(hh + 1) * w] for hh in range(group)], axis=0)
        o_ref[rows, g * group * HEAD_DIM:(g + 1) * group * HEAD_DIM] = o_t.T.astype(BF16)

    items = [(u, g) for u in range(nb) for g in range(SWA_KV_HEADS)]
    slots = st_ref.shape[0]
    ahead = slots - 1
    for n in range(min(ahead, len(items))):
        scores(*items[n], n % slots)
    for n, item in enumerate(items):
        if n + ahead < len(items):
            scores(*items[n + ahead], (n + ahead) % slots)
        finish(*item, n % slots)
    kp_ref[...] = kc_ref[(nb - 1) * w:, :]
    vtp_ref[...] = vtc_ref[:, (nb - 1) * w:]


def _swa_attn_call(q, k, vt, sinks, ffn_weights, nb=4):
    bsz, s, d = q.shape
    w = SWA_WINDOW
    nkv = k.shape[2]
    sink_row = jnp.repeat(sinks, w).reshape(1, -1)
    bias = _swa_band_bias()
    cur = lambda b, i: (b, i, 0)
    return _call_with_ffn_cast(
        _swa_attn_kernel,
        grid=(bsz, s // (nb * w)),
        in_specs=[
            _const_spec(sink_row.shape),
            _const_spec(bias.shape),
            pl.BlockSpec((None, nb * w, d), cur),
            pl.BlockSpec((None, nb * w, nkv), cur),
            pl.BlockSpec((None, nkv, nb * w), lambda b, i: (b, 0, i)),
        ],
        out_specs=[pl.BlockSpec((None, nb * w, d), cur)],
        out_shape=[jax.ShapeDtypeStruct((bsz, s, d), BF16)],
        scratch_shapes=[
            pltpu.VMEM((SWA_SCORE_SLOTS, 2 * w, (SWA_Q_HEADS // SWA_KV_HEADS) * w), F32),
            pltpu.VMEM((w, nkv), BF16),
            pltpu.VMEM((nkv, w), BF16),
        ],
        name="swa_attn", args=(sink_row, bias, q, k, vt), ffn_weights=ffn_weights)


def _chunk_cumsum(x):
    n = x.shape[0]
    row = lax.broadcasted_iota(jnp.int32, (n, n), 0)
    col = lax.broadcasted_iota(jnp.int32, (n, n), 1)
    same_chunk = (row // GLA_CHUNK) == (col // GLA_CHUNK)
    tril = jnp.where(jnp.logical_and(row >= col, same_chunk), 1.0, 0.0).astype(BF16)
    hi, mid, lo = _split3(x)
    return _dot(tril, hi) + _dot(tril, mid) + _dot(tril, lo)


def _gla_intra(q, k, b2, k_rows, b_rows, base):
    c, sub = GLA_CHUNK, GLA_SUB
    col = lax.broadcasted_iota(jnp.int32, (sub, c), 1)
    row = lax.broadcasted_iota(jnp.int32, (sub, c), 0)
    blocks = []
    for i in range(c // sub):
        lo = i * sub
        q_i = q[lo:lo + sub, :]
        b_i = b2[lo:lo + sub, :]
        if i == 0:
            a = jnp.zeros((sub, c), F32)
        else:
            ref = b2[lo - 1:lo, :]
            n = -(-lo // BF16_ROWS) * BF16_ROWS
            q_t = (q_i * jnp.exp2(b_i - ref)).astype(BF16)
            k_t = (k[:n, :] * jnp.exp2(jnp.minimum(ref - b2[:n, :], 0.0))).astype(BF16)
            if n < c:
                k_t = jnp.concatenate([k_t, jnp.zeros((c - n, k_t.shape[1]), BF16)], axis=0)
            a = _dot_nt(q_t, k_t)
        for s in range(lo, lo + sub):
            w = jnp.exp2(b_i - b_rows[base + s:base + s + 1, :])
            val = jnp.sum(q_i * k_rows[base + s:base + s + 1, :] * w, axis=1, keepdims=True)
            a = jnp.where(col == s, val, a)
        blocks.append(jnp.where(row + lo >= col, a, 0.0))
    return jnp.concatenate(blocks, axis=0)


def _gla_kernel(q_ref, k_ref, v_ref, r_ref, la_ref, hn_ref, o_ref, state_ref, krow_ref, brow_ref):
    @pl.when(pl.program_id(2) == 0)
    def _():
        state_ref[...] = jnp.zeros_like(state_ref)

    for hh in range(state_ref.shape[0]):
        kq = slice(hh * GLA_DK, (hh + 1) * GLA_DK)
        vv = slice(hh * GLA_DV, (hh + 1) * GLA_DV)
        _gla_head(q_ref.at[:, kq], k_ref.at[:, kq], v_ref.at[:, vv], r_ref.at[:, vv], la_ref.at[:, kq],
                  hn_ref, o_ref.at[:, vv], state_ref.at[hh], krow_ref.at[hh], brow_ref.at[hh])


def _gla_head(q_ref, k_ref, v_ref, r_ref, la_ref, hn_ref, o_ref, state_ref, krow_ref, brow_ref):
    c = GLA_CHUNK
    nc = q_ref.shape[0] // c
    rows = [slice(ci * c, (ci + 1) * c) for ci in range(nc)]
    b2_all = _chunk_cumsum(la_ref[...]) * LOG2E
    q_all = q_ref[...].astype(F32) * (GLA_DK ** -0.5)
    k_all = k_ref[...].astype(F32)
    krow_ref[...] = k_all
    brow_ref[...] = b2_all
    b2 = [b2_all[r] for r in rows]
    q = [q_all[r] for r in rows]
    k = [k_all[r] for r in rows]
    last = [b[c - 1:c, :] for b in b2]
    q_in = [(q[i] * jnp.exp2(b2[i])).astype(BF16) for i in range(nc)]
    k_out = [(k[i] * jnp.exp2(last[i] - b2[i])).astype(BF16) for i in range(nc)]
    kv = [_dot_tn(v_ref[rows[i], :], k_out[i]) for i in range(nc)]
    attn = [_gla_intra(q[i], k[i], b2[i], krow_ref, brow_ref, i * c).astype(BF16) for i in range(nc)]
    intra = [_dot(attn[i], v_ref[rows[i], :]) for i in range(nc)]
    state_t = state_ref[...]
    for i in range(nc):
        o = intra[i] + _dot_nt(q_in[i], state_t.astype(BF16))
        state_t = state_t * jnp.exp2(last[i]) + kv[i]
        r = r_ref[rows[i], :].astype(F32)
        o_ref[rows[i], :] = (_rms(o, hn_ref[...]) * _silu(r)).astype(BF16)
    state_ref[...] = state_t


def _gla_call(q, k, v, r, la, head_norm, ffn_weights, tm=256, nh=4):
    bsz, s, _ = q.shape
    dk, dv = GLA_DK, GLA_DV
    blk = lambda b, h, i: (b, i, h)
    return _call_with_ffn_cast(
        _gla_kernel,
        grid=(bsz, GLA_HEADS // nh, s // tm),
        in_specs=[
            pl.BlockSpec((None, tm, nh * dk), blk),
            pl.BlockSpec((None, tm, nh * dk), blk),
            pl.BlockSpec((None, tm, nh * dv), blk),
            pl.BlockSpec((None, tm, nh * dv), blk),
            pl.BlockSpec((None, tm, nh * dk), blk),
            _const_spec((1, dv)),
        ],
        out_specs=[pl.BlockSpec((None, tm, nh * dv), blk)],
        out_shape=[jax.ShapeDtypeStruct((bsz, s, GLA_HEADS * dv), BF16)],
        scratch_shapes=[pltpu.VMEM((nh, dv, dk), F32), pltpu.VMEM((nh, tm, dk), F32),
                        pltpu.VMEM((nh, tm, dk), F32)],
        name="gla_mix", args=(q, k, v, r, la, head_norm), ffn_weights=ffn_weights)


def _fox_first_live_block(stats_ref, b, first_head, nh, i):
    heads = [first_head + e for e in range(nh)]
    q_max = [FOX_BOUND_SLACK * stats_ref[b, i, Q_MAX, hd] for hd in heads]
    margin = [stats_ref[b, i, LC_FIRST, hd] + FOX_BOUND_SLACK * stats_ref[b, i, QK_MAX, hd]
              - FOX_DEAD_LOG2 for hd in heads]

    def dead(j):
        is_dead = True
        for e, hd in enumerate(heads):
            gap = q_max[e] * stats_ref[b, j, K_MAX, hd] - stats_ref[b, j, LC_LAST, hd] + margin[e]
            is_dead = jnp.logical_and(is_dead, gap <= 0.0)
        return is_dead

    last = jnp.maximum(i - 1, 0)
    return lax.while_loop(lambda j: jnp.logical_and(j < i, dead(jnp.minimum(j, last))),
                          lambda j: j + 1, jnp.int32(0))


def _fox_attn_kernel(stats_ref, q_ref, k_ref, vt_ref, o_ref, m_ref, acc_ref, sa_ref, sb_ref):
    tk = vt_ref.shape[2]
    nh = m_ref.shape[0]
    i = pl.program_id(2)
    j0 = _fox_first_live_block(stats_ref, pl.program_id(0), pl.program_id(1) * nh, nh, i)
    m_ref[...] = jnp.full_like(m_ref, NEG_BIG)
    acc_ref[...] = jnp.zeros_like(acc_ref)

    half = tk // 2

    def scores(j, buf, e, diag=False):
        lanes = slice(e * LANES, (e + 1) * LANES)
        row0 = pl.multiple_of(j * tk, tk)
        if not diag:
            buf[e] = _dot_nt(k_ref[pl.ds(row0, tk), lanes], q_ref[:, lanes])
        else:
            buf[e, :half, :] = _dot_nt(k_ref[pl.ds(row0, half), lanes], q_ref[:, lanes])
            buf[e, half:, half:] = _dot_nt(k_ref[pl.ds(row0 + half, half), lanes], q_ref[half:, lanes])

    def online_update(e, st, vt, cols):
        m_old = m_ref[e, :, cols]
        m_new = jnp.maximum(m_old, jnp.max(st, axis=0, keepdims=True))
        alpha = jnp.exp2(m_old - m_new)
        pt = jnp.exp2(st - m_new).astype(BF16)
        acc_ref[e, :, cols] = alpha * acc_ref[e, :, cols] + _dot(vt, pt)
        m_ref[e, :, cols] = m_new

    def accumulate(j, buf, e, masked):
        rows = slice(e * LANES, (e + 1) * LANES)
        if not masked:
            online_update(e, buf[e], vt_ref[j, rows, :], slice(None))
            return
        key = lax.broadcasted_iota(jnp.int32, (half, half), 0)
        qry = lax.broadcasted_iota(jnp.int32, (half, half), 1)
        causal = key <= qry
        st = jnp.concatenate([jnp.where(causal, buf[e, :half, :half], NEG_BIG), buf[e, :half, half:]], axis=1)
        online_update(e, st, vt_ref[j, rows, :half], slice(None))
        online_update(e, jnp.where(causal, buf[e, half:, half:], NEG_BIG), vt_ref[j, rows, half:],
                      slice(half, None))

    def block(j, buf, masked, following):
        for e in range(nh):
            if e + 1 < nh:
                scores(j, buf, e + 1, masked)
            elif following is not None:
                scores(following[0], following[1], 0, following[2])
            accumulate(j, buf, e, masked)

    scores(j0, sa_ref, 0)
    n_full = i - j0

    def body(t, carry):
        j = j0 + 2 * t
        block(j, sa_ref, False, (j + 1, sb_ref, False))
        block(j + 1, sb_ref, False, (j + 2, sa_ref, False))
        return carry

    lax.fori_loop(0, n_full // 2, body, 0)

    @pl.when(n_full % 2 == 0)
    def _():
        block(i, sa_ref, True, None)

    @pl.when(n_full % 2 == 1)
    def _():
        block(i - 1, sa_ref, False, (i, sb_ref, True))
        block(i, sb_ref, True, None)

    outs = []
    for e in range(nh):
        acc = acc_ref[e]
        outs.append(acc[:HEAD_DIM, :] / acc[HEAD_DIM:HEAD_DIM + 1, :])
    o_ref[...] = jnp.concatenate(outs, axis=0).T.astype(BF16)


def _fox_attn_call(q, k, vt, stats, ffn_weights, nh=4):
    bsz, s, _ = q.shape
    tk = vt.shape[3]
    tq = tk
    return _call_with_ffn_cast(
        _fox_attn_kernel,
        grid=(bsz, FOX_HEADS // nh, s // tq),
        in_specs=[
            pl.BlockSpec(memory_space=pltpu.SMEM),
            pl.BlockSpec((None, tq, nh * LANES), lambda b, p, i: (b, i, p)),
            pl.BlockSpec((None, s, nh * LANES), lambda b, p, i: (b, 0, p)),
            pl.BlockSpec((None, s // tk, nh * LANES, tk), lambda b, p, i: (b, 0, p, 0)),
        ],
        out_specs=[pl.BlockSpec((None, tq, nh * HEAD_DIM), lambda b, p, i: (b, i, p))],
        out_shape=[jax.ShapeDtypeStruct((bsz, s, D_MODEL), BF16)],
        scratch_shapes=[
            pltpu.VMEM((nh, 1, tq), F32),
            pltpu.VMEM((nh, LANES, tq), F32),
            pltpu.VMEM((nh, tk, tq), F32),
            pltpu.VMEM((nh, tk, tq), F32),
        ],
        name="fox_attn", args=(stats, q, k, vt), ffn_weights=ffn_weights)


def _post_kernel(x_ref, o_ref, mod_ref, gain_ref, wo_ref, wgu_ref, wd_ref, fn_ref, out_ref,
                 *, ff_chunk, final):
    x1 = x_ref[...] + mod_ref[2:3, :] * _dot(o_ref[...], wo_ref[...])
    h = _norm_mod(x1, gain_ref[...], mod_ref[3:4, :], mod_ref[4:5, :]).astype(BF16)
    acc = jnp.zeros(x1.shape, F32)
    for c0 in range(0, D_FF, ff_chunk):
        g = _dot(h, wgu_ref[:, c0:c0 + ff_chunk])
        u = _dot(h, wgu_ref[:, D_FF + c0:D_FF + c0 + ff_chunk])
        acc = acc + _dot((_silu(g) * u).astype(BF16), wd_ref[c0:c0 + ff_chunk, :])
    x2 = x1 + mod_ref[5:6, :] * acc
    if final:
        x2 = _rms(x2, fn_ref[...])
    out_ref[...] = x2


def _post_call(x, o, mod, gain, wo, wgu, wd, final_norm, final, tm=1024, ff_chunk=256):
    bsz, s, d = x.shape
    row = lambda b, i: (b, i, 0)
    return pl.pallas_call(
        functools.partial(_post_kernel, ff_chunk=ff_chunk, final=final),
        grid=(bsz, s // tm),
        in_specs=[
            pl.BlockSpec((None, tm, d), row),
            pl.BlockSpec((None, tm, d), row),
            pl.BlockSpec((None, 6, d), lambda b, i: (b, 0, 0)),
            _const_spec((1, d)),
            _const_spec(wo.shape),
            _const_spec(wgu.shape),
            _const_spec(wd.shape),
            _const_spec((1, d)),
        ],
        out_specs=pl.BlockSpec((None, tm, d), row),
        out_shape=jax.ShapeDtypeStruct((bsz, s, d), F32),
        compiler_params=_params("arbitrary", "arbitrary"),
        name="post_ffn",
    )(x, o, mod, gain, wo, wgu, wd, final_norm)


def _rope_tables(s):
    half = HEAD_DIM // 2
    inv = 1.0 / (ROPE_THETA ** (jnp.arange(0, HEAD_DIM, 2, dtype=F32) / HEAD_DIM))
    ang = jnp.arange(s, dtype=F32)[:, None] * inv[None, :]
    cos, sin = jnp.cos(ang), jnp.sin(ang)
    reps = LANES // HEAD_DIM
    cos_t = jnp.tile(jnp.concatenate([cos, cos], axis=1), (1, reps))
    sin_t = jnp.tile(jnp.concatenate([-sin, sin], axis=1), (1, reps))
    assert half * 2 == HEAD_DIM
    return cos_t, sin_t


def _dup_heads(w, heads):
    w3 = w.reshape(w.shape[0], heads, HEAD_DIM)
    return jnp.concatenate([w3, w3], axis=2).reshape(w.shape[0], heads * LANES)


def _pad_heads(w, heads):
    w3 = w.reshape(w.shape[0], heads, HEAD_DIM)
    return jnp.concatenate([w3, jnp.zeros_like(w3)], axis=2).reshape(w.shape[0], heads * LANES)


def _fox_placement():
    h = FOX_HEADS
    pq = np.zeros((4 * h, h * HEAD_DIM), np.float32)
    pk = np.zeros((4 * h, h * HEAD_DIM), np.float32)
    for head in range(h):
        base = (head // 2) * LANES + (HEAD_DIM if head % 2 == 0 else 0)
        for part in range(3):
            pq[part * h + head, base + part] = 1.0
            pk[3 * h + head, base + part] = 1.0
            pq[3 * h + head, base + 3 + part] = 1.0
            pk[part * h + head, base + 3 + part] = -1.0
    return jnp.asarray(pq, BF16), jnp.asarray(pk, BF16)


def kernel(x, c, ada_w, ada_b, norm_gain, ffn_w_gu, ffn_w_down, swa_w_in, swa_sinks, swa_w_o,
           gla_w_in, gla_w_gate_up, gla_b_gate, gla_head_norm, gla_w_o, fox_w_in, fox_b_f, fox_w_o,
           final_norm):
    bsz, s, d = x.shape
    depth = ada_w.shape[0]
    mod_all = _ada_call(c, ada_w, ada_b).reshape(depth, bsz, 6, d)
    cos_t, sin_t = _rope_tables(s)
    pq, pk = _fox_placement()
    fn = final_norm.reshape(1, d)

    for i in range(depth):
        kind, j = i % N_MIXERS, i // N_MIXERS
        mod = mod_all[i]
        gain1 = norm_gain[i, 0].reshape(1, d)
        gain2 = norm_gain[i, 1].reshape(1, d)
        ffn_weights = (ffn_w_gu, ffn_w_down, i)
        if kind == 0:
            w = swa_w_in[j]
            nq, nkv = SWA_Q_HEADS * HEAD_DIM, SWA_KV_HEADS * HEAD_DIM
            w_all = jnp.concatenate([w[:, :nq], _dup_heads(w[:, nq:nq + nkv], SWA_KV_HEADS)],
                                    axis=1).astype(BF16)
            wvt = _pad_heads(w[:, nq + nkv:], SWA_KV_HEADS).T.astype(BF16)
            q, k, v, wgu, wd = _swa_proj_call(x, mod, gain1, w_all, wvt, cos_t, sin_t, ffn_weights)
            o, = _swa_attn_call(q, k, v, swa_sinks[j], None)
            wo = swa_w_o[j]
        elif kind == 1:
            w = gla_w_in[j]
            n_main = 2 * GLA_HEADS * GLA_DK + 2 * GLA_HEADS * GLA_DV
            q, k, v, r, la, wgu, wd = _gla_proj_call(
                x, mod, gain1, w[:, :n_main].astype(BF16), w[:, n_main:].astype(BF16),
                gla_w_gate_up[j].astype(BF16), gla_b_gate[j].reshape(1, -1), ffn_weights)
            o, = _gla_call(q, k, v, r, la, gla_head_norm[j].reshape(1, -1), None)
            wo = gla_w_o[j]
        else:
            w = fox_w_in[j]
            order = jnp.argsort(fox_b_f[j])
            by_head = lambda m: jnp.take(m.reshape(d, FOX_HEADS, HEAD_DIM), order, axis=1).reshape(d, d)
            q, k, v, stats = _fox_proj_call(
                x, mod, gain1,
                by_head(w[:, :d]).astype(BF16), by_head(w[:, d:2 * d]).astype(BF16),
                by_head(w[:, 2 * d:3 * d]).T.astype(BF16), jnp.take(w[:, 3 * d:], order, axis=1).astype(BF16),
                jnp.take(fox_b_f[j], order).reshape(1, -1), pq, pk)
            o, wgu, wd = _fox_attn_call(q, k, v, stats, ffn_weights)
            wo = jnp.take(fox_w_o[j].reshape(FOX_HEADS, HEAD_DIM, d), order, axis=0).reshape(d, d)
        x = _post_call(x, o, mod, gain2, wo.astype(BF16), wgu, wd, fn, final=(i == depth - 1))
    return x
```
